```python
import jax
import jax.numpy as jnp
from jax import lax
import numpy as np

D_MODEL = 2048
BATCH = 32
SEQ = 256
DEPTH = 2
DEC_BATCH = 8
DEC_SEQ = 2048
PAST_LEN = 512

GRID_W = 64
N_EVEN = (DEPTH + 1) // 2
N_ODD = DEPTH // 2
HEAD_DIM = 128
A_WIDTH = D_MODEL // 2
B_WIDTH = D_MODEL - A_WIDTH
A_HEADS = A_WIDTH // HEAD_DIM
B_HEADS = B_WIDTH // HEAD_DIM
EVEN_SPLITS = (A_WIDTH,) * 5 + (B_WIDTH,) * 3
EVEN_IN = sum(EVEN_SPLITS)
HGRN_CHUNK = 32
NA_ROWS = 8
NA_COLS = 16
MLA_HEADS = D_MODEL // 128
QK_NOPE = 128
QK_ROPE = 64
V_DIM = 128
Q_RANK = D_MODEL // 4
KV_RANK = D_MODEL // 4
ODD_SPLITS = (Q_RANK, KV_RANK, QK_ROPE)
ODD_IN = sum(ODD_SPLITS)
MLA_SCALE = (QK_NOPE + QK_ROPE) ** -0.5
Q_BLOCK = 128
D_FF = 4 * D_MODEL
ROPE_BASE = 10000.0
EPS = 1e-6
NEG = -1e30
F32 = jnp.float32

kernel_name = 'hybrid_dit_hgrn2_natten_mla_step'


def _rmsnorm(x, g):
    xf = x.astype(F32)
    y = xf * lax.rsqrt(jnp.mean(xf * xf, axis=-1, keepdims=True) + EPS)
    return (y * g.astype(F32)).astype(x.dtype)


def _split(p, sizes):
    out, o = [], 0
    for s in sizes:
        out.append(p[..., o:o + s])
        o += s
    return out


def _heads(a, n):
    return a.reshape(a.shape[0], a.shape[1], n, -1)


def _modulation(cvec, w, b):
    m = jax.nn.silu(cvec) @ w + b
    return m.reshape(m.shape[0], 1, 6, D_MODEL)


def _pre(x, g, mod, k):
    return _rmsnorm(x, g) * (1 + mod[:, :, k + 1]) + mod[:, :, k]


def _post(x, y, g, mod, k):
    return x + mod[:, :, k + 2] * _rmsnorm(y, g)


def _mlp(h, w1, w2):
    a = jax.nn.relu(h @ w1)
    return (a * a) @ w2


def _hgrn_scan(q, k, v, logf, s0):
    B, T, H, _ = q.shape
    n = T // HGRN_CHUNK

    def chunks(a):
        return a.reshape(B, n, HGRN_CHUNK, H, a.shape[-1]).transpose(1, 0, 3, 2, 4)

    tril = jnp.tril(jnp.ones((HGRN_CHUNK, HGRN_CHUNK), bool))[:, :, None]

    def step(S, inp):
        qc, kc, vc, gc = inp
        b = jnp.cumsum(gc, axis=2)
        dec = jnp.where(tril, jnp.exp(jnp.minimum(b[:, :, :, None, :] - b[:, :, None, :, :], 0.0)), 0.0)
        a = jnp.einsum('bhtk,bhsk,bhtsk->bhts', qc, kc, dec)
        o = jnp.einsum('bhtk,bhkv->bhtv', qc * jnp.exp(b), S) + jnp.einsum('bhts,bhsv->bhtv', a, vc)
        b_last = b[:, :, -1, :]
        S_new = jnp.exp(b_last)[..., None] * S + jnp.einsum(
            'bhsk,bhsv->bhkv', kc * jnp.exp(b_last[:, :, None, :] - b), vc)
        return S_new, o

    S, o = lax.scan(step, s0.astype(F32), (chunks(q), chunks(k), chunks(v), chunks(logf)))
    return o.transpose(1, 0, 3, 2, 4).reshape(B, T, H, v.shape[-1]), S


def _hgrn_gate(f_raw, lb):
    f = lb + (1.0 - lb) * jax.nn.sigmoid(f_raw.astype(F32))
    return _heads(1.0 - f, A_HEADS), _heads(jnp.log(f), A_HEADS)


def _hgrn_mixer(q_a, f_fw, f_bw, i_a, g_a, lb_fw, lb_bw, norm_g, s0_fw, s0_bw):
    B, T, _ = q_a.shape
    flip = lambda a: jnp.flip(a, axis=1)
    q = _heads(jax.nn.silu(q_a.astype(F32)) * HEAD_DIM ** -0.5, A_HEADS)
    v = _heads(i_a.astype(F32), A_HEADS)
    k_fw, lf_fw = _hgrn_gate(f_fw, lb_fw)
    k_bw, lf_bw = _hgrn_gate(f_bw, lb_bw)
    o_fw, s_fw = _hgrn_scan(q, k_fw, v, lf_fw, s0_fw)
    o_bw, s_bw = _hgrn_scan(flip(q), flip(k_bw), flip(v), flip(lf_bw), s0_bw)
    o = _rmsnorm(o_fw + flip(o_bw), norm_g) * jax.nn.silu(_heads(g_a.astype(F32), A_HEADS))
    return o.reshape(B, T, A_WIDTH).astype(q_a.dtype), s_fw, s_bw


def _dense_attn(q, k, v):
    s = jnp.einsum('bqhd,bkhd->bhqk', q, k, preferred_element_type=F32) * q.shape[-1] ** -0.5
    p = jax.nn.softmax(s, axis=-1).astype(v.dtype)
    o = jnp.einsum('bhqk,bkhd->bqhd', p, v)
    return o.reshape(o.shape[0], o.shape[1], -1)


def _neighbourhood_attn(q, k, v, k_ctx, v_ctx, rpb):
    B, S, H, Dh = q.shape
    rows = S // GRID_W
    kr = min(NA_ROWS, rows)
    r = jnp.arange(rows)
    row_idx = jnp.clip(r - kr // 2, 0, rows - kr)[:, None] + jnp.arange(kr)[None, :]
    col = jnp.arange(GRID_W)
    cs = jnp.clip(col - NA_COLS // 2, 0, GRID_W - NA_COLS)
    col_ok = (col[None, :] >= cs[:, None]) & (col[None, :] < cs[:, None] + NA_COLS)
    ri = (row_idx - r[:, None] + NA_ROWS - 1)[:, None, :, None]
    ci = jnp.clip(col[None, :] - col[:, None] + NA_COLS - 1, 0, 2 * NA_COLS - 2)[None, :, None, :]
    bias = jnp.where(col_ok[:, None, :], rpb.astype(F32)[:, ri, ci], NEG)
    qg = q.reshape(B, rows, GRID_W, H, Dh)
    k_band = k.reshape(B, rows, GRID_W, H, Dh)[:, row_idx]
    v_band = v.reshape(B, rows, GRID_W, H, Dh)[:, row_idx]
    scale = Dh ** -0.5
    s_lat = jnp.einsum('brqhd,brikhd->bhrqik', qg, k_band, preferred_element_type=F32) * scale + bias[None]
    s_ctx = jnp.einsum('brqhd,blhd->bhrql', qg, k_ctx, preferred_element_type=F32) * scale
    n_lat = kr * GRID_W
    p = jax.nn.softmax(jnp.concatenate([s_lat.reshape(B, H, rows, GRID_W, n_lat), s_ctx], axis=-1),
                       axis=-1).astype(v.dtype)
    o = (jnp.einsum('bhrqik,brikhd->brqhd', p[..., :n_lat].reshape(B, H, rows, GRID_W, kr, GRID_W), v_band)
         + jnp.einsum('bhrql,blhd->brqhd', p[..., n_lat:], v_ctx))
    return o.reshape(B, S, H * Dh)


def _axial_angles(T):
    t = jnp.arange(T)
    half = QK_ROPE // 2
    inv = jnp.power(ROPE_BASE, -jnp.arange(0, half, 2, dtype=F32) / half)
    ang_r = (t // GRID_W).astype(F32)[:, None] * inv
    ang_c = (t % GRID_W).astype(F32)[:, None] * inv
    return ang_r, ang_c


def _rotate(x, ang):
    m = ang.shape[-1]
    cos, sin = jnp.cos(ang)[:, None, :], jnp.sin(ang)[:, None, :]
    x1, x2 = x[..., :m], x[..., m:]
    return jnp.concatenate([x1 * cos - x2 * sin, x2 * cos + x1 * sin], axis=-1)


def _rope2d(x, ang_r, ang_c):
    half = QK_ROPE // 2
    xf = x.astype(F32)
    return jnp.concatenate([_rotate(xf[..., :half], ang_r), _rotate(xf[..., half:], ang_c)],
                           axis=-1).astype(x.dtype)


def _even_context(h, lb_fw, lb_bw, w_in, norm_g, w_out):
    q_a, f_fw, f_bw, i_a, g_a, q_b, k_b, v_b = _split(h @ w_in, EVEN_SPLITS)
    s0 = jnp.zeros((h.shape[0], A_HEADS, HEAD_DIM, HEAD_DIM), F32)
    o_a, s_fw, s_bw = _hgrn_mixer(q_a, f_fw, f_bw, i_a, g_a, lb_fw, lb_bw, norm_g, s0, s0)
    k = _heads(k_b, B_HEADS)
    v = _heads(v_b, B_HEADS)
    o_b = _dense_attn(_heads(q_b, B_HEADS), k, v)
    return jnp.concatenate([o_a, o_b], axis=-1) @ w_out, s_fw, s_bw, k, v


def _even_latent(h, s_fw, s_bw, k_ctx, v_ctx, lb_fw, lb_bw, w_in, norm_g, rpb, w_out):
    q_a, f_fw, f_bw, i_a, g_a, q_b, k_b, v_b = _split(h @ w_in, EVEN_SPLITS)
    o_a, _, _ = _hgrn_mixer(q_a, f_fw, f_bw, i_a, g_a, lb_fw, lb_bw, norm_g, s_fw, s_bw)
    o_b = _neighbourhood_attn(_heads(q_b, B_HEADS), _heads(k_b, B_HEADS), _heads(v_b, B_HEADS),
                              k_ctx, v_ctx, rpb)
    return jnp.concatenate([o_a, o_b], axis=-1) @ w_out


def _mla_project(h, w_in, q_norm_g, w_uq, kv_norm_g):
    cq, ckv, kpe = _split(h @ w_in, ODD_SPLITS)
    q = _heads(_rmsnorm(cq, q_norm_g) @ w_uq, MLA_HEADS)
    return q[..., :QK_NOPE], q[..., QK_NOPE:], _rmsnorm(ckv, kv_norm_g), kpe


def _mla_expand(ckv_n, w_ukv):
    kv = _heads(ckv_n @ w_ukv, MLA_HEADS)
    return kv[..., :QK_NOPE], kv[..., QK_NOPE:]


def _mla_context(h, w_in, q_norm_g, w_uq, kv_norm_g, w_ukv, w_out):
    B, T, _ = h.shape
    q_nope, q_pe, ckv_n, kpe = _mla_project(h, w_in, q_norm_g, w_uq, kv_norm_g)
    k_nope, v = _mla_expand(ckv_n, w_ukv)
    s = (jnp.einsum('bqhd,bkhd->bhqk', q_nope, k_nope, preferred_element_type=F32)
         + jnp.einsum('bqhr,bkr->bhqk', q_pe, kpe, preferred_element_type=F32)) * MLA_SCALE
    p = jax.nn.softmax(s, axis=-1).astype(v.dtype)
    o = jnp.einsum('bhqk,bkhd->bqhd', p, v)
    return o.reshape(B, T, MLA_HEADS * V_DIM) @ w_out, ckv_n, kpe


def _mla_latent(h, ckv_ctx, kpe_ctx, w_in, q_norm_g, w_uq, kv_norm_g, w_ukv, w_out):
    B, T, _ = h.shape
    q_nope, q_pe, ckv_n, kpe = _mla_project(h, w_in, q_norm_g, w_uq, kv_norm_g)
    ang_r, ang_c = _axial_angles(T)
    q_rot = _rope2d(q_pe, ang_r, ang_c)
    k_rot = _rope2d(kpe[:, :, None, :], ang_r, ang_c)[:, :, 0]
    k_nope, v = _mla_expand(ckv_n, w_ukv)
    k_nope_c, v_c = _mla_expand(ckv_ctx, w_ukv)
    nb = T // Q_BLOCK

    def blocks(a):
        return a.reshape(B, nb, Q_BLOCK, *a.shape[2:]).swapaxes(0, 1)

    def attend(blk):
        qn, qr, qp = blk
        s_lat = (jnp.einsum('bqhd,bkhd->bhqk', qn, k_nope, preferred_element_type=F32)
                 + jnp.einsum('bqhr,bkr->bhqk', qr, k_rot, preferred_element_type=F32))
        s_ctx = (jnp.einsum('bqhd,bkhd->bhqk', qn, k_nope_c, preferred_element_type=F32)
                 + jnp.einsum('bqhr,bkr->bhqk', qp, kpe_ctx, preferred_element_type=F32))
        p = jax.nn.softmax(jnp.concatenate([s_lat, s_ctx], axis=-1) * MLA_SCALE, axis=-1).astype(v.dtype)
        return (jnp.einsum('bhqk,bkhd->bqhd', p[..., :T], v)
                + jnp.einsum('bhqk,bkhd->bqhd', p[..., T:], v_c))

    o = lax.map(attend, (blocks(q_nope), blocks(q_rot), blocks(q_pe)))
    return o.swapaxes(0, 1).reshape(B, T, MLA_HEADS * V_DIM) @ w_out


def setup_inputs(seed: int = 0) -> dict:
    key = jax.random.key(seed)
    ks = iter(jax.random.split(key, 32))

    def nrm(shape, scale):
        return jax.random.normal(next(ks), shape, F32) * scale

    def gain(shape):
        return 1.0 + nrm(shape, 0.02)

    return {
        'x_prompt': nrm((BATCH, SEQ, D_MODEL), 1.0),
        'x_sample': nrm((DEC_BATCH, DEC_SEQ, D_MODEL), 1.0),
        'state_hgrn_fwd': nrm((DEC_BATCH, N_EVEN, A_HEADS, HEAD_DIM, HEAD_DIM), 0.5),
        'state_hgrn_bwd': nrm((DEC_BATCH, N_EVEN, A_HEADS, HEAD_DIM, HEAD_DIM), 0.5),
        'cache_na_k': nrm((DEC_BATCH, N_EVEN, PAST_LEN, B_HEADS, HEAD_DIM), 1.0),
        'cache_na_v': nrm((DEC_BATCH, N_EVEN, PAST_LEN, B_HEADS, HEAD_DIM), 1.0),
        'cache_mla_ckv': nrm((DEC_BATCH, N_ODD, PAST_LEN, KV_RANK), 1.0),
        'cache_mla_kpe': nrm((DEC_BATCH, N_ODD, PAST_LEN, QK_ROPE), 1.0),
        'c': nrm((DEC_BATCH, D_MODEL), 1.0),
        'c_ctx': nrm((D_MODEL,), 1.0),
        'ada_w': nrm((DEPTH, D_MODEL, 6 * D_MODEL), 0.5 * D_MODEL ** -0.5),
        'ada_b': nrm((DEPTH, 6 * D_MODEL), 0.02),
        'norm_g': gain((DEPTH, 4, D_MODEL)),
        'hgrn_lb_fwd': nrm((DEPTH + 1, A_WIDTH), 0.5),
        'hgrn_lb_bwd': nrm((DEPTH + 1, A_WIDTH), 0.5),
        'w_in_even': nrm((N_EVEN, D_MODEL, EVEN_IN), D_MODEL ** -0.5),
        'hgrn_norm_g': gain((N_EVEN, HEAD_DIM)),
        'na_rpb': nrm((N_EVEN, B_HEADS, 2 * NA_ROWS - 1, 2 * NA_COLS - 1), 0.1),
        'w_out_even': nrm((N_EVEN, D_MODEL, D_MODEL), D_MODEL ** -0.5),
        'w_in_odd': nrm((N_ODD, D_MODEL, ODD_IN), D_MODEL ** -0.5),
        'mla_q_norm_g': gain((N_ODD, Q_RANK)),
        'w_uq': nrm((N_ODD, Q_RANK, MLA_HEADS * (QK_NOPE + QK_ROPE)), Q_RANK ** -0.5),
        'mla_kv_norm_g': gain((N_ODD, KV_RANK)),
        'w_ukv': nrm((N_ODD, KV_RANK, MLA_HEADS * (QK_NOPE + V_DIM)), KV_RANK ** -0.5),
        'w_out_odd': nrm((N_ODD, MLA_HEADS * V_DIM, D_MODEL), (MLA_HEADS * V_DIM) ** -0.5),
        'mlp_w1': nrm((DEPTH, D_MODEL, D_FF), D_MODEL ** -0.5),
        'mlp_w2': nrm((DEPTH, D_FF, D_MODEL), D_FF ** -0.5),
    }


def reference(x_prompt, x_sample, state_hgrn_fwd, state_hgrn_bwd, cache_na_k, cache_na_v,
              cache_mla_ckv, cache_mla_kpe, c, c_ctx, ada_w, ada_b, norm_g, hgrn_lb_fwd,
              hgrn_lb_bwd, w_in_even, hgrn_norm_g, na_rpb, w_out_even, w_in_odd, mla_q_norm_g,
              w_uq, mla_kv_norm_g, w_ukv, w_out_odd, mlp_w1, mlp_w2):
    lb_fw = jnp.cumsum(jax.nn.softmax(hgrn_lb_fwd.astype(F32), axis=0), axis=0)
    lb_bw = jnp.cumsum(jax.nn.softmax(hgrn_lb_bwd.astype(F32), axis=0), axis=0)
    xc, xs = x_prompt, x_sample
    new_sf, new_sb, new_nk, new_nv, new_ckv, new_kpe = [], [], [], [], [], []
    for l in range(DEPTH):
        j = l // 2
        mod_c = _modulation(c_ctx[None, :], ada_w[l], ada_b[l])
        mod_s = _modulation(c, ada_w[l], ada_b[l])
        hc = _pre(xc, norm_g[l, 0], mod_c, 0)
        hs = _pre(xs, norm_g[l, 0], mod_s, 0)
        if l % 2 == 0:
            yc, sf, sb, nk, nv = _even_context(hc, lb_fw[l], lb_bw[l], w_in_even[j], hgrn_norm_g[j],
                                               w_out_even[j])
            ys = _even_latent(hs, state_hgrn_fwd[:, j], state_hgrn_bwd[:, j], cache_na_k[:, j],
                              cache_na_v[:, j], lb_fw[l], lb_bw[l], w_in_even[j], hgrn_norm_g[j],
                              na_rpb[j], w_out_even[j])
            new_sf.append(sf)
            new_sb.append(sb)
            new_nk.append(nk)
            new_nv.append(nv)
        else:
            yc, ckv, kpe = _mla_context(hc, w_in_odd[j], mla_q_norm_g[j], w_uq[j], mla_kv_norm_g[j],
                                        w_ukv[j], w_out_odd[j])
            ys = _mla_latent(hs, cache_mla_ckv[:, j], cache_mla_kpe[:, j], w_in_odd[j], mla_q_norm_g[j],
                             w_uq[j], mla_kv_norm_g[j], w_ukv[j], w_out_odd[j])
            new_ckv.append(ckv)
            new_kpe.append(kpe)
        xc = _post(xc, yc, norm_g[l, 1], mod_c, 0)
        xs = _post(xs, ys, norm_g[l, 1], mod_s, 0)
        xc = _post(xc, _mlp(_pre(xc, norm_g[l, 2], mod_c, 3), mlp_w1[l], mlp_w2[l]), norm_g[l, 3], mod_c, 3)
        xs = _post(xs, _mlp(_pre(xs, norm_g[l, 2], mod_s, 3), mlp_w1[l], mlp_w2[l]), norm_g[l, 3], mod_s, 3)
    return (xc, xs, jnp.stack(new_sf, axis=1), jnp.stack(new_sb, axis=1), jnp.stack(new_nk, axis=1),
            jnp.stack(new_nv, axis=1), jnp.stack(new_ckv, axis=1), jnp.stack(new_kpe, axis=1))
```

```python
import functools

import numpy as np
import jax
import jax.numpy as jnp
from jax import lax
from jax.experimental import pallas as pl
from jax.experimental.pallas import tpu as pltpu

F32 = jnp.float32
BF16 = jnp.bfloat16

GRID_W = 64
HEAD_DIM = 128
NA_ROWS = 8
NA_COLS = 16
QK_NOPE = 128
QK_ROPE = 64
V_DIM = 128
ROPE_BASE = 10000.0
EPS = 1e-6
NEG = -1e30

LANES = 128
VMEM_LIMIT = 48 * 1024 * 1024
MOD_ROWS = 16

NT = (((1,), (1,)), ((), ()))
TN = (((0,), (0,)), ((), ()))


def _cparams(*sem):
    return pltpu.CompilerParams(dimension_semantics=sem, vmem_limit_bytes=VMEM_LIMIT)


def _fit(n, pref):
    t = min(pref, n)
    while n % t:
        t //= 2
    return t


def _silu(x):
    return x * jax.nn.sigmoid(x)


def _rms(x, g):
    return x * lax.rsqrt(jnp.mean(x * x, axis=-1, keepdims=True) + EPS) * g


def _softmax_pv(scores, values):
    m = functools.reduce(jnp.maximum, [jnp.max(s, axis=-1, keepdims=True) for s in scores])
    ps = [jnp.exp(s - m) for s in scores]
    l = functools.reduce(jnp.add, [jnp.sum(p, axis=-1, keepdims=True) for p in ps])
    o = functools.reduce(jnp.add, [jnp.dot(p.astype(BF16), v, preferred_element_type=F32)
                                   for p, v in zip(ps, values)])
    return o / l


def _mod_kernel(c_ref, w_ref, b_ref, o_ref):
    s = _silu(c_ref[...]).astype(BF16)
    o_ref[...] = jnp.dot(s, w_ref[...].astype(BF16), preferred_element_type=F32) + b_ref[...]


def modulation(cvec, ada_w, ada_b, tn=512):
    L, D, N6 = ada_w.shape
    R = cvec.shape[0]
    tn = _fit(N6, tn)
    return pl.pallas_call(
        _mod_kernel,
        grid=(L, N6 // tn),
        in_specs=[pl.BlockSpec((R, D), lambda l, j: (0, 0)),
                  pl.BlockSpec((None, D, tn), lambda l, j: (l, 0, j)),
                  pl.BlockSpec((None, 1, tn), lambda l, j: (l, 0, j))],
        out_specs=pl.BlockSpec((None, R, tn), lambda l, j: (l, 0, j)),
        out_shape=jax.ShapeDtypeStruct((L, R, N6), F32),
        compiler_params=_cparams("arbitrary", "arbitrary"),
        name="modulation",
    )(cvec, ada_w, ada_b.reshape(L, 1, N6))


class Rows:
    def __init__(self, nc, nl, t):
        assert nc % t == 0, "latent sequences must start on a multiple of their length"
        self.nc, self.nl, self.t, self.n = nc, nl, t, nc + nl

    def tile(self, pref):
        return _fit(np.gcd(self.nc, self.t), pref)

    def mod_row(self, i, tm):
        ncb = self.nc // tm
        return jnp.where(i < ncb, 0, 1 + (i - ncb) // (self.t // tm))

    def pos_block(self, i, tm):
        ncb = self.nc // tm
        return jnp.where(i < ncb, self.t // tm, (i - ncb) % (self.t // tm))


def _mod_spec(rows, tm, k, D):
    return pl.BlockSpec((None, 1, D), lambda i, *_: (rows.mod_row(i, tm) * 6 + k, 0, 0))


def _vec_spec(D):
    return pl.BlockSpec((1, D), lambda i, *_: (0, 0))


def _pre_kernel(x_ref, g_ref, sh_ref, sc_ref, h_ref):
    h = _rms(x_ref[...], g_ref[...]) * (1.0 + sc_ref[...]) + sh_ref[...]
    h_ref[...] = h.astype(h_ref.dtype)


def prenorm(x, g, mod3, k, rows, tm=512):
    n, D = x.shape
    tm = rows.tile(tm)
    return pl.pallas_call(
        _pre_kernel,
        grid=(n // tm,),
        in_specs=[pl.BlockSpec((tm, D), lambda i: (i, 0)), _vec_spec(D),
                  _mod_spec(rows, tm, k, D), _mod_spec(rows, tm, k + 1, D)],
        out_specs=pl.BlockSpec((tm, D), lambda i: (i, 0)),
        out_shape=jax.ShapeDtypeStruct((n, D), BF16),
        compiler_params=_cparams("arbitrary"),
        name="prenorm",
    )(x, g.reshape(1, D), mod3, mod3)


def _post_kernel(*refs, with_pre):
    if with_pre:
        x_ref, y_ref, gp_ref, gate_ref, gn_ref, sh_ref, sc_ref, x1_ref, h_ref = refs
    else:
        x_ref, y_ref, gp_ref, gate_ref, x1_ref = refs
    x1 = x_ref[...] + gate_ref[...] * _rms(y_ref[...], gp_ref[...])
    x1_ref[...] = x1
    if with_pre:
        h = _rms(x1, gn_ref[...]) * (1.0 + sc_ref[...]) + sh_ref[...]
        h_ref[...] = h.astype(h_ref.dtype)


def postnorm(x, y, g_post, mod3, k, rows, pre=None, tm=512):
    n, D = x.shape
    tm = rows.tile(tm)
    row_spec = pl.BlockSpec((tm, D), lambda i: (i, 0))
    in_specs = [row_spec, row_spec, _vec_spec(D), _mod_spec(rows, tm, k + 2, D)]
    args = [x, y, g_post.reshape(1, D), mod3]
    out_specs = [row_spec]
    out_shape = [jax.ShapeDtypeStruct((n, D), F32)]
    if pre is not None:
        g_pre, mod3_pre, kp = pre
        in_specs += [_vec_spec(D), _mod_spec(rows, tm, kp, D), _mod_spec(rows, tm, kp + 1, D)]
        args += [g_pre.reshape(1, D), mod3_pre, mod3_pre]
        out_specs.append(row_spec)
        out_shape.append(jax.ShapeDtypeStruct((n, D), BF16))
    out = pl.pallas_call(
        functools.partial(_post_kernel, with_pre=pre is not None),
        grid=(n // tm,),
        in_specs=in_specs, out_specs=out_specs, out_shape=out_shape,
        compiler_params=_cparams("arbitrary"),
        name="postnorm",
    )(*args)
    return out if pre is not None else out[0]


def _mm_kernel(x_ref, w_ref, o_ref, *scratch, nk, act):
    def finish(acc):
        if act == "relu2":
            a = jnp.maximum(acc, 0.0)
            acc = a * a
        o_ref[...] = acc.astype(o_ref.dtype)

    if nk == 1:
        finish(jnp.dot(x_ref[...], w_ref[...], preferred_element_type=F32))
        return
    acc_ref, = scratch
    k = pl.program_id(2)

    @pl.when(k == 0)
    def _():
        acc_ref[...] = jnp.zeros_like(acc_ref)

    acc_ref[...] += jnp.dot(x_ref[...], w_ref[...], preferred_element_type=F32)

    @pl.when(k == nk - 1)
    def _():
        finish(acc_ref[...])


def matmul(x, w, *, tm, tn, tk=None, out_dtype=F32, act=None, name="matmul"):
    M, K = x.shape
    _, N = w.shape
    tm, tn, tk = _fit(M, tm), _fit(N, tn), _fit(K, tk or K)
    nk = K // tk
    assert M % tm == 0 and N % tn == 0 and K % tk == 0
    return pl.pallas_call(
        functools.partial(_mm_kernel, nk=nk, act=act),
        grid=(M // tm, N // tn, nk),
        in_specs=[pl.BlockSpec((tm, tk), lambda i, j, k: (i, k)),
                  pl.BlockSpec((tk, tn), lambda i, j, k: (k, j))],
        out_specs=pl.BlockSpec((tm, tn), lambda i, j, k: (i, j)),
        out_shape=jax.ShapeDtypeStruct((M, N), out_dtype),
        scratch_shapes=[pltpu.VMEM((tm, tn), F32)] if nk > 1 else [],
        compiler_params=_cparams("arbitrary", "arbitrary", "arbitrary"),
        name=name,
    )(x, w)


def _hgrn_kernel(*refs, T, C, layer, has_init, emit_state):
    it = iter(refs)
    q_ref, ff_ref, fb_ref, i_ref, g_ref, lbf_ref, lbb_ref, ng_ref = (next(it) for _ in range(8))
    s0f_ref, s0b_ref = (next(it), next(it)) if has_init else (None, None)
    o_ref = next(it)
    sf_ref, sb_ref = (next(it), next(it)) if emit_state else (None, None)
    ofw_ref = next(it)
    nch = T // C

    def lower_bound(lb_ref):
        z = lb_ref[...]
        e = jnp.exp(z - jnp.max(z, axis=0, keepdims=True))
        sm = e / jnp.sum(e, axis=0, keepdims=True)
        return jnp.sum(sm[:layer + 1], axis=0, keepdims=True)

    row = lax.broadcasted_iota(jnp.int32, (C, C), 0)
    col = lax.broadcasted_iota(jnp.int32, (C, C), 1)
    lower = row >= col
    upper = row <= col

    def cumsum(tri, x):
        t = tri.astype(BF16)
        hi = x.astype(BF16)
        r1 = x - hi.astype(F32)
        mid = r1.astype(BF16)
        lo = (r1 - mid.astype(F32)).astype(BF16)
        return (jnp.dot(t, hi, preferred_element_type=F32) + jnp.dot(t, mid, preferred_element_type=F32)
                + jnp.dot(t, lo, preferred_element_type=F32))

    def chunk(c, St, f_ref, lb, tri, total_row):
        r = pl.ds(pl.multiple_of(c * C, C), C)
        f = lb + (1.0 - lb) * jax.nn.sigmoid(f_ref[r, :])
        b = cumsum(tri, jnp.log(f))
        qe = (_silu(q_ref[r, :]) * HEAD_DIM ** -0.5 * jnp.exp(b)).astype(BF16)
        ke = (1.0 - f) * jnp.exp(-b)
        vb = i_ref[r, :].astype(BF16)
        a = lax.dot_general(qe, ke.astype(BF16), NT, preferred_element_type=F32)
        a = jnp.where(tri, a, 0.0).astype(BF16)
        o = (lax.dot_general(qe, St.astype(BF16), NT, preferred_element_type=F32)
             + jnp.dot(a, vb, preferred_element_type=F32))
        ebt = jnp.exp(b[total_row:total_row + 1, :])
        kd = (ke * ebt).astype(BF16)
        return o, St * ebt + lax.dot_general(vb, kd, TN, preferred_element_type=F32)

    lbf = lower_bound(lbf_ref)
    lbb = lower_bound(lbb_ref)
    zeros = jnp.zeros((HEAD_DIM, HEAD_DIM), F32)

    def fw_body(c, St):
        o, St = chunk(c, St, ff_ref, lbf, lower, C - 1)
        ofw_ref[pl.ds(pl.multiple_of(c * C, C), C), :] = o
        return St

    St_f = lax.fori_loop(0, nch, fw_body, s0f_ref[...].T if has_init else zeros)

    ng = ng_ref[...]

    def bw_body(j, St):
        c = nch - 1 - j
        o, St = chunk(c, St, fb_ref, lbb, upper, 0)
        r = pl.ds(pl.multiple_of(c * C, C), C)
        g = g_ref[r, :]
        o_ref[r, :] = (_rms(ofw_ref[r, :] + o, ng) * _silu(g)).astype(o_ref.dtype)
        return St

    St_b = lax.fori_loop(0, nch, bw_body, s0b_ref[...].T if has_init else zeros)
    if emit_state:
        sf_ref[...] = St_f.T
        sb_ref[...] = St_b.T


def hgrn(p, lb_f, lb_b, norm_g, *, nseq, T, row0, A_heads, layer, init=None, emit_state=False, C=64):
    rb0 = row0 // T
    H = A_heads

    def slab(k):
        return pl.BlockSpec((T, HEAD_DIM), lambda b, h: (rb0 + b, k * H + h))

    nl = lb_f.shape[0]
    in_specs = [slab(k) for k in range(5)] + [
        pl.BlockSpec((nl, HEAD_DIM), lambda b, h: (0, h)),
        pl.BlockSpec((nl, HEAD_DIM), lambda b, h: (0, h)),
        pl.BlockSpec((1, HEAD_DIM), lambda b, h: (0, 0))]
    args = [p] * 5 + [lb_f, lb_b, norm_g.reshape(1, HEAD_DIM)]
    state_spec = pl.BlockSpec((None, None, HEAD_DIM, HEAD_DIM), lambda b, h: (b, h, 0, 0))
    if init is not None:
        in_specs += [state_spec, state_spec]
        args += list(init)
    out_specs = [pl.BlockSpec((T, HEAD_DIM), lambda b, h: (b, h))]
    out_shape = [jax.ShapeDtypeStruct((nseq * T, H * HEAD_DIM), BF16)]
    if emit_state:
        out_specs += [state_spec, state_spec]
        out_shape += [jax.ShapeDtypeStruct((nseq, H, HEAD_DIM, HEAD_DIM), F32)] * 2
    out = pl.pallas_call(
        functools.partial(_hgrn_kernel, T=T, C=C, layer=layer, has_init=init is not None, emit_state=emit_state),
        grid=(nseq, H),
        in_specs=in_specs, out_specs=out_specs, out_shape=out_shape,
        scratch_shapes=[pltpu.VMEM((T, HEAD_DIM), F32)],
        compiler_params=_cparams("arbitrary", "arbitrary"),
        name="hgrn",
    )(*args)
    return out if emit_state else out[0]


def _dense_attn_kernel(q_ref, k_ref, v_ref, o_ref, *, H, scale):
    for h in range(H):
        sl = slice(h * HEAD_DIM, (h + 1) * HEAD_DIM)
        q = (q_ref[:, sl] * scale).astype(BF16)
        s = lax.dot_general(q, k_ref[:, sl].astype(BF16), NT, preferred_element_type=F32)
        o_ref[:, sl] = _softmax_pv([s], [v_ref[:, sl].astype(BF16)]).astype(o_ref.dtype)


def dense_attn(p, *, nseq, T, H, col0):
    W = H * HEAD_DIM
    cb = col0 // W
    return pl.pallas_call(
        functools.partial(_dense_attn_kernel, H=H, scale=HEAD_DIM ** -0.5),
        grid=(nseq,),
        in_specs=[pl.BlockSpec((T, W), lambda b, k=k: (b, cb + k)) for k in range(3)],
        out_specs=pl.BlockSpec((T, W), lambda b: (b, 0)),
        out_shape=jax.ShapeDtypeStruct((nseq * T, W), BF16),
        compiler_params=_cparams("arbitrary"),
        name="dense_attn",
    )(p, p, p)


NA_QROWS = 4
NA_KROWS = 12


def na_bias_tables(rpb, rows):
    W = GRID_W
    kr = min(NA_ROWS, rows)
    assert kr == NA_ROWS and rows >= NA_KROWS and rows % NA_QROWS == 0
    nblk = rows // NA_QROWS
    tables = []
    for blk in (0, 1, nblk - 1):
        r0 = blk * NA_QROWS
        u0 = min(max(r0 - NA_ROWS // 2, 0), rows - NA_KROWS)
        r = r0 + jnp.arange(NA_QROWS)
        ka = u0 + jnp.arange(NA_KROWS)
        start = jnp.clip(r - kr // 2, 0, rows - kr)
        row_ok = (ka[None, :] >= start[:, None]) & (ka[None, :] < start[:, None] + kr)
        ri = jnp.clip(ka[None, :] - r[:, None] + NA_ROWS - 1, 0, 2 * NA_ROWS - 2)
        col = jnp.arange(W)
        cs = jnp.clip(col - NA_COLS // 2, 0, W - NA_COLS)
        col_ok = (col[None, :] >= cs[:, None]) & (col[None, :] < cs[:, None] + NA_COLS)
        ci = jnp.clip(col[None, :] - col[:, None] + NA_COLS - 1, 0, 2 * NA_COLS - 2)
        vals = rpb.astype(F32)[:, ri[:, None, :, None], ci[None, :, None, :]]
        ok = row_ok[:, None, :, None] & col_ok[None, :, None, :]
        t = jnp.where(ok[None], vals, NEG)
        tables.append(t.reshape(rpb.shape[0], NA_QROWS * W, NA_KROWS * W))
    return jnp.stack(tables)


def _na_kernel(q_ref, k_ref, v_ref, kc_ref, vc_ref, bias_ref, o_ref, *, rows, scale):
    blk = pl.program_id(2)
    u0 = jnp.clip(blk * NA_QROWS - NA_ROWS // 2, 0, rows - NA_KROWS)
    band = pl.ds(pl.multiple_of(u0 * GRID_W, GRID_W), NA_KROWS * GRID_W)
    q = (q_ref[...] * scale).astype(BF16)
    s_lat = lax.dot_general(q, k_ref[band, :].astype(BF16), NT, preferred_element_type=F32) + bias_ref[...]
    s_ctx = lax.dot_general(q, kc_ref[...].astype(BF16), NT, preferred_element_type=F32)
    o = _softmax_pv([s_lat, s_ctx], [v_ref[band, :].astype(BF16), vc_ref[...].astype(BF16)])
    o_ref[...] = o.astype(o_ref.dtype)


def na_attn(p, kc, vc, bias, *, nseq, T, H, row0, col0):
    rows = T // GRID_W
    nblk = rows // NA_QROWS
    tq = NA_QROWS * GRID_W
    L = kc.shape[0] // nseq
    qb0, kb0, cb = row0 // tq, row0 // T, col0 // HEAD_DIM

    def kind(blk):
        return jnp.where(blk == 0, 0, jnp.where(blk == nblk - 1, 2, 1))

    return pl.pallas_call(
        functools.partial(_na_kernel, rows=rows, scale=HEAD_DIM ** -0.5),
        grid=(nseq, H, nblk),
        in_specs=[pl.BlockSpec((tq, HEAD_DIM), lambda b, h, i: (qb0 + b * nblk + i, cb + h)),
                  pl.BlockSpec((T, HEAD_DIM), lambda b, h, i: (kb0 + b, cb + H + h)),
                  pl.BlockSpec((T, HEAD_DIM), lambda b, h, i: (kb0 + b, cb + 2 * H + h)),
                  pl.BlockSpec((L, HEAD_DIM), lambda b, h, i: (b, h)),
                  pl.BlockSpec((L, HEAD_DIM), lambda b, h, i: (b, h)),
                  pl.BlockSpec((None, None, tq, NA_KROWS * GRID_W), lambda b, h, i: (kind(i), h, 0, 0))],
        out_specs=pl.BlockSpec((tq, HEAD_DIM), lambda b, h, i: (b * nblk + i, h)),
        out_shape=jax.ShapeDtypeStruct((nseq * T, H * HEAD_DIM), BF16),
        compiler_params=_cparams("arbitrary", "arbitrary", "arbitrary"),
        name="na_attn",
    )(p, p, p, kc, vc, bias)


ROPE_SWAP = np.concatenate([np.arange(16, 32), np.arange(0, 16), np.arange(48, 64), np.arange(32, 48)])


def rope_tables(T, tm):
    t = jnp.arange(T)
    half = QK_ROPE // 2
    inv = jnp.power(ROPE_BASE, -jnp.arange(0, half, 2, dtype=F32) / half)
    ang_r = (t // GRID_W).astype(F32)[:, None] * inv
    ang_c = (t % GRID_W).astype(F32)[:, None] * inv
    cos = jnp.concatenate([jnp.cos(ang_r), jnp.cos(ang_r), jnp.cos(ang_c), jnp.cos(ang_c)], axis=-1)
    sin = jnp.concatenate([-jnp.sin(ang_r), jnp.sin(ang_r), -jnp.sin(ang_c), jnp.sin(ang_c)], axis=-1)
    cos = jnp.concatenate([cos, jnp.ones((T, LANES - QK_ROPE), F32)], axis=-1)
    sin = jnp.concatenate([sin, jnp.zeros((T, LANES - QK_ROPE), F32)], axis=-1)
    cos = jnp.concatenate([cos, jnp.ones((tm, LANES), F32)], axis=0)
    sin = jnp.concatenate([sin, jnp.zeros((tm, LANES), F32)], axis=0)
    return cos, sin


def _mla_mid_kernel(pr_ref, qg_ref, kvg_ref, cos_ref, sin_ref, cq_ref, ckv32_ref, ckv16_ref, k2_ref, *, qr, kvr):
    cq_ref[...] = _rms(pr_ref[:, :qr], qg_ref[...]).astype(cq_ref.dtype)
    ckv = _rms(pr_ref[:, qr:qr + kvr], kvg_ref[...])
    ckv32_ref[...] = ckv
    ckv16_ref[...] = ckv.astype(ckv16_ref.dtype)
    x = pr_ref[:, qr + kvr:]
    rot = x * cos_ref[...] + pltpu.roll(x, LANES // 2, axis=1) * sin_ref[...]
    lane = lax.broadcasted_iota(jnp.int32, rot.shape, 1)
    k2_ref[...] = jnp.where(lane < QK_ROPE, rot, 0.0).astype(k2_ref.dtype)


def mla_mid(pr, q_norm_g, kv_norm_g, cos, sin, rows, *, qr, kvr, tm):
    n = pr.shape[0]
    pos = pl.BlockSpec((tm, LANES), lambda i: (rows.pos_block(i, tm), 0))

    def out(w):
        return pl.BlockSpec((tm, w), lambda i: (i, 0))

    return pl.pallas_call(
        functools.partial(_mla_mid_kernel, qr=qr, kvr=kvr),
        grid=(n // tm,),
        in_specs=[pl.BlockSpec((tm, pr.shape[1]), lambda i: (i, 0)), _vec_spec(qr), _vec_spec(kvr), pos, pos],
        out_specs=[out(qr), out(kvr), out(kvr), out(LANES)],
        out_shape=[jax.ShapeDtypeStruct((n, qr), BF16), jax.ShapeDtypeStruct((n, kvr), F32),
                   jax.ShapeDtypeStruct((n, kvr), BF16), jax.ShapeDtypeStruct((n, LANES), BF16)],
        compiler_params=_cparams("arbitrary"),
        name="mla_mid",
    )(pr, q_norm_g.reshape(1, qr), kv_norm_g.reshape(1, kvr), cos, sin)


MLA_Q_IN = 3 * LANES
MLA_Q_OUT = 2 * LANES


def widen_w_uq(w_uq, heads):
    r = w_uq.shape[0]
    w = w_uq.reshape(r, heads, QK_NOPE + QK_ROPE)
    nope, pe = w[..., :QK_NOPE], w[..., QK_NOPE:]
    return jnp.concatenate([nope, pe, pe, pe[..., ROPE_SWAP], jnp.zeros_like(pe)], axis=-1).reshape(r, heads * MLA_Q_IN)


def _mla_q_kernel(x_ref, w_ref, cos_ref, sin_ref, o_ref, *, hpt, scale):
    acc = jnp.dot(x_ref[...], w_ref[...], preferred_element_type=F32)
    cos, sin = cos_ref[...], sin_ref[...]
    for j in range(hpt):
        a = acc[:, j * MLA_Q_IN:(j + 1) * MLA_Q_IN]
        o_ref[:, j * MLA_Q_OUT:j * MLA_Q_OUT + LANES] = (a[:, :LANES] * scale).astype(o_ref.dtype)
        q2 = a[:, LANES:2 * LANES] * cos + a[:, 2 * LANES:] * sin
        o_ref[:, j * MLA_Q_OUT + LANES:(j + 1) * MLA_Q_OUT] = (q2 * scale).astype(o_ref.dtype)


def mla_q(cq, w_uq_wide, cos, sin, rows, *, heads, scale, tm, hpt=4):
    n, r = cq.shape
    pos = pl.BlockSpec((tm, LANES), lambda i, j: (rows.pos_block(i, tm), 0))
    return pl.pallas_call(
        functools.partial(_mla_q_kernel, hpt=hpt, scale=scale),
        grid=(n // tm, heads // hpt),
        in_specs=[pl.BlockSpec((tm, r), lambda i, j: (i, 0)),
                  pl.BlockSpec((r, hpt * MLA_Q_IN), lambda i, j: (0, j)), pos, pos],
        out_specs=pl.BlockSpec((tm, hpt * MLA_Q_OUT), lambda i, j: (i, j)),
        out_shape=jax.ShapeDtypeStruct((n, heads * MLA_Q_OUT), BF16),
        compiler_params=_cparams("arbitrary", "arbitrary"),
        name="mla_q",
    )(cq, w_uq_wide, cos, sin)


def _mla_ctx_attn_kernel(q_ref, kv_ref, k2_ref, o_ref, *, heads):
    k2 = k2_ref[...]
    for h in range(heads):
        q = q_ref[:, h * MLA_Q_OUT:(h + 1) * MLA_Q_OUT]
        k = jnp.concatenate([kv_ref[:, 2 * h * LANES:(2 * h + 1) * LANES], k2], axis=1)
        s = lax.dot_general(q, k, NT, preferred_element_type=F32)
        v = kv_ref[:, (2 * h + 1) * LANES:(2 * h + 2) * LANES]
        o_ref[:, h * V_DIM:(h + 1) * V_DIM] = _softmax_pv([s], [v]).astype(o_ref.dtype)


def mla_ctx_attn(q, kv, k2, *, nseq, T, heads):
    return pl.pallas_call(
        functools.partial(_mla_ctx_attn_kernel, heads=heads),
        grid=(nseq,),
        in_specs=[pl.BlockSpec((T, heads * MLA_Q_OUT), lambda b: (b, 0)),
                  pl.BlockSpec((T, kv.shape[1]), lambda b: (b, 0)),
                  pl.BlockSpec((T, LANES), lambda b: (b, 0))],
        out_specs=pl.BlockSpec((T, heads * V_DIM), lambda b: (b, 0)),
        out_shape=jax.ShapeDtypeStruct((nseq * T, heads * V_DIM), BF16),
        compiler_params=_cparams("arbitrary"),
        name="mla_ctx_attn",
    )(q, kv, k2)


def _mla_lat_attn_kernel(q_ref, k1_ref, v_ref, k2_ref, k1c_ref, vc_ref, k2c_ref, o_ref):
    q = q_ref[...]
    s_lat = lax.dot_general(q, jnp.concatenate([k1_ref[...], k2_ref[...]], axis=1), NT, preferred_element_type=F32)
    s_ctx = lax.dot_general(q, jnp.concatenate([k1c_ref[...], k2c_ref[...]], axis=1), NT, preferred_element_type=F32)
    o_ref[...] = _softmax_pv([s_lat, s_ctx], [v_ref[...], vc_ref[...]]).astype(o_ref.dtype)


def mla_lat_attn(q, kv, k2, kvc, k2c, *, nseq, T, heads, row0, tq):
    nq = T // tq
    P = kvc.shape[0] // nseq
    qb0, kb0 = row0 // tq, row0 // T
    return pl.pallas_call(
        _mla_lat_attn_kernel,
        grid=(nseq, heads, nq),
        in_specs=[pl.BlockSpec((tq, MLA_Q_OUT), lambda b, h, i: (qb0 + b * nq + i, h)),
                  pl.BlockSpec((T, LANES), lambda b, h, i: (kb0 + b, 2 * h)),
                  pl.BlockSpec((T, LANES), lambda b, h, i: (kb0 + b, 2 * h + 1)),
                  pl.BlockSpec((T, LANES), lambda b, h, i: (kb0 + b, 0)),
                  pl.BlockSpec((P, LANES), lambda b, h, i: (b, 2 * h)),
                  pl.BlockSpec((P, LANES), lambda b, h, i: (b, 2 * h + 1)),
                  pl.BlockSpec((P, LANES), lambda b, h, i: (b, 0))],
        out_specs=pl.BlockSpec((tq, V_DIM), lambda b, h, i: (b * nq + i, h)),
        out_shape=jax.ShapeDtypeStruct((nseq * T, heads * V_DIM), BF16),
        compiler_params=_cparams("arbitrary", "arbitrary", "arbitrary"),
        name="mla_lat_attn",
    )(q, kv, kv, k2, kvc, kvc, k2c)


def even_layer(h, rows, B, SEQ, Bd, state_f, state_b, cache_k, cache_v, lb_f, lb_b, w_in, hgrn_g, rpb, w_out, layer):
    D = h.shape[1]
    AW = D // 2
    AH = BH = AW // HEAD_DIM
    T = rows.t
    p = matmul(h, w_in.astype(BF16), tm=1024, tn=1024, out_dtype=F32, name="even_in_proj")
    oa_c, s_fw, s_bw = hgrn(p, lb_f, lb_b, hgrn_g, nseq=B, T=SEQ, row0=0, A_heads=AH, layer=layer, emit_state=True)
    oa_l = hgrn(p, lb_f, lb_b, hgrn_g, nseq=Bd, T=T, row0=rows.nc, A_heads=AH, layer=layer,
                init=(state_f, state_b))
    ob_c = dense_attn(p, nseq=B, T=SEQ, H=BH, col0=5 * AW)
    past = cache_k.shape[1]
    bias = na_bias_tables(rpb, T // GRID_W)
    ob_l = na_attn(p, cache_k.reshape(Bd * past, BH * HEAD_DIM), cache_v.reshape(Bd * past, BH * HEAD_DIM), bias,
                   nseq=Bd, T=T, H=BH, row0=rows.nc, col0=5 * AW)
    o = jnp.concatenate([jnp.concatenate([oa_c, ob_c], axis=1), jnp.concatenate([oa_l, ob_l], axis=1)], axis=0)
    y = matmul(o, w_out.astype(BF16), tm=1024, tn=1024, out_dtype=F32, name="even_out_proj")
    new_k = p[:rows.nc, 6 * AW:7 * AW].reshape(B, SEQ, BH, HEAD_DIM)
    new_v = p[:rows.nc, 7 * AW:8 * AW].reshape(B, SEQ, BH, HEAD_DIM)
    return y, s_fw, s_bw, new_k, new_v


def odd_layer(h, rows, B, SEQ, Bd, cache_ckv, cache_kpe, w_in, q_norm_g, w_uq, kv_norm_g, w_ukv, w_out):
    D = h.shape[1]
    heads = D // 128
    T = rows.t
    qr, kvr = w_uq.shape[0], w_ukv.shape[0]
    scale = (QK_NOPE + QK_ROPE) ** -0.5
    tm = rows.tile(512)
    w_in_wide = jnp.concatenate([w_in, w_in[:, qr + kvr + ROPE_SWAP]], axis=1).astype(BF16)
    pr = matmul(h, w_in_wide, tm=1024, tn=w_in_wide.shape[1], out_dtype=F32, name="odd_in_proj")
    cos, sin = rope_tables(T, tm)
    cq, ckv32, ckv16, k2 = mla_mid(pr, q_norm_g, kv_norm_g, cos, sin, rows, qr=qr, kvr=kvr, tm=tm)
    q = mla_q(cq, widen_w_uq(w_uq, heads).astype(BF16), cos, sin, rows, heads=heads, scale=scale, tm=tm)
    w_ukv16 = w_ukv.astype(BF16)
    kv = matmul(ckv16, w_ukv16, tm=1024, tn=1024, out_dtype=BF16, name="mla_kv")
    past = cache_ckv.shape[1]
    kvc = matmul(cache_ckv.reshape(Bd * past, kvr).astype(BF16), w_ukv16, tm=1024, tn=1024, out_dtype=BF16,
                 name="mla_kv_cache")
    k2c = jnp.concatenate([jnp.zeros((Bd * past, LANES - QK_ROPE), F32), cache_kpe.reshape(Bd * past, QK_ROPE)],
                          axis=1).astype(BF16)
    o_c = mla_ctx_attn(q, kv, k2, nseq=B, T=SEQ, heads=heads)
    o_l = mla_lat_attn(q, kv, k2, kvc, k2c, nseq=Bd, T=T, heads=heads, row0=rows.nc, tq=tm)
    y = matmul(jnp.concatenate([o_c, o_l], axis=0), w_out.astype(BF16), tm=1024, tn=1024, out_dtype=F32,
               name="odd_out_proj")
    new_ckv = ckv32[:rows.nc].reshape(B, SEQ, kvr)
    new_kpe = pr[:rows.nc, qr + kvr:qr + kvr + QK_ROPE].reshape(B, SEQ, QK_ROPE)
    return y, new_ckv, new_kpe


def kernel(x_prompt, x_sample, state_hgrn_fwd, state_hgrn_bwd, cache_na_k, cache_na_v, cache_mla_ckv, cache_mla_kpe, c, c_ctx, ada_w, ada_b, norm_g, hgrn_lb_fwd, hgrn_lb_bwd, w_in_even, hgrn_norm_g, na_rpb, w_out_even, w_in_odd, mla_q_norm_g, w_uq, mla_kv_norm_g, w_ukv, w_out_odd, mlp_w1, mlp_w2):
    B, SEQ, D = x_prompt.shape
    Bd, T, _ = x_sample.shape
    depth = ada_w.shape[0]
    rows = Rows(B * SEQ, Bd * T, T)
    x = jnp.concatenate([x_prompt.reshape(rows.nc, D), x_sample.reshape(rows.nl, D)], axis=0)
    cvec = jnp.zeros((MOD_ROWS, D), F32).at[0].set(c_ctx).at[1:1 + Bd].set(c)
    mod = modulation(cvec, ada_w, ada_b).reshape(depth, MOD_ROWS * 6, 1, D)

    new_sf, new_sb, new_nk, new_nv, new_ckv, new_kpe = [], [], [], [], [], []
    h = prenorm(x, norm_g[0, 0], mod[0], 0, rows)
    for l in range(depth):
        j = l // 2
        if l % 2 == 0:
            y, sf, sb, nk, nv = even_layer(h, rows, B, SEQ, Bd, state_hgrn_fwd[:, j], state_hgrn_bwd[:, j],
                                           cache_na_k[:, j], cache_na_v[:, j], hgrn_lb_fwd, hgrn_lb_bwd,
                                           w_in_even[j], hgrn_norm_g[j], na_rpb[j], w_out_even[j], l)
            new_sf.append(sf)
            new_sb.append(sb)
            new_nk.append(nk)
            new_nv.append(nv)
        else:
            y, ckv, kpe = odd_layer(h, rows, B, SEQ, Bd, cache_mla_ckv[:, j], cache_mla_kpe[:, j], w_in_odd[j],
                                    mla_q_norm_g[j], w_uq[j], mla_kv_norm_g[j], w_ukv[j], w_out_odd[j])
            new_ckv.append(ckv)
            new_kpe.append(kpe)
        x, h = postnorm(x, y, norm_g[l, 1], mod[l], 0, rows, pre=(norm_g[l, 2], mod[l], 3))
        a = matmul(h, mlp_w1[l].astype(BF16), tm=1024, tn=1024, out_dtype=BF16, act="relu2", name="mlp_up")
        y = matmul(a, mlp_w2[l].astype(BF16), tm=1024, tn=2048, tk=1024, out_dtype=F32, name="mlp_down")
        if l + 1 < depth:
            x, h = postnorm(x, y, norm_g[l, 3], mod[l], 3, rows, pre=(norm_g[l + 1, 0], mod[l + 1], 0))
        else:
            x = postnorm(x, y, norm_g[l, 3], mod[l], 3, rows)
    return (x[:rows.nc].reshape(B, SEQ, D), x[rows.nc:].reshape(Bd, T, D),
            jnp.stack(new_sf, axis=1), jnp.stack(new_sb, axis=1), jnp.stack(new_nk, axis=1),
            jnp.stack(new_nv, axis=1), jnp.stack(new_ckv, axis=1), jnp.stack(new_kpe, axis=1))
```

```python
import functools

import numpy as np
import jax
import jax.numpy as jnp
from jax import lax
from jax.experimental import pallas as pl
from jax.experimental.pallas import tpu as pltpu

F32 = jnp.float32
BF16 = jnp.bfloat16

GRID_W = 64
HEAD_DIM = 128
NA_ROWS = 8
NA_COLS = 16
QK_NOPE = 128
QK_ROPE = 64
V_DIM = 128
ROPE_BASE = 10000.0
EPS = 1e-6
NEG = -1e30

LANES = 128
VMEM_LIMIT = 48 * 1024 * 1024
MOD_ROWS = 16

NT = (((1,), (1,)), ((), ()))
TN = (((0,), (0,)), ((), ()))


def _cparams(*sem):
    return pltpu.CompilerParams(dimension_semantics=sem, vmem_limit_bytes=VMEM_LIMIT)


def _fit(n, pref):
    t = min(pref, n)
    while n % t:
        t //= 2
    return t


def _silu(x):
    return x * jax.nn.sigmoid(x)


def _rms(x, g):
    return x * lax.rsqrt(jnp.mean(x * x, axis=-1, keepdims=True) + EPS) * g


def _softmax_pv(scores, values):
    m = functools.reduce(jnp.maximum, [jnp.max(s, axis=-1, keepdims=True) for s in scores])
    ps = [jnp.exp(s - m) for s in scores]
    l = functools.reduce(jnp.add, [jnp.sum(p, axis=-1, keepdims=True) for p in ps])
    o = functools.reduce(jnp.add, [jnp.dot(p.astype(BF16), v, preferred_element_type=F32)
                                   for p, v in zip(ps, values)])
    return o / l


def _mod_kernel(c_ref, w_ref, b_ref, o_ref):
    s = _silu(c_ref[...]).astype(BF16)
    o_ref[...] = jnp.dot(s, w_ref[...].astype(BF16), preferred_element_type=F32) + b_ref[...]


def modulation(cvec, ada_w, ada_b, tn=512):
    L, D, N6 = ada_w.shape
    R = cvec.shape[0]
    tn = _fit(N6, tn)
    return pl.pallas_call(
        _mod_kernel,
        grid=(L, N6 // tn),
        in_specs=[pl.BlockSpec((R, D), lambda l, j: (0, 0)),
                  pl.BlockSpec((None, D, tn), lambda l, j: (l, 0, j)),
                  pl.BlockSpec((None, 1, tn), lambda l, j: (l, 0, j))],
        out_specs=pl.BlockSpec((None, R, tn), lambda l, j: (l, 0, j)),
        out_shape=jax.ShapeDtypeStruct((L, R, N6), F32),
        compiler_params=_cparams("arbitrary", "arbitrary"),
        name="modulation",
    )(cvec, ada_w, ada_b.reshape(L, 1, N6))


class Rows:
    def __init__(self, nc, nl, t):
        assert nc % t == 0, "latent sequences must start on a multiple of their length"
        self.nc, self.nl, self.t, self.n = nc, nl, t, nc + nl

    def tile(self, pref):
        return _fit(np.gcd(self.nc, self.t), pref)

    def mod_row(self, i, tm):
        ncb = self.nc // tm
        return jnp.where(i < ncb, 0, 1 + (i - ncb) // (self.t // tm))

    def pos_block(self, i, tm):
        ncb = self.nc // tm
        return jnp.where(i < ncb, self.t // tm, (i - ncb) % (self.t // tm))


def _mod_spec(rows, tm, k, D):
    return pl.BlockSpec((None, 1, D), lambda i, *_: (rows.mod_row(i, tm) * 6 + k, 0, 0))


def _vec_spec(D):
    return pl.BlockSpec((1, D), lambda i, *_: (0, 0))


def _pre_kernel(x_ref, g_ref, sh_ref, sc_ref, h_ref):
    h = _rms(x_ref[...], g_ref[...]) * (1.0 + sc_ref[...]) + sh_ref[...]
    h_ref[...] = h.astype(h_ref.dtype)


def prenorm(x, g, mod3, k, rows, tm=512):
    n, D = x.shape
    tm = rows.tile(tm)
    return pl.pallas_call(
        _pre_kernel,
        grid=(n // tm,),
        in_specs=[pl.BlockSpec((tm, D), lambda i: (i, 0)), _vec_spec(D),
                  _mod_spec(rows, tm, k, D), _mod_spec(rows, tm, k + 1, D)],
        out_specs=pl.BlockSpec((tm, D), lambda i: (i, 0)),
        out_shape=jax.ShapeDtypeStruct((n, D), BF16),
        compiler_params=_cparams("arbitrary"),
        name="prenorm",
    )(x, g.reshape(1, D), mod3, mod3)


def _mm_kernel(x_ref, w_ref, o_ref, *scratch, nk, act):
    def finish(acc):
        if act == "relu2":
            a = jnp.maximum(acc, 0.0)
            acc = a * a
        o_ref[...] = acc.astype(o_ref.dtype)

    if nk == 1:
        finish(jnp.dot(x_ref[...], w_ref[...], preferred_element_type=F32))
        return
    acc_ref, = scratch
    k = pl.program_id(2)

    @pl.when(k == 0)
    def _():
        acc_ref[...] = jnp.zeros_like(acc_ref)

    acc_ref[...] += jnp.dot(x_ref[...], w_ref[...], preferred_element_type=F32)

    @pl.when(k == nk - 1)
    def _():
        finish(acc_ref[...])


def matmul(x, w, *, tm, tn, tk=None, out_dtype=F32, act=None, name="matmul"):
    M, K = x.shape
    _, N = w.shape
    tm, tn, tk = _fit(M, tm), _fit(N, tn), _fit(K, tk or K)
    nk = K // tk
    assert M % tm == 0 and N % tn == 0 and K % tk == 0
    return pl.pallas_call(
        functools.partial(_mm_kernel, nk=nk, act=act),
        grid=(M // tm, N // tn, nk),
        in_specs=[pl.BlockSpec((tm, tk), lambda i, j, k: (i, k)),
                  pl.BlockSpec((tk, tn), lambda i, j, k: (k, j))],
        out_specs=pl.BlockSpec((tm, tn), lambda i, j, k: (i, j)),
        out_shape=jax.ShapeDtypeStruct((M, N), out_dtype),
        scratch_shapes=[pltpu.VMEM((tm, tn), F32)] if nk > 1 else [],
        compiler_params=_cparams("arbitrary", "arbitrary", "arbitrary"),
        name=name,
    )(x, w)


def _mm_post_kernel(*refs, nk, with_pre, sub):
    if with_pre:
        a_ref, w_ref, x_ref, gp_ref, gate_ref, gn_ref, sh_ref, sc_ref, x1_ref, h_ref, *scratch = refs
    else:
        a_ref, w_ref, x_ref, gp_ref, gate_ref, x1_ref, *scratch = refs
    tm = x_ref.shape[0]

    def finish(r, y):
        x1 = x_ref[r, :] + gate_ref[...] * _rms(y, gp_ref[...])
        x1_ref[r, :] = x1
        if with_pre:
            h_ref[r, :] = (_rms(x1, gn_ref[...]) * (1.0 + sc_ref[...]) + sh_ref[...]).astype(h_ref.dtype)

    blocks = [slice(s, s + sub) for s in range(0, tm, sub)]
    if nk == 1:
        for r in blocks:
            finish(r, jnp.dot(a_ref[r, :], w_ref[...], preferred_element_type=F32))
        return
    acc_ref, = scratch
    k = pl.program_id(1)

    @pl.when(k == 0)
    def _():
        acc_ref[...] = jnp.zeros_like(acc_ref)

    @pl.when(k < nk - 1)
    def _():
        acc_ref[...] += jnp.dot(a_ref[...], w_ref[...], preferred_element_type=F32)

    @pl.when(k == nk - 1)
    def _():
        for r in blocks:
            finish(r, acc_ref[r, :] + jnp.dot(a_ref[r, :], w_ref[...], preferred_element_type=F32))


def matmul_post(a, w, x, g_post, mod3, k, rows, pre=None, *, tm=512, tk=None, sub=256, name="matmul_post"):
    M, K = a.shape
    D = w.shape[1]
    tm = rows.tile(tm)
    tk = _fit(K, tk or K)
    nk = K // tk
    row_spec = pl.BlockSpec((tm, D), lambda i, kk: (i, 0))
    in_specs = [pl.BlockSpec((tm, tk), lambda i, kk: (i, kk)), pl.BlockSpec((tk, D), lambda i, kk: (kk, 0)),
                row_spec, _vec_spec(D), _mod_spec(rows, tm, k + 2, D)]
    args = [a, w, x, g_post.reshape(1, D), mod3]
    out_specs = [row_spec]
    out_shape = [jax.ShapeDtypeStruct((M, D), F32)]
    if pre is not None:
        g_pre, mod3_pre, kp = pre
        in_specs += [_vec_spec(D), _mod_spec(rows, tm, kp, D), _mod_spec(rows, tm, kp + 1, D)]
        args += [g_pre.reshape(1, D), mod3_pre, mod3_pre]
        out_specs.append(row_spec)
        out_shape.append(jax.ShapeDtypeStruct((M, D), BF16))
    out = pl.pallas_call(
        functools.partial(_mm_post_kernel, nk=nk, with_pre=pre is not None, sub=_fit(tm, sub)),
        grid=(M // tm, nk),
        in_specs=in_specs, out_specs=out_specs, out_shape=out_shape,
        scratch_shapes=[pltpu.VMEM((tm, D), F32)] if nk > 1 else [],
        compiler_params=_cparams("arbitrary", "arbitrary"),
        name=name,
    )(*args)
    return out if pre is not None else out[0]


def _hgrn_kernel(*refs, T, C, layer, has_init, emit_state, unroll):
    it = iter(refs)
    q_ref, ff_ref, fb_ref, i_ref, g_ref, lbf_ref, lbb_ref, ng_ref = (next(it) for _ in range(8))
    s0f_ref, s0b_ref = (next(it), next(it)) if has_init else (None, None)
    o_ref = next(it)
    sf_ref, sb_ref = (next(it), next(it)) if emit_state else (None, None)
    qs_ref, lff_ref, kf_ref, lfb_ref, kb_ref, ofw_ref, obw_ref = (next(it) for _ in range(7))
    nch = T // C
    mid = C // 2

    def lower_bound(lb_ref):
        z = lb_ref[...]
        e = jnp.exp(z - jnp.max(z, axis=0, keepdims=True))
        sm = e / jnp.sum(e, axis=0, keepdims=True)
        return jnp.sum(sm[:layer + 1], axis=0, keepdims=True)

    def gates(f_ref, lb_ref, lf_ref, k_ref):
        lb = lower_bound(lb_ref)
        f = lb + (1.0 - lb) * jax.nn.sigmoid(f_ref[...])
        lf_ref[...] = jnp.log(f)
        k_ref[...] = 1.0 - f

    gates(ff_ref, lbf_ref, lff_ref, kf_ref)
    gates(fb_ref, lbb_ref, lfb_ref, kb_ref)
    qs_ref[...] = _silu(q_ref[...]) * HEAD_DIM ** -0.5

    row = lax.broadcasted_iota(jnp.int32, (C, C), 0)
    col = lax.broadcasted_iota(jnp.int32, (C, C), 1)
    lower = row >= col
    upper = row <= col

    def cumsum(tri, x):
        t = tri.astype(BF16)
        hi = x.astype(BF16)
        r1 = x - hi.astype(F32)
        md = r1.astype(BF16)
        lo = (r1 - md.astype(F32)).astype(BF16)
        return (jnp.dot(t, hi, preferred_element_type=F32) + jnp.dot(t, md, preferred_element_type=F32)
                + jnp.dot(t, lo, preferred_element_type=F32))

    def chunk(c, St, lf_ref, k_ref, tri, total_row, out_ref):
        r = pl.ds(pl.multiple_of(c * C, C), C)
        b = cumsum(tri, lf_ref[r, :])
        m = b[mid:mid + 1, :]
        qm = qs_ref[r, :] * jnp.exp(b - m)
        km = k_ref[r, :] * jnp.exp(m - b)
        vb = i_ref[r, :].astype(BF16)
        a = lax.dot_general(qm.astype(BF16), km.astype(BF16), NT, preferred_element_type=F32)
        a = jnp.where(tri, a, 0.0).astype(BF16)
        qe = (qm * jnp.exp(m)).astype(BF16)
        out_ref[r, :] = (lax.dot_general(qe, St.astype(BF16), NT, preferred_element_type=F32)
                         + jnp.dot(a, vb, preferred_element_type=F32))
        bt = b[total_row:total_row + 1, :]
        kd = (km * jnp.exp(bt - m)).astype(BF16)
        return St * jnp.exp(bt) + lax.dot_general(vb, kd, TN, preferred_element_type=F32)

    def body(j, carry):
        St_f, St_b = carry
        St_f = chunk(j, St_f, lff_ref, kf_ref, lower, C - 1, ofw_ref)
        St_b = chunk(nch - 1 - j, St_b, lfb_ref, kb_ref, upper, 0, obw_ref)
        return St_f, St_b

    zeros = jnp.zeros((HEAD_DIM, HEAD_DIM), F32)
    init = (s0f_ref[...].T, s0b_ref[...].T) if has_init else (zeros, zeros)
    St_f, St_b = lax.fori_loop(0, nch, body, init, unroll=unroll)
    o_ref[...] = (_rms(ofw_ref[...] + obw_ref[...], ng_ref[...]) * _silu(g_ref[...])).astype(o_ref.dtype)
    if emit_state:
        sf_ref[...] = St_f.T
        sb_ref[...] = St_b.T


def hgrn(p, lb_f, lb_b, norm_g, *, nseq, T, row0, A_heads, layer, init=None, emit_state=False, C=64, unroll=2):
    rb0 = row0 // T
    H = A_heads

    def slab(k):
        return pl.BlockSpec((T, HEAD_DIM), lambda b, h: (rb0 + b, k * H + h))

    nl = lb_f.shape[0]
    in_specs = [slab(k) for k in range(5)] + [
        pl.BlockSpec((nl, HEAD_DIM), lambda b, h: (0, h)),
        pl.BlockSpec((nl, HEAD_DIM), lambda b, h: (0, h)),
        pl.BlockSpec((1, HEAD_DIM), lambda b, h: (0, 0))]
    args = [p] * 5 + [lb_f, lb_b, norm_g.reshape(1, HEAD_DIM)]
    state_spec = pl.BlockSpec((None, None, HEAD_DIM, HEAD_DIM), lambda b, h: (b, h, 0, 0))
    if init is not None:
        in_specs += [state_spec, state_spec]
        args += list(init)
    out_specs = [pl.BlockSpec((T, HEAD_DIM), lambda b, h: (b, h))]
    out_shape = [jax.ShapeDtypeStruct((nseq * T, H * HEAD_DIM), BF16)]
    if emit_state:
        out_specs += [state_spec, state_spec]
        out_shape += [jax.ShapeDtypeStruct((nseq, H, HEAD_DIM, HEAD_DIM), F32)] * 2
    out = pl.pallas_call(
        functools.partial(_hgrn_kernel, T=T, C=C, layer=layer, has_init=init is not None, emit_state=emit_state,
                          unroll=unroll),
        grid=(nseq, H),
        in_specs=in_specs, out_specs=out_specs, out_shape=out_shape,
        scratch_shapes=[pltpu.VMEM((T, HEAD_DIM), F32)] * 7,
        compiler_params=_cparams("arbitrary", "arbitrary"),
        name="hgrn",
    )(*args)
    return out if emit_state else out[0]


def _dense_attn_kernel(q_ref, k_ref, v_ref, o_ref, *, H, scale):
    for h in range(H):
        sl = slice(h * HEAD_DIM, (h + 1) * HEAD_DIM)
        q = (q_ref[:, sl] * scale).astype(BF16)
        s = lax.dot_general(q, k_ref[:, sl].astype(BF16), NT, preferred_element_type=F32)
        o_ref[:, sl] = _softmax_pv([s], [v_ref[:, sl].astype(BF16)]).astype(o_ref.dtype)


def dense_attn(p, *, nseq, T, H, col0):
    W = H * HEAD_DIM
    cb = col0 // W
    return pl.pallas_call(
        functools.partial(_dense_attn_kernel, H=H, scale=HEAD_DIM ** -0.5),
        grid=(nseq,),
        in_specs=[pl.BlockSpec((T, W), lambda b, k=k: (b, cb + k)) for k in range(3)],
        out_specs=pl.BlockSpec((T, W), lambda b: (b, 0)),
        out_shape=jax.ShapeDtypeStruct((nseq * T, W), BF16),
        compiler_params=_cparams("arbitrary"),
        name="dense_attn",
    )(p, p, p)


NA_QROWS = 4
NA_KROWS = 12


def na_bias_tables(rpb, rows):
    W = GRID_W
    kr = min(NA_ROWS, rows)
    assert kr == NA_ROWS and rows >= NA_KROWS and rows % NA_QROWS == 0
    nblk = rows // NA_QROWS
    col = np.arange(W)
    cs = np.clip(col - NA_COLS // 2, 0, W - NA_COLS)
    col_ok = (col[None, :] >= cs[:, None]) & (col[None, :] < cs[:, None] + NA_COLS)
    ci = np.clip(col[None, :] - col[:, None] + NA_COLS - 1, 0, 2 * NA_COLS - 2)
    col_sel = (ci[..., None] == np.arange(2 * NA_COLS - 1)).astype(np.float32)
    row_sel, ok = [], []
    for blk in (0, 1, nblk - 1):
        r0 = blk * NA_QROWS
        u0 = min(max(r0 - NA_ROWS // 2, 0), rows - NA_KROWS)
        r = r0 + np.arange(NA_QROWS)
        ka = u0 + np.arange(NA_KROWS)
        start = np.clip(r - kr // 2, 0, rows - kr)
        row_ok = (ka[None, :] >= start[:, None]) & (ka[None, :] < start[:, None] + kr)
        ri = np.clip(ka[None, :] - r[:, None] + NA_ROWS - 1, 0, 2 * NA_ROWS - 2)
        row_sel.append((ri[..., None] == np.arange(2 * NA_ROWS - 1)).astype(np.float32))
        ok.append(row_ok[:, None, :, None] & col_ok[None, :, None, :])
    vals = jnp.einsum("kqua,hac,xyc->khqxuy", np.stack(row_sel), rpb.astype(F32), col_sel,
                      precision=lax.Precision.HIGHEST)
    t = jnp.where(np.stack(ok)[:, None], vals, NEG)
    return t.reshape(3, rpb.shape[0], NA_QROWS * W, NA_KROWS * W)


def _na_kernel(q_ref, k_ref, v_ref, kc_ref, vc_ref, bias_ref, o_ref, *, rows, scale):
    blk = pl.program_id(2)
    u0 = jnp.clip(blk * NA_QROWS - NA_ROWS // 2, 0, rows - NA_KROWS)
    band = pl.ds(pl.multiple_of(u0 * GRID_W, GRID_W), NA_KROWS * GRID_W)
    q = (q_ref[...] * scale).astype(BF16)
    s_lat = lax.dot_general(q, k_ref[band, :].astype(BF16), NT, preferred_element_type=F32) + bias_ref[...]
    s_ctx = lax.dot_general(q, kc_ref[...].astype(BF16), NT, preferred_element_type=F32)
    o = _softmax_pv([s_lat, s_ctx], [v_ref[band, :].astype(BF16), vc_ref[...].astype(BF16)])
    o_ref[...] = o.astype(o_ref.dtype)


def na_attn(p, kc, vc, bias, *, nseq, T, H, row0, col0):
    rows = T // GRID_W
    nblk = rows // NA_QROWS
    tq = NA_QROWS * GRID_W
    L = kc.shape[0] // nseq
    qb0, kb0, cb = row0 // tq, row0 // T, col0 // HEAD_DIM

    def kind(blk):
        return jnp.where(blk == 0, 0, jnp.where(blk == nblk - 1, 2, 1))

    return pl.pallas_call(
        functools.partial(_na_kernel, rows=rows, scale=HEAD_DIM ** -0.5),
        grid=(nseq, H, nblk),
        in_specs=[pl.BlockSpec((tq, HEAD_DIM), lambda b, h, i: (qb0 + b * nblk + i, cb + h)),
                  pl.BlockSpec((T, HEAD_DIM), lambda b, h, i: (kb0 + b, cb + H + h)),
                  pl.BlockSpec((T, HEAD_DIM), lambda b, h, i: (kb0 + b, cb + 2 * H + h)),
                  pl.BlockSpec((L, HEAD_DIM), lambda b, h, i: (b, h)),
                  pl.BlockSpec((L, HEAD_DIM), lambda b, h, i: (b, h)),
                  pl.BlockSpec((None, None, tq, NA_KROWS * GRID_W), lambda b, h, i: (kind(i), h, 0, 0))],
        out_specs=pl.BlockSpec((tq, HEAD_DIM), lambda b, h, i: (b * nblk + i, h)),
        out_shape=jax.ShapeDtypeStruct((nseq * T, H * HEAD_DIM), BF16),
        compiler_params=_cparams("arbitrary", "arbitrary", "arbitrary"),
        name="na_attn",
    )(p, p, p, kc, vc, bias)


ROPE_SWAP = np.concatenate([np.arange(16, 32), np.arange(0, 16), np.arange(48, 64), np.arange(32, 48)])


def rope_tables(T, tm):
    t = jnp.arange(T)
    half = QK_ROPE // 2
    inv = jnp.power(ROPE_BASE, -jnp.arange(0, half, 2, dtype=F32) / half)
    ang_r = (t // GRID_W).astype(F32)[:, None] * inv
    ang_c = (t % GRID_W).astype(F32)[:, None] * inv
    cos = jnp.concatenate([jnp.cos(ang_r), jnp.cos(ang_r), jnp.cos(ang_c), jnp.cos(ang_c)], axis=-1)
    sin = jnp.concatenate([-jnp.sin(ang_r), jnp.sin(ang_r), -jnp.sin(ang_c), jnp.sin(ang_c)], axis=-1)
    cos = jnp.concatenate([cos, jnp.ones((T, LANES - QK_ROPE), F32)], axis=-1)
    sin = jnp.concatenate([sin, jnp.zeros((T, LANES - QK_ROPE), F32)], axis=-1)
    cos = jnp.concatenate([cos, jnp.ones((tm, LANES), F32)], axis=0)
    sin = jnp.concatenate([sin, jnp.zeros((tm, LANES), F32)], axis=0)
    return cos, sin


def _mla_mid_kernel(pr_ref, qg_ref, kvg_ref, cos_ref, sin_ref, cq_ref, ckv32_ref, ckv16_ref, k2_ref, *, qr, kvr):
    cq_ref[...] = _rms(pr_ref[:, :qr], qg_ref[...]).astype(cq_ref.dtype)
    ckv = _rms(pr_ref[:, qr:qr + kvr], kvg_ref[...])
    ckv32_ref[...] = ckv
    ckv16_ref[...] = ckv.astype(ckv16_ref.dtype)
    x = pr_ref[:, qr + kvr:]
    rot = x * cos_ref[...] + pltpu.roll(x, LANES // 2, axis=1) * sin_ref[...]
    lane = lax.broadcasted_iota(jnp.int32, rot.shape, 1)
    k2_ref[...] = jnp.where(lane < QK_ROPE, rot, 0.0).astype(k2_ref.dtype)


def mla_mid(pr, q_norm_g, kv_norm_g, cos, sin, rows, *, qr, kvr, tm):
    n = pr.shape[0]
    pos = pl.BlockSpec((tm, LANES), lambda i: (rows.pos_block(i, tm), 0))

    def out(w):
        return pl.BlockSpec((tm, w), lambda i: (i, 0))

    return pl.pallas_call(
        functools.partial(_mla_mid_kernel, qr=qr, kvr=kvr),
        grid=(n // tm,),
        in_specs=[pl.BlockSpec((tm, pr.shape[1]), lambda i: (i, 0)), _vec_spec(qr), _vec_spec(kvr), pos, pos],
        out_specs=[out(qr), out(kvr), out(kvr), out(LANES)],
        out_shape=[jax.ShapeDtypeStruct((n, qr), BF16), jax.ShapeDtypeStruct((n, kvr), F32),
                   jax.ShapeDtypeStruct((n, kvr), BF16), jax.ShapeDtypeStruct((n, LANES), BF16)],
        compiler_params=_cparams("arbitrary"),
        name="mla_mid",
    )(pr, q_norm_g.reshape(1, qr), kv_norm_g.reshape(1, kvr), cos, sin)


MLA_Q_IN = 3 * LANES
MLA_Q_OUT = 2 * LANES


def widen_w_uq(w_uq, heads):
    r = w_uq.shape[0]
    w = w_uq.reshape(r, heads, QK_NOPE + QK_ROPE)
    nope, pe = w[..., :QK_NOPE], w[..., QK_NOPE:]
    return jnp.concatenate([nope, pe, pe, pe[..., ROPE_SWAP], jnp.zeros_like(pe)], axis=-1).reshape(r, heads * MLA_Q_IN)


def _mla_q_kernel(x_ref, w_ref, cos_ref, sin_ref, o_ref, *, hpt, scale):
    acc = jnp.dot(x_ref[...], w_ref[...], preferred_element_type=F32)
    cos, sin = cos_ref[...], sin_ref[...]
    for j in range(hpt):
        a = acc[:, j * MLA_Q_IN:(j + 1) * MLA_Q_IN]
        o_ref[:, j * MLA_Q_OUT:j * MLA_Q_OUT + LANES] = (a[:, :LANES] * scale).astype(o_ref.dtype)
        q2 = a[:, LANES:2 * LANES] * cos + a[:, 2 * LANES:] * sin
        o_ref[:, j * MLA_Q_OUT + LANES:(j + 1) * MLA_Q_OUT] = (q2 * scale).astype(o_ref.dtype)


def mla_q(cq, w_uq_wide, cos, sin, rows, *, heads, scale, tm, hpt=4):
    n, r = cq.shape
    pos = pl.BlockSpec((tm, LANES), lambda i, j: (rows.pos_block(i, tm), 0))
    return pl.pallas_call(
        functools.partial(_mla_q_kernel, hpt=hpt, scale=scale),
        grid=(n // tm, heads // hpt),
        in_specs=[pl.BlockSpec((tm, r), lambda i, j: (i, 0)),
                  pl.BlockSpec((r, hpt * MLA_Q_IN), lambda i, j: (0, j)), pos, pos],
        out_specs=pl.BlockSpec((tm, hpt * MLA_Q_OUT), lambda i, j: (i, j)),
        out_shape=jax.ShapeDtypeStruct((n, heads * MLA_Q_OUT), BF16),
        compiler_params=_cparams("arbitrary", "arbitrary"),
        name="mla_q",
    )(cq, w_uq_wide, cos, sin)


def _mla_ctx_attn_kernel(q_ref, kv_ref, k2_ref, o_ref, *, heads):
    k2 = k2_ref[...]
    for h in range(heads):
        q = q_ref[:, h * MLA_Q_OUT:(h + 1) * MLA_Q_OUT]
        k = jnp.concatenate([kv_ref[:, 2 * h * LANES:(2 * h + 1) * LANES], k2], axis=1)
        s = lax.dot_general(q, k, NT, preferred_element_type=F32)
        v = kv_ref[:, (2 * h + 1) * LANES:(2 * h + 2) * LANES]
        o_ref[:, h * V_DIM:(h + 1) * V_DIM] = _softmax_pv([s], [v]).astype(o_ref.dtype)


def mla_ctx_attn(q, kv, k2, *, nseq, T, heads):
    return pl.pallas_call(
        functools.partial(_mla_ctx_attn_kernel, heads=heads),
        grid=(nseq,),
        in_specs=[pl.BlockSpec((T, heads * MLA_Q_OUT), lambda b: (b, 0)),
                  pl.BlockSpec((T, kv.shape[1]), lambda b: (b, 0)),
                  pl.BlockSpec((T, LANES), lambda b: (b, 0))],
        out_specs=pl.BlockSpec((T, heads * V_DIM), lambda b: (b, 0)),
        out_shape=jax.ShapeDtypeStruct((nseq * T, heads * V_DIM), BF16),
        compiler_params=_cparams("arbitrary"),
        name="mla_ctx_attn",
    )(q, kv, k2)


def _mla_lat_attn_kernel(q_ref, k1_ref, v_ref, k2_ref, k1c_ref, vc_ref, k2c_ref, o_ref, *, sub):
    k = jnp.concatenate([k1_ref[...], k2_ref[...]], axis=1)
    kc = jnp.concatenate([k1c_ref[...], k2c_ref[...]], axis=1)
    for i in range(q_ref.shape[0] // sub):
        r = slice(i * sub, (i + 1) * sub)
        q = q_ref[r, :]
        s_lat = lax.dot_general(q, k, NT, preferred_element_type=F32)
        s_ctx = lax.dot_general(q, kc, NT, preferred_element_type=F32)
        o_ref[r, :] = _softmax_pv([s_lat, s_ctx], [v_ref[...], vc_ref[...]]).astype(o_ref.dtype)


def mla_lat_attn(q, kv, k2, kvc, k2c, *, nseq, T, heads, row0, tq):
    nq = T // tq
    P = kvc.shape[0] // nseq
    qb0, kb0 = row0 // tq, row0 // T
    return pl.pallas_call(
        functools.partial(_mla_lat_attn_kernel, sub=_fit(tq, 128)),
        grid=(nseq, heads, nq),
        in_specs=[pl.BlockSpec((tq, MLA_Q_OUT), lambda b, h, i: (qb0 + b * nq + i, h)),
                  pl.BlockSpec((T, LANES), lambda b, h, i: (kb0 + b, 2 * h)),
                  pl.BlockSpec((T, LANES), lambda b, h, i: (kb0 + b, 2 * h + 1)),
                  pl.BlockSpec((T, LANES), lambda b, h, i: (kb0 + b, 0)),
                  pl.BlockSpec((P, LANES), lambda b, h, i: (b, 2 * h)),
                  pl.BlockSpec((P, LANES), lambda b, h, i: (b, 2 * h + 1)),
                  pl.BlockSpec((P, LANES), lambda b, h, i: (b, 0))],
        out_specs=pl.BlockSpec((tq, V_DIM), lambda b, h, i: (b * nq + i, h)),
        out_shape=jax.ShapeDtypeStruct((nseq * T, heads * V_DIM), BF16),
        compiler_params=_cparams("arbitrary", "arbitrary", "arbitrary"),
        name="mla_lat_attn",
    )(q, kv, kv, k2, kvc, kvc, k2c)


def even_layer(h, rows, B, SEQ, Bd, state_f, state_b, cache_k, cache_v, lb_f, lb_b, w_in, hgrn_g, rpb, layer):
    D = h.shape[1]
    AW = D // 2
    AH = BH = AW // HEAD_DIM
    T = rows.t
    p = matmul(h, w_in.astype(BF16), tm=1024, tn=1024, out_dtype=F32, name="even_in_proj")
    oa_c, s_fw, s_bw = hgrn(p, lb_f, lb_b, hgrn_g, nseq=B, T=SEQ, row0=0, A_heads=AH, layer=layer, emit_state=True)
    oa_l = hgrn(p, lb_f, lb_b, hgrn_g, nseq=Bd, T=T, row0=rows.nc, A_heads=AH, layer=layer,
                init=(state_f, state_b))
    ob_c = dense_attn(p, nseq=B, T=SEQ, H=BH, col0=5 * AW)
    past = cache_k.shape[1]
    bias = na_bias_tables(rpb, T // GRID_W)
    ob_l = na_attn(p, cache_k.reshape(Bd * past, BH * HEAD_DIM), cache_v.reshape(Bd * past, BH * HEAD_DIM), bias,
                   nseq=Bd, T=T, H=BH, row0=rows.nc, col0=5 * AW)
    o = jnp.concatenate([jnp.concatenate([oa_c, ob_c], axis=1), jnp.concatenate([oa_l, ob_l], axis=1)], axis=0)
    new_k = p[:rows.nc, 6 * AW:7 * AW].reshape(B, SEQ, BH, HEAD_DIM)
    new_v = p[:rows.nc, 7 * AW:8 * AW].reshape(B, SEQ, BH, HEAD_DIM)
    return o, s_fw, s_bw, new_k, new_v


def odd_layer(h, rows, B, SEQ, Bd, cache_ckv, cache_kpe, w_in, q_norm_g, w_uq, kv_norm_g, w_ukv):
    D = h.shape[1]
    heads = D // 128
    T = rows.t
    qr, kvr = w_uq.shape[0], w_ukv.shape[0]
    scale = (QK_NOPE + QK_ROPE) ** -0.5
    tm = rows.tile(512)
    w_in_wide = jnp.concatenate([w_in, w_in[:, qr + kvr + ROPE_SWAP]], axis=1).astype(BF16)
    pr = matmul(h, w_in_wide, tm=1024, tn=w_in_wide.shape[1], out_dtype=F32, name="odd_in_proj")
    cos, sin = rope_tables(T, tm)
    cq, ckv32, ckv16, k2 = mla_mid(pr, q_norm_g, kv_norm_g, cos, sin, rows, qr=qr, kvr=kvr, tm=tm)
    q = mla_q(cq, widen_w_uq(w_uq, heads).astype(BF16), cos, sin, rows, heads=heads, scale=scale, tm=tm)
    w_ukv16 = w_ukv.astype(BF16)
    kv = matmul(ckv16, w_ukv16, tm=1024, tn=1024, out_dtype=BF16, name="mla_kv")
    past = cache_ckv.shape[1]
    kvc = matmul(cache_ckv.reshape(Bd * past, kvr).astype(BF16), w_ukv16, tm=1024, tn=1024, out_dtype=BF16,
                 name="mla_kv_cache")
    k2c = jnp.concatenate([jnp.zeros((Bd * past, LANES - QK_ROPE), F32), cache_kpe.reshape(Bd * past, QK_ROPE)],
                          axis=1).astype(BF16)
    o_c = mla_ctx_attn(q, kv, k2, nseq=B, T=SEQ, heads=heads)
    o_l = mla_lat_attn(q, kv, k2, kvc, k2c, nseq=Bd, T=T, heads=heads, row0=rows.nc, tq=tm)
    new_ckv = ckv32[:rows.nc].reshape(B, SEQ, kvr)
    new_kpe = pr[:rows.nc, qr + kvr:qr + kvr + QK_ROPE].reshape(B, SEQ, QK_ROPE)
    return jnp.concatenate([o_c, o_l], axis=0), new_ckv, new_kpe


def kernel(x_prompt, x_sample, state_hgrn_fwd, state_hgrn_bwd, cache_na_k, cache_na_v, cache_mla_ckv, cache_mla_kpe, c, c_ctx, ada_w, ada_b, norm_g, hgrn_lb_fwd, hgrn_lb_bwd, w_in_even, hgrn_norm_g, na_rpb, w_out_even, w_in_odd, mla_q_norm_g, w_uq, mla_kv_norm_g, w_ukv, w_out_odd, mlp_w1, mlp_w2):
    B, SEQ, D = x_prompt.shape
    Bd, T, _ = x_sample.shape
    depth = ada_w.shape[0]
    rows = Rows(B * SEQ, Bd * T, T)
    x = jnp.concatenate([x_prompt.reshape(rows.nc, D), x_sample.reshape(rows.nl, D)], axis=0)
    cvec = jnp.zeros((MOD_ROWS, D), F32).at[0].set(c_ctx).at[1:1 + Bd].set(c)
    mod = modulation(cvec, ada_w, ada_b).reshape(depth, MOD_ROWS * 6, 1, D)

    new_sf, new_sb, new_nk, new_nv, new_ckv, new_kpe = [], [], [], [], [], []
    h = prenorm(x, norm_g[0, 0], mod[0], 0, rows)
    for l in range(depth):
        j = l // 2
        if l % 2 == 0:
            o, sf, sb, nk, nv = even_layer(h, rows, B, SEQ, Bd, state_hgrn_fwd[:, j], state_hgrn_bwd[:, j],
                                           cache_na_k[:, j], cache_na_v[:, j], hgrn_lb_fwd, hgrn_lb_bwd,
                                           w_in_even[j], hgrn_norm_g[j], na_rpb[j], l)
            w_out = w_out_even[j]
            new_sf.append(sf)
            new_sb.append(sb)
            new_nk.append(nk)
            new_nv.append(nv)
        else:
            o, ckv, kpe = odd_layer(h, rows, B, SEQ, Bd, cache_mla_ckv[:, j], cache_mla_kpe[:, j], w_in_odd[j],
                                    mla_q_norm_g[j], w_uq[j], mla_kv_norm_g[j], w_ukv[j])
            w_out = w_out_odd[j]
            new_ckv.append(ckv)
            new_kpe.append(kpe)
        x, h = matmul_post(o, w_out.astype(BF16), x, norm_g[l, 1], mod[l], 0, rows, pre=(norm_g[l, 2], mod[l], 3),
                           name="out_proj_post")
        a = matmul(h, mlp_w1[l].astype(BF16), tm=1024, tn=1024, out_dtype=BF16, act="relu2", name="mlp_up")
        pre = (norm_g[l + 1, 0], mod[l + 1], 0) if l + 1 < depth else None
        out = matmul_post(a, mlp_w2[l].astype(BF16), x, norm_g[l, 3], mod[l], 3, rows, pre=pre, tk=1024,
                          name="mlp_down_post")
        x, h = out if pre is not None else (out, None)
    return (x[:rows.nc].reshape(B, SEQ, D), x[rows.nc:].reshape(Bd, T, D),
            jnp.stack(new_sf, axis=1), jnp.stack(new_sb, axis=1), jnp.stack(new_nk, axis=1),
            jnp.stack(new_nv, axis=1), jnp.stack(new_ckv, axis=1), jnp.stack(new_kpe, axis=1))
```

```python
import functools

import numpy as np
import jax
import jax.numpy as jnp
from jax import lax
from jax.experimental import pallas as pl
from jax.experimental.pallas import tpu as pltpu

F32 = jnp.float32
BF16 = jnp.bfloat16

GRID_W = 64
HEAD_DIM = 128
NA_ROWS = 8
NA_COLS = 16
QK_NOPE = 128
QK_ROPE = 64
V_DIM = 128
ROPE_BASE = 10000.0
EPS = 1e-6
NEG = -1e30

LANES = 128
SUBLANES = 8
VMEM_LIMIT = 48 * 1024 * 1024
MOD_ROWS = 16

NT = (((1,), (1,)), ((), ()))
TN = (((0,), (0,)), ((), ()))


def _cparams(*sem):
    return pltpu.CompilerParams(dimension_semantics=sem, vmem_limit_bytes=VMEM_LIMIT)


def _fit(n, pref):
    t = min(pref, n)
    while n % t:
        t //= 2
    return t


def _silu(x):
    return x * jax.nn.sigmoid(x)


def _rms(x, g):
    return x * lax.rsqrt(jnp.mean(x * x, axis=-1, keepdims=True) + EPS) * g


def _softmax_pv(scores, values):
    m = functools.reduce(jnp.maximum, [jnp.max(s, axis=-1, keepdims=True) for s in scores])
    ps = [jnp.exp(s - m) for s in scores]
    l = functools.reduce(jnp.add, [jnp.sum(p, axis=-1, keepdims=True) for p in ps])
    o = functools.reduce(jnp.add, [jnp.dot(p.astype(BF16), v, preferred_element_type=F32)
                                   for p, v in zip(ps, values)])
    return o / l


def _mod_kernel(c_ref, w_ref, b_ref, o_ref):
    s = _silu(c_ref[...]).astype(BF16)
    o_ref[...] = jnp.dot(s, w_ref[...].astype(BF16), preferred_element_type=F32) + b_ref[...]


def modulation(cvec, ada_w, ada_b, tn=512):
    L, D, N6 = ada_w.shape
    R = cvec.shape[0]
    tn = _fit(N6, tn)
    return pl.pallas_call(
        _mod_kernel,
        grid=(L, N6 // tn),
        in_specs=[pl.BlockSpec((R, D), lambda l, j: (0, 0)),
                  pl.BlockSpec((None, D, tn), lambda l, j: (l, 0, j)),
                  pl.BlockSpec((None, 1, tn), lambda l, j: (l, 0, j))],
        out_specs=pl.BlockSpec((None, R, tn), lambda l, j: (l, 0, j)),
        out_shape=jax.ShapeDtypeStruct((L, R, N6), F32),
        compiler_params=_cparams("arbitrary", "arbitrary"),
        name="modulation",
    )(cvec, ada_w, ada_b.reshape(L, 1, N6))


class Rows:
    def __init__(self, nc, nl, t):
        assert nc % t == 0, "latent sequences must start on a multiple of their length"
        self.nc, self.nl, self.t, self.n = nc, nl, t, nc + nl

    def tile(self, pref):
        return _fit(np.gcd(self.nc, self.t), pref)

    def mod_row(self, i, tm):
        ncb = self.nc // tm
        return jnp.where(i < ncb, 0, 1 + (i - ncb) // (self.t // tm))

    def pos_block(self, i, tm):
        ncb = self.nc // tm
        return jnp.where(i < ncb, self.t // tm, (i - ncb) % (self.t // tm))


def _mod_spec(rows, tm, k, D):
    return pl.BlockSpec((None, 1, D), lambda i, *_: (rows.mod_row(i, tm) * 6 + k, 0, 0))


def _vec_spec(D):
    return pl.BlockSpec((1, D), lambda i, *_: (0, 0))


def _split_specs(rows, tm, D):
    ncb = rows.nc // tm
    return [pl.BlockSpec((tm, D), lambda i, *_: (jnp.minimum(i, ncb - 1), 0)),
            pl.BlockSpec((tm, D), lambda i, *_: (jnp.maximum(i - ncb, 0), 0))]


def _into(dst, in_specs, args):
    if dst is None:
        return in_specs, args, {}
    return in_specs + [pl.BlockSpec(memory_space=pl.ANY)], args + [dst], {len(args): 0}


def _without_ref(kernel, idx):
    if idx is None:
        return kernel
    return lambda *refs: kernel(*refs[:idx], *refs[idx + 1:])


def _pre_kernel(xc_ref, xs_ref, g_ref, sh_ref, sc_ref, h_ref, *, ncb):
    x = jnp.where(pl.program_id(0) < ncb, xc_ref[...], xs_ref[...])
    h = _rms(x, g_ref[...]) * (1.0 + sc_ref[...]) + sh_ref[...]
    h_ref[...] = h.astype(h_ref.dtype)


def prenorm(xc, xs, g, mod3, k, rows, tm=512):
    D = xc.shape[1]
    tm = rows.tile(tm)
    return pl.pallas_call(
        functools.partial(_pre_kernel, ncb=rows.nc // tm),
        grid=(rows.n // tm,),
        in_specs=_split_specs(rows, tm, D) + [_vec_spec(D), _mod_spec(rows, tm, k, D),
                                              _mod_spec(rows, tm, k + 1, D)],
        out_specs=pl.BlockSpec((tm, D), lambda i: (i, 0)),
        out_shape=jax.ShapeDtypeStruct((rows.n, D), BF16),
        compiler_params=_cparams("arbitrary"),
        name="prenorm",
    )(xc, xs, g.reshape(1, D), mod3, mod3)


def _mm_kernel(x_ref, w_ref, o_ref, *scratch, nk, act):
    def finish(acc):
        if act == "relu2":
            a = jnp.maximum(acc, 0.0)
            acc = a * a
        o_ref[...] = acc.astype(o_ref.dtype)

    if nk == 1:
        finish(jnp.dot(x_ref[...], w_ref[...], preferred_element_type=F32))
        return
    acc_ref, = scratch
    k = pl.program_id(2)

    @pl.when(k == 0)
    def _():
        acc_ref[...] = jnp.zeros_like(acc_ref)

    acc_ref[...] += jnp.dot(x_ref[...], w_ref[...], preferred_element_type=F32)

    @pl.when(k == nk - 1)
    def _():
        finish(acc_ref[...])


def matmul(x, w, *, tm, tn, tk=None, out_dtype=F32, act=None, name="matmul"):
    M, K = x.shape
    _, N = w.shape
    tm, tn, tk = _fit(M, tm), _fit(N, tn), _fit(K, tk or K)
    nk = K // tk
    assert M % tm == 0 and N % tn == 0 and K % tk == 0
    return pl.pallas_call(
        functools.partial(_mm_kernel, nk=nk, act=act),
        grid=(M // tm, N // tn, nk),
        in_specs=[pl.BlockSpec((tm, tk), lambda i, j, k: (i, k)),
                  pl.BlockSpec((tk, tn), lambda i, j, k: (k, j))],
        out_specs=pl.BlockSpec((tm, tn), lambda i, j, k: (i, j)),
        out_shape=jax.ShapeDtypeStruct((M, N), out_dtype),
        scratch_shapes=[pltpu.VMEM((tm, tn), F32)] if nk > 1 else [],
        compiler_params=_cparams("arbitrary", "arbitrary", "arbitrary"),
        name=name,
    )(x, w)


def _mm_post_kernel(*refs, nk, with_pre, sub, ncb, split_in, split_out):
    it = iter(refs)
    a_ref, w_ref = next(it), next(it)
    x_refs = [next(it) for _ in range(2 if split_in else 1)]
    gp_ref, gate_ref = next(it), next(it)
    gn_ref, sh_ref, sc_ref = (next(it), next(it), next(it)) if with_pre else (None, None, None)
    x1_refs = [next(it) for _ in range(2 if split_out else 1)]
    h_ref = next(it) if with_pre else None
    scratch = list(it)
    tm = a_ref.shape[0]
    is_ctx = pl.program_id(0) < ncb

    def finish(r, y):
        x = jnp.where(is_ctx, x_refs[0][r, :], x_refs[1][r, :]) if split_in else x_refs[0][r, :]
        x1 = x + gate_ref[...] * _rms(y, gp_ref[...])
        if split_out:
            @pl.when(is_ctx)
            def _():
                x1_refs[0][r, :] = x1

            @pl.when(jnp.logical_not(is_ctx))
            def _():
                x1_refs[1][r, :] = x1
        else:
            x1_refs[0][r, :] = x1
        if with_pre:
            h_ref[r, :] = (_rms(x1, gn_ref[...]) * (1.0 + sc_ref[...]) + sh_ref[...]).astype(h_ref.dtype)

    blocks = [slice(s, s + sub) for s in range(0, tm, sub)]
    if nk == 1:
        for r in blocks:
            finish(r, jnp.dot(a_ref[r, :], w_ref[...], preferred_element_type=F32))
        return
    acc_ref, = scratch
    k = pl.program_id(1)

    @pl.when(k == 0)
    def _():
        acc_ref[...] = jnp.zeros_like(acc_ref)

    @pl.when(k < nk - 1)
    def _():
        acc_ref[...] += jnp.dot(a_ref[...], w_ref[...], preferred_element_type=F32)

    @pl.when(k == nk - 1)
    def _():
        for r in blocks:
            finish(r, acc_ref[r, :] + jnp.dot(a_ref[r, :], w_ref[...], preferred_element_type=F32))


def matmul_post(a, w, x, g_post, mod3, k, rows, pre=None, *, split_out=False, tm=512, tk=None, sub=256,
                name="matmul_post"):
    M, K = a.shape
    D = w.shape[1]
    tm = rows.tile(tm)
    tk = _fit(K, tk or K)
    nk = K // tk
    row_spec = pl.BlockSpec((tm, D), lambda i, kk: (i, 0))
    split_in = isinstance(x, tuple)
    x_specs, xs = (_split_specs(rows, tm, D), list(x)) if split_in else ([row_spec], [x])
    in_specs = [pl.BlockSpec((tm, tk), lambda i, kk: (i, kk)), pl.BlockSpec((tk, D), lambda i, kk: (kk, 0)),
                *x_specs, _vec_spec(D), _mod_spec(rows, tm, k + 2, D)]
    args = [a, w, *xs, g_post.reshape(1, D), mod3]
    if split_out:
        out_specs = _split_specs(rows, tm, D)
        out_shape = [jax.ShapeDtypeStruct((rows.nc, D), F32), jax.ShapeDtypeStruct((rows.nl, D), F32)]
    else:
        out_specs = [row_spec]
        out_shape = [jax.ShapeDtypeStruct((M, D), F32)]
    if pre is not None:
        g_pre, mod3_pre, kp = pre
        in_specs += [_vec_spec(D), _mod_spec(rows, tm, kp, D), _mod_spec(rows, tm, kp + 1, D)]
        args += [g_pre.reshape(1, D), mod3_pre, mod3_pre]
        out_specs.append(row_spec)
        out_shape.append(jax.ShapeDtypeStruct((M, D), BF16))
    out = pl.pallas_call(
        functools.partial(_mm_post_kernel, nk=nk, with_pre=pre is not None, sub=_fit(tm, sub), ncb=rows.nc // tm,
                          split_in=split_in, split_out=split_out),
        grid=(M // tm, nk),
        in_specs=in_specs, out_specs=out_specs, out_shape=out_shape,
        scratch_shapes=[pltpu.VMEM((tm, D), F32)] if nk > 1 else [],
        compiler_params=_cparams("arbitrary", "arbitrary"),
        name=name,
    )(*args)
    return out if len(out) > 1 else out[0]


HGRN_CHUNK = 64
HGRN_BLOCK = 256


def _hgrn_kernel(*refs, T, hb, layer, has_init, emit_state, unroll_blocks, unroll_scan):
    it = iter(refs)
    q_ref, ff_ref, fb_ref, i_ref, g_ref, lbf_ref, lbb_ref, ng_ref = (next(it) for _ in range(8))
    s0_refs = (next(it), next(it)) if has_init else None
    o_ref = next(it)
    s_out_refs = (next(it), next(it)) if emit_state else None
    vb_ref = next(it)
    qe_refs, kd_refs, oi_refs, ebt_refs = ((next(it), next(it)) for _ in range(4))
    C, R = HGRN_CHUNK, min(HGRN_BLOCK, T)
    nch, nblk, cpb, mid = T // C, T // R, R // C, C // 2
    f_refs, lb_refs = (ff_ref, fb_ref), (lbf_ref, lbb_ref)
    total_row = (C - 1, 0)

    def lower_bound(lb_ref, sl):
        z = lb_ref[:, sl]
        e = jnp.exp(z - jnp.max(z, axis=0, keepdims=True))
        sm = e / jnp.sum(e, axis=0, keepdims=True)
        return jnp.sum(sm[:layer + 1], axis=0, keepdims=True)

    row = lax.broadcasted_iota(jnp.int32, (R, R), 0)
    col = lax.broadcasted_iota(jnp.int32, (R, R), 1)
    same_chunk = (row // C) == (col // C)
    tri = (same_chunk & (row >= col), same_chunk & (row <= col))

    def cumsum(mask, x):
        t = mask.astype(BF16)
        hi = x.astype(BF16)
        r1 = x - hi.astype(F32)
        md = r1.astype(BF16)
        lo = (r1 - md.astype(F32)).astype(BF16)
        return (jnp.dot(t, hi, preferred_element_type=F32) + jnp.dot(t, md, preferred_element_type=F32)
                + jnp.dot(t, lo, preferred_element_type=F32))

    heads = [slice(h * HEAD_DIM, (h + 1) * HEAD_DIM) for h in range(hb)]
    lbs = [[lower_bound(lb_refs[d], sl) for sl in heads] for d in range(2)]

    def block_pass(blk, carry):
        r = pl.ds(pl.multiple_of(blk * R, R), R)
        for h, sl in enumerate(heads):
            qs = _silu(q_ref[r, sl]) * HEAD_DIM ** -0.5
            vb = i_ref[r, sl].astype(BF16)
            vb_ref[r, sl] = vb
            for d in range(2):
                lb = lbs[d][h]
                f = lb + (1.0 - lb) * jax.nn.sigmoid(f_refs[d][r, sl])
                kk = 1.0 - f
                b = cumsum(tri[d], jnp.log(f))
                qm, km = [], []
                for cc in range(cpb):
                    cs = slice(cc * C, (cc + 1) * C)
                    bc = b[cs]
                    m = bc[mid:mid + 1]
                    bt = bc[total_row[d]:total_row[d] + 1]
                    qmc = qs[cs] * jnp.exp(bc - m)
                    kmc = kk[cs] * jnp.exp(m - bc)
                    rc = pl.ds(pl.multiple_of(blk * R + cc * C, C), C)
                    qe_refs[d][rc, sl] = (qmc * jnp.exp(m)).astype(BF16)
                    kd_refs[d][rc, sl] = (kmc * jnp.exp(bt - m)).astype(BF16)
                    slot = pl.ds(pl.multiple_of((blk * cpb + cc) * SUBLANES, SUBLANES), SUBLANES)
                    ebt_refs[d][slot, sl] = jnp.broadcast_to(jnp.exp(bt), (SUBLANES, HEAD_DIM))
                    qm.append(qmc.astype(BF16))
                    km.append(kmc.astype(BF16))
                a = lax.dot_general(jnp.concatenate(qm, axis=0), jnp.concatenate(km, axis=0), NT,
                                    preferred_element_type=F32)
                a = jnp.where(tri[d], a, 0.0).astype(BF16)
                oi_refs[d][r, sl] = jnp.dot(a, vb, preferred_element_type=F32)
        return carry

    lax.fori_loop(0, nblk, block_pass, 0, unroll=unroll_blocks)

    def scan_step(j, states):
        new = []
        for h, sl in enumerate(heads):
            for d in range(2):
                c = j if d == 0 else nch - 1 - j
                St = states[2 * h + d]
                r = pl.ds(pl.multiple_of(c * C, C), C)
                oi_refs[d][r, sl] += lax.dot_general(qe_refs[d][r, sl], St.astype(BF16), NT,
                                                     preferred_element_type=F32)
                ebt = ebt_refs[d][pl.ds(pl.multiple_of(c * SUBLANES, SUBLANES), SUBLANES), sl][:1]
                new.append(St * ebt
                           + lax.dot_general(vb_ref[r, sl], kd_refs[d][r, sl], TN, preferred_element_type=F32))
        return tuple(new)

    if has_init:
        init = tuple(s0_refs[d][h].T for h in range(hb) for d in range(2))
    else:
        init = tuple(jnp.zeros((HEAD_DIM, HEAD_DIM), F32) for _ in range(2 * hb))
    states = lax.fori_loop(0, nch, scan_step, init, unroll=unroll_scan)
    if emit_state:
        for h in range(hb):
            for d in range(2):
                s_out_refs[d][h] = states[2 * h + d].T

    def final_pass(blk, carry):
        r = pl.ds(pl.multiple_of(blk * R, R), R)
        for sl in heads:
            o = _rms(oi_refs[0][r, sl] + oi_refs[1][r, sl], ng_ref[...]) * _silu(g_ref[r, sl])
            o_ref[r, sl] = o.astype(o_ref.dtype)
        return carry

    lax.fori_loop(0, nblk, final_pass, 0)


def hgrn(p, lb_f, lb_b, norm_g, dst, *, dst_cols, nseq, T, row0, A_heads, layer, hb, init=None, emit_state=False,
         unroll_blocks=1, unroll_scan=2):
    assert T % min(HGRN_BLOCK, T) == 0 and A_heads % hb == 0
    rb0 = row0 // T
    H = A_heads
    W = hb * HEAD_DIM
    ng = H // hb

    def slab(k):
        return pl.BlockSpec((T, W), lambda b, h: (rb0 + b, k * ng + h))

    nl = lb_f.shape[0]
    in_specs = [slab(k) for k in range(5)] + [
        pl.BlockSpec((nl, W), lambda b, h: (0, h)),
        pl.BlockSpec((nl, W), lambda b, h: (0, h)),
        pl.BlockSpec((1, HEAD_DIM), lambda b, h: (0, 0))]
    args = [p] * 5 + [lb_f, lb_b, norm_g.reshape(1, HEAD_DIM)]
    state_spec = pl.BlockSpec((None, hb, HEAD_DIM, HEAD_DIM), lambda b, h: (b, h, 0, 0))
    if init is not None:
        in_specs += [state_spec, state_spec]
        args += list(init)
    in_specs, args, aliases = _into(dst, in_specs, args)
    out_specs = [pl.BlockSpec((T, W), lambda b, h: (rb0 + b, h))]
    out_shape = [jax.ShapeDtypeStruct((p.shape[0], dst_cols), BF16)]
    if emit_state:
        out_specs += [state_spec, state_spec]
        out_shape += [jax.ShapeDtypeStruct((nseq, H, HEAD_DIM, HEAD_DIM), F32)] * 2
    scratch = ([pltpu.VMEM((T, W), BF16)] * 5 + [pltpu.VMEM((T, W), F32)] * 2
               + [pltpu.VMEM((T // HGRN_CHUNK * SUBLANES, W), F32)] * 2)
    body = functools.partial(_hgrn_kernel, T=T, hb=hb, layer=layer, has_init=init is not None, emit_state=emit_state,
                             unroll_blocks=unroll_blocks, unroll_scan=unroll_scan)
    out = pl.pallas_call(
        _without_ref(body, len(args) - 1 if aliases else None),
        grid=(nseq, ng),
        in_specs=in_specs, out_specs=out_specs, out_shape=out_shape,
        input_output_aliases=aliases,
        scratch_shapes=scratch,
        compiler_params=_cparams("arbitrary", "arbitrary"),
        name="hgrn",
    )(*args)
    return out if emit_state else out[0]


def _dense_attn_kernel(q_ref, k_ref, v_ref, o_ref, *, H, scale):
    for h in range(H):
        sl = slice(h * HEAD_DIM, (h + 1) * HEAD_DIM)
        q = (q_ref[:, sl] * scale).astype(BF16)
        s = lax.dot_general(q, k_ref[:, sl].astype(BF16), NT, preferred_element_type=F32)
        o_ref[:, sl] = _softmax_pv([s], [v_ref[:, sl].astype(BF16)]).astype(o_ref.dtype)


def dense_attn(p, dst, *, nseq, T, H, col0, dst_col0):
    W = H * HEAD_DIM
    cb = col0 // W
    in_specs, args, aliases = _into(dst, [pl.BlockSpec((T, W), lambda b, k=k: (b, cb + k)) for k in range(3)],
                                    [p, p, p])
    return pl.pallas_call(
        _without_ref(functools.partial(_dense_attn_kernel, H=H, scale=HEAD_DIM ** -0.5), 3),
        grid=(nseq,),
        in_specs=in_specs,
        out_specs=pl.BlockSpec((T, W), lambda b: (b, dst_col0 // W)),
        out_shape=jax.ShapeDtypeStruct(dst.shape, dst.dtype),
        input_output_aliases=aliases,
        compiler_params=_cparams("arbitrary"),
        name="dense_attn",
    )(*args)


NA_QROWS = 4
NA_KROWS = 12


def na_bias_tables(rpb, rows):
    W = GRID_W
    kr = min(NA_ROWS, rows)
    assert kr == NA_ROWS and rows >= NA_KROWS and rows % NA_QROWS == 0
    nblk = rows // NA_QROWS
    col = np.arange(W)
    cs = np.clip(col - NA_COLS // 2, 0, W - NA_COLS)
    col_ok = (col[None, :] >= cs[:, None]) & (col[None, :] < cs[:, None] + NA_COLS)
    ci = np.clip(col[None, :] - col[:, None] + NA_COLS - 1, 0, 2 * NA_COLS - 2)
    col_sel = (ci[..., None] == np.arange(2 * NA_COLS - 1)).astype(np.float32)
    row_sel, ok = [], []
    for blk in (0, 1, nblk - 1):
        r0 = blk * NA_QROWS
        u0 = min(max(r0 - NA_ROWS // 2, 0), rows - NA_KROWS)
        r = r0 + np.arange(NA_QROWS)
        ka = u0 + np.arange(NA_KROWS)
        start = np.clip(r - kr // 2, 0, rows - kr)
        row_ok = (ka[None, :] >= start[:, None]) & (ka[None, :] < start[:, None] + kr)
        ri = np.clip(ka[None, :] - r[:, None] + NA_ROWS - 1, 0, 2 * NA_ROWS - 2)
        row_sel.append((ri[..., None] == np.arange(2 * NA_ROWS - 1)).astype(np.float32))
        ok.append(row_ok[:, None, :, None] & col_ok[None, :, None, :])
    vals = jnp.einsum("kqua,hac,xyc->khqxuy", np.stack(row_sel), rpb.astype(F32), col_sel,
                      precision=lax.Precision.HIGHEST)
    t = jnp.where(np.stack(ok)[:, None], vals, NEG)
    return t.reshape(3, rpb.shape[0], NA_QROWS * W, NA_KROWS * W)


def _na_kernel(q_ref, k_ref, v_ref, kc_ref, vc_ref, bias_ref, o_ref, *, rows, scale):
    blk = pl.program_id(2)
    u0 = jnp.clip(blk * NA_QROWS - NA_ROWS // 2, 0, rows - NA_KROWS)
    band = pl.ds(pl.multiple_of(u0 * GRID_W, GRID_W), NA_KROWS * GRID_W)
    q = (q_ref[...] * scale).astype(BF16)
    s_lat = lax.dot_general(q, k_ref[band, :].astype(BF16), NT, preferred_element_type=F32) + bias_ref[...]
    s_ctx = lax.dot_general(q, kc_ref[...].astype(BF16), NT, preferred_element_type=F32)
    o = _softmax_pv([s_lat, s_ctx], [v_ref[band, :].astype(BF16), vc_ref[...].astype(BF16)])
    o_ref[...] = o.astype(o_ref.dtype)


def na_attn(p, kc, vc, bias, dst, *, nseq, T, H, row0, col0, dst_col0):
    rows = T // GRID_W
    nblk = rows // NA_QROWS
    tq = NA_QROWS * GRID_W
    L = kc.shape[0] // nseq
    qb0, kb0, cb = row0 // tq, row0 // T, col0 // HEAD_DIM

    def kind(blk):
        return jnp.where(blk == 0, 0, jnp.where(blk == nblk - 1, 2, 1))

    in_specs, args, aliases = _into(dst, [
        pl.BlockSpec((tq, HEAD_DIM), lambda b, h, i: (qb0 + b * nblk + i, cb + h)),
        pl.BlockSpec((T, HEAD_DIM), lambda b, h, i: (kb0 + b, cb + H + h)),
        pl.BlockSpec((T, HEAD_DIM), lambda b, h, i: (kb0 + b, cb + 2 * H + h)),
        pl.BlockSpec((L, HEAD_DIM), lambda b, h, i: (b, h)),
        pl.BlockSpec((L, HEAD_DIM), lambda b, h, i: (b, h)),
        pl.BlockSpec((None, None, tq, NA_KROWS * GRID_W), lambda b, h, i: (kind(i), h, 0, 0))],
        [p, p, p, kc, vc, bias])
    return pl.pallas_call(
        _without_ref(functools.partial(_na_kernel, rows=rows, scale=HEAD_DIM ** -0.5), 6),
        grid=(nseq, H, nblk),
        in_specs=in_specs,
        out_specs=pl.BlockSpec((tq, HEAD_DIM), lambda b, h, i: (qb0 + b * nblk + i, dst_col0 // HEAD_DIM + h)),
        out_shape=jax.ShapeDtypeStruct(dst.shape, dst.dtype),
        input_output_aliases=aliases,
        compiler_params=_cparams("arbitrary", "arbitrary", "arbitrary"),
        name="na_attn",
    )(*args)


ROPE_SWAP = np.concatenate([np.arange(16, 32), np.arange(0, 16), np.arange(48, 64), np.arange(32, 48)])


def rope_tables(T, tm):
    t = jnp.arange(T)
    half = QK_ROPE // 2
    inv = jnp.power(ROPE_BASE, -jnp.arange(0, half, 2, dtype=F32) / half)
    ang_r = (t // GRID_W).astype(F32)[:, None] * inv
    ang_c = (t % GRID_W).astype(F32)[:, None] * inv
    cos = jnp.concatenate([jnp.cos(ang_r), jnp.cos(ang_r), jnp.cos(ang_c), jnp.cos(ang_c)], axis=-1)
    sin = jnp.concatenate([-jnp.sin(ang_r), jnp.sin(ang_r), -jnp.sin(ang_c), jnp.sin(ang_c)], axis=-1)
    cos = jnp.concatenate([cos, jnp.ones((T, LANES - QK_ROPE), F32)], axis=-1)
    sin = jnp.concatenate([sin, jnp.zeros((T, LANES - QK_ROPE), F32)], axis=-1)
    cos = jnp.concatenate([cos, jnp.ones((tm, LANES), F32)], axis=0)
    sin = jnp.concatenate([sin, jnp.zeros((tm, LANES), F32)], axis=0)
    return cos, sin


def _mla_mid_kernel(pr_ref, qg_ref, kvg_ref, cos_ref, sin_ref, cq_ref, ckv32_ref, ckv16_ref, k2_ref, *, qr, kvr):
    cq_ref[...] = _rms(pr_ref[:, :qr], qg_ref[...]).astype(cq_ref.dtype)
    ckv = _rms(pr_ref[:, qr:qr + kvr], kvg_ref[...])
    ckv32_ref[...] = ckv
    ckv16_ref[...] = ckv.astype(ckv16_ref.dtype)
    x = pr_ref[:, qr + kvr:]
    rot = x * cos_ref[...] + pltpu.roll(x, LANES // 2, axis=1) * sin_ref[...]
    lane = lax.broadcasted_iota(jnp.int32, rot.shape, 1)
    k2_ref[...] = jnp.where(lane < QK_ROPE, rot, 0.0).astype(k2_ref.dtype)


def mla_mid(pr, q_norm_g, kv_norm_g, cos, sin, rows, *, qr, kvr, tm):
    n = pr.shape[0]
    pos = pl.BlockSpec((tm, LANES), lambda i: (rows.pos_block(i, tm), 0))

    def out(w):
        return pl.BlockSpec((tm, w), lambda i: (i, 0))

    return pl.pallas_call(
        functools.partial(_mla_mid_kernel, qr=qr, kvr=kvr),
        grid=(n // tm,),
        in_specs=[pl.BlockSpec((tm, pr.shape[1]), lambda i: (i, 0)), _vec_spec(qr), _vec_spec(kvr), pos, pos],
        out_specs=[out(qr), out(kvr), out(kvr), out(LANES)],
        out_shape=[jax.ShapeDtypeStruct((n, qr), BF16), jax.ShapeDtypeStruct((n, kvr), F32),
                   jax.ShapeDtypeStruct((n, kvr), BF16), jax.ShapeDtypeStruct((n, LANES), BF16)],
        compiler_params=_cparams("arbitrary"),
        name="mla_mid",
    )(pr, q_norm_g.reshape(1, qr), kv_norm_g.reshape(1, kvr), cos, sin)


MLA_Q_IN = 3 * LANES
MLA_Q_OUT = 2 * LANES


def widen_w_uq(w_uq, heads):
    r = w_uq.shape[0]
    w = w_uq.reshape(r, heads, QK_NOPE + QK_ROPE)
    nope, pe = w[..., :QK_NOPE], w[..., QK_NOPE:]
    return jnp.concatenate([nope, pe, pe, pe[..., ROPE_SWAP], jnp.zeros_like(pe)], axis=-1).reshape(r, heads * MLA_Q_IN)


def _mla_q_kernel(x_ref, w_ref, cos_ref, sin_ref, o_ref, *, hpt, scale):
    acc = jnp.dot(x_ref[...], w_ref[...], preferred_element_type=F32)
    cos, sin = cos_ref[...], sin_ref[...]
    for j in range(hpt):
        a = acc[:, j * MLA_Q_IN:(j + 1) * MLA_Q_IN]
        o_ref[:, j * MLA_Q_OUT:j * MLA_Q_OUT + LANES] = (a[:, :LANES] * scale).astype(o_ref.dtype)
        q2 = a[:, LANES:2 * LANES] * cos + a[:, 2 * LANES:] * sin
        o_ref[:, j * MLA_Q_OUT + LANES:(j + 1) * MLA_Q_OUT] = (q2 * scale).astype(o_ref.dtype)


def mla_q(cq, w_uq_wide, cos, sin, rows, *, heads, scale, tm, hpt=4):
    n, r = cq.shape
    pos = pl.BlockSpec((tm, LANES), lambda i, j: (rows.pos_block(i, tm), 0))
    return pl.pallas_call(
        functools.partial(_mla_q_kernel, hpt=hpt, scale=scale),
        grid=(n // tm, heads // hpt),
        in_specs=[pl.BlockSpec((tm, r), lambda i, j: (i, 0)),
                  pl.BlockSpec((r, hpt * MLA_Q_IN), lambda i, j: (0, j)), pos, pos],
        out_specs=pl.BlockSpec((tm, hpt * MLA_Q_OUT), lambda i, j: (i, j)),
        out_shape=jax.ShapeDtypeStruct((n, heads * MLA_Q_OUT), BF16),
        compiler_params=_cparams("arbitrary", "arbitrary"),
        name="mla_q",
    )(cq, w_uq_wide, cos, sin)


def _mla_ctx_attn_kernel(q_ref, kv_ref, k2_ref, o_ref, *, heads):
    k2 = k2_ref[...]
    for h in range(heads):
        q = q_ref[:, h * MLA_Q_OUT:(h + 1) * MLA_Q_OUT]
        k = jnp.concatenate([kv_ref[:, 2 * h * LANES:(2 * h + 1) * LANES], k2], axis=1)
        s = lax.dot_general(q, k, NT, preferred_element_type=F32)
        v = kv_ref[:, (2 * h + 1) * LANES:(2 * h + 2) * LANES]
        o_ref[:, h * V_DIM:(h + 1) * V_DIM] = _softmax_pv([s], [v]).astype(o_ref.dtype)


def mla_ctx_attn(q, kv, k2, *, nseq, T, heads):
    return pl.pallas_call(
        functools.partial(_mla_ctx_attn_kernel, heads=heads),
        grid=(nseq,),
        in_specs=[pl.BlockSpec((T, heads * MLA_Q_OUT), lambda b: (b, 0)),
                  pl.BlockSpec((T, kv.shape[1]), lambda b: (b, 0)),
                  pl.BlockSpec((T, LANES), lambda b: (b, 0))],
        out_specs=pl.BlockSpec((T, heads * V_DIM), lambda b: (b, 0)),
        out_shape=jax.ShapeDtypeStruct((q.shape[0], heads * V_DIM), BF16),
        compiler_params=_cparams("arbitrary"),
        name="mla_ctx_attn",
    )(q, kv, k2)


def _mla_lat_attn_kernel(q_ref, k1_ref, v_ref, k2_ref, k1c_ref, vc_ref, k2c_ref, o_ref, *, sub):
    k = jnp.concatenate([k1_ref[...], k2_ref[...]], axis=1)
    kc = jnp.concatenate([k1c_ref[...], k2c_ref[...]], axis=1)
    for i in range(q_ref.shape[0] // sub):
        r = slice(i * sub, (i + 1) * sub)
        q = q_ref[r, :]
        s_lat = lax.dot_general(q, k, NT, preferred_element_type=F32)
        s_ctx = lax.dot_general(q, kc, NT, preferred_element_type=F32)
        o_ref[r, :] = _softmax_pv([s_lat, s_ctx], [v_ref[...], vc_ref[...]]).astype(o_ref.dtype)


def mla_lat_attn(q, kv, k2, kvc, k2c, dst, *, nseq, T, heads, row0, tq, sub):
    nq = T // tq
    P = kvc.shape[0] // nseq
    qb0, kb0 = row0 // tq, row0 // T
    in_specs, args, aliases = _into(dst, [
        pl.BlockSpec((tq, MLA_Q_OUT), lambda b, h, i: (qb0 + b * nq + i, h)),
        pl.BlockSpec((T, LANES), lambda b, h, i: (kb0 + b, 2 * h)),
        pl.BlockSpec((T, LANES), lambda b, h, i: (kb0 + b, 2 * h + 1)),
        pl.BlockSpec((T, LANES), lambda b, h, i: (kb0 + b, 0)),
        pl.BlockSpec((P, LANES), lambda b, h, i: (b, 2 * h)),
        pl.BlockSpec((P, LANES), lambda b, h, i: (b, 2 * h + 1)),
        pl.BlockSpec((P, LANES), lambda b, h, i: (b, 0))],
        [q, kv, kv, k2, kvc, kvc, k2c])
    return pl.pallas_call(
        _without_ref(functools.partial(_mla_lat_attn_kernel, sub=_fit(tq, sub)), 7),
        grid=(nseq, heads, nq),
        in_specs=in_specs,
        out_specs=pl.BlockSpec((tq, V_DIM), lambda b, h, i: (qb0 + b * nq + i, h)),
        out_shape=jax.ShapeDtypeStruct(dst.shape, dst.dtype),
        input_output_aliases=aliases,
        compiler_params=_cparams("arbitrary", "arbitrary", "arbitrary"),
        name="mla_lat_attn",
    )(*args)


def even_layer(h, rows, B, SEQ, Bd, state_f, state_b, cache_k, cache_v, lb_f, lb_b, w_in, hgrn_g, rpb, layer):
    D = h.shape[1]
    AW = D // 2
    AH = BH = AW // HEAD_DIM
    T = rows.t
    p = matmul(h, w_in.astype(BF16), tm=1024, tn=1024, out_dtype=F32, name="even_in_proj")
    o, s_fw, s_bw = hgrn(p, lb_f, lb_b, hgrn_g, None, dst_cols=D, nseq=B, T=SEQ, row0=0, A_heads=AH, layer=layer,
                         hb=_fit(AH, 4), emit_state=True)
    o = hgrn(p, lb_f, lb_b, hgrn_g, o, dst_cols=D, nseq=Bd, T=T, row0=rows.nc, A_heads=AH, layer=layer, hb=1,
             init=(state_f, state_b), unroll_blocks=2)
    o = dense_attn(p, o, nseq=B, T=SEQ, H=BH, col0=5 * AW, dst_col0=AW)
    past = cache_k.shape[1]
    bias = na_bias_tables(rpb, T // GRID_W)
    o = na_attn(p, cache_k.reshape(Bd * past, BH * HEAD_DIM), cache_v.reshape(Bd * past, BH * HEAD_DIM), bias, o,
                nseq=Bd, T=T, H=BH, row0=rows.nc, col0=5 * AW, dst_col0=AW)
    new_k = p[:rows.nc, 6 * AW:7 * AW].reshape(B, SEQ, BH, HEAD_DIM)
    new_v = p[:rows.nc, 7 * AW:8 * AW].reshape(B, SEQ, BH, HEAD_DIM)
    return o, s_fw, s_bw, new_k, new_v


def odd_layer(h, rows, B, SEQ, Bd, cache_ckv, cache_kpe, w_in, q_norm_g, w_uq, kv_norm_g, w_ukv):
    D = h.shape[1]
    heads = D // 128
    T = rows.t
    qr, kvr = w_uq.shape[0], w_ukv.shape[0]
    scale = (QK_NOPE + QK_ROPE) ** -0.5
    tm = rows.tile(512)
    w_in_wide = jnp.concatenate([w_in, w_in[:, qr + kvr + ROPE_SWAP]], axis=1).astype(BF16)
    pr = matmul(h, w_in_wide, tm=1024, tn=w_in_wide.shape[1], out_dtype=F32, name="odd_in_proj")
    cos, sin = rope_tables(T, tm)
    cq, ckv32, ckv16, k2 = mla_mid(pr, q_norm_g, kv_norm_g, cos, sin, rows, qr=qr, kvr=kvr, tm=tm)
    q = mla_q(cq, widen_w_uq(w_uq, heads).astype(BF16), cos, sin, rows, heads=heads, scale=scale, tm=tm)
    w_ukv16 = w_ukv.astype(BF16)
    kv = matmul(ckv16, w_ukv16, tm=1024, tn=1024, out_dtype=BF16, name="mla_kv")
    past = cache_ckv.shape[1]
    kvc = matmul(cache_ckv.reshape(Bd * past, kvr).astype(BF16), w_ukv16, tm=1024, tn=1024, out_dtype=BF16,
                 name="mla_kv_cache")
    k2c = jnp.concatenate([jnp.zeros((Bd * past, LANES - QK_ROPE), F32), cache_kpe.reshape(Bd * past, QK_ROPE)],
                          axis=1).astype(BF16)
    o = mla_ctx_attn(q, kv, k2, nseq=B, T=SEQ, heads=heads)
    o = mla_lat_attn(q, kv, k2, kvc, k2c, o, nseq=Bd, T=T, heads=heads, row0=rows.nc, tq=rows.tile(1024), sub=512)
    new_ckv = ckv32[:rows.nc].reshape(B, SEQ, kvr)
    new_kpe = pr[:rows.nc, qr + kvr:qr + kvr + QK_ROPE].reshape(B, SEQ, QK_ROPE)
    return o, new_ckv, new_kpe


def kernel(x_prompt, x_sample, state_hgrn_fwd, state_hgrn_bwd, cache_na_k, cache_na_v, cache_mla_ckv, cache_mla_kpe, c, c_ctx, ada_w, ada_b, norm_g, hgrn_lb_fwd, hgrn_lb_bwd, w_in_even, hgrn_norm_g, na_rpb, w_out_even, w_in_odd, mla_q_norm_g, w_uq, mla_kv_norm_g, w_ukv, w_out_odd, mlp_w1, mlp_w2):
    B, SEQ, D = x_prompt.shape
    Bd, T, _ = x_sample.shape
    depth = ada_w.shape[0]
    rows = Rows(B * SEQ, Bd * T, T)
    x = (x_prompt.reshape(rows.nc, D), x_sample.reshape(rows.nl, D))
    cvec = jnp.zeros((MOD_ROWS, D), F32).at[0].set(c_ctx).at[1:1 + Bd].set(c)
    mod = modulation(cvec, ada_w, ada_b).reshape(depth, MOD_ROWS * 6, 1, D)

    new_sf, new_sb, new_nk, new_nv, new_ckv, new_kpe = [], [], [], [], [], []
    h = prenorm(*x, norm_g[0, 0], mod[0], 0, rows)
    for l in range(depth):
        j = l // 2
        if l % 2 == 0:
            o, sf, sb, nk, nv = even_layer(h, rows, B, SEQ, Bd, state_hgrn_fwd[:, j], state_hgrn_bwd[:, j],
                                           cache_na_k[:, j], cache_na_v[:, j], hgrn_lb_fwd, hgrn_lb_bwd,
                                           w_in_even[j], hgrn_norm_g[j], na_rpb[j], l)
            w_out = w_out_even[j]
            new_sf.append(sf)
            new_sb.append(sb)
            new_nk.append(nk)
            new_nv.append(nv)
        else:
            o, ckv, kpe = odd_layer(h, rows, B, SEQ, Bd, cache_mla_ckv[:, j], cache_mla_kpe[:, j], w_in_odd[j],
                                    mla_q_norm_g[j], w_uq[j], mla_kv_norm_g[j], w_ukv[j])
            w_out = w_out_odd[j]
            new_ckv.append(ckv)
            new_kpe.append(kpe)
        x, h = matmul_post(o, w_out.astype(BF16), x, norm_g[l, 1], mod[l], 0, rows, pre=(norm_g[l, 2], mod[l], 3),
                           name="out_proj_post")
        a = matmul(h, mlp_w1[l].astype(BF16), tm=1024, tn=1024, out_dtype=BF16, act="relu2", name="mlp_up")
        if l + 1 < depth:
            x, h = matmul_post(a, mlp_w2[l].astype(BF16), x, norm_g[l, 3], mod[l], 3, rows,
                               pre=(norm_g[l + 1, 0], mod[l + 1], 0), tk=1024, name="mlp_down_post")
        else:
            x = matmul_post(a, mlp_w2[l].astype(BF16), x, norm_g[l, 3], mod[l], 3, rows, split_out=True, tk=1024,
                            name="mlp_down_post")
    return (x[0].reshape(B, SEQ, D), x[1].reshape(Bd, T, D),
            jnp.stack(new_sf, axis=1), jnp.stack(new_sb, axis=1), jnp.stack(new_nk, axis=1),
            jnp.stack(new_nv, axis=1), jnp.stack(new_ckv, axis=1), jnp.stack(new_kpe, axis=1))
```

```python
import functools

import numpy as np
import jax
import jax.numpy as jnp
from jax import lax
from jax.experimental import pallas as pl
from jax.experimental.pallas import tpu as pltpu

F32 = jnp.float32
BF16 = jnp.bfloat16

GRID_W = 64
HEAD_DIM = 128
NA_ROWS = 8
NA_COLS = 16
QK_NOPE = 128
QK_ROPE = 64
V_DIM = 128
ROPE_BASE = 10000.0
EPS = 1e-6
NEG = -1e30

LANES = 128
SUBLANES = 8
VMEM_LIMIT = 48 * 1024 * 1024
VMEM_LIMIT_BIG = 60 * 1024 * 1024
MOD_ROWS = 16

NT = (((1,), (1,)), ((), ()))
TN = (((0,), (0,)), ((), ()))


def _cparams(*sem, vmem_limit=VMEM_LIMIT):
    return pltpu.CompilerParams(dimension_semantics=sem, vmem_limit_bytes=vmem_limit)


def _fit(n, pref):
    t = min(pref, n)
    while n % t:
        t //= 2
    return t


def _silu(x):
    return x * jax.nn.sigmoid(x)


def _rms(x, g):
    return x * lax.rsqrt(jnp.mean(x * x, axis=-1, keepdims=True) + EPS) * g


def _softmax_pv(problems):
    ms = [functools.reduce(jnp.maximum, [jnp.max(s, axis=-1, keepdims=True) for s in scores])
          for scores, _ in problems]
    ps = [[jnp.exp(s - m) for s in scores] for (scores, _), m in zip(problems, ms)]
    ls = [functools.reduce(jnp.add, [jnp.sum(p, axis=-1, keepdims=True) for p in pp]) for pp in ps]
    os = [functools.reduce(jnp.add, [jnp.dot(p.astype(BF16), v, preferred_element_type=F32)
                                     for p, v in zip(pp, values)]) for pp, (_, values) in zip(ps, problems)]
    return [o / l for o, l in zip(os, ls)]


def _mod_kernel(c_ref, w_ref, b_ref, o_ref):
    s = _silu(c_ref[...]).astype(BF16)
    o_ref[...] = jnp.dot(s, w_ref[...].astype(BF16), preferred_element_type=F32) + b_ref[...]


def modulation(cvec, ada_w, ada_b, tn=512):
    L, D, N6 = ada_w.shape
    R = cvec.shape[0]
    tn = _fit(N6, tn)
    return pl.pallas_call(
        _mod_kernel,
        grid=(L, N6 // tn),
        in_specs=[pl.BlockSpec((R, D), lambda l, j: (0, 0)),
                  pl.BlockSpec((None, D, tn), lambda l, j: (l, 0, j)),
                  pl.BlockSpec((None, 1, tn), lambda l, j: (l, 0, j))],
        out_specs=pl.BlockSpec((None, R, tn), lambda l, j: (l, 0, j)),
        out_shape=jax.ShapeDtypeStruct((L, R, N6), F32),
        compiler_params=_cparams("arbitrary", "arbitrary"),
        name="modulation",
    )(cvec, ada_w, ada_b.reshape(L, 1, N6))


class Rows:
    def __init__(self, nc, nl, t):
        assert nc % t == 0, "latent sequences must start on a multiple of their length"
        self.nc, self.nl, self.t, self.n = nc, nl, t, nc + nl

    def tile(self, pref):
        return _fit(np.gcd(self.nc, self.t), pref)

    def mod_row(self, i, tm):
        ncb = self.nc // tm
        return jnp.where(i < ncb, 0, 1 + (i - ncb) // (self.t // tm))

    def pos_block(self, i, tm):
        ncb = self.nc // tm
        return jnp.where(i < ncb, self.t // tm, (i - ncb) % (self.t // tm))


def _mod_spec(rows, tm, k, D):
    return pl.BlockSpec((None, 1, D), lambda i, *_: (rows.mod_row(i, tm) * 6 + k, 0, 0))


def _vec_spec(D):
    return pl.BlockSpec((1, D), lambda i, *_: (0, 0))


def _split_specs(rows, tm, D):
    ncb = rows.nc // tm
    return [pl.BlockSpec((tm, D), lambda i, *_: (jnp.minimum(i, ncb - 1), 0)),
            pl.BlockSpec((tm, D), lambda i, *_: (jnp.maximum(i - ncb, 0), 0))]


def _into(dst, in_specs, args):
    if dst is None:
        return in_specs, args, {}
    return in_specs + [pl.BlockSpec(memory_space=pl.ANY)], args + [dst], {len(args): 0}


def _without_ref(kernel, idx):
    if idx is None:
        return kernel
    return lambda *refs: kernel(*refs[:idx], *refs[idx + 1:])


def _pre_kernel(xc_ref, xs_ref, g_ref, sh_ref, sc_ref, h_ref, *, ncb):
    x = jnp.where(pl.program_id(0) < ncb, xc_ref[...], xs_ref[...])
    h = _rms(x, g_ref[...]) * (1.0 + sc_ref[...]) + sh_ref[...]
    h_ref[...] = h.astype(h_ref.dtype)


def prenorm(xc, xs, g, mod3, k, rows, tm=512):
    D = xc.shape[1]
    tm = rows.tile(tm)
    return pl.pallas_call(
        functools.partial(_pre_kernel, ncb=rows.nc // tm),
        grid=(rows.n // tm,),
        in_specs=_split_specs(rows, tm, D) + [_vec_spec(D), _mod_spec(rows, tm, k, D),
                                              _mod_spec(rows, tm, k + 1, D)],
        out_specs=pl.BlockSpec((tm, D), lambda i: (i, 0)),
        out_shape=jax.ShapeDtypeStruct((rows.n, D), BF16),
        compiler_params=_cparams("arbitrary"),
        name="prenorm",
    )(xc, xs, g.reshape(1, D), mod3, mod3)


def _mm_kernel(x_ref, w_ref, o_ref, *scratch, nk, act):
    def finish(acc):
        if act == "relu2":
            a = jnp.maximum(acc, 0.0)
            acc = a * a
        o_ref[...] = acc.astype(o_ref.dtype)

    if nk == 1:
        finish(jnp.dot(x_ref[...], w_ref[...], preferred_element_type=F32))
        return
    acc_ref, = scratch
    k = pl.program_id(2)

    @pl.when(k == 0)
    def _():
        acc_ref[...] = jnp.zeros_like(acc_ref)

    acc_ref[...] += jnp.dot(x_ref[...], w_ref[...], preferred_element_type=F32)

    @pl.when(k == nk - 1)
    def _():
        finish(acc_ref[...])


def matmul(x, w, *, tm, tn, tk=None, out_dtype=F32, act=None, name="matmul"):
    M, K = x.shape
    _, N = w.shape
    tm, tn, tk = _fit(M, tm), _fit(N, tn), _fit(K, tk or K)
    nk = K // tk
    assert M % tm == 0 and N % tn == 0 and K % tk == 0
    return pl.pallas_call(
        functools.partial(_mm_kernel, nk=nk, act=act),
        grid=(M // tm, N // tn, nk),
        in_specs=[pl.BlockSpec((tm, tk), lambda i, j, k: (i, k)),
                  pl.BlockSpec((tk, tn), lambda i, j, k: (k, j))],
        out_specs=pl.BlockSpec((tm, tn), lambda i, j, k: (i, j)),
        out_shape=jax.ShapeDtypeStruct((M, N), out_dtype),
        scratch_shapes=[pltpu.VMEM((tm, tn), F32)] if nk > 1 else [],
        compiler_params=_cparams("arbitrary", "arbitrary", "arbitrary"),
        name=name,
    )(x, w)


def _mm_wcast_kernel(x_ref, w_ref, o_ref, wb_ref, *, act):
    @pl.when(pl.program_id(1) == 0)
    def _():
        wb_ref[...] = w_ref[...].astype(BF16)

    acc = jnp.dot(x_ref[...], wb_ref[...], preferred_element_type=F32)
    if act == "relu2":
        a = jnp.maximum(acc, 0.0)
        acc = a * a
    o_ref[...] = acc.astype(o_ref.dtype)


def matmul_wcast(x, w, *, tm, tn, out_dtype=F32, act=None, name="matmul_wcast"):
    M, K = x.shape
    _, N = w.shape
    tm, tn = _fit(M, tm), _fit(N, tn)
    return pl.pallas_call(
        functools.partial(_mm_wcast_kernel, act=act),
        grid=(N // tn, M // tm),
        in_specs=[pl.BlockSpec((tm, K), lambda j, i: (i, 0)),
                  pl.BlockSpec((K, tn), lambda j, i: (0, j))],
        out_specs=pl.BlockSpec((tm, tn), lambda j, i: (i, j)),
        out_shape=jax.ShapeDtypeStruct((M, N), out_dtype),
        scratch_shapes=[pltpu.VMEM((K, tn), BF16)],
        compiler_params=_cparams("arbitrary", "arbitrary"),
        name=name,
    )(x, w)


def _mm_post_kernel(*refs, nk, with_pre, sub, ncb, split_in, split_out):
    it = iter(refs)
    a_ref, w_ref = next(it), next(it)
    x_refs = [next(it) for _ in range(2 if split_in else 1)]
    gp_ref, gate_ref = next(it), next(it)
    gn_ref, sh_ref, sc_ref = (next(it), next(it), next(it)) if with_pre else (None, None, None)
    x1_refs = [next(it) for _ in range(2 if split_out else 1)]
    h_ref = next(it) if with_pre else None
    tm = a_ref.shape[0]
    is_ctx = pl.program_id(0) < ncb

    def each_out(fn):
        if split_out:
            pl.when(is_ctx)(lambda: fn(x1_refs[0]))
            pl.when(jnp.logical_not(is_ctx))(lambda: fn(x1_refs[1]))
        else:
            fn(x1_refs[0])

    def finish(x1_ref, r, y):
        x = jnp.where(is_ctx, x_refs[0][r, :], x_refs[1][r, :]) if split_in else x_refs[0][r, :]
        x1 = x + gate_ref[...] * _rms(y, gp_ref[...])
        x1_ref[r, :] = x1
        if with_pre:
            h_ref[r, :] = (_rms(x1, gn_ref[...]) * (1.0 + sc_ref[...]) + sh_ref[...]).astype(h_ref.dtype)

    def product(r):
        return jnp.dot(a_ref[r, :], w_ref[...], preferred_element_type=F32)

    blocks = [slice(s, s + sub) for s in range(0, tm, sub)]
    everything = slice(None)

    def single(x1_ref):
        for r in blocks:
            finish(x1_ref, r, product(r))

    def first(x1_ref):
        x1_ref[...] = product(everything)

    def accumulate(x1_ref):
        x1_ref[...] += product(everything)

    def last(x1_ref):
        for r in blocks:
            finish(x1_ref, r, x1_ref[r, :] + product(r))

    if nk == 1:
        each_out(single)
        return
    k = pl.program_id(1)
    pl.when(k == 0)(lambda: each_out(first))
    pl.when(jnp.logical_and(k > 0, k < nk - 1))(lambda: each_out(accumulate))
    pl.when(k == nk - 1)(lambda: each_out(last))


def matmul_post(a, w, x, g_post, mod3, k, rows, pre=None, *, split_out=False, tm=512, tk=None, sub=256,
                vmem_limit=VMEM_LIMIT, name="matmul_post"):
    M, K = a.shape
    D = w.shape[1]
    tm = rows.tile(tm)
    tk = _fit(K, tk or K)
    nk = K // tk
    row_spec = pl.BlockSpec((tm, D), lambda i, kk: (i, 0))
    split_in = isinstance(x, tuple)
    x_specs, xs = (_split_specs(rows, tm, D), list(x)) if split_in else ([row_spec], [x])
    in_specs = [pl.BlockSpec((tm, tk), lambda i, kk: (i, kk)), pl.BlockSpec((tk, D), lambda i, kk: (kk, 0)),
                *x_specs, _vec_spec(D), _mod_spec(rows, tm, k + 2, D)]
    args = [a, w, *xs, g_post.reshape(1, D), mod3]
    if split_out:
        out_specs = _split_specs(rows, tm, D)
        out_shape = [jax.ShapeDtypeStruct((rows.nc, D), F32), jax.ShapeDtypeStruct((rows.nl, D), F32)]
    else:
        out_specs = [row_spec]
        out_shape = [jax.ShapeDtypeStruct((M, D), F32)]
    if pre is not None:
        g_pre, mod3_pre, kp = pre
        in_specs += [_vec_spec(D), _mod_spec(rows, tm, kp, D), _mod_spec(rows, tm, kp + 1, D)]
        args += [g_pre.reshape(1, D), mod3_pre, mod3_pre]
        out_specs.append(row_spec)
        out_shape.append(jax.ShapeDtypeStruct((M, D), BF16))
    out = pl.pallas_call(
        functools.partial(_mm_post_kernel, nk=nk, with_pre=pre is not None, sub=_fit(tm, sub), ncb=rows.nc // tm,
                          split_in=split_in, split_out=split_out),
        grid=(M // tm, nk),
        in_specs=in_specs, out_specs=out_specs, out_shape=out_shape,
        compiler_params=_cparams("arbitrary", "arbitrary", vmem_limit=vmem_limit),
        name=name,
    )(*args)
    return out if len(out) > 1 else out[0]


HGRN_CHUNK = 64
HGRN_BLOCK = 256


def _hgrn_kernel(*refs, T, hb, layer, has_init, emit_state, unroll_blocks, unroll_scan):
    it = iter(refs)
    q_ref, ff_ref, fb_ref, i_ref, g_ref, lbf_ref, lbb_ref, ng_ref = (next(it) for _ in range(8))
    s0_refs = (next(it), next(it)) if has_init else None
    o_ref = next(it)
    s_out_refs = (next(it), next(it)) if emit_state else None
    qe_refs, ds_refs, oi_refs, ebt_refs = ((next(it), next(it)) for _ in range(4))
    C, R = HGRN_CHUNK, min(HGRN_BLOCK, T)
    nch, nblk, cpb, mid = T // C, T // R, R // C, C // 2
    f_refs, lb_refs = (ff_ref, fb_ref), (lbf_ref, lbb_ref)
    total_row = (C - 1, 0)

    def lower_bound(lb_ref, sl):
        z = lb_ref[:, sl]
        e = jnp.exp(z - jnp.max(z, axis=0, keepdims=True))
        sm = e / jnp.sum(e, axis=0, keepdims=True)
        return jnp.sum(sm[:layer + 1], axis=0, keepdims=True)

    row = lax.broadcasted_iota(jnp.int32, (R, R), 0)
    col = lax.broadcasted_iota(jnp.int32, (R, R), 1)
    same_chunk = (row // C) == (col // C)
    tri = (same_chunk & (row >= col), same_chunk & (row <= col))

    def cumsum(mask, x):
        hi = x.astype(BF16)
        r1 = x - hi.astype(F32)
        md = r1.astype(BF16)
        lo = (r1 - md.astype(F32)).astype(BF16)
        s = jnp.dot(mask.astype(BF16), jnp.concatenate([hi, md, lo], axis=1), preferred_element_type=F32)
        return s[:, :HEAD_DIM] + s[:, HEAD_DIM:2 * HEAD_DIM] + s[:, 2 * HEAD_DIM:]

    heads = [slice(h * HEAD_DIM, (h + 1) * HEAD_DIM) for h in range(hb)]
    lbs = [[lower_bound(lb_refs[d], sl) for sl in heads] for d in range(2)]
    chains = [(h, sl, d) for h, sl in enumerate(heads) for d in range(2)]

    def block_pass(blk, carry):
        r = pl.ds(pl.multiple_of(blk * R, R), R)
        qs, vb, kk, b, qm, km, kd, a = {}, {}, {}, {}, {}, {}, {}, {}
        for h, sl in enumerate(heads):
            qs[h] = _silu(q_ref[r, sl]) * HEAD_DIM ** -0.5
            vb[h] = i_ref[r, sl].astype(BF16)
        for h, sl, d in chains:
            lb = lbs[d][h]
            f = lb + (1.0 - lb) * jax.nn.sigmoid(f_refs[d][r, sl])
            kk[h, d] = 1.0 - f
            b[h, d] = cumsum(tri[d], jnp.log(f))
        for h, sl, d in chains:
            qmc, kmc, kd[h, d] = [], [], []
            for cc in range(cpb):
                cs = slice(cc * C, (cc + 1) * C)
                bc = b[h, d][cs]
                m = bc[mid:mid + 1]
                bt = bc[total_row[d]:total_row[d] + 1]
                qc = qs[h][cs] * jnp.exp(bc - m)
                kc = kk[h, d][cs] * jnp.exp(m - bc)
                rc = pl.ds(pl.multiple_of(blk * R + cc * C, C), C)
                qe_refs[d][rc, sl] = (qc * jnp.exp(m)).astype(BF16)
                kd[h, d].append((kc * jnp.exp(bt - m)).astype(BF16))
                slot = pl.ds(pl.multiple_of((blk * cpb + cc) * SUBLANES, SUBLANES), SUBLANES)
                ebt_refs[d][slot, sl] = jnp.broadcast_to(jnp.exp(bt), (SUBLANES, HEAD_DIM))
                qmc.append(qc.astype(BF16))
                kmc.append(kc.astype(BF16))
            qm[h, d] = jnp.concatenate(qmc, axis=0)
            km[h, d] = jnp.concatenate(kmc, axis=0)
        for h, sl, d in chains:
            s = lax.dot_general(qm[h, d], km[h, d], NT, preferred_element_type=F32)
            a[h, d] = jnp.where(tri[d], s, 0.0).astype(BF16)
        for h, sl, d in chains:
            oi_refs[d][r, sl] = jnp.dot(a[h, d], vb[h], preferred_element_type=F32)
        for h, sl, d in chains:
            for cc in range(cpb):
                rows_c = pl.ds(pl.multiple_of((blk * cpb + cc) * HEAD_DIM, HEAD_DIM), HEAD_DIM)
                ds_refs[d][rows_c, sl] = lax.dot_general(vb[h][cc * C:(cc + 1) * C], kd[h, d][cc], TN,
                                                         preferred_element_type=F32)
        return carry

    lax.fori_loop(0, nblk, block_pass, 0, unroll=unroll_blocks)

    def scan_step(j, states):
        cs = [j, nch - 1 - j]
        rs = [pl.ds(pl.multiple_of(c * C, C), C) for c in cs]
        slots = [pl.ds(pl.multiple_of(c * SUBLANES, SUBLANES), SUBLANES) for c in cs]
        grow = [pl.ds(pl.multiple_of(c * HEAD_DIM, HEAD_DIM), HEAD_DIM) for c in cs]
        new = []
        for h, sl, d in chains:
            St = states[2 * h + d]
            oi_refs[d][rs[d], sl] += lax.dot_general(qe_refs[d][rs[d], sl], St.astype(BF16), NT,
                                                     preferred_element_type=F32)
            new.append(St * ebt_refs[d][slots[d], sl][:1] + ds_refs[d][grow[d], sl])
        return tuple(new)

    if has_init:
        init = tuple(s0_refs[d][h].T for h in range(hb) for d in range(2))
    else:
        init = tuple(jnp.zeros((HEAD_DIM, HEAD_DIM), F32) for _ in range(2 * hb))
    states = lax.fori_loop(0, nch, scan_step, init, unroll=unroll_scan)
    if emit_state:
        for h in range(hb):
            for d in range(2):
                s_out_refs[d][h] = states[2 * h + d].T

    def final_pass(blk, carry):
        r = pl.ds(pl.multiple_of(blk * R, R), R)
        for sl in heads:
            o = _rms(oi_refs[0][r, sl] + oi_refs[1][r, sl], ng_ref[...]) * _silu(g_ref[r, sl])
            o_ref[r, sl] = o.astype(o_ref.dtype)
        return carry

    lax.fori_loop(0, nblk, final_pass, 0)


def hgrn(p, lb_f, lb_b, norm_g, dst, *, dst_cols, nseq, T, row0, A_heads, layer, hb, init=None, emit_state=False,
         unroll_blocks=1, unroll_scan=4):
    assert T % min(HGRN_BLOCK, T) == 0 and A_heads % hb == 0
    rb0 = row0 // T
    H = A_heads
    W = hb * HEAD_DIM
    ng = H // hb

    def slab(k):
        return pl.BlockSpec((T, W), lambda b, h: (rb0 + b, k * ng + h))

    nl = lb_f.shape[0]
    in_specs = [slab(k) for k in range(5)] + [
        pl.BlockSpec((nl, W), lambda b, h: (0, h)),
        pl.BlockSpec((nl, W), lambda b, h: (0, h)),
        pl.BlockSpec((1, HEAD_DIM), lambda b, h: (0, 0))]
    args = [p] * 5 + [lb_f, lb_b, norm_g.reshape(1, HEAD_DIM)]
    state_spec = pl.BlockSpec((None, hb, HEAD_DIM, HEAD_DIM), lambda b, h: (b, h, 0, 0))
    if init is not None:
        in_specs += [state_spec, state_spec]
        args += list(init)
    in_specs, args, aliases = _into(dst, in_specs, args)
    out_specs = [pl.BlockSpec((T, W), lambda b, h: (rb0 + b, h))]
    out_shape = [jax.ShapeDtypeStruct((p.shape[0], dst_cols), BF16)]
    if emit_state:
        out_specs += [state_spec, state_spec]
        out_shape += [jax.ShapeDtypeStruct((nseq, H, HEAD_DIM, HEAD_DIM), F32)] * 2
    nch = T // HGRN_CHUNK
    scratch = ([pltpu.VMEM((T, W), BF16)] * 2
               + [pltpu.VMEM((nch * HEAD_DIM, W), F32)] * 2
               + [pltpu.VMEM((T, W), F32)] * 2
               + [pltpu.VMEM((nch * SUBLANES, W), F32)] * 2)
    body = functools.partial(_hgrn_kernel, T=T, hb=hb, layer=layer, has_init=init is not None, emit_state=emit_state,
                             unroll_blocks=unroll_blocks, unroll_scan=unroll_scan)
    out = pl.pallas_call(
        _without_ref(body, len(args) - 1 if aliases else None),
        grid=(nseq, ng),
        in_specs=in_specs, out_specs=out_specs, out_shape=out_shape,
        input_output_aliases=aliases,
        scratch_shapes=scratch,
        compiler_params=_cparams("arbitrary", "arbitrary"),
        name="hgrn",
    )(*args)
    return out if emit_state else out[0]


def _dense_attn_kernel(q_ref, k_ref, v_ref, o_ref, *, H, scale):
    heads = [slice(h * HEAD_DIM, (h + 1) * HEAD_DIM) for h in range(H)]
    problems = []
    for sl in heads:
        q = (q_ref[:, sl] * scale).astype(BF16)
        s = lax.dot_general(q, k_ref[:, sl].astype(BF16), NT, preferred_element_type=F32)
        problems.append(([s], [v_ref[:, sl].astype(BF16)]))
    for sl, o in zip(heads, _softmax_pv(problems)):
        o_ref[:, sl] = o.astype(o_ref.dtype)


def dense_attn(p, dst, *, nseq, T, H, col0, dst_col0):
    W = H * HEAD_DIM
    cb = col0 // W
    in_specs, args, aliases = _into(dst, [pl.BlockSpec((T, W), lambda b, k=k: (b, cb + k)) for k in range(3)],
                                    [p, p, p])
    return pl.pallas_call(
        _without_ref(functools.partial(_dense_attn_kernel, H=H, scale=HEAD_DIM ** -0.5), 3),
        grid=(nseq,),
        in_specs=in_specs,
        out_specs=pl.BlockSpec((T, W), lambda b: (b, dst_col0 // W)),
        out_shape=jax.ShapeDtypeStruct(dst.shape, dst.dtype),
        input_output_aliases=aliases,
        compiler_params=_cparams("arbitrary"),
        name="dense_attn",
    )(*args)


NA_QROWS = 4
NA_KROWS = 12


def na_bias_tables(rpb, rows):
    W = GRID_W
    kr = min(NA_ROWS, rows)
    assert kr == NA_ROWS and rows >= NA_KROWS and rows % NA_QROWS == 0
    nblk = rows // NA_QROWS
    col = np.arange(W)
    cs = np.clip(col - NA_COLS // 2, 0, W - NA_COLS)
    col_ok = (col[None, :] >= cs[:, None]) & (col[None, :] < cs[:, None] + NA_COLS)
    ci = np.clip(col[None, :] - col[:, None] + NA_COLS - 1, 0, 2 * NA_COLS - 2)
    col_sel = (ci[..., None] == np.arange(2 * NA_COLS - 1)).astype(np.float32)
    row_sel, ok = [], []
    for blk in (0, 1, nblk - 1):
        r0 = blk * NA_QROWS
        u0 = min(max(r0 - NA_ROWS // 2, 0), rows - NA_KROWS)
        r = r0 + np.arange(NA_QROWS)
        ka = u0 + np.arange(NA_KROWS)
        start = np.clip(r - kr // 2, 0, rows - kr)
        row_ok = (ka[None, :] >= start[:, None]) & (ka[None, :] < start[:, None] + kr)
        ri = np.clip(ka[None, :] - r[:, None] + NA_ROWS - 1, 0, 2 * NA_ROWS - 2)
        row_sel.append((ri[..., None] == np.arange(2 * NA_ROWS - 1)).astype(np.float32))
        ok.append(row_ok[:, None, :, None] & col_ok[None, :, None, :])
    vals = jnp.einsum("kqua,hac,xyc->khqxuy", np.stack(row_sel), rpb.astype(F32), col_sel,
                      precision=lax.Precision.HIGHEST)
    t = jnp.where(np.stack(ok)[:, None], vals, NEG)
    return t.reshape(3, rpb.shape[0], NA_QROWS * W, NA_KROWS * W)


def _na_kernel(q_ref, k_ref, v_ref, kc_ref, vc_ref, bias_ref, o_ref, *, rows, scale, hp):
    blk = pl.program_id(2)
    u0 = jnp.clip(blk * NA_QROWS - NA_ROWS // 2, 0, rows - NA_KROWS)
    band = pl.ds(pl.multiple_of(u0 * GRID_W, GRID_W), NA_KROWS * GRID_W)
    heads = [slice(h * HEAD_DIM, (h + 1) * HEAD_DIM) for h in range(hp)]
    problems = []
    for h, sl in enumerate(heads):
        q = (q_ref[:, sl] * scale).astype(BF16)
        s_lat = lax.dot_general(q, k_ref[band, sl].astype(BF16), NT, preferred_element_type=F32) + bias_ref[h]
        s_ctx = lax.dot_general(q, kc_ref[:, sl].astype(BF16), NT, preferred_element_type=F32)
        problems.append(([s_lat, s_ctx], [v_ref[band, sl].astype(BF16), vc_ref[:, sl].astype(BF16)]))
    for sl, o in zip(heads, _softmax_pv(problems)):
        o_ref[:, sl] = o.astype(o_ref.dtype)


def na_attn(p, kc, vc, bias, dst, *, nseq, T, H, row0, col0, dst_col0, hp):
    rows = T // GRID_W
    nblk = rows // NA_QROWS
    tq = NA_QROWS * GRID_W
    L = kc.shape[0] // nseq
    W = hp * HEAD_DIM
    qb0, kb0, cb = row0 // tq, row0 // T, col0 // W
    ng = H // hp

    def kind(blk):
        return jnp.where(blk == 0, 0, jnp.where(blk == nblk - 1, 2, 1))

    in_specs, args, aliases = _into(dst, [
        pl.BlockSpec((tq, W), lambda b, h, i: (qb0 + b * nblk + i, cb + h)),
        pl.BlockSpec((T, W), lambda b, h, i: (kb0 + b, cb + ng + h)),
        pl.BlockSpec((T, W), lambda b, h, i: (kb0 + b, cb + 2 * ng + h)),
        pl.BlockSpec((L, W), lambda b, h, i: (b, h)),
        pl.BlockSpec((L, W), lambda b, h, i: (b, h)),
        pl.BlockSpec((None, hp, tq, NA_KROWS * GRID_W), lambda b, h, i: (kind(i), h, 0, 0))],
        [p, p, p, kc, vc, bias])
    return pl.pallas_call(
        _without_ref(functools.partial(_na_kernel, rows=rows, scale=HEAD_DIM ** -0.5, hp=hp), 6),
        grid=(nseq, ng, nblk),
        in_specs=in_specs,
        out_specs=pl.BlockSpec((tq, W), lambda b, h, i: (qb0 + b * nblk + i, dst_col0 // W + h)),
        out_shape=jax.ShapeDtypeStruct(dst.shape, dst.dtype),
        input_output_aliases=aliases,
        compiler_params=_cparams("arbitrary", "arbitrary", "arbitrary"),
        name="na_attn",
    )(*args)


ROPE_SWAP = np.concatenate([np.arange(16, 32), np.arange(0, 16), np.arange(48, 64), np.arange(32, 48)])


def rope_tables(T, tm):
    t = jnp.arange(T)
    half = QK_ROPE // 2
    inv = jnp.power(ROPE_BASE, -jnp.arange(0, half, 2, dtype=F32) / half)
    ang_r = (t // GRID_W).astype(F32)[:, None] * inv
    ang_c = (t % GRID_W).astype(F32)[:, None] * inv
    cos = jnp.concatenate([jnp.cos(ang_r), jnp.cos(ang_r), jnp.cos(ang_c), jnp.cos(ang_c)], axis=-1)
    sin = jnp.concatenate([-jnp.sin(ang_r), jnp.sin(ang_r), -jnp.sin(ang_c), jnp.sin(ang_c)], axis=-1)
    cos = jnp.concatenate([cos, jnp.ones((T, LANES - QK_ROPE), F32)], axis=-1)
    sin = jnp.concatenate([sin, jnp.zeros((T, LANES - QK_ROPE), F32)], axis=-1)
    cos = jnp.concatenate([cos, jnp.ones((tm, LANES), F32)], axis=0)
    sin = jnp.concatenate([sin, jnp.zeros((tm, LANES), F32)], axis=0)
    return cos, sin


def _mla_mid_kernel(pr_ref, qg_ref, kvg_ref, cos_ref, sin_ref, cq_ref, ckv32_ref, ckv16_ref, k2_ref, *, qr, kvr):
    cq_ref[...] = _rms(pr_ref[:, :qr], qg_ref[...]).astype(cq_ref.dtype)
    ckv = _rms(pr_ref[:, qr:qr + kvr], kvg_ref[...])
    ckv32_ref[...] = ckv
    ckv16_ref[...] = ckv.astype(ckv16_ref.dtype)
    x = pr_ref[:, qr + kvr:]
    rot = x * cos_ref[...] + pltpu.roll(x, LANES // 2, axis=1) * sin_ref[...]
    lane = lax.broadcasted_iota(jnp.int32, rot.shape, 1)
    k2_ref[...] = jnp.where(lane < QK_ROPE, rot, 0.0).astype(k2_ref.dtype)


def mla_mid(pr, q_norm_g, kv_norm_g, cos, sin, rows, *, qr, kvr, tm):
    n = pr.shape[0]
    pos = pl.BlockSpec((tm, LANES), lambda i: (rows.pos_block(i, tm), 0))

    def out(w):
        return pl.BlockSpec((tm, w), lambda i: (i, 0))

    return pl.pallas_call(
        functools.partial(_mla_mid_kernel, qr=qr, kvr=kvr),
        grid=(n // tm,),
        in_specs=[pl.BlockSpec((tm, pr.shape[1]), lambda i: (i, 0)), _vec_spec(qr), _vec_spec(kvr), pos, pos],
        out_specs=[out(qr), out(kvr), out(kvr), out(LANES)],
        out_shape=[jax.ShapeDtypeStruct((n, qr), BF16), jax.ShapeDtypeStruct((n, kvr), F32),
                   jax.ShapeDtypeStruct((n, kvr), BF16), jax.ShapeDtypeStruct((n, LANES), BF16)],
        compiler_params=_cparams("arbitrary"),
        name="mla_mid",
    )(pr, q_norm_g.reshape(1, qr), kv_norm_g.reshape(1, kvr), cos, sin)


MLA_Q_IN = 3 * LANES
MLA_Q_OUT = 2 * LANES


def widen_w_uq(w_uq, heads):
    r = w_uq.shape[0]
    w = w_uq.reshape(r, heads, QK_NOPE + QK_ROPE)
    nope, pe = w[..., :QK_NOPE], w[..., QK_NOPE:]
    return jnp.concatenate([nope, pe, pe, pe[..., ROPE_SWAP], jnp.zeros_like(pe)], axis=-1).reshape(r, heads * MLA_Q_IN)


def _mla_q_kernel(x_ref, w_ref, cos_ref, sin_ref, o_ref, *, hpt, scale):
    acc = jnp.dot(x_ref[...], w_ref[...], preferred_element_type=F32)
    cos, sin = cos_ref[...], sin_ref[...]
    for j in range(hpt):
        a = acc[:, j * MLA_Q_IN:(j + 1) * MLA_Q_IN]
        o_ref[:, j * MLA_Q_OUT:j * MLA_Q_OUT + LANES] = (a[:, :LANES] * scale).astype(o_ref.dtype)
        q2 = a[:, LANES:2 * LANES] * cos + a[:, 2 * LANES:] * sin
        o_ref[:, j * MLA_Q_OUT + LANES:(j + 1) * MLA_Q_OUT] = (q2 * scale).astype(o_ref.dtype)


def mla_q(cq, w_uq_wide, cos, sin, rows, *, heads, scale, tm, hpt=4):
    n, r = cq.shape
    pos = pl.BlockSpec((tm, LANES), lambda i, j: (rows.pos_block(i, tm), 0))
    return pl.pallas_call(
        functools.partial(_mla_q_kernel, hpt=hpt, scale=scale),
        grid=(n // tm, heads // hpt),
        in_specs=[pl.BlockSpec((tm, r), lambda i, j: (i, 0)),
                  pl.BlockSpec((r, hpt * MLA_Q_IN), lambda i, j: (0, j)), pos, pos],
        out_specs=pl.BlockSpec((tm, hpt * MLA_Q_OUT), lambda i, j: (i, j)),
        out_shape=jax.ShapeDtypeStruct((n, heads * MLA_Q_OUT), BF16),
        compiler_params=_cparams("arbitrary", "arbitrary"),
        name="mla_q",
    )(cq, w_uq_wide, cos, sin)


def _mla_ctx_attn_kernel(q_ref, kv_ref, k2_ref, o_ref, *, heads):
    k2 = k2_ref[...]
    problems = []
    for h in range(heads):
        q = q_ref[:, h * MLA_Q_OUT:(h + 1) * MLA_Q_OUT]
        k = jnp.concatenate([kv_ref[:, 2 * h * LANES:(2 * h + 1) * LANES], k2], axis=1)
        s = lax.dot_general(q, k, NT, preferred_element_type=F32)
        problems.append(([s], [kv_ref[:, (2 * h + 1) * LANES:(2 * h + 2) * LANES]]))
    for h, o in enumerate(_softmax_pv(problems)):
        o_ref[:, h * V_DIM:(h + 1) * V_DIM] = o.astype(o_ref.dtype)


def mla_ctx_attn(q, kv, k2, *, nseq, T, heads):
    return pl.pallas_call(
        functools.partial(_mla_ctx_attn_kernel, heads=heads),
        grid=(nseq,),
        in_specs=[pl.BlockSpec((T, heads * MLA_Q_OUT), lambda b: (b, 0)),
                  pl.BlockSpec((T, kv.shape[1]), lambda b: (b, 0)),
                  pl.BlockSpec((T, LANES), lambda b: (b, 0))],
        out_specs=pl.BlockSpec((T, heads * V_DIM), lambda b: (b, 0)),
        out_shape=jax.ShapeDtypeStruct((q.shape[0], heads * V_DIM), BF16),
        compiler_params=_cparams("arbitrary"),
        name="mla_ctx_attn",
    )(q, kv, k2)


def _mla_lat_attn_kernel(q_ref, k1_ref, v_ref, k2_ref, k1c_ref, vc_ref, k2c_ref, o_ref, *, sub):
    k = jnp.concatenate([k1_ref[...], k2_ref[...]], axis=1)
    kc = jnp.concatenate([k1c_ref[...], k2c_ref[...]], axis=1)
    blocks = [slice(s, s + sub) for s in range(0, q_ref.shape[0], sub)]
    problems = []
    for r in blocks:
        q = q_ref[r, :]
        s_lat = lax.dot_general(q, k, NT, preferred_element_type=F32)
        s_ctx = lax.dot_general(q, kc, NT, preferred_element_type=F32)
        problems.append(([s_lat, s_ctx], [v_ref[...], vc_ref[...]]))
    for r, o in zip(blocks, _softmax_pv(problems)):
        o_ref[r, :] = o.astype(o_ref.dtype)


def mla_lat_attn(q, kv, k2, kvc, k2c, dst, *, nseq, T, heads, row0, tq, sub):
    nq = T // tq
    P = kvc.shape[0] // nseq
    qb0, kb0 = row0 // tq, row0 // T
    in_specs, args, aliases = _into(dst, [
        pl.BlockSpec((tq, MLA_Q_OUT), lambda b, h, i: (qb0 + b * nq + i, h)),
        pl.BlockSpec((T, LANES), lambda b, h, i: (kb0 + b, 2 * h)),
        pl.BlockSpec((T, LANES), lambda b, h, i: (kb0 + b, 2 * h + 1)),
        pl.BlockSpec((T, LANES), lambda b, h, i: (kb0 + b, 0)),
        pl.BlockSpec((P, LANES), lambda b, h, i: (b, 2 * h)),
        pl.BlockSpec((P, LANES), lambda b, h, i: (b, 2 * h + 1)),
        pl.BlockSpec((P, LANES), lambda b, h, i: (b, 0))],
        [q, kv, kv, k2, kvc, kvc, k2c])
    return pl.pallas_call(
        _without_ref(functools.partial(_mla_lat_attn_kernel, sub=_fit(tq, sub)), 7),
        grid=(nseq, heads, nq),
        in_specs=in_specs,
        out_specs=pl.BlockSpec((tq, V_DIM), lambda b, h, i: (qb0 + b * nq + i, h)),
        out_shape=jax.ShapeDtypeStruct(dst.shape, dst.dtype),
        input_output_aliases=aliases,
        compiler_params=_cparams("arbitrary", "arbitrary", "arbitrary"),
        name="mla_lat_attn",
    )(*args)


def even_layer(h, rows, B, SEQ, Bd, state_f, state_b, cache_k, cache_v, lb_f, lb_b, w_in, hgrn_g, rpb, layer):
    D = h.shape[1]
    AW = D // 2
    AH = BH = AW // HEAD_DIM
    T = rows.t
    p = matmul_wcast(h, w_in, tm=1024, tn=1024, out_dtype=F32, name="even_in_proj")
    o, s_fw, s_bw = hgrn(p, lb_f, lb_b, hgrn_g, None, dst_cols=D, nseq=B, T=SEQ, row0=0, A_heads=AH, layer=layer,
                         hb=_fit(AH, 4), emit_state=True)
    o = hgrn(p, lb_f, lb_b, hgrn_g, o, dst_cols=D, nseq=Bd, T=T, row0=rows.nc, A_heads=AH, layer=layer,
             hb=_fit(AH, 2), init=(state_f, state_b))
    o = dense_attn(p, o, nseq=B, T=SEQ, H=BH, col0=5 * AW, dst_col0=AW)
    past = cache_k.shape[1]
    bias = na_bias_tables(rpb, T // GRID_W)
    o = na_attn(p, cache_k.reshape(Bd * past, BH * HEAD_DIM), cache_v.reshape(Bd * past, BH * HEAD_DIM), bias, o,
                nseq=Bd, T=T, H=BH, row0=rows.nc, col0=5 * AW, dst_col0=AW, hp=_fit(BH, 2))
    new_k = p[:rows.nc, 6 * AW:7 * AW].reshape(B, SEQ, BH, HEAD_DIM)
    new_v = p[:rows.nc, 7 * AW:8 * AW].reshape(B, SEQ, BH, HEAD_DIM)
    return o, s_fw, s_bw, new_k, new_v


def odd_layer(h, rows, B, SEQ, Bd, cache_ckv, cache_kpe, w_in, q_norm_g, w_uq, kv_norm_g, w_ukv):
    D = h.shape[1]
    heads = D // 128
    T = rows.t
    qr, kvr = w_uq.shape[0], w_ukv.shape[0]
    scale = (QK_NOPE + QK_ROPE) ** -0.5
    tm = rows.tile(512)
    w_in_wide = jnp.concatenate([w_in, w_in[:, qr + kvr + ROPE_SWAP]], axis=1).astype(BF16)
    pr = matmul(h, w_in_wide, tm=1024, tn=w_in_wide.shape[1], out_dtype=F32, name="odd_in_proj")
    cos, sin = rope_tables(T, tm)
    cq, ckv32, ckv16, k2 = mla_mid(pr, q_norm_g, kv_norm_g, cos, sin, rows, qr=qr, kvr=kvr, tm=tm)
    q = mla_q(cq, widen_w_uq(w_uq, heads).astype(BF16), cos, sin, rows, heads=heads, scale=scale, tm=tm)
    w_ukv16 = w_ukv.astype(BF16)
    kv = matmul(ckv16, w_ukv16, tm=1024, tn=1024, out_dtype=BF16, name="mla_kv")
    past = cache_ckv.shape[1]
    kvc = matmul(cache_ckv.reshape(Bd * past, kvr).astype(BF16), w_ukv16, tm=1024, tn=1024, out_dtype=BF16,
                 name="mla_kv_cache")
    k2c = jnp.concatenate([jnp.zeros((Bd * past, LANES - QK_ROPE), F32), cache_kpe.reshape(Bd * past, QK_ROPE)],
                          axis=1).astype(BF16)
    o = mla_ctx_attn(q, kv, k2, nseq=B, T=SEQ, heads=heads)
    o = mla_lat_attn(q, kv, k2, kvc, k2c, o, nseq=Bd, T=T, heads=heads, row0=rows.nc, tq=rows.tile(2048), sub=512)
    new_ckv = ckv32[:rows.nc].reshape(B, SEQ, kvr)
    new_kpe = pr[:rows.nc, qr + kvr:qr + kvr + QK_ROPE].reshape(B, SEQ, QK_ROPE)
    return o, new_ckv, new_kpe


def kernel(x_prompt, x_sample, state_hgrn_fwd, state_hgrn_bwd, cache_na_k, cache_na_v, cache_mla_ckv, cache_mla_kpe, c, c_ctx, ada_w, ada_b, norm_g, hgrn_lb_fwd, hgrn_lb_bwd, w_in_even, hgrn_norm_g, na_rpb, w_out_even, w_in_odd, mla_q_norm_g, w_uq, mla_kv_norm_g, w_ukv, w_out_odd, mlp_w1, mlp_w2):
    B, SEQ, D = x_prompt.shape
    Bd, T, _ = x_sample.shape
    depth = ada_w.shape[0]
    rows = Rows(B * SEQ, Bd * T, T)
    x = (x_prompt.reshape(rows.nc, D), x_sample.reshape(rows.nl, D))
    cvec = jnp.zeros((MOD_ROWS, D), F32).at[0].set(c_ctx).at[1:1 + Bd].set(c)
    mod = modulation(cvec, ada_w, ada_b).reshape(depth, MOD_ROWS * 6, 1, D)

    new_sf, new_sb, new_nk, new_nv, new_ckv, new_kpe = [], [], [], [], [], []
    h = prenorm(*x, norm_g[0, 0], mod[0], 0, rows)
    for l in range(depth):
        j = l // 2
        if l % 2 == 0:
            o, sf, sb, nk, nv = even_layer(h, rows, B, SEQ, Bd, state_hgrn_fwd[:, j], state_hgrn_bwd[:, j],
                                           cache_na_k[:, j], cache_na_v[:, j], hgrn_lb_fwd, hgrn_lb_bwd,
                                           w_in_even[j], hgrn_norm_g[j], na_rpb[j], l)
            w_out = w_out_even[j]
            new_sf.append(sf)
            new_sb.append(sb)
            new_nk.append(nk)
            new_nv.append(nv)
        else:
            o, ckv, kpe = odd_layer(h, rows, B, SEQ, Bd, cache_mla_ckv[:, j], cache_mla_kpe[:, j], w_in_odd[j],
                                    mla_q_norm_g[j], w_uq[j], mla_kv_norm_g[j], w_ukv[j])
            w_out = w_out_odd[j]
            new_ckv.append(ckv)
            new_kpe.append(kpe)
        x, h = matmul_post(o, w_out.astype(BF16), x, norm_g[l, 1], mod[l], 0, rows, pre=(norm_g[l, 2], mod[l], 3),
                           name="out_proj_post")
        a = matmul_wcast(h, mlp_w1[l], tm=1024, tn=1024, out_dtype=BF16, act="relu2", name="mlp_up")
        down = dict(tm=1024, tk=512, vmem_limit=VMEM_LIMIT_BIG, name="mlp_down_post")
        if l + 1 < depth:
            x, h = matmul_post(a, mlp_w2[l].astype(BF16), x, norm_g[l, 3], mod[l], 3, rows,
                               pre=(norm_g[l + 1, 0], mod[l + 1], 0), **down)
        else:
            x = matmul_post(a, mlp_w2[l].astype(BF16), x, norm_g[l, 3], mod[l], 3, rows, split_out=True, **down)
    return (x[0].reshape(B, SEQ, D), x[1].reshape(Bd, T, D),
            jnp.stack(new_sf, axis=1), jnp.stack(new_sb, axis=1), jnp.stack(new_nk, axis=1),
            jnp.stack(new_nv, axis=1), jnp.stack(new_ckv, axis=1), jnp.stack(new_kpe, axis=1))
```

```python
import functools

import numpy as np
import jax
import jax.numpy as jnp
from jax import lax
from jax.experimental import pallas as pl
from jax.experimental.pallas import tpu as pltpu

F32 = jnp.float32
BF16 = jnp.bfloat16

GRID_W = 64
HEAD_DIM = 128
NA_ROWS = 8
NA_COLS = 16
QK_NOPE = 128
QK_ROPE = 64
V_DIM = 128
ROPE_BASE = 10000.0
EPS = 1e-6
NEG = -1e30

LANES = 128
SUBLANES = 8
VMEM_LIMIT = 48 * 1024 * 1024
VMEM_LIMIT_BIG = 60 * 1024 * 1024
MOD_ROWS = 16

NT = (((1,), (1,)), ((), ()))
TN = (((0,), (0,)), ((), ()))


def _cparams(*sem, vmem_limit=VMEM_LIMIT):
    return pltpu.CompilerParams(dimension_semantics=sem, vmem_limit_bytes=vmem_limit)


def _fit(n, pref):
    t = min(pref, n)
    while n % t:
        t //= 2
    return t


def _silu(x):
    return x * jax.nn.sigmoid(x)


def _rms(x, g):
    return x * lax.rsqrt(jnp.mean(x * x, axis=-1, keepdims=True) + EPS) * g


def _softmax_pv(problems):
    ms = [functools.reduce(jnp.maximum, [jnp.max(s, axis=-1, keepdims=True) for s in scores])
          for scores, _ in problems]
    ps = [[jnp.exp(s - m) for s in scores] for (scores, _), m in zip(problems, ms)]
    ls = [functools.reduce(jnp.add, [jnp.sum(p, axis=-1, keepdims=True) for p in pp]) for pp in ps]
    os = [functools.reduce(jnp.add, [jnp.dot(p.astype(BF16), v, preferred_element_type=F32)
                                     for p, v in zip(pp, values)]) for pp, (_, values) in zip(ps, problems)]
    return [o / l for o, l in zip(os, ls)]


def _mod_kernel(c_ref, w_ref, b_ref, o_ref):
    s = _silu(c_ref[...]).astype(BF16)
    o_ref[...] = jnp.dot(s, w_ref[...].astype(BF16), preferred_element_type=F32) + b_ref[...]


def modulation(cvec, ada_w, ada_b, tn=512):
    L, D, N6 = ada_w.shape
    R = cvec.shape[0]
    tn = _fit(N6, tn)
    return pl.pallas_call(
        _mod_kernel,
        grid=(L, N6 // tn),
        in_specs=[pl.BlockSpec((R, D), lambda l, j: (0, 0)),
                  pl.BlockSpec((None, D, tn), lambda l, j: (l, 0, j)),
                  pl.BlockSpec((None, 1, tn), lambda l, j: (l, 0, j))],
        out_specs=pl.BlockSpec((None, R, tn), lambda l, j: (l, 0, j)),
        out_shape=jax.ShapeDtypeStruct((L, R, N6), F32),
        compiler_params=_cparams("arbitrary", "arbitrary"),
        name="modulation",
    )(cvec, ada_w, ada_b.reshape(L, 1, N6))


class Rows:
    def __init__(self, nc, nl, t):
        assert nc % t == 0, "latent sequences must start on a multiple of their length"
        self.nc, self.nl, self.t, self.n = nc, nl, t, nc + nl

    def tile(self, pref):
        return _fit(np.gcd(self.nc, self.t), pref)

    def mod_row(self, i, tm):
        ncb = self.nc // tm
        return jnp.where(i < ncb, 0, 1 + (i - ncb) // (self.t // tm))

    def pos_block(self, i, tm):
        ncb = self.nc // tm
        return jnp.where(i < ncb, self.t // tm, (i - ncb) % (self.t // tm))


def _mod_spec(rows, tm, k, D):
    return pl.BlockSpec((None, 1, D), lambda i, *_: (rows.mod_row(i, tm) * 6 + k, 0, 0))


def _vec_spec(D):
    return pl.BlockSpec((1, D), lambda i, *_: (0, 0))


def _split_specs(rows, tm, D):
    ncb = rows.nc // tm
    return [pl.BlockSpec((tm, D), lambda i, *_: (jnp.minimum(i, ncb - 1), 0)),
            pl.BlockSpec((tm, D), lambda i, *_: (jnp.maximum(i - ncb, 0), 0))]


def _into(dst, in_specs, args):
    return in_specs + [pl.BlockSpec(memory_space=pl.ANY)], args + [dst], {len(args): 0}


def _without_ref(kernel, idx):
    return lambda *refs: kernel(*refs[:idx], *refs[idx + 1:])


def _pre_kernel(xc_ref, xs_ref, g_ref, sh_ref, sc_ref, h_ref, *, ncb):
    x = jnp.where(pl.program_id(0) < ncb, xc_ref[...], xs_ref[...])
    h = _rms(x, g_ref[...]) * (1.0 + sc_ref[...]) + sh_ref[...]
    h_ref[...] = h.astype(h_ref.dtype)


def prenorm(xc, xs, g, mod3, k, rows, tm=512):
    D = xc.shape[1]
    tm = rows.tile(tm)
    return pl.pallas_call(
        functools.partial(_pre_kernel, ncb=rows.nc // tm),
        grid=(rows.n // tm,),
        in_specs=_split_specs(rows, tm, D) + [_vec_spec(D), _mod_spec(rows, tm, k, D),
                                              _mod_spec(rows, tm, k + 1, D)],
        out_specs=pl.BlockSpec((tm, D), lambda i: (i, 0)),
        out_shape=jax.ShapeDtypeStruct((rows.n, D), BF16),
        compiler_params=_cparams("arbitrary"),
        name="prenorm",
    )(xc, xs, g.reshape(1, D), mod3, mod3)


def _mm_kernel(x_ref, w_ref, o_ref, *scratch, nk, act):
    def finish(acc):
        if act == "relu2":
            a = jnp.maximum(acc, 0.0)
            acc = a * a
        o_ref[...] = acc.astype(o_ref.dtype)

    if nk == 1:
        finish(jnp.dot(x_ref[...], w_ref[...], preferred_element_type=F32))
        return
    acc_ref, = scratch
    k = pl.program_id(2)

    @pl.when(k == 0)
    def _():
        acc_ref[...] = jnp.zeros_like(acc_ref)

    acc_ref[...] += jnp.dot(x_ref[...], w_ref[...], preferred_element_type=F32)

    @pl.when(k == nk - 1)
    def _():
        finish(acc_ref[...])


def matmul(x, w, *, tm, tn, tk=None, out_dtype=F32, act=None, name="matmul"):
    M, K = x.shape
    _, N = w.shape
    tm, tn, tk = _fit(M, tm), _fit(N, tn), _fit(K, tk or K)
    nk = K // tk
    assert M % tm == 0 and N % tn == 0 and K % tk == 0
    return pl.pallas_call(
        functools.partial(_mm_kernel, nk=nk, act=act),
        grid=(M // tm, N // tn, nk),
        in_specs=[pl.BlockSpec((tm, tk), lambda i, j, k: (i, k)),
                  pl.BlockSpec((tk, tn), lambda i, j, k: (k, j))],
        out_specs=pl.BlockSpec((tm, tn), lambda i, j, k: (i, j)),
        out_shape=jax.ShapeDtypeStruct((M, N), out_dtype),
        scratch_shapes=[pltpu.VMEM((tm, tn), F32)] if nk > 1 else [],
        compiler_params=_cparams("arbitrary", "arbitrary", "arbitrary"),
        name=name,
    )(x, w)


def _mm_wcast_kernel(x_ref, w_ref, o_ref, wb_ref, *, act):
    @pl.when(pl.program_id(1) == 0)
    def _():
        wb_ref[...] = w_ref[...].astype(BF16)

    acc = jnp.dot(x_ref[...], wb_ref[...], preferred_element_type=F32)
    if act == "relu2":
        a = jnp.maximum(acc, 0.0)
        acc = a * a
    o_ref[...] = acc.astype(o_ref.dtype)


def matmul_wcast(x, w, *, tm, tn, out_dtype=F32, act=None, vmem_limit=VMEM_LIMIT, name="matmul_wcast"):
    M, K = x.shape
    _, N = w.shape
    tm, tn = _fit(M, tm), _fit(N, tn)
    return pl.pallas_call(
        functools.partial(_mm_wcast_kernel, act=act),
        grid=(N // tn, M // tm),
        in_specs=[pl.BlockSpec((tm, K), lambda j, i: (i, 0)),
                  pl.BlockSpec((K, tn), lambda j, i: (0, j))],
        out_specs=pl.BlockSpec((tm, tn), lambda j, i: (i, j)),
        out_shape=jax.ShapeDtypeStruct((M, N), out_dtype),
        scratch_shapes=[pltpu.VMEM((K, tn), BF16)],
        compiler_params=_cparams("arbitrary", "arbitrary", vmem_limit=vmem_limit),
        name=name,
    )(x, w)


def _mm_post_kernel(*refs, nk, with_pre, sub, ncb, split_in, split_out):
    it = iter(refs)
    a_ref, w_ref = next(it), next(it)
    x_refs = [next(it) for _ in range(2 if split_in else 1)]
    gp_ref, gate_ref = next(it), next(it)
    gn_ref, sh_ref, sc_ref = (next(it), next(it), next(it)) if with_pre else (None, None, None)
    x1_refs = [next(it) for _ in range(2 if split_out else 1)]
    h_ref = next(it) if with_pre else None
    tm = a_ref.shape[0]
    is_ctx = pl.program_id(0) < ncb

    def each_out(fn):
        if split_out:
            pl.when(is_ctx)(lambda: fn(x1_refs[0]))
            pl.when(jnp.logical_not(is_ctx))(lambda: fn(x1_refs[1]))
        else:
            fn(x1_refs[0])

    def finish(x1_ref, r, y):
        x = jnp.where(is_ctx, x_refs[0][r, :], x_refs[1][r, :]) if split_in else x_refs[0][r, :]
        x1 = x + gate_ref[...] * _rms(y, gp_ref[...])
        x1_ref[r, :] = x1
        if with_pre:
            h_ref[r, :] = (_rms(x1, gn_ref[...]) * (1.0 + sc_ref[...]) + sh_ref[...]).astype(h_ref.dtype)

    def product(r):
        return jnp.dot(a_ref[r, :], w_ref[...], preferred_element_type=F32)

    blocks = [slice(s, s + sub) for s in range(0, tm, sub)]
    everything = slice(None)

    def single(x1_ref):
        for r in blocks:
            finish(x1_ref, r, product(r))

    def first(x1_ref):
        x1_ref[...] = product(everything)

    def accumulate(x1_ref):
        x1_ref[...] += product(everything)

    def last(x1_ref):
        for r in blocks:
            finish(x1_ref, r, x1_ref[r, :] + product(r))

    if nk == 1:
        each_out(single)
        return
    k = pl.program_id(1)
    pl.when(k == 0)(lambda: each_out(first))
    pl.when(jnp.logical_and(k > 0, k < nk - 1))(lambda: each_out(accumulate))
    pl.when(k == nk - 1)(lambda: each_out(last))


def matmul_post(a, w, x, g_post, mod3, k, rows, pre=None, *, split_out=False, tm=512, tk=None, sub=256,
                vmem_limit=VMEM_LIMIT, name="matmul_post"):
    M, K = a.shape
    D = w.shape[1]
    tm = rows.tile(tm)
    tk = _fit(K, tk or K)
    nk = K // tk
    row_spec = pl.BlockSpec((tm, D), lambda i, kk: (i, 0))
    split_in = isinstance(x, tuple)
    x_specs, xs = (_split_specs(rows, tm, D), list(x)) if split_in else ([row_spec], [x])
    in_specs = [pl.BlockSpec((tm, tk), lambda i, kk: (i, kk)), pl.BlockSpec((tk, D), lambda i, kk: (kk, 0)),
                *x_specs, _vec_spec(D), _mod_spec(rows, tm, k + 2, D)]
    args = [a, w, *xs, g_post.reshape(1, D), mod3]
    if split_out:
        out_specs = _split_specs(rows, tm, D)
        out_shape = [jax.ShapeDtypeStruct((rows.nc, D), F32), jax.ShapeDtypeStruct((rows.nl, D), F32)]
    else:
        out_specs = [row_spec]
        out_shape = [jax.ShapeDtypeStruct((M, D), F32)]
    if pre is not None:
        g_pre, mod3_pre, kp = pre
        in_specs += [_vec_spec(D), _mod_spec(rows, tm, kp, D), _mod_spec(rows, tm, kp + 1, D)]
        args += [g_pre.reshape(1, D), mod3_pre, mod3_pre]
        out_specs.append(row_spec)
        out_shape.append(jax.ShapeDtypeStruct((M, D), BF16))
    out = pl.pallas_call(
        functools.partial(_mm_post_kernel, nk=nk, with_pre=pre is not None, sub=_fit(tm, sub), ncb=rows.nc // tm,
                          split_in=split_in, split_out=split_out),
        grid=(M // tm, nk),
        in_specs=in_specs, out_specs=out_specs, out_shape=out_shape,
        compiler_params=_cparams("arbitrary", "arbitrary", vmem_limit=vmem_limit),
        name=name,
    )(*args)
    return out if len(out) > 1 else out[0]


HGRN_CHUNK = 64
HGRN_BLOCK = 256


def _hgrn_kernel(*refs, T, hb, layer, has_init, emit_state, unroll_blocks, unroll_scan):
    it = iter(refs)
    q_ref, ff_ref, fb_ref, i_ref, g_ref, lbf_ref, lbb_ref, ng_ref = (next(it) for _ in range(8))
    s0_refs = (next(it), next(it)) if has_init else None
    o_ref = next(it)
    s_out_refs = (next(it), next(it)) if emit_state else None
    qe_refs, ds_refs, oi_refs, ebt_refs = ((next(it), next(it)) for _ in range(4))
    C, R = HGRN_CHUNK, min(HGRN_BLOCK, T)
    nch, nblk, cpb, mid = T // C, T // R, R // C, C // 2
    f_refs, lb_refs = (ff_ref, fb_ref), (lbf_ref, lbb_ref)
    total_row = (C - 1, 0)

    def lower_bound(lb_ref, sl):
        z = lb_ref[:, sl]
        e = jnp.exp(z - jnp.max(z, axis=0, keepdims=True))
        sm = e / jnp.sum(e, axis=0, keepdims=True)
        return jnp.sum(sm[:layer + 1], axis=0, keepdims=True)

    row = lax.broadcasted_iota(jnp.int32, (R, R), 0)
    col = lax.broadcasted_iota(jnp.int32, (R, R), 1)
    same_chunk = (row // C) == (col // C)
    tri = (same_chunk & (row >= col), same_chunk & (row <= col))

    def cumsum(mask, x):
        hi = x.astype(BF16)
        r1 = x - hi.astype(F32)
        md = r1.astype(BF16)
        lo = (r1 - md.astype(F32)).astype(BF16)
        s = jnp.dot(mask.astype(BF16), jnp.concatenate([hi, md, lo], axis=1), preferred_element_type=F32)
        return s[:, :HEAD_DIM] + s[:, HEAD_DIM:2 * HEAD_DIM] + s[:, 2 * HEAD_DIM:]

    heads = [slice(h * HEAD_DIM, (h + 1) * HEAD_DIM) for h in range(hb)]
    lbs = [[lower_bound(lb_refs[d], sl) for sl in heads] for d in range(2)]
    chains = [(h, sl, d) for h, sl in enumerate(heads) for d in range(2)]

    def block_pass(blk, carry):
        r = pl.ds(pl.multiple_of(blk * R, R), R)
        qs, vb, kk, b, qm, km, kd, a = {}, {}, {}, {}, {}, {}, {}, {}
        for h, sl in enumerate(heads):
            qs[h] = _silu(q_ref[r, sl]) * HEAD_DIM ** -0.5
            vb[h] = i_ref[r, sl].astype(BF16)
        for h, sl, d in chains:
            lb = lbs[d][h]
            f = lb + (1.0 - lb) * jax.nn.sigmoid(f_refs[d][r, sl])
            kk[h, d] = 1.0 - f
            b[h, d] = cumsum(tri[d], jnp.log(f))
        for h, sl, d in chains:
            qmc, kmc, kd[h, d] = [], [], []
            for cc in range(cpb):
                cs = slice(cc * C, (cc + 1) * C)
                bc = b[h, d][cs]
                m = bc[mid:mid + 1]
                bt = bc[total_row[d]:total_row[d] + 1]
                qc = qs[h][cs] * jnp.exp(bc - m)
                kc = kk[h, d][cs] * jnp.exp(m - bc)
                rc = pl.ds(pl.multiple_of(blk * R + cc * C, C), C)
                qe_refs[d][rc, sl] = (qc * jnp.exp(m)).astype(BF16)
                kd[h, d].append((kc * jnp.exp(bt - m)).astype(BF16))
                slot = pl.ds(pl.multiple_of((blk * cpb + cc) * SUBLANES, SUBLANES), SUBLANES)
                ebt_refs[d][slot, sl] = jnp.broadcast_to(jnp.exp(bt), (SUBLANES, HEAD_DIM))
                qmc.append(qc.astype(BF16))
                kmc.append(kc.astype(BF16))
            qm[h, d] = jnp.concatenate(qmc, axis=0)
            km[h, d] = jnp.concatenate(kmc, axis=0)
        for h, sl, d in chains:
            s = lax.dot_general(qm[h, d], km[h, d], NT, preferred_element_type=F32)
            a[h, d] = jnp.where(tri[d], s, 0.0).astype(BF16)
        for h, sl, d in chains:
            oi_refs[d][r, sl] = jnp.dot(a[h, d], vb[h], preferred_element_type=F32)
        for h, sl, d in chains:
            for cc in range(cpb):
                rows_c = pl.ds(pl.multiple_of((blk * cpb + cc) * HEAD_DIM, HEAD_DIM), HEAD_DIM)
                ds_refs[d][rows_c, sl] = lax.dot_general(vb[h][cc * C:(cc + 1) * C], kd[h, d][cc], TN,
                                                         preferred_element_type=F32)
        return carry

    lax.fori_loop(0, nblk, block_pass, 0, unroll=unroll_blocks)

    def scan_step(j, states):
        cs = [j, nch - 1 - j]
        rs = [pl.ds(pl.multiple_of(c * C, C), C) for c in cs]
        slots = [pl.ds(pl.multiple_of(c * SUBLANES, SUBLANES), SUBLANES) for c in cs]
        grow = [pl.ds(pl.multiple_of(c * HEAD_DIM, HEAD_DIM), HEAD_DIM) for c in cs]
        new = []
        for h, sl, d in chains:
            St = states[2 * h + d]
            oi_refs[d][rs[d], sl] += lax.dot_general(qe_refs[d][rs[d], sl], St.astype(BF16), NT,
                                                     preferred_element_type=F32)
            new.append(St * ebt_refs[d][slots[d], sl][:1] + ds_refs[d][grow[d], sl])
        return tuple(new)

    if has_init:
        init = tuple(s0_refs[d][h].T for h in range(hb) for d in range(2))
    else:
        init = tuple(jnp.zeros((HEAD_DIM, HEAD_DIM), F32) for _ in range(2 * hb))
    states = lax.fori_loop(0, nch, scan_step, init, unroll=unroll_scan)
    if emit_state:
        for h in range(hb):
            for d in range(2):
                s_out_refs[d][h] = states[2 * h + d].T

    def final_pass(blk, carry):
        r = pl.ds(pl.multiple_of(blk * R, R), R)
        for sl in heads:
            o = _rms(oi_refs[0][r, sl] + oi_refs[1][r, sl], ng_ref[...]) * _silu(g_ref[r, sl])
            o_ref[r, sl] = o.astype(o_ref.dtype)
        return carry

    lax.fori_loop(0, nblk, final_pass, 0)


def hgrn(p, lb_f, lb_b, norm_g, dst, *, nseq, T, row0, A_heads, layer, hb, init=None, emit_state=False,
         unroll_blocks=1, unroll_scan=4):
    assert T % min(HGRN_BLOCK, T) == 0 and A_heads % hb == 0
    rb0 = row0 // T
    H = A_heads
    W = hb * HEAD_DIM
    ng = H // hb

    def slab(k):
        return pl.BlockSpec((T, W), lambda b, h: (rb0 + b, k * ng + h))

    nl = lb_f.shape[0]
    in_specs = [slab(k) for k in range(5)] + [
        pl.BlockSpec((nl, W), lambda b, h: (0, h)),
        pl.BlockSpec((nl, W), lambda b, h: (0, h)),
        pl.BlockSpec((1, HEAD_DIM), lambda b, h: (0, 0))]
    args = [p] * 5 + [lb_f, lb_b, norm_g.reshape(1, HEAD_DIM)]
    state_spec = pl.BlockSpec((None, hb, HEAD_DIM, HEAD_DIM), lambda b, h: (b, h, 0, 0))
    if init is not None:
        in_specs += [state_spec, state_spec]
        args += list(init)
    in_specs, args, aliases = _into(dst, in_specs, args)
    out_specs = [pl.BlockSpec((T, W), lambda b, h: (rb0 + b, h))]
    out_shape = [jax.ShapeDtypeStruct(dst.shape, dst.dtype)]
    if emit_state:
        out_specs += [state_spec, state_spec]
        out_shape += [jax.ShapeDtypeStruct((nseq, H, HEAD_DIM, HEAD_DIM), F32)] * 2
    nch = T // HGRN_CHUNK
    scratch = ([pltpu.VMEM((T, W), BF16)] * 2
               + [pltpu.VMEM((nch * HEAD_DIM, W), F32)] * 2
               + [pltpu.VMEM((T, W), F32)] * 2
               + [pltpu.VMEM((nch * SUBLANES, W), F32)] * 2)
    body = functools.partial(_hgrn_kernel, T=T, hb=hb, layer=layer, has_init=init is not None, emit_state=emit_state,
                             unroll_blocks=unroll_blocks, unroll_scan=unroll_scan)
    out = pl.pallas_call(
        _without_ref(body, len(args) - 1),
        grid=(nseq, ng),
        in_specs=in_specs, out_specs=out_specs, out_shape=out_shape,
        input_output_aliases=aliases,
        scratch_shapes=scratch,
        compiler_params=_cparams("arbitrary", "arbitrary"),
        name="hgrn",
    )(*args)
    return out if emit_state else out[0]


def _dense_attn_kernel(q_ref, k_ref, v_ref, o_ref, *, H, scale):
    heads = [slice(h * HEAD_DIM, (h + 1) * HEAD_DIM) for h in range(H)]
    problems = []
    for sl in heads:
        q = (q_ref[:, sl] * scale).astype(BF16)
        s = lax.dot_general(q, k_ref[:, sl].astype(BF16), NT, preferred_element_type=F32)
        problems.append(([s], [v_ref[:, sl].astype(BF16)]))
    for sl, o in zip(heads, _softmax_pv(problems)):
        o_ref[:, sl] = o.astype(o_ref.dtype)


def dense_attn(p, dst, *, nseq, T, H, col0, dst_col0):
    W = H * HEAD_DIM
    cb = col0 // W
    in_specs, args, aliases = _into(dst, [pl.BlockSpec((T, W), lambda b, k=k: (b, cb + k)) for k in range(3)],
                                    [p, p, p])
    return pl.pallas_call(
        _without_ref(functools.partial(_dense_attn_kernel, H=H, scale=HEAD_DIM ** -0.5), 3),
        grid=(nseq,),
        in_specs=in_specs,
        out_specs=pl.BlockSpec((T, W), lambda b: (b, dst_col0 // W)),
        out_shape=jax.ShapeDtypeStruct(dst.shape, dst.dtype),
        input_output_aliases=aliases,
        compiler_params=_cparams("arbitrary"),
        name="dense_attn",
    )(*args)


NA_QROWS = 4
NA_KROWS = 12


def na_bias_tables(rpb, rows):
    W = GRID_W
    kr = min(NA_ROWS, rows)
    assert kr == NA_ROWS and rows >= NA_KROWS and rows % NA_QROWS == 0
    nblk = rows // NA_QROWS
    col = np.arange(W)
    cs = np.clip(col - NA_COLS // 2, 0, W - NA_COLS)
    col_ok = (col[None, :] >= cs[:, None]) & (col[None, :] < cs[:, None] + NA_COLS)
    ci = np.clip(col[None, :] - col[:, None] + NA_COLS - 1, 0, 2 * NA_COLS - 2)
    col_sel = (ci[..., None] == np.arange(2 * NA_COLS - 1)).astype(np.float32)
    row_sel, ok = [], []
    for blk in (0, 1, nblk - 1):
        r0 = blk * NA_QROWS
        u0 = min(max(r0 - NA_ROWS // 2, 0), rows - NA_KROWS)
        r = r0 + np.arange(NA_QROWS)
        ka = u0 + np.arange(NA_KROWS)
        start = np.clip(r - kr // 2, 0, rows - kr)
        row_ok = (ka[None, :] >= start[:, None]) & (ka[None, :] < start[:, None] + kr)
        ri = np.clip(ka[None, :] - r[:, None] + NA_ROWS - 1, 0, 2 * NA_ROWS - 2)
        row_sel.append((ri[..., None] == np.arange(2 * NA_ROWS - 1)).astype(np.float32))
        ok.append(row_ok[:, None, :, None] & col_ok[None, :, None, :])
    vals = jnp.einsum("kqua,hac,xyc->khqxuy", np.stack(row_sel), rpb.astype(F32), col_sel,
                      precision=lax.Precision.HIGHEST)
    t = jnp.where(np.stack(ok)[:, None], vals, NEG)
    return t.reshape(3, rpb.shape[0], NA_QROWS * W, NA_KROWS * W)


def _na_kernel(q_ref, k_ref, v_ref, kc_ref, vc_ref, bias_ref, o_ref, *, rows, scale, hp):
    blk = pl.program_id(2)
    u0 = jnp.clip(blk * NA_QROWS - NA_ROWS // 2, 0, rows - NA_KROWS)
    band = pl.ds(pl.multiple_of(u0 * GRID_W, GRID_W), NA_KROWS * GRID_W)
    heads = [slice(h * HEAD_DIM, (h + 1) * HEAD_DIM) for h in range(hp)]
    problems = []
    for h, sl in enumerate(heads):
        q = (q_ref[:, sl] * scale).astype(BF16)
        s_lat = lax.dot_general(q, k_ref[band, sl].astype(BF16), NT, preferred_element_type=F32) + bias_ref[h]
        s_ctx = lax.dot_general(q, kc_ref[:, sl].astype(BF16), NT, preferred_element_type=F32)
        problems.append(([s_lat, s_ctx], [v_ref[band, sl].astype(BF16), vc_ref[:, sl].astype(BF16)]))
    for sl, o in zip(heads, _softmax_pv(problems)):
        o_ref[:, sl] = o.astype(o_ref.dtype)


def na_attn(p, kc, vc, bias, dst, *, nseq, T, H, row0, col0, dst_col0, hp):
    rows = T // GRID_W
    nblk = rows // NA_QROWS
    tq = NA_QROWS * GRID_W
    L = kc.shape[0] // nseq
    W = hp * HEAD_DIM
    qb0, kb0, cb = row0 // tq, row0 // T, col0 // W
    ng = H // hp

    def kind(blk):
        return jnp.where(blk == 0, 0, jnp.where(blk == nblk - 1, 2, 1))

    in_specs, args, aliases = _into(dst, [
        pl.BlockSpec((tq, W), lambda b, h, i: (qb0 + b * nblk + i, cb + h)),
        pl.BlockSpec((T, W), lambda b, h, i: (kb0 + b, cb + ng + h)),
        pl.BlockSpec((T, W), lambda b, h, i: (kb0 + b, cb + 2 * ng + h)),
        pl.BlockSpec((L, W), lambda b, h, i: (b, h)),
        pl.BlockSpec((L, W), lambda b, h, i: (b, h)),
        pl.BlockSpec((None, hp, tq, NA_KROWS * GRID_W), lambda b, h, i: (kind(i), h, 0, 0))],
        [p, p, p, kc, vc, bias])
    return pl.pallas_call(
        _without_ref(functools.partial(_na_kernel, rows=rows, scale=HEAD_DIM ** -0.5, hp=hp), 6),
        grid=(nseq, ng, nblk),
        in_specs=in_specs,
        out_specs=pl.BlockSpec((tq, W), lambda b, h, i: (qb0 + b * nblk + i, dst_col0 // W + h)),
        out_shape=jax.ShapeDtypeStruct(dst.shape, dst.dtype),
        input_output_aliases=aliases,
        compiler_params=_cparams("arbitrary", "arbitrary", "arbitrary"),
        name="na_attn",
    )(*args)


ROPE_SWAP = np.concatenate([np.arange(16, 32), np.arange(0, 16), np.arange(48, 64), np.arange(32, 48)])


def rope_tables(T, tm):
    t = jnp.arange(T)
    half = QK_ROPE // 2
    inv = jnp.power(ROPE_BASE, -jnp.arange(0, half, 2, dtype=F32) / half)
    ang_r = (t // GRID_W).astype(F32)[:, None] * inv
    ang_c = (t % GRID_W).astype(F32)[:, None] * inv
    cos = jnp.concatenate([jnp.cos(ang_r), jnp.cos(ang_r), jnp.cos(ang_c), jnp.cos(ang_c)], axis=-1)
    sin = jnp.concatenate([-jnp.sin(ang_r), jnp.sin(ang_r), -jnp.sin(ang_c), jnp.sin(ang_c)], axis=-1)
    cos = jnp.concatenate([cos, jnp.ones((T, LANES - QK_ROPE), F32)], axis=-1)
    sin = jnp.concatenate([sin, jnp.zeros((T, LANES - QK_ROPE), F32)], axis=-1)
    cos = jnp.concatenate([cos, jnp.ones((tm, LANES), F32)], axis=0)
    sin = jnp.concatenate([sin, jnp.zeros((tm, LANES), F32)], axis=0)
    return cos, sin


def _mla_mid_kernel(pr_ref, qg_ref, kvg_ref, cos_ref, sin_ref, cq_ref, ckv32_ref, ckv16_ref, k2_ref, *, qr, kvr):
    cq_ref[...] = _rms(pr_ref[:, :qr], qg_ref[...]).astype(cq_ref.dtype)
    ckv = _rms(pr_ref[:, qr:qr + kvr], kvg_ref[...])
    ckv32_ref[...] = ckv
    ckv16_ref[...] = ckv.astype(ckv16_ref.dtype)
    x = pr_ref[:, qr + kvr:]
    rot = x * cos_ref[...] + pltpu.roll(x, LANES // 2, axis=1) * sin_ref[...]
    lane = lax.broadcasted_iota(jnp.int32, rot.shape, 1)
    k2_ref[...] = jnp.where(lane < QK_ROPE, rot, 0.0).astype(k2_ref.dtype)


def mla_mid(pr, q_norm_g, kv_norm_g, cos, sin, rows, *, qr, kvr, tm):
    n = pr.shape[0]
    pos = pl.BlockSpec((tm, LANES), lambda i: (rows.pos_block(i, tm), 0))

    def out(w):
        return pl.BlockSpec((tm, w), lambda i: (i, 0))

    return pl.pallas_call(
        functools.partial(_mla_mid_kernel, qr=qr, kvr=kvr),
        grid=(n // tm,),
        in_specs=[pl.BlockSpec((tm, pr.shape[1]), lambda i: (i, 0)), _vec_spec(qr), _vec_spec(kvr), pos, pos],
        out_specs=[out(qr), out(kvr), out(kvr), out(LANES)],
        out_shape=[jax.ShapeDtypeStruct((n, qr), BF16), jax.ShapeDtypeStruct((n, kvr), F32),
                   jax.ShapeDtypeStruct((n, kvr), BF16), jax.ShapeDtypeStruct((n, LANES), BF16)],
        compiler_params=_cparams("arbitrary"),
        name="mla_mid",
    )(pr, q_norm_g.reshape(1, qr), kv_norm_g.reshape(1, kvr), cos, sin)


MLA_Q_IN = 3 * LANES
MLA_Q_OUT = 2 * LANES


def widen_w_uq(w_uq, heads):
    r = w_uq.shape[0]
    w = w_uq.reshape(r, heads, QK_NOPE + QK_ROPE)
    nope, pe = w[..., :QK_NOPE], w[..., QK_NOPE:]
    return jnp.concatenate([nope, pe, pe, pe[..., ROPE_SWAP], jnp.zeros_like(pe)], axis=-1).reshape(r, heads * MLA_Q_IN)


def _mla_q_kernel(x_ref, w_ref, cos_ref, sin_ref, o_ref, *, hpt, scale):
    acc = jnp.dot(x_ref[...], w_ref[...], preferred_element_type=F32)
    cos, sin = cos_ref[...], sin_ref[...]
    for j in range(hpt):
        a = acc[:, j * MLA_Q_IN:(j + 1) * MLA_Q_IN]
        o_ref[:, j * MLA_Q_OUT:j * MLA_Q_OUT + LANES] = (a[:, :LANES] * scale).astype(o_ref.dtype)
        q2 = a[:, LANES:2 * LANES] * cos + a[:, 2 * LANES:] * sin
        o_ref[:, j * MLA_Q_OUT + LANES:(j + 1) * MLA_Q_OUT] = (q2 * scale).astype(o_ref.dtype)


def mla_q(cq, w_uq_wide, cos, sin, rows, *, heads, scale, tm, hpt=4):
    n, r = cq.shape
    pos = pl.BlockSpec((tm, LANES), lambda i, j: (rows.pos_block(i, tm), 0))
    return pl.pallas_call(
        functools.partial(_mla_q_kernel, hpt=hpt, scale=scale),
        grid=(n // tm, heads // hpt),
        in_specs=[pl.BlockSpec((tm, r), lambda i, j: (i, 0)),
                  pl.BlockSpec((r, hpt * MLA_Q_IN), lambda i, j: (0, j)), pos, pos],
        out_specs=pl.BlockSpec((tm, hpt * MLA_Q_OUT), lambda i, j: (i, j)),
        out_shape=jax.ShapeDtypeStruct((n, heads * MLA_Q_OUT), BF16),
        compiler_params=_cparams("arbitrary", "arbitrary"),
        name="mla_q",
    )(cq, w_uq_wide, cos, sin)


def _mla_ctx_attn_kernel(q_ref, kv_ref, k2_ref, o_ref, *, heads):
    k2 = k2_ref[...]
    problems = []
    for h in range(heads):
        q = q_ref[:, h * MLA_Q_OUT:(h + 1) * MLA_Q_OUT]
        k = jnp.concatenate([kv_ref[:, 2 * h * LANES:(2 * h + 1) * LANES], k2], axis=1)
        s = lax.dot_general(q, k, NT, preferred_element_type=F32)
        problems.append(([s], [kv_ref[:, (2 * h + 1) * LANES:(2 * h + 2) * LANES]]))
    for h, o in enumerate(_softmax_pv(problems)):
        o_ref[:, h * V_DIM:(h + 1) * V_DIM] = o.astype(o_ref.dtype)


def mla_ctx_attn(q, kv, k2, dst, *, nseq, T, heads):
    in_specs, args, aliases = _into(dst, [pl.BlockSpec((T, heads * MLA_Q_OUT), lambda b: (b, 0)),
                                          pl.BlockSpec((T, kv.shape[1]), lambda b: (b, 0)),
                                          pl.BlockSpec((T, LANES), lambda b: (b, 0))], [q, kv, k2])
    return pl.pallas_call(
        _without_ref(functools.partial(_mla_ctx_attn_kernel, heads=heads), 3),
        grid=(nseq,),
        in_specs=in_specs,
        out_specs=pl.BlockSpec((T, heads * V_DIM), lambda b: (b, 0)),
        out_shape=jax.ShapeDtypeStruct(dst.shape, dst.dtype),
        input_output_aliases=aliases,
        compiler_params=_cparams("arbitrary"),
        name="mla_ctx_attn",
    )(*args)


def _mla_lat_attn_kernel(q_ref, k1_ref, v_ref, k2_ref, k1c_ref, vc_ref, k2c_ref, o_ref, *, sub):
    k = jnp.concatenate([k1_ref[...], k2_ref[...]], axis=1)
    kc = jnp.concatenate([k1c_ref[...], k2c_ref[...]], axis=1)
    blocks = [slice(s, s + sub) for s in range(0, q_ref.shape[0], sub)]
    problems = []
    for r in blocks:
        q = q_ref[r, :]
        s_lat = lax.dot_general(q, k, NT, preferred_element_type=F32)
        s_ctx = lax.dot_general(q, kc, NT, preferred_element_type=F32)
        problems.append(([s_lat, s_ctx], [v_ref[...], vc_ref[...]]))
    for r, o in zip(blocks, _softmax_pv(problems)):
        o_ref[r, :] = o.astype(o_ref.dtype)


def mla_lat_attn(q, kv, k2, kvc, k2c, dst, *, nseq, T, heads, row0, tq, sub):
    nq = T // tq
    P = kvc.shape[0] // nseq
    qb0, kb0 = row0 // tq, row0 // T
    in_specs, args, aliases = _into(dst, [
        pl.BlockSpec((tq, MLA_Q_OUT), lambda b, h, i: (qb0 + b * nq + i, h)),
        pl.BlockSpec((T, LANES), lambda b, h, i: (kb0 + b, 2 * h)),
        pl.BlockSpec((T, LANES), lambda b, h, i: (kb0 + b, 2 * h + 1)),
        pl.BlockSpec((T, LANES), lambda b, h, i: (kb0 + b, 0)),
        pl.BlockSpec((P, LANES), lambda b, h, i: (b, 2 * h)),
        pl.BlockSpec((P, LANES), lambda b, h, i: (b, 2 * h + 1)),
        pl.BlockSpec((P, LANES), lambda b, h, i: (b, 0))],
        [q, kv, kv, k2, kvc, kvc, k2c])
    return pl.pallas_call(
        _without_ref(functools.partial(_mla_lat_attn_kernel, sub=_fit(tq, sub)), 7),
        grid=(nseq, heads, nq),
        in_specs=in_specs,
        out_specs=pl.BlockSpec((tq, V_DIM), lambda b, h, i: (qb0 + b * nq + i, h)),
        out_shape=jax.ShapeDtypeStruct(dst.shape, dst.dtype),
        input_output_aliases=aliases,
        compiler_params=_cparams("arbitrary", "arbitrary", "arbitrary"),
        name="mla_lat_attn",
    )(*args)


def even_layer(h, rows, B, SEQ, Bd, state_f, state_b, cache_k, cache_v, lb_f, lb_b, w_in, hgrn_g, rpb, layer):
    D = h.shape[1]
    AW = D // 2
    AH = BH = AW // HEAD_DIM
    T = rows.t
    p = matmul_wcast(h, w_in, tm=2048, tn=512, out_dtype=F32, name="even_in_proj")
    o, s_fw, s_bw = hgrn(p, lb_f, lb_b, hgrn_g, h, nseq=B, T=SEQ, row0=0, A_heads=AH, layer=layer, hb=_fit(AH, 4),
                         emit_state=True)
    o = hgrn(p, lb_f, lb_b, hgrn_g, o, nseq=Bd, T=T, row0=rows.nc, A_heads=AH, layer=layer, hb=_fit(AH, 2),
             init=(state_f, state_b))
    o = dense_attn(p, o, nseq=B, T=SEQ, H=BH, col0=5 * AW, dst_col0=AW)
    past = cache_k.shape[1]
    bias = na_bias_tables(rpb, T // GRID_W)
    o = na_attn(p, cache_k.reshape(Bd * past, BH * HEAD_DIM), cache_v.reshape(Bd * past, BH * HEAD_DIM), bias, o,
                nseq=Bd, T=T, H=BH, row0=rows.nc, col0=5 * AW, dst_col0=AW, hp=_fit(BH, 2))
    new_k = p[:rows.nc, 6 * AW:7 * AW].reshape(B, SEQ, BH, HEAD_DIM)
    new_v = p[:rows.nc, 7 * AW:8 * AW].reshape(B, SEQ, BH, HEAD_DIM)
    return o, s_fw, s_bw, new_k, new_v


def odd_layer(h, rows, B, SEQ, Bd, cache_ckv, cache_kpe, w_in, q_norm_g, w_uq, kv_norm_g, w_ukv):
    D = h.shape[1]
    heads = D // 128
    T = rows.t
    qr, kvr = w_uq.shape[0], w_ukv.shape[0]
    scale = (QK_NOPE + QK_ROPE) ** -0.5
    tm = rows.tile(512)
    w_in_wide = jnp.concatenate([w_in, w_in[:, qr + kvr + ROPE_SWAP]], axis=1).astype(BF16)
    pr = matmul(h, w_in_wide, tm=1024, tn=w_in_wide.shape[1], out_dtype=F32, name="odd_in_proj")
    cos, sin = rope_tables(T, tm)
    cq, ckv32, ckv16, k2 = mla_mid(pr, q_norm_g, kv_norm_g, cos, sin, rows, qr=qr, kvr=kvr, tm=tm)
    q = mla_q(cq, widen_w_uq(w_uq, heads).astype(BF16), cos, sin, rows, heads=heads, scale=scale, tm=tm)
    w_ukv16 = w_ukv.astype(BF16)
    kv = matmul(ckv16, w_ukv16, tm=1024, tn=1024, out_dtype=BF16, name="mla_kv")
    past = cache_ckv.shape[1]
    kvc = matmul(cache_ckv.reshape(Bd * past, kvr).astype(BF16), w_ukv16, tm=1024, tn=1024, out_dtype=BF16,
                 name="mla_kv_cache")
    k2c = jnp.concatenate([jnp.zeros((Bd * past, LANES - QK_ROPE), F32), cache_kpe.reshape(Bd * past, QK_ROPE)],
                          axis=1).astype(BF16)
    o = mla_ctx_attn(q, kv, k2, h, nseq=B, T=SEQ, heads=heads)
    o = mla_lat_attn(q, kv, k2, kvc, k2c, o, nseq=Bd, T=T, heads=heads, row0=rows.nc, tq=rows.tile(2048), sub=512)
    new_ckv = ckv32[:rows.nc].reshape(B, SEQ, kvr)
    new_kpe = pr[:rows.nc, qr + kvr:qr + kvr + QK_ROPE].reshape(B, SEQ, QK_ROPE)
    return o, new_ckv, new_kpe


def kernel(x_prompt, x_sample, state_hgrn_fwd, state_hgrn_bwd, cache_na_k, cache_na_v, cache_mla_ckv, cache_mla_kpe, c, c_ctx, ada_w, ada_b, norm_g, hgrn_lb_fwd, hgrn_lb_bwd, w_in_even, hgrn_norm_g, na_rpb, w_out_even, w_in_odd, mla_q_norm_g, w_uq, mla_kv_norm_g, w_ukv, w_out_odd, mlp_w1, mlp_w2):
    B, SEQ, D = x_prompt.shape
    Bd, T, _ = x_sample.shape
    depth = ada_w.shape[0]
    rows = Rows(B * SEQ, Bd * T, T)
    x = (x_prompt.reshape(rows.nc, D), x_sample.reshape(rows.nl, D))
    cvec = jnp.zeros((MOD_ROWS, D), F32).at[0].set(c_ctx).at[1:1 + Bd].set(c)
    mod = modulation(cvec, ada_w, ada_b).reshape(depth, MOD_ROWS * 6, 1, D)

    new_sf, new_sb, new_nk, new_nv, new_ckv, new_kpe = [], [], [], [], [], []
    h = prenorm(*x, norm_g[0, 0], mod[0], 0, rows)
    for l in range(depth):
        j = l // 2
        if l % 2 == 0:
            o, sf, sb, nk, nv = even_layer(h, rows, B, SEQ, Bd, state_hgrn_fwd[:, j], state_hgrn_bwd[:, j],
                                           cache_na_k[:, j], cache_na_v[:, j], hgrn_lb_fwd, hgrn_lb_bwd,
                                           w_in_even[j], hgrn_norm_g[j], na_rpb[j], l)
            w_out = w_out_even[j]
            new_sf.append(sf)
            new_sb.append(sb)
            new_nk.append(nk)
            new_nv.append(nv)
        else:
            o, ckv, kpe = odd_layer(h, rows, B, SEQ, Bd, cache_mla_ckv[:, j], cache_mla_kpe[:, j], w_in_odd[j],
                                    mla_q_norm_g[j], w_uq[j], mla_kv_norm_g[j], w_ukv[j])
            w_out = w_out_odd[j]
            new_ckv.append(ckv)
            new_kpe.append(kpe)
        x, h = matmul_post(o, w_out.astype(BF16), x, norm_g[l, 1], mod[l], 0, rows, pre=(norm_g[l, 2], mod[l], 3),
                           name="out_proj_post")
        a = matmul_wcast(h, mlp_w1[l], tm=2048, tn=1024, out_dtype=BF16, act="relu2", vmem_limit=VMEM_LIMIT_BIG,
                         name="mlp_up")
        down = dict(tm=1024, vmem_limit=VMEM_LIMIT_BIG, name="mlp_down_post")
        if l + 1 < depth:
            x, h = matmul_post(a, mlp_w2[l].astype(BF16), x, norm_g[l, 3], mod[l], 3, rows,
                               pre=(norm_g[l + 1, 0], mod[l + 1], 0), tk=1024, **down)
        else:
            x = matmul_post(a, mlp_w2[l].astype(BF16), x, norm_g[l, 3], mod[l], 3, rows, split_out=True, tk=512,
                            **down)
    return (x[0].reshape(B, SEQ, D), x[1].reshape(Bd, T, D),
            jnp.stack(new_sf, axis=1), jnp.stack(new_sb, axis=1), jnp.stack(new_nk, axis=1),
            jnp.stack(new_nv, axis=1), jnp.stack(new_ckv, axis=1), jnp.stack(new_kpe, axis=1))
```

```python
import functools

import numpy as np
import jax
import jax.numpy as jnp
from jax import lax
from jax.experimental import pallas as pl
from jax.experimental.pallas import tpu as pltpu

F32 = jnp.float32
BF16 = jnp.bfloat16

GRID_W = 64
HEAD_DIM = 128
NA_ROWS = 8
NA_COLS = 16
QK_NOPE = 128
QK_ROPE = 64
V_DIM = 128
ROPE_BASE = 10000.0
EPS = 1e-6
NEG = -1e30

LANES = 128
SUBLANES = 8
VMEM_LIMIT = 48 * 1024 * 1024
VMEM_LIMIT_BIG = 60 * 1024 * 1024
MOD_ROWS = 16

NT = (((1,), (1,)), ((), ()))
TN = (((0,), (0,)), ((), ()))


def _cparams(*sem, vmem_limit=VMEM_LIMIT):
    return pltpu.CompilerParams(dimension_semantics=sem, vmem_limit_bytes=vmem_limit)


def _fit(n, pref):
    t = min(pref, n)
    while n % t:
        t //= 2
    return t


def _silu(x):
    return x * jax.nn.sigmoid(x)


def _rms(x, g):
    return x * lax.rsqrt(jnp.mean(x * x, axis=-1, keepdims=True) + EPS) * g


def _softmax_pv(problems):
    ms = [functools.reduce(jnp.maximum, [jnp.max(s, axis=-1, keepdims=True) for s in scores])
          for scores, _ in problems]
    ps = [[jnp.exp(s - m) for s in scores] for (scores, _), m in zip(problems, ms)]
    ls = [functools.reduce(jnp.add, [jnp.sum(p, axis=-1, keepdims=True) for p in pp]) for pp in ps]
    os = [functools.reduce(jnp.add, [jnp.dot(p.astype(BF16), v, preferred_element_type=F32)
                                     for p, v in zip(pp, values)]) for pp, (_, values) in zip(ps, problems)]
    return [o / l for o, l in zip(os, ls)]


def _mod_kernel(c_ref, w_ref, b_ref, o_ref):
    s = _silu(c_ref[...]).astype(BF16)
    o_ref[...] = jnp.dot(s, w_ref[...].astype(BF16), preferred_element_type=F32) + b_ref[...]


def modulation(cvec, ada_w, ada_b, tn=512):
    L, D, N6 = ada_w.shape
    R = cvec.shape[0]
    tn = _fit(N6, tn)
    return pl.pallas_call(
        _mod_kernel,
        grid=(L, N6 // tn),
        in_specs=[pl.BlockSpec((R, D), lambda l, j: (0, 0)),
                  pl.BlockSpec((None, D, tn), lambda l, j: (l, 0, j)),
                  pl.BlockSpec((None, 1, tn), lambda l, j: (l, 0, j))],
        out_specs=pl.BlockSpec((None, R, tn), lambda l, j: (l, 0, j)),
        out_shape=jax.ShapeDtypeStruct((L, R, N6), F32),
        compiler_params=_cparams("arbitrary", "arbitrary"),
        name="modulation",
    )(cvec, ada_w, ada_b.reshape(L, 1, N6))


class Rows:
    def __init__(self, nc, nl, t):
        assert nc % t == 0, "latent sequences must start on a multiple of their length"
        self.nc, self.nl, self.t, self.n = nc, nl, t, nc + nl

    def tile(self, pref):
        return _fit(np.gcd(self.nc, self.t), pref)

    def mod_row(self, i, tm):
        ncb = self.nc // tm
        return jnp.where(i < ncb, 0, 1 + (i - ncb) // (self.t // tm))

    def pos_block(self, i, tm):
        ncb = self.nc // tm
        return jnp.where(i < ncb, self.t // tm, (i - ncb) % (self.t // tm))


def _mod_spec(rows, tm, k, D):
    return pl.BlockSpec((None, 1, D), lambda i, *_: (rows.mod_row(i, tm) * 6 + k, 0, 0))


def _vec_spec(D):
    return pl.BlockSpec((1, D), lambda i, *_: (0, 0))


def _split_specs(rows, tm, D):
    ncb = rows.nc // tm
    return [pl.BlockSpec((tm, D), lambda i, *_: (jnp.minimum(i, ncb - 1), 0)),
            pl.BlockSpec((tm, D), lambda i, *_: (jnp.maximum(i - ncb, 0), 0))]


def _into(dst, in_specs, args):
    return in_specs + [pl.BlockSpec(memory_space=pl.ANY)], args + [dst], {len(args): 0}


def _without_ref(kernel, idx):
    return lambda *refs: kernel(*refs[:idx], *refs[idx + 1:])


def _pre_kernel(xc_ref, xs_ref, g_ref, sh_ref, sc_ref, h_ref, *, ncb):
    x = jnp.where(pl.program_id(0) < ncb, xc_ref[...], xs_ref[...])
    h = _rms(x, g_ref[...]) * (1.0 + sc_ref[...]) + sh_ref[...]
    h_ref[...] = h.astype(h_ref.dtype)


def prenorm(xc, xs, g, mod3, k, rows, tm=512):
    D = xc.shape[1]
    tm = rows.tile(tm)
    return pl.pallas_call(
        functools.partial(_pre_kernel, ncb=rows.nc // tm),
        grid=(rows.n // tm,),
        in_specs=_split_specs(rows, tm, D) + [_vec_spec(D), _mod_spec(rows, tm, k, D),
                                              _mod_spec(rows, tm, k + 1, D)],
        out_specs=pl.BlockSpec((tm, D), lambda i: (i, 0)),
        out_shape=jax.ShapeDtypeStruct((rows.n, D), BF16),
        compiler_params=_cparams("arbitrary"),
        name="prenorm",
    )(xc, xs, g.reshape(1, D), mod3, mod3)


def _mm_kernel(x_ref, w_ref, o_ref, *scratch, nk, act):
    def finish(acc):
        if act == "relu2":
            a = jnp.maximum(acc, 0.0)
            acc = a * a
        o_ref[...] = acc.astype(o_ref.dtype)

    if nk == 1:
        finish(jnp.dot(x_ref[...], w_ref[...], preferred_element_type=F32))
        return
    acc_ref, = scratch
    k = pl.program_id(2)

    @pl.when(k == 0)
    def _():
        acc_ref[...] = jnp.zeros_like(acc_ref)

    acc_ref[...] += jnp.dot(x_ref[...], w_ref[...], preferred_element_type=F32)

    @pl.when(k == nk - 1)
    def _():
        finish(acc_ref[...])


def matmul(x, w, *, tm, tn, tk=None, out_dtype=F32, act=None, name="matmul"):
    M, K = x.shape
    _, N = w.shape
    tm, tn, tk = _fit(M, tm), _fit(N, tn), _fit(K, tk or K)
    nk = K // tk
    assert M % tm == 0 and N % tn == 0 and K % tk == 0
    return pl.pallas_call(
        functools.partial(_mm_kernel, nk=nk, act=act),
        grid=(M // tm, N // tn, nk),
        in_specs=[pl.BlockSpec((tm, tk), lambda i, j, k: (i, k)),
                  pl.BlockSpec((tk, tn), lambda i, j, k: (k, j))],
        out_specs=pl.BlockSpec((tm, tn), lambda i, j, k: (i, j)),
        out_shape=jax.ShapeDtypeStruct((M, N), out_dtype),
        scratch_shapes=[pltpu.VMEM((tm, tn), F32)] if nk > 1 else [],
        compiler_params=_cparams("arbitrary", "arbitrary", "arbitrary"),
        name=name,
    )(x, w)


def _mm_wcast_kernel(x_ref, w_ref, o_ref, wb_ref, *, act):
    @pl.when(pl.program_id(1) == 0)
    def _():
        wb_ref[...] = w_ref[...].astype(BF16)

    acc = jnp.dot(x_ref[...], wb_ref[...], preferred_element_type=F32)
    if act == "relu2":
        a = jnp.maximum(acc, 0.0)
        acc = a * a
    o_ref[...] = acc.astype(o_ref.dtype)


def matmul_wcast(x, w, layer, *, tm, tn, out_dtype=F32, act=None, vmem_limit=VMEM_LIMIT, name="matmul_wcast"):
    M, K = x.shape
    _, _, N = w.shape
    tm, tn = _fit(M, tm), _fit(N, tn)
    return pl.pallas_call(
        functools.partial(_mm_wcast_kernel, act=act),
        grid=(N // tn, M // tm),
        in_specs=[pl.BlockSpec((tm, K), lambda j, i: (i, 0)),
                  pl.BlockSpec((None, K, tn), lambda j, i: (layer, 0, j))],
        out_specs=pl.BlockSpec((tm, tn), lambda j, i: (i, j)),
        out_shape=jax.ShapeDtypeStruct((M, N), out_dtype),
        scratch_shapes=[pltpu.VMEM((K, tn), BF16)],
        compiler_params=_cparams("arbitrary", "arbitrary", vmem_limit=vmem_limit),
        name=name,
    )(x, w)


def _mm_post_kernel(*refs, nk, with_pre, sub, ncb, split_in, split_out):
    it = iter(refs)
    a_ref, w_ref = next(it), next(it)
    x_refs = [next(it) for _ in range(2 if split_in else 1)]
    gp_ref, gate_ref = next(it), next(it)
    gn_ref, sh_ref, sc_ref = (next(it), next(it), next(it)) if with_pre else (None, None, None)
    x1_refs = [next(it) for _ in range(2 if split_out else 1)]
    h_ref = next(it) if with_pre else None
    tm = a_ref.shape[0]
    is_ctx = pl.program_id(0) < ncb

    def each_out(fn):
        if split_out:
            pl.when(is_ctx)(lambda: fn(x1_refs[0]))
            pl.when(jnp.logical_not(is_ctx))(lambda: fn(x1_refs[1]))
        else:
            fn(x1_refs[0])

    def finish(x1_ref, r, y):
        x = jnp.where(is_ctx, x_refs[0][r, :], x_refs[1][r, :]) if split_in else x_refs[0][r, :]
        x1 = x + gate_ref[...] * _rms(y, gp_ref[...])
        x1_ref[r, :] = x1
        if with_pre:
            h_ref[r, :] = (_rms(x1, gn_ref[...]) * (1.0 + sc_ref[...]) + sh_ref[...]).astype(h_ref.dtype)

    def product(r):
        return jnp.dot(a_ref[r, :], w_ref[...], preferred_element_type=F32)

    blocks = [slice(s, s + sub) for s in range(0, tm, sub)]
    everything = slice(None)

    def single(x1_ref):
        for r in blocks:
            finish(x1_ref, r, product(r))

    def first(x1_ref):
        x1_ref[...] = product(everything)

    def accumulate(x1_ref):
        x1_ref[...] += product(everything)

    def last(x1_ref):
        for r in blocks:
            finish(x1_ref, r, x1_ref[r, :] + product(r))

    if nk == 1:
        each_out(single)
        return
    k = pl.program_id(1)
    pl.when(k == 0)(lambda: each_out(first))
    pl.when(jnp.logical_and(k > 0, k < nk - 1))(lambda: each_out(accumulate))
    pl.when(k == nk - 1)(lambda: each_out(last))


def matmul_post(a, w, layer, x, g_post, mod3, k, rows, pre=None, *, split_out=False, tm=512, tk=None, sub=256,
                vmem_limit=VMEM_LIMIT, name="matmul_post"):
    M, K = a.shape
    D = w.shape[2]
    tm = rows.tile(tm)
    tk = _fit(K, tk or K)
    nk = K // tk
    row_spec = pl.BlockSpec((tm, D), lambda i, kk: (i, 0))
    split_in = isinstance(x, tuple)
    x_specs, xs = (_split_specs(rows, tm, D), list(x)) if split_in else ([row_spec], [x])
    in_specs = [pl.BlockSpec((tm, tk), lambda i, kk: (i, kk)),
                pl.BlockSpec((None, tk, D), lambda i, kk: (layer, kk, 0)),
                *x_specs, _vec_spec(D), _mod_spec(rows, tm, k + 2, D)]
    args = [a, w, *xs, g_post.reshape(1, D), mod3]
    if split_out:
        out_specs = _split_specs(rows, tm, D)
        out_shape = [jax.ShapeDtypeStruct((rows.nc, D), F32), jax.ShapeDtypeStruct((rows.nl, D), F32)]
    else:
        out_specs = [row_spec]
        out_shape = [jax.ShapeDtypeStruct((M, D), F32)]
    if pre is not None:
        g_pre, mod3_pre, kp = pre
        in_specs += [_vec_spec(D), _mod_spec(rows, tm, kp, D), _mod_spec(rows, tm, kp + 1, D)]
        args += [g_pre.reshape(1, D), mod3_pre, mod3_pre]
        out_specs.append(row_spec)
        out_shape.append(jax.ShapeDtypeStruct((M, D), BF16))
    out = pl.pallas_call(
        functools.partial(_mm_post_kernel, nk=nk, with_pre=pre is not None, sub=_fit(tm, sub), ncb=rows.nc // tm,
                          split_in=split_in, split_out=split_out),
        grid=(M // tm, nk),
        in_specs=in_specs, out_specs=out_specs, out_shape=out_shape,
        compiler_params=_cparams("arbitrary", "arbitrary", vmem_limit=vmem_limit),
        name=name,
    )(*args)
    return out if len(out) > 1 else out[0]


HGRN_CHUNK = 64
HGRN_BLOCK = 256


def _hgrn_kernel(*refs, T, hb, layer, has_init, emit_state, unroll_blocks, unroll_scan):
    it = iter(refs)
    q_ref, ff_ref, fb_ref, i_ref, g_ref, lbf_ref, lbb_ref, ng_ref = (next(it) for _ in range(8))
    s0_refs = (next(it), next(it)) if has_init else None
    o_ref = next(it)
    s_out_refs = (next(it), next(it)) if emit_state else None
    qe_refs, ds_refs, oi_refs, ebt_refs = ((next(it), next(it)) for _ in range(4))
    C, R = HGRN_CHUNK, min(HGRN_BLOCK, T)
    nch, nblk, cpb, mid = T // C, T // R, R // C, C // 2
    f_refs, lb_refs = (ff_ref, fb_ref), (lbf_ref, lbb_ref)
    total_row = (C - 1, 0)

    def lower_bound(lb_ref, sl):
        z = lb_ref[:, sl]
        e = jnp.exp(z - jnp.max(z, axis=0, keepdims=True))
        sm = e / jnp.sum(e, axis=0, keepdims=True)
        return jnp.sum(sm[:layer + 1], axis=0, keepdims=True)

    row = lax.broadcasted_iota(jnp.int32, (R, R), 0)
    col = lax.broadcasted_iota(jnp.int32, (R, R), 1)
    same_chunk = (row // C) == (col // C)
    tri = (same_chunk & (row >= col), same_chunk & (row <= col))

    def cumsum(mask, x):
        hi = x.astype(BF16)
        r1 = x - hi.astype(F32)
        md = r1.astype(BF16)
        lo = (r1 - md.astype(F32)).astype(BF16)
        s = jnp.dot(mask.astype(BF16), jnp.concatenate([hi, md, lo], axis=1), preferred_element_type=F32)
        return s[:, :HEAD_DIM] + s[:, HEAD_DIM:2 * HEAD_DIM] + s[:, 2 * HEAD_DIM:]

    heads = [slice(h * HEAD_DIM, (h + 1) * HEAD_DIM) for h in range(hb)]
    lbs = [[lower_bound(lb_refs[d], sl) for sl in heads] for d in range(2)]
    chains = [(h, sl, d) for h, sl in enumerate(heads) for d in range(2)]

    def block_pass(blk, carry):
        r = pl.ds(pl.multiple_of(blk * R, R), R)
        qs, vb, kk, b, qm, km, kd, a = {}, {}, {}, {}, {}, {}, {}, {}
        for h, sl in enumerate(heads):
            qs[h] = _silu(q_ref[r, sl]) * HEAD_DIM ** -0.5
            vb[h] = i_ref[r, sl].astype(BF16)
        for h, sl, d in chains:
            lb = lbs[d][h]
            f = lb + (1.0 - lb) * jax.nn.sigmoid(f_refs[d][r, sl])
            kk[h, d] = 1.0 - f
            b[h, d] = cumsum(tri[d], jnp.log(f))
        for h, sl, d in chains:
            qmc, kmc, kd[h, d] = [], [], []
            for cc in range(cpb):
                cs = slice(cc * C, (cc + 1) * C)
                bc = b[h, d][cs]
                m = bc[mid:mid + 1]
                bt = bc[total_row[d]:total_row[d] + 1]
                qc = qs[h][cs] * jnp.exp(bc - m)
                kc = kk[h, d][cs] * jnp.exp(m - bc)
                rc = pl.ds(pl.multiple_of(blk * R + cc * C, C), C)
                qe_refs[d][rc, sl] = (qc * jnp.exp(m)).astype(BF16)
                kd[h, d].append((kc * jnp.exp(bt - m)).astype(BF16))
                slot = pl.ds(pl.multiple_of((blk * cpb + cc) * SUBLANES, SUBLANES), SUBLANES)
                ebt_refs[d][slot, sl] = jnp.broadcast_to(jnp.exp(bt), (SUBLANES, HEAD_DIM))
                qmc.append(qc.astype(BF16))
                kmc.append(kc.astype(BF16))
            qm[h, d] = jnp.concatenate(qmc, axis=0)
            km[h, d] = jnp.concatenate(kmc, axis=0)
        for h, sl, d in chains:
            s = lax.dot_general(qm[h, d], km[h, d], NT, preferred_element_type=F32)
            a[h, d] = jnp.where(tri[d], s, 0.0).astype(BF16)
        for h, sl, d in chains:
            oi_refs[d][r, sl] = jnp.dot(a[h, d], vb[h], preferred_element_type=F32)
        for h, sl, d in chains:
            for cc in range(cpb):
                rows_c = pl.ds(pl.multiple_of((blk * cpb + cc) * HEAD_DIM, HEAD_DIM), HEAD_DIM)
                ds_refs[d][rows_c, sl] = lax.dot_general(vb[h][cc * C:(cc + 1) * C], kd[h, d][cc], TN,
                                                         preferred_element_type=F32)
        return carry

    lax.fori_loop(0, nblk, block_pass, 0, unroll=unroll_blocks)

    def scan_step(j, states):
        cs = [j, nch - 1 - j]
        rs = [pl.ds(pl.multiple_of(c * C, C), C) for c in cs]
        slots = [pl.ds(pl.multiple_of(c * SUBLANES, SUBLANES), SUBLANES) for c in cs]
        grow = [pl.ds(pl.multiple_of(c * HEAD_DIM, HEAD_DIM), HEAD_DIM) for c in cs]
        new = []
        for h, sl, d in chains:
            St = states[2 * h + d]
            oi_refs[d][rs[d], sl] += lax.dot_general(qe_refs[d][rs[d], sl], St.astype(BF16), NT,
                                                     preferred_element_type=F32)
            new.append(St * ebt_refs[d][slots[d], sl][:1] + ds_refs[d][grow[d], sl])
        return tuple(new)

    if has_init:
        init = tuple(s0_refs[d][h].T for h in range(hb) for d in range(2))
    else:
        init = tuple(jnp.zeros((HEAD_DIM, HEAD_DIM), F32) for _ in range(2 * hb))
    states = lax.fori_loop(0, nch, scan_step, init, unroll=unroll_scan)
    if emit_state:
        for h in range(hb):
            for d in range(2):
                s_out_refs[d][h] = states[2 * h + d].T

    def final_pass(blk, carry):
        r = pl.ds(pl.multiple_of(blk * R, R), R)
        for sl in heads:
            o = _rms(oi_refs[0][r, sl] + oi_refs[1][r, sl], ng_ref[...]) * _silu(g_ref[r, sl])
            o_ref[r, sl] = o.astype(o_ref.dtype)
        return carry

    lax.fori_loop(0, nblk, final_pass, 0)


def hgrn(p, lb_f, lb_b, norm_g, dst, *, nseq, T, row0, A_heads, layer, hb, init=None, emit_state=False,
         unroll_blocks=1, unroll_scan=4):
    assert T % min(HGRN_BLOCK, T) == 0 and A_heads % hb == 0
    rb0 = row0 // T
    H = A_heads
    W = hb * HEAD_DIM
    ng = H // hb

    def slab(k):
        return pl.BlockSpec((T, W), lambda b, h: (rb0 + b, k * ng + h))

    nl = lb_f.shape[0]
    in_specs = [slab(k) for k in range(5)] + [
        pl.BlockSpec((nl, W), lambda b, h: (0, h)),
        pl.BlockSpec((nl, W), lambda b, h: (0, h)),
        pl.BlockSpec((1, HEAD_DIM), lambda b, h: (0, 0))]
    args = [p] * 5 + [lb_f, lb_b, norm_g.reshape(1, HEAD_DIM)]
    state_spec = pl.BlockSpec((None, hb, HEAD_DIM, HEAD_DIM), lambda b, h: (b, h, 0, 0))
    if init is not None:
        in_specs += [state_spec, state_spec]
        args += list(init)
    in_specs, args, aliases = _into(dst, in_specs, args)
    out_specs = [pl.BlockSpec((T, W), lambda b, h: (rb0 + b, h))]
    out_shape = [jax.ShapeDtypeStruct(dst.shape, dst.dtype)]
    if emit_state:
        out_specs += [state_spec, state_spec]
        out_shape += [jax.ShapeDtypeStruct((nseq, H, HEAD_DIM, HEAD_DIM), F32)] * 2
    nch = T // HGRN_CHUNK
    scratch = ([pltpu.VMEM((T, W), BF16)] * 2
               + [pltpu.VMEM((nch * HEAD_DIM, W), F32)] * 2
               + [pltpu.VMEM((T, W), F32)] * 2
               + [pltpu.VMEM((nch * SUBLANES, W), F32)] * 2)
    body = functools.partial(_hgrn_kernel, T=T, hb=hb, layer=layer, has_init=init is not None, emit_state=emit_state,
                             unroll_blocks=unroll_blocks, unroll_scan=unroll_scan)
    out = pl.pallas_call(
        _without_ref(body, len(args) - 1),
        grid=(nseq, ng),
        in_specs=in_specs, out_specs=out_specs, out_shape=out_shape,
        input_output_aliases=aliases,
        scratch_shapes=scratch,
        compiler_params=_cparams("arbitrary", "arbitrary"),
        name="hgrn",
    )(*args)
    return out if emit_state else out[0]


def _dense_attn_kernel(q_ref, k_ref, v_ref, o_ref, kout_ref, vout_ref, *, H, scale):
    heads = [slice(h * HEAD_DIM, (h + 1) * HEAD_DIM) for h in range(H)]
    problems = []
    for h, sl in enumerate(heads):
        kout_ref[:, h, :] = k_ref[:, sl]
        vout_ref[:, h, :] = v_ref[:, sl]
        q = (q_ref[:, sl] * scale).astype(BF16)
        s = lax.dot_general(q, k_ref[:, sl].astype(BF16), NT, preferred_element_type=F32)
        problems.append(([s], [v_ref[:, sl].astype(BF16)]))
    for sl, o in zip(heads, _softmax_pv(problems)):
        o_ref[:, sl] = o.astype(o_ref.dtype)


def dense_attn(p, dst, *, nseq, T, H, col0, dst_col0):
    W = H * HEAD_DIM
    cb = col0 // W
    in_specs, args, aliases = _into(dst, [pl.BlockSpec((T, W), lambda b, k=k: (b, cb + k)) for k in range(3)],
                                    [p, p, p])
    cache_spec = pl.BlockSpec((None, T, H, HEAD_DIM), lambda b: (b, 0, 0, 0))
    cache_shape = jax.ShapeDtypeStruct((nseq, T, H, HEAD_DIM), p.dtype)
    return pl.pallas_call(
        _without_ref(functools.partial(_dense_attn_kernel, H=H, scale=HEAD_DIM ** -0.5), 3),
        grid=(nseq,),
        in_specs=in_specs,
        out_specs=[pl.BlockSpec((T, W), lambda b: (b, dst_col0 // W)), cache_spec, cache_spec],
        out_shape=[jax.ShapeDtypeStruct(dst.shape, dst.dtype), cache_shape, cache_shape],
        input_output_aliases=aliases,
        compiler_params=_cparams("arbitrary"),
        name="dense_attn",
    )(*args)


NA_QROWS = 4
NA_KROWS = 12


def na_bias_tables(rpb, rows):
    W = GRID_W
    kr = min(NA_ROWS, rows)
    assert kr == NA_ROWS and rows >= NA_KROWS and rows % NA_QROWS == 0
    nblk = rows // NA_QROWS
    col = np.arange(W)
    cs = np.clip(col - NA_COLS // 2, 0, W - NA_COLS)
    col_ok = (col[None, :] >= cs[:, None]) & (col[None, :] < cs[:, None] + NA_COLS)
    ci = np.clip(col[None, :] - col[:, None] + NA_COLS - 1, 0, 2 * NA_COLS - 2)
    col_sel = (ci[..., None] == np.arange(2 * NA_COLS - 1)).astype(np.float32)
    G = jnp.einsum("hac,xyc->haxy", rpb.astype(F32), col_sel, precision=lax.Precision.HIGHEST)
    G = jnp.where(col_ok, G, NEG)
    outside = 2 * NA_ROWS - 1
    G = jnp.concatenate([G, jnp.full_like(G[:, :1], NEG)], axis=1)
    tables = []
    for blk in (0, 1, nblk - 1):
        r0 = blk * NA_QROWS
        u0 = min(max(r0 - NA_ROWS // 2, 0), rows - NA_KROWS)
        r = r0 + np.arange(NA_QROWS)
        ka = u0 + np.arange(NA_KROWS)
        start = np.clip(r - kr // 2, 0, rows - kr)
        row_ok = (ka[None, :] >= start[:, None]) & (ka[None, :] < start[:, None] + kr)
        ri = np.where(row_ok, ka[None, :] - r[:, None] + NA_ROWS - 1, outside)
        tables.append(jnp.concatenate(
            [jnp.concatenate([G[:, int(a)] for a in ri[q]], axis=-1) for q in range(NA_QROWS)], axis=-2))
    return jnp.stack(tables)


def _na_kernel(q_ref, k_ref, v_ref, kc_ref, vc_ref, bias_ref, o_ref, *, rows, scale, hp):
    blk = pl.program_id(2)
    u0 = jnp.clip(blk * NA_QROWS - NA_ROWS // 2, 0, rows - NA_KROWS)
    band = pl.ds(pl.multiple_of(u0 * GRID_W, GRID_W), NA_KROWS * GRID_W)
    heads = [slice(h * HEAD_DIM, (h + 1) * HEAD_DIM) for h in range(hp)]
    problems = []
    for h, sl in enumerate(heads):
        q = (q_ref[:, sl] * scale).astype(BF16)
        s_lat = lax.dot_general(q, k_ref[band, sl].astype(BF16), NT, preferred_element_type=F32) + bias_ref[h]
        s_ctx = lax.dot_general(q, kc_ref[:, sl].astype(BF16), NT, preferred_element_type=F32)
        problems.append(([s_lat, s_ctx], [v_ref[band, sl].astype(BF16), vc_ref[:, sl].astype(BF16)]))
    for sl, o in zip(heads, _softmax_pv(problems)):
        o_ref[:, sl] = o.astype(o_ref.dtype)


def na_attn(p, kc, vc, bias, dst, *, nseq, T, H, row0, col0, dst_col0, hp):
    rows = T // GRID_W
    nblk = rows // NA_QROWS
    tq = NA_QROWS * GRID_W
    L = kc.shape[0] // nseq
    W = hp * HEAD_DIM
    qb0, kb0, cb = row0 // tq, row0 // T, col0 // W
    ng = H // hp

    def kind(blk):
        return jnp.where(blk == 0, 0, jnp.where(blk == nblk - 1, 2, 1))

    in_specs, args, aliases = _into(dst, [
        pl.BlockSpec((tq, W), lambda b, h, i: (qb0 + b * nblk + i, cb + h)),
        pl.BlockSpec((T, W), lambda b, h, i: (kb0 + b, cb + ng + h)),
        pl.BlockSpec((T, W), lambda b, h, i: (kb0 + b, cb + 2 * ng + h)),
        pl.BlockSpec((L, W), lambda b, h, i: (b, h)),
        pl.BlockSpec((L, W), lambda b, h, i: (b, h)),
        pl.BlockSpec((None, hp, tq, NA_KROWS * GRID_W), lambda b, h, i: (kind(i), h, 0, 0))],
        [p, p, p, kc, vc, bias])
    return pl.pallas_call(
        _without_ref(functools.partial(_na_kernel, rows=rows, scale=HEAD_DIM ** -0.5, hp=hp), 6),
        grid=(nseq, ng, nblk),
        in_specs=in_specs,
        out_specs=pl.BlockSpec((tq, W), lambda b, h, i: (qb0 + b * nblk + i, dst_col0 // W + h)),
        out_shape=jax.ShapeDtypeStruct(dst.shape, dst.dtype),
        input_output_aliases=aliases,
        compiler_params=_cparams("arbitrary", "arbitrary", "arbitrary"),
        name="na_attn",
    )(*args)


ROPE_SWAP = np.concatenate([np.arange(16, 32), np.arange(0, 16), np.arange(48, 64), np.arange(32, 48)])


def rope_tables(T, tm):
    t = jnp.arange(T)
    half = QK_ROPE // 2
    inv = jnp.power(ROPE_BASE, -jnp.arange(0, half, 2, dtype=F32) / half)
    ang_r = (t // GRID_W).astype(F32)[:, None] * inv
    ang_c = (t % GRID_W).astype(F32)[:, None] * inv
    cos = jnp.concatenate([jnp.cos(ang_r), jnp.cos(ang_r), jnp.cos(ang_c), jnp.cos(ang_c)], axis=-1)
    sin = jnp.concatenate([-jnp.sin(ang_r), jnp.sin(ang_r), -jnp.sin(ang_c), jnp.sin(ang_c)], axis=-1)
    cos = jnp.concatenate([cos, jnp.ones((T, LANES - QK_ROPE), F32)], axis=-1)
    sin = jnp.concatenate([sin, jnp.zeros((T, LANES - QK_ROPE), F32)], axis=-1)
    cos = jnp.concatenate([cos, jnp.ones((tm, LANES), F32)], axis=0)
    sin = jnp.concatenate([sin, jnp.zeros((tm, LANES), F32)], axis=0)
    return cos, sin


def _mla_mid_kernel(pr_ref, qg_ref, kvg_ref, cos_ref, sin_ref, cq_ref, ckv32_ref, ckv16_ref, k2_ref, *, qr, kvr):
    cq_ref[...] = _rms(pr_ref[:, :qr], qg_ref[...]).astype(cq_ref.dtype)
    ckv = _rms(pr_ref[:, qr:qr + kvr], kvg_ref[...])
    ckv32_ref[...] = ckv
    ckv16_ref[...] = ckv.astype(ckv16_ref.dtype)
    x = pr_ref[:, qr + kvr:]
    rot = x * cos_ref[...] + pltpu.roll(x, LANES // 2, axis=1) * sin_ref[...]
    lane = lax.broadcasted_iota(jnp.int32, rot.shape, 1)
    k2_ref[...] = jnp.where(lane < QK_ROPE, rot, 0.0).astype(k2_ref.dtype)


def mla_mid(pr, q_norm_g, kv_norm_g, cos, sin, rows, *, qr, kvr, tm):
    n = pr.shape[0]
    pos = pl.BlockSpec((tm, LANES), lambda i: (rows.pos_block(i, tm), 0))

    def out(w):
        return pl.BlockSpec((tm, w), lambda i: (i, 0))

    return pl.pallas_call(
        functools.partial(_mla_mid_kernel, qr=qr, kvr=kvr),
        grid=(n // tm,),
        in_specs=[pl.BlockSpec((tm, pr.shape[1]), lambda i: (i, 0)), _vec_spec(qr), _vec_spec(kvr), pos, pos],
        out_specs=[out(qr), out(kvr), out(kvr), out(LANES)],
        out_shape=[jax.ShapeDtypeStruct((n, qr), BF16), jax.ShapeDtypeStruct((n, kvr), F32),
                   jax.ShapeDtypeStruct((n, kvr), BF16), jax.ShapeDtypeStruct((n, LANES), BF16)],
        compiler_params=_cparams("arbitrary"),
        name="mla_mid",
    )(pr, q_norm_g.reshape(1, qr), kv_norm_g.reshape(1, kvr), cos, sin)


MLA_Q_IN = 3 * LANES
MLA_Q_OUT = 2 * LANES


def widen_w_uq(w_uq, heads):
    r = w_uq.shape[0]
    w = w_uq.reshape(r, heads, QK_NOPE + QK_ROPE)
    nope, pe = w[..., :QK_NOPE], w[..., QK_NOPE:]
    return jnp.concatenate([nope, pe, pe, pe[..., ROPE_SWAP], jnp.zeros_like(pe)], axis=-1).reshape(r, heads * MLA_Q_IN)


def _mla_q_kernel(x_ref, w_ref, cos_ref, sin_ref, o_ref, *, hpt, scale):
    acc = jnp.dot(x_ref[...], w_ref[...], preferred_element_type=F32)
    cos, sin = cos_ref[...], sin_ref[...]
    for j in range(hpt):
        a = acc[:, j * MLA_Q_IN:(j + 1) * MLA_Q_IN]
        o_ref[:, j * MLA_Q_OUT:j * MLA_Q_OUT + LANES] = (a[:, :LANES] * scale).astype(o_ref.dtype)
        q2 = a[:, LANES:2 * LANES] * cos + a[:, 2 * LANES:] * sin
        o_ref[:, j * MLA_Q_OUT + LANES:(j + 1) * MLA_Q_OUT] = (q2 * scale).astype(o_ref.dtype)


def mla_q(cq, w_uq_wide, cos, sin, rows, *, heads, scale, tm, hpt=4):
    n, r = cq.shape
    pos = pl.BlockSpec((tm, LANES), lambda i, j: (rows.pos_block(i, tm), 0))
    return pl.pallas_call(
        functools.partial(_mla_q_kernel, hpt=hpt, scale=scale),
        grid=(n // tm, heads // hpt),
        in_specs=[pl.BlockSpec((tm, r), lambda i, j: (i, 0)),
                  pl.BlockSpec((r, hpt * MLA_Q_IN), lambda i, j: (0, j)), pos, pos],
        out_specs=pl.BlockSpec((tm, hpt * MLA_Q_OUT), lambda i, j: (i, j)),
        out_shape=jax.ShapeDtypeStruct((n, heads * MLA_Q_OUT), BF16),
        compiler_params=_cparams("arbitrary", "arbitrary"),
        name="mla_q",
    )(cq, w_uq_wide, cos, sin)


def _mla_ctx_attn_kernel(q_ref, kv_ref, k2_ref, o_ref, *, heads):
    k2 = k2_ref[...]
    problems = []
    for h in range(heads):
        q = q_ref[:, h * MLA_Q_OUT:(h + 1) * MLA_Q_OUT]
        k = jnp.concatenate([kv_ref[:, 2 * h * LANES:(2 * h + 1) * LANES], k2], axis=1)
        s = lax.dot_general(q, k, NT, preferred_element_type=F32)
        problems.append(([s], [kv_ref[:, (2 * h + 1) * LANES:(2 * h + 2) * LANES]]))
    for h, o in enumerate(_softmax_pv(problems)):
        o_ref[:, h * V_DIM:(h + 1) * V_DIM] = o.astype(o_ref.dtype)


def mla_ctx_attn(q, kv, k2, dst, *, nseq, T, heads):
    in_specs, args, aliases = _into(dst, [pl.BlockSpec((T, heads * MLA_Q_OUT), lambda b: (b, 0)),
                                          pl.BlockSpec((T, kv.shape[1]), lambda b: (b, 0)),
                                          pl.BlockSpec((T, LANES), lambda b: (b, 0))], [q, kv, k2])
    return pl.pallas_call(
        _without_ref(functools.partial(_mla_ctx_attn_kernel, heads=heads), 3),
        grid=(nseq,),
        in_specs=in_specs,
        out_specs=pl.BlockSpec((T, heads * V_DIM), lambda b: (b, 0)),
        out_shape=jax.ShapeDtypeStruct(dst.shape, dst.dtype),
        input_output_aliases=aliases,
        compiler_params=_cparams("arbitrary"),
        name="mla_ctx_attn",
    )(*args)


def _mla_lat_attn_kernel(q_ref, k1_ref, v_ref, k2_ref, k1c_ref, vc_ref, k2c_ref, o_ref, *, sub):
    k = jnp.concatenate([k1_ref[...], k2_ref[...]], axis=1)
    kc = jnp.concatenate([k1c_ref[...], k2c_ref[...]], axis=1)
    blocks = [slice(s, s + sub) for s in range(0, q_ref.shape[0], sub)]
    problems = []
    for r in blocks:
        q = q_ref[r, :]
        s_lat = lax.dot_general(q, k, NT, preferred_element_type=F32)
        s_ctx = lax.dot_general(q, kc, NT, preferred_element_type=F32)
        problems.append(([s_lat, s_ctx], [v_ref[...], vc_ref[...]]))
    for r, o in zip(blocks, _softmax_pv(problems)):
        o_ref[r, :] = o.astype(o_ref.dtype)


def mla_lat_attn(q, kv, k2, kvc, k2c, dst, *, nseq, T, heads, row0, tq, sub):
    nq = T // tq
    P = kvc.shape[0] // nseq
    qb0, kb0 = row0 // tq, row0 // T
    in_specs, args, aliases = _into(dst, [
        pl.BlockSpec((tq, MLA_Q_OUT), lambda b, h, i: (qb0 + b * nq + i, h)),
        pl.BlockSpec((T, LANES), lambda b, h, i: (kb0 + b, 2 * h)),
        pl.BlockSpec((T, LANES), lambda b, h, i: (kb0 + b, 2 * h + 1)),
        pl.BlockSpec((T, LANES), lambda b, h, i: (kb0 + b, 0)),
        pl.BlockSpec((P, LANES), lambda b, h, i: (b, 2 * h)),
        pl.BlockSpec((P, LANES), lambda b, h, i: (b, 2 * h + 1)),
        pl.BlockSpec((P, LANES), lambda b, h, i: (b, 0))],
        [q, kv, kv, k2, kvc, kvc, k2c])
    return pl.pallas_call(
        _without_ref(functools.partial(_mla_lat_attn_kernel, sub=_fit(tq, sub)), 7),
        grid=(nseq, heads, nq),
        in_specs=in_specs,
        out_specs=pl.BlockSpec((tq, V_DIM), lambda b, h, i: (qb0 + b * nq + i, h)),
        out_shape=jax.ShapeDtypeStruct(dst.shape, dst.dtype),
        input_output_aliases=aliases,
        compiler_params=_cparams("arbitrary", "arbitrary", "arbitrary"),
        name="mla_lat_attn",
    )(*args)


def even_layer(h, rows, B, SEQ, Bd, state_f, state_b, cache_k, cache_v, lb_f, lb_b, w_in, j, hgrn_g, rpb, layer):
    D = h.shape[1]
    AW = D // 2
    AH = BH = AW // HEAD_DIM
    T = rows.t
    p = matmul_wcast(h, w_in, j, tm=1024, tn=1024, out_dtype=F32, name="even_in_proj")
    o, s_fw, s_bw = hgrn(p, lb_f, lb_b, hgrn_g, h, nseq=B, T=SEQ, row0=0, A_heads=AH, layer=layer, hb=_fit(AH, 4),
                         emit_state=True)
    o = hgrn(p, lb_f, lb_b, hgrn_g, o, nseq=Bd, T=T, row0=rows.nc, A_heads=AH, layer=layer, hb=_fit(AH, 2),
             init=(state_f, state_b))
    o, new_k, new_v = dense_attn(p, o, nseq=B, T=SEQ, H=BH, col0=5 * AW, dst_col0=AW)
    past = cache_k.shape[1]
    bias = na_bias_tables(rpb, T // GRID_W)
    o = na_attn(p, cache_k.reshape(Bd * past, BH * HEAD_DIM), cache_v.reshape(Bd * past, BH * HEAD_DIM), bias, o,
                nseq=Bd, T=T, H=BH, row0=rows.nc, col0=5 * AW, dst_col0=AW, hp=_fit(BH, 2))
    return o, s_fw, s_bw, new_k, new_v


def odd_layer(h, rows, B, SEQ, Bd, cache_ckv, cache_kpe, w_in, q_norm_g, w_uq, kv_norm_g, w_ukv):
    D = h.shape[1]
    heads = D // 128
    T = rows.t
    qr, kvr = w_uq.shape[0], w_ukv.shape[0]
    scale = (QK_NOPE + QK_ROPE) ** -0.5
    tm = rows.tile(512)
    w_in_wide = jnp.concatenate([w_in, w_in[:, qr + kvr + ROPE_SWAP]], axis=1).astype(BF16)
    pr = matmul(h, w_in_wide, tm=1024, tn=w_in_wide.shape[1], out_dtype=F32, name="odd_in_proj")
    cos, sin = rope_tables(T, tm)
    cq, ckv32, ckv16, k2 = mla_mid(pr, q_norm_g, kv_norm_g, cos, sin, rows, qr=qr, kvr=kvr, tm=tm)
    q = mla_q(cq, widen_w_uq(w_uq, heads).astype(BF16), cos, sin, rows, heads=heads, scale=scale, tm=tm)
    w_ukv16 = w_ukv.astype(BF16)
    kv = matmul(ckv16, w_ukv16, tm=1024, tn=1024, out_dtype=BF16, name="mla_kv")
    past = cache_ckv.shape[1]
    kvc = matmul(cache_ckv.reshape(Bd * past, kvr).astype(BF16), w_ukv16, tm=1024, tn=1024, out_dtype=BF16,
                 name="mla_kv_cache")
    k2c = jnp.concatenate([jnp.zeros((Bd * past, LANES - QK_ROPE), F32), cache_kpe.reshape(Bd * past, QK_ROPE)],
                          axis=1).astype(BF16)
    o = mla_ctx_attn(q, kv, k2, h, nseq=B, T=SEQ, heads=heads)
    o = mla_lat_attn(q, kv, k2, kvc, k2c, o, nseq=Bd, T=T, heads=heads, row0=rows.nc, tq=rows.tile(2048), sub=512)
    new_ckv = ckv32[:rows.nc].reshape(B, SEQ, kvr)
    new_kpe = pr[:rows.nc, qr + kvr:qr + kvr + QK_ROPE].reshape(B, SEQ, QK_ROPE)
    return o, new_ckv, new_kpe


def kernel(x_prompt, x_sample, state_hgrn_fwd, state_hgrn_bwd, cache_na_k, cache_na_v, cache_mla_ckv, cache_mla_kpe, c, c_ctx, ada_w, ada_b, norm_g, hgrn_lb_fwd, hgrn_lb_bwd, w_in_even, hgrn_norm_g, na_rpb, w_out_even, w_in_odd, mla_q_norm_g, w_uq, mla_kv_norm_g, w_ukv, w_out_odd, mlp_w1, mlp_w2):
    B, SEQ, D = x_prompt.shape
    Bd, T, _ = x_sample.shape
    depth = ada_w.shape[0]
    rows = Rows(B * SEQ, Bd * T, T)
    x = (x_prompt.reshape(rows.nc, D), x_sample.reshape(rows.nl, D))
    cvec = jnp.zeros((MOD_ROWS, D), F32).at[0].set(c_ctx).at[1:1 + Bd].set(c)
    mod = modulation(cvec, ada_w, ada_b).reshape(depth, MOD_ROWS * 6, 1, D)

    w2 = mlp_w2.astype(BF16)
    new_sf, new_sb, new_nk, new_nv, new_ckv, new_kpe = [], [], [], [], [], []
    h = prenorm(*x, norm_g[0, 0], mod[0], 0, rows)
    for l in range(depth):
        j = l // 2
        if l % 2 == 0:
            o, sf, sb, nk, nv = even_layer(h, rows, B, SEQ, Bd, state_hgrn_fwd[:, j], state_hgrn_bwd[:, j],
                                           cache_na_k[:, j], cache_na_v[:, j], hgrn_lb_fwd, hgrn_lb_bwd,
                                           w_in_even, j, hgrn_norm_g[j], na_rpb[j], l)
            w_out = w_out_even
            new_sf.append(sf)
            new_sb.append(sb)
            new_nk.append(nk)
            new_nv.append(nv)
        else:
            o, ckv, kpe = odd_layer(h, rows, B, SEQ, Bd, cache_mla_ckv[:, j], cache_mla_kpe[:, j], w_in_odd[j],
                                    mla_q_norm_g[j], w_uq[j], mla_kv_norm_g[j], w_ukv[j])
            w_out = w_out_odd
            new_ckv.append(ckv)
            new_kpe.append(kpe)
        x, h = matmul_post(o, w_out.astype(BF16), j, x, norm_g[l, 1], mod[l], 0, rows,
                           pre=(norm_g[l, 2], mod[l], 3), name="out_proj_post")
        a = matmul_wcast(h, mlp_w1, l, tm=2048, tn=1024, out_dtype=BF16, act="relu2", vmem_limit=VMEM_LIMIT_BIG,
                         name="mlp_up")
        down = dict(tm=1024, vmem_limit=VMEM_LIMIT_BIG, name="mlp_down_post")
        if l + 1 < depth:
            x, h = matmul_post(a, w2, l, x, norm_g[l, 3], mod[l], 3, rows,
                               pre=(norm_g[l + 1, 0], mod[l + 1], 0), tk=1024, **down)
        else:
            x = matmul_post(a, w2, l, x, norm_g[l, 3], mod[l], 3, rows, split_out=True, tk=512, **down)
    return (x[0].reshape(B, SEQ, D), x[1].reshape(Bd, T, D),
            jnp.stack(new_sf, axis=1), jnp.stack(new_sb, axis=1), jnp.stack(new_nk, axis=1),
            jnp.stack(new_nv, axis=1), jnp.stack(new_ckv, axis=1), jnp.stack(new_kpe, axis=1))
```

```python
import functools

import numpy as np
import jax
import jax.numpy as jnp
from jax import lax
from jax.experimental import pallas as pl
from jax.experimental.pallas import tpu as pltpu

F32 = jnp.float32
BF16 = jnp.bfloat16

GRID_W = 64
HEAD_DIM = 128
NA_ROWS = 8
NA_COLS = 16
QK_NOPE = 128
QK_ROPE = 64
V_DIM = 128
ROPE_BASE = 10000.0
EPS = 1e-6
NEG = -1e30

LANES = 128
SUBLANES = 8
VMEM_LIMIT = 48 * 1024 * 1024
VMEM_LIMIT_BIG = 60 * 1024 * 1024
MOD_ROWS = 16

NT = (((1,), (1,)), ((), ()))
TN = (((0,), (0,)), ((), ()))


def _cparams(*sem, vmem_limit=VMEM_LIMIT):
    return pltpu.CompilerParams(dimension_semantics=sem, vmem_limit_bytes=vmem_limit)


def _fit(n, pref):
    t = min(pref, n)
    while n % t:
        t //= 2
    return t


def _silu(x):
    return x * jax.nn.sigmoid(x)


def _rms(x, g):
    return x * lax.rsqrt(jnp.mean(x * x, axis=-1, keepdims=True) + EPS) * g


def _softmax_pv(problems):
    ms = [functools.reduce(jnp.maximum, [jnp.max(s, axis=-1, keepdims=True) for s in scores])
          for scores, _ in problems]
    ps = [[jnp.exp(s - m) for s in scores] for (scores, _), m in zip(problems, ms)]
    ls = [functools.reduce(jnp.add, [jnp.sum(p, axis=-1, keepdims=True) for p in pp]) for pp in ps]
    os = [functools.reduce(jnp.add, [jnp.dot(p.astype(BF16), v, preferred_element_type=F32)
                                     for p, v in zip(pp, values)]) for pp, (_, values) in zip(ps, problems)]
    return [o / l for o, l in zip(os, ls)]


def _mod_kernel(c_ref, w_ref, b_ref, o_ref):
    s = _silu(c_ref[...]).astype(BF16)
    o_ref[...] = jnp.dot(s, w_ref[...].astype(BF16), preferred_element_type=F32) + b_ref[...]


def modulation(cvec, ada_w, ada_b, tn=512):
    L, D, N6 = ada_w.shape
    R = cvec.shape[0]
    tn = _fit(N6, tn)
    return pl.pallas_call(
        _mod_kernel,
        grid=(L, N6 // tn),
        in_specs=[pl.BlockSpec((R, D), lambda l, j: (0, 0)),
                  pl.BlockSpec((None, D, tn), lambda l, j: (l, 0, j)),
                  pl.BlockSpec((None, 1, tn), lambda l, j: (l, 0, j))],
        out_specs=pl.BlockSpec((None, R, tn), lambda l, j: (l, 0, j)),
        out_shape=jax.ShapeDtypeStruct((L, R, N6), F32),
        compiler_params=_cparams("arbitrary", "arbitrary"),
        name="modulation",
    )(cvec, ada_w, ada_b.reshape(L, 1, N6))


class Rows:
    def __init__(self, nc, nl, t):
        assert nc % t == 0, "latent sequences must start on a multiple of their length"
        self.nc, self.nl, self.t, self.n = nc, nl, t, nc + nl

    def tile(self, pref):
        return _fit(np.gcd(self.nc, self.t), pref)

    def mod_row(self, i, tm):
        ncb = self.nc // tm
        return jnp.where(i < ncb, 0, 1 + (i - ncb) // (self.t // tm))

    def pos_block(self, i, tm):
        ncb = self.nc // tm
        return jnp.where(i < ncb, self.t // tm, (i - ncb) % (self.t // tm))


def _mod_spec(rows, tm, k, D):
    return pl.BlockSpec((None, 1, D), lambda i, *_: (rows.mod_row(i, tm) * 6 + k, 0, 0))


def _vec_spec(D):
    return pl.BlockSpec((1, D), lambda i, *_: (0, 0))


def _split_specs(rows, tm, D):
    ncb = rows.nc // tm
    return [pl.BlockSpec((tm, D), lambda i, *_: (jnp.minimum(i, ncb - 1), 0)),
            pl.BlockSpec((tm, D), lambda i, *_: (jnp.maximum(i - ncb, 0), 0))]


def _into(dst, in_specs, args):
    return in_specs + [pl.BlockSpec(memory_space=pl.ANY)], args + [dst], {len(args): 0}


def _without_ref(kernel, idx):
    return lambda *refs: kernel(*refs[:idx], *refs[idx + 1:])


def _pre_kernel(xc_ref, xs_ref, g_ref, sh_ref, sc_ref, h_ref, *, ncb):
    x = jnp.where(pl.program_id(0) < ncb, xc_ref[...], xs_ref[...])
    h = _rms(x, g_ref[...]) * (1.0 + sc_ref[...]) + sh_ref[...]
    h_ref[...] = h.astype(h_ref.dtype)


def prenorm(xc, xs, g, mod3, k, rows, tm=512):
    D = xc.shape[1]
    tm = rows.tile(tm)
    return pl.pallas_call(
        functools.partial(_pre_kernel, ncb=rows.nc // tm),
        grid=(rows.n // tm,),
        in_specs=_split_specs(rows, tm, D) + [_vec_spec(D), _mod_spec(rows, tm, k, D),
                                              _mod_spec(rows, tm, k + 1, D)],
        out_specs=pl.BlockSpec((tm, D), lambda i: (i, 0)),
        out_shape=jax.ShapeDtypeStruct((rows.n, D), BF16),
        compiler_params=_cparams("arbitrary"),
        name="prenorm",
    )(xc, xs, g.reshape(1, D), mod3, mod3)


def _mm_kernel(x_ref, w_ref, o_ref, *scratch, nk, act):
    def finish(acc):
        if act == "relu2":
            a = jnp.maximum(acc, 0.0)
            acc = a * a
        o_ref[...] = acc.astype(o_ref.dtype)

    if nk == 1:
        finish(jnp.dot(x_ref[...], w_ref[...], preferred_element_type=F32))
        return
    acc_ref, = scratch
    k = pl.program_id(2)

    @pl.when(k == 0)
    def _():
        acc_ref[...] = jnp.zeros_like(acc_ref)

    acc_ref[...] += jnp.dot(x_ref[...], w_ref[...], preferred_element_type=F32)

    @pl.when(k == nk - 1)
    def _():
        finish(acc_ref[...])


def matmul(x, w, *, tm, tn, tk=None, out_dtype=F32, act=None, name="matmul"):
    M, K = x.shape
    _, N = w.shape
    tm, tn, tk = _fit(M, tm), _fit(N, tn), _fit(K, tk or K)
    nk = K // tk
    assert M % tm == 0 and N % tn == 0 and K % tk == 0
    return pl.pallas_call(
        functools.partial(_mm_kernel, nk=nk, act=act),
        grid=(M // tm, N // tn, nk),
        in_specs=[pl.BlockSpec((tm, tk), lambda i, j, k: (i, k)),
                  pl.BlockSpec((tk, tn), lambda i, j, k: (k, j))],
        out_specs=pl.BlockSpec((tm, tn), lambda i, j, k: (i, j)),
        out_shape=jax.ShapeDtypeStruct((M, N), out_dtype),
        scratch_shapes=[pltpu.VMEM((tm, tn), F32)] if nk > 1 else [],
        compiler_params=_cparams("arbitrary", "arbitrary", "arbitrary"),
        name=name,
    )(x, w)


def _mm_wcast_kernel(x_ref, w_ref, o_ref, wb_ref, *, act):
    @pl.when(pl.program_id(1) == 0)
    def _():
        wb_ref[...] = w_ref[...].astype(BF16)

    acc = jnp.dot(x_ref[...], wb_ref[...], preferred_element_type=F32)
    if act == "relu2":
        a = jnp.maximum(acc, 0.0)
        acc = a * a
    o_ref[...] = acc.astype(o_ref.dtype)


def matmul_wcast(x, w, layer, *, tm, tn, out_dtype=F32, act=None, vmem_limit=VMEM_LIMIT, name="matmul_wcast"):
    M, K = x.shape
    _, _, N = w.shape
    tm, tn = _fit(M, tm), _fit(N, tn)
    return pl.pallas_call(
        functools.partial(_mm_wcast_kernel, act=act),
        grid=(N // tn, M // tm),
        in_specs=[pl.BlockSpec((tm, K), lambda j, i: (i, 0)),
                  pl.BlockSpec((None, K, tn), lambda j, i: (layer, 0, j))],
        out_specs=pl.BlockSpec((tm, tn), lambda j, i: (i, j)),
        out_shape=jax.ShapeDtypeStruct((M, N), out_dtype),
        scratch_shapes=[pltpu.VMEM((K, tn), BF16)],
        compiler_params=_cparams("arbitrary", "arbitrary", vmem_limit=vmem_limit),
        name=name,
    )(x, w)


def _mm_post_kernel(*refs, nk, with_pre, sub, ncb, split_in, split_out):
    it = iter(refs)
    a_ref, w_ref = next(it), next(it)
    x_refs = [next(it) for _ in range(2 if split_in else 1)]
    gp_ref, gate_ref = next(it), next(it)
    gn_ref, sh_ref, sc_ref = (next(it), next(it), next(it)) if with_pre else (None, None, None)
    x1_refs = [next(it) for _ in range(2 if split_out else 1)]
    h_ref = next(it) if with_pre else None
    tm = a_ref.shape[0]
    is_ctx = pl.program_id(0) < ncb

    def each_out(fn):
        if split_out:
            pl.when(is_ctx)(lambda: fn(x1_refs[0]))
            pl.when(jnp.logical_not(is_ctx))(lambda: fn(x1_refs[1]))
        else:
            fn(x1_refs[0])

    def finish(x1_ref, r, y):
        x = jnp.where(is_ctx, x_refs[0][r, :], x_refs[1][r, :]) if split_in else x_refs[0][r, :]
        x1 = x + gate_ref[...] * _rms(y, gp_ref[...])
        x1_ref[r, :] = x1
        if with_pre:
            h_ref[r, :] = (_rms(x1, gn_ref[...]) * (1.0 + sc_ref[...]) + sh_ref[...]).astype(h_ref.dtype)

    def product(r):
        return jnp.dot(a_ref[r, :], w_ref[...], preferred_element_type=F32)

    blocks = [slice(s, s + sub) for s in range(0, tm, sub)]
    everything = slice(None)

    def single(x1_ref):
        for r in blocks:
            finish(x1_ref, r, product(r))

    def first(x1_ref):
        x1_ref[...] = product(everything)

    def accumulate(x1_ref):
        x1_ref[...] += product(everything)

    def last(x1_ref):
        for r in blocks:
            finish(x1_ref, r, x1_ref[r, :] + product(r))

    if nk == 1:
        each_out(single)
        return
    k = pl.program_id(1)
    pl.when(k == 0)(lambda: each_out(first))
    pl.when(jnp.logical_and(k > 0, k < nk - 1))(lambda: each_out(accumulate))
    pl.when(k == nk - 1)(lambda: each_out(last))


def matmul_post(a, w, layer, x, g_post, mod3, k, rows, pre=None, *, split_out=False, tm=512, tk=None, sub=256,
                vmem_limit=VMEM_LIMIT, name="matmul_post"):
    M, K = a.shape
    D = w.shape[2]
    tm = rows.tile(tm)
    tk = _fit(K, tk or K)
    nk = K // tk
    row_spec = pl.BlockSpec((tm, D), lambda i, kk: (i, 0))
    split_in = isinstance(x, tuple)
    x_specs, xs = (_split_specs(rows, tm, D), list(x)) if split_in else ([row_spec], [x])
    in_specs = [pl.BlockSpec((tm, tk), lambda i, kk: (i, kk)),
                pl.BlockSpec((None, tk, D), lambda i, kk: (layer, kk, 0)),
                *x_specs, _vec_spec(D), _mod_spec(rows, tm, k + 2, D)]
    args = [a, w, *xs, g_post.reshape(1, D), mod3]
    if split_out:
        out_specs = _split_specs(rows, tm, D)
        out_shape = [jax.ShapeDtypeStruct((rows.nc, D), F32), jax.ShapeDtypeStruct((rows.nl, D), F32)]
    else:
        out_specs = [row_spec]
        out_shape = [jax.ShapeDtypeStruct((M, D), F32)]
    if pre is not None:
        g_pre, mod3_pre, kp = pre
        in_specs += [_vec_spec(D), _mod_spec(rows, tm, kp, D), _mod_spec(rows, tm, kp + 1, D)]
        args += [g_pre.reshape(1, D), mod3_pre, mod3_pre]
        out_specs.append(row_spec)
        out_shape.append(jax.ShapeDtypeStruct((M, D), BF16))
    out = pl.pallas_call(
        functools.partial(_mm_post_kernel, nk=nk, with_pre=pre is not None, sub=_fit(tm, sub), ncb=rows.nc // tm,
                          split_in=split_in, split_out=split_out),
        grid=(M // tm, nk),
        in_specs=in_specs, out_specs=out_specs, out_shape=out_shape,
        compiler_params=_cparams("arbitrary", "arbitrary", vmem_limit=vmem_limit),
        name=name,
    )(*args)
    return out if len(out) > 1 else out[0]


HGRN_CHUNK = 64
HGRN_BLOCK = 256
HGRN_SAFE_EXPONENT = 60.0


def _hgrn_kernel(*refs, T, hb, layer, has_init, emit_state, unroll_blocks, unroll_scan):
    it = iter(refs)
    q_ref, ff_ref, fb_ref, i_ref, g_ref, lbf_ref, lbb_ref, ng_ref = (next(it) for _ in range(8))
    s0_refs = (next(it), next(it)) if has_init else None
    o_ref = next(it)
    s_out_refs = (next(it), next(it)) if emit_state else None
    qe_refs, ds_refs, oi_refs, ebt_refs = ((next(it), next(it)) for _ in range(4))
    C, R = HGRN_CHUNK, min(HGRN_BLOCK, T)
    nch, nblk, cpb, mid = T // C, T // R, R // C, C // 2
    f_refs, lb_refs = (ff_ref, fb_ref), (lbf_ref, lbb_ref)
    total_row = (C - 1, 0)

    def lower_bound(lb_ref, sl):
        z = lb_ref[:, sl]
        e = jnp.exp(z - jnp.max(z, axis=0, keepdims=True))
        sm = e / jnp.sum(e, axis=0, keepdims=True)
        return jnp.sum(sm[:layer + 1], axis=0, keepdims=True)

    row = lax.broadcasted_iota(jnp.int32, (R, R), 0)
    col = lax.broadcasted_iota(jnp.int32, (R, R), 1)
    same_chunk = (row // C) == (col // C)
    tri = (same_chunk & (row >= col), same_chunk & (row <= col))

    def cumsum(mask, x):
        hi = x.astype(BF16)
        r1 = x - hi.astype(F32)
        md = r1.astype(BF16)
        lo = (r1 - md.astype(F32)).astype(BF16)
        s = jnp.dot(mask.astype(BF16), jnp.concatenate([hi, md, lo], axis=1), preferred_element_type=F32)
        return s[:, :HEAD_DIM] + s[:, HEAD_DIM:2 * HEAD_DIM] + s[:, 2 * HEAD_DIM:]

    heads = [slice(h * HEAD_DIM, (h + 1) * HEAD_DIM) for h in range(hb)]
    lbs = [[lower_bound(lb_refs[d], sl) for sl in heads] for d in range(2)]
    chains = [(h, sl, d) for h, sl in enumerate(heads) for d in range(2)]

    def block_pass(blk, risk):
        r = pl.ds(pl.multiple_of(blk * R, R), R)
        qs, vb, kk, b, qm, km, kd, a = {}, {}, {}, {}, {}, {}, {}, {}
        for h, sl in enumerate(heads):
            qs[h] = _silu(q_ref[r, sl]) * HEAD_DIM ** -0.5
            vb[h] = i_ref[r, sl].astype(BF16)
        for h, sl, d in chains:
            lb = lbs[d][h]
            f = lb + (1.0 - lb) * jax.nn.sigmoid(f_refs[d][r, sl])
            kk[h, d] = 1.0 - f
            b[h, d] = cumsum(tri[d], jnp.log(f))
        for h, sl, d in chains:
            qmc, kmc, kd[h, d] = [], [], []
            for cc in range(cpb):
                cs = slice(cc * C, (cc + 1) * C)
                bc = b[h, d][cs]
                m = bc[mid:mid + 1]
                bt = bc[total_row[d]:total_row[d] + 1]
                qc = qs[h][cs] * jnp.exp(bc - m)
                kc = kk[h, d][cs] * jnp.exp(m - bc)
                risk = jnp.maximum(risk, jnp.maximum(jnp.abs(bc[:1] - m), jnp.abs(bc[C - 1:] - m)))
                rc = pl.ds(pl.multiple_of(blk * R + cc * C, C), C)
                qe_refs[d][rc, sl] = (qs[h][cs] * jnp.exp(bc)).astype(BF16)
                kd[h, d].append((kk[h, d][cs] * jnp.exp(bt - bc)).astype(BF16))
                slot = pl.ds(pl.multiple_of((blk * cpb + cc) * SUBLANES, SUBLANES), SUBLANES)
                ebt_refs[d][slot, sl] = jnp.broadcast_to(jnp.exp(bt), (SUBLANES, HEAD_DIM))
                qmc.append(qc.astype(BF16))
                kmc.append(kc.astype(BF16))
            qm[h, d] = jnp.concatenate(qmc, axis=0)
            km[h, d] = jnp.concatenate(kmc, axis=0)
        for h, sl, d in chains:
            s = lax.dot_general(qm[h, d], km[h, d], NT, preferred_element_type=F32)
            a[h, d] = jnp.where(tri[d], s, 0.0).astype(BF16)
        for h, sl, d in chains:
            oi_refs[d][r, sl] = jnp.dot(a[h, d], vb[h], preferred_element_type=F32)
        for h, sl, d in chains:
            for cc in range(cpb):
                rows_c = pl.ds(pl.multiple_of((blk * cpb + cc) * HEAD_DIM, HEAD_DIM), HEAD_DIM)
                ds_refs[d][rows_c, sl] = lax.dot_general(vb[h][cc * C:(cc + 1) * C], kd[h, d][cc], TN,
                                                         preferred_element_type=F32)
        return risk

    risk = lax.fori_loop(0, nblk, block_pass, jnp.zeros((1, HEAD_DIM), F32), unroll=unroll_blocks)

    def safe_scores(d, qs, kk, lf):
        scores = jnp.where(row == col, lax.dot_general(qs.astype(BF16), kk.astype(BF16), NT,
                                                       preferred_element_type=F32), 0.0)
        half = 1
        while half < C:
            same = (row // (2 * half)) == (col // (2 * half))
            r_up, c_up = (row % (2 * half)) >= half, (col % (2 * half)) >= half
            r_lo, c_lo = jnp.logical_not(r_up), jnp.logical_not(c_up)
            if d == 0:
                to_t, from_s, pair = same & r_up & c_up & (col <= row), same & r_lo & c_lo & (col > row), same & r_up & c_lo
            else:
                to_t, from_s, pair = same & r_lo & c_lo & (col >= row), same & r_up & c_up & (col < row), same & r_lo & c_up
            qd = (qs * jnp.exp(cumsum(to_t, lf))).astype(BF16)
            kd = (kk * jnp.exp(cumsum(from_s, lf))).astype(BF16)
            scores = scores + jnp.where(pair, lax.dot_general(qd, kd, NT, preferred_element_type=F32), 0.0)
            half *= 2
        return scores

    def safe_block(blk, carry):
        r = pl.ds(pl.multiple_of(blk * R, R), R)
        for h, sl, d in chains:
            lb = lbs[d][h]
            f = lb + (1.0 - lb) * jax.nn.sigmoid(f_refs[d][r, sl])
            scores = safe_scores(d, _silu(q_ref[r, sl]) * HEAD_DIM ** -0.5, 1.0 - f, jnp.log(f))
            oi_refs[d][r, sl] = jnp.dot(scores.astype(BF16), i_ref[r, sl].astype(BF16), preferred_element_type=F32)
        return carry

    @pl.when(jnp.max(risk) > HGRN_SAFE_EXPONENT)
    def _():
        lax.fori_loop(0, nblk, safe_block, 0)

    def scan_step(j, states):
        cs = [j, nch - 1 - j]
        rs = [pl.ds(pl.multiple_of(c * C, C), C) for c in cs]
        slots = [pl.ds(pl.multiple_of(c * SUBLANES, SUBLANES), SUBLANES) for c in cs]
        grow = [pl.ds(pl.multiple_of(c * HEAD_DIM, HEAD_DIM), HEAD_DIM) for c in cs]
        new = []
        for h, sl, d in chains:
            St = states[2 * h + d]
            oi_refs[d][rs[d], sl] += lax.dot_general(qe_refs[d][rs[d], sl], St.astype(BF16), NT,
                                                     preferred_element_type=F32)
            new.append(St * ebt_refs[d][slots[d], sl][:1] + ds_refs[d][grow[d], sl])
        return tuple(new)

    if has_init:
        init = tuple(s0_refs[d][h].T for h in range(hb) for d in range(2))
    else:
        init = tuple(jnp.zeros((HEAD_DIM, HEAD_DIM), F32) for _ in range(2 * hb))
    states = lax.fori_loop(0, nch, scan_step, init, unroll=unroll_scan)
    if emit_state:
        for h in range(hb):
            for d in range(2):
                s_out_refs[d][h] = states[2 * h + d].T

    def final_pass(blk, carry):
        r = pl.ds(pl.multiple_of(blk * R, R), R)
        for sl in heads:
            o = _rms(oi_refs[0][r, sl] + oi_refs[1][r, sl], ng_ref[...]) * _silu(g_ref[r, sl])
            o_ref[r, sl] = o.astype(o_ref.dtype)
        return carry

    lax.fori_loop(0, nblk, final_pass, 0)


def hgrn(p, lb_f, lb_b, norm_g, dst, *, nseq, T, row0, A_heads, layer, hb, init=None, emit_state=False,
         unroll_blocks=1, unroll_scan=4):
    assert T % min(HGRN_BLOCK, T) == 0 and A_heads % hb == 0
    rb0 = row0 // T
    H = A_heads
    W = hb * HEAD_DIM
    ng = H // hb

    def slab(k):
        return pl.BlockSpec((T, W), lambda b, h: (rb0 + b, k * ng + h))

    nl = lb_f.shape[0]
    in_specs = [slab(k) for k in range(5)] + [
        pl.BlockSpec((nl, W), lambda b, h: (0, h)),
        pl.BlockSpec((nl, W), lambda b, h: (0, h)),
        pl.BlockSpec((1, HEAD_DIM), lambda b, h: (0, 0))]
    args = [p] * 5 + [lb_f, lb_b, norm_g.reshape(1, HEAD_DIM)]
    state_spec = pl.BlockSpec((None, hb, HEAD_DIM, HEAD_DIM), lambda b, h: (b, h, 0, 0))
    if init is not None:
        in_specs += [state_spec, state_spec]
        args += list(init)
    in_specs, args, aliases = _into(dst, in_specs, args)
    out_specs = [pl.BlockSpec((T, W), lambda b, h: (rb0 + b, h))]
    out_shape = [jax.ShapeDtypeStruct(dst.shape, dst.dtype)]
    if emit_state:
        out_specs += [state_spec, state_spec]
        out_shape += [jax.ShapeDtypeStruct((nseq, H, HEAD_DIM, HEAD_DIM), F32)] * 2
    nch = T // HGRN_CHUNK
    scratch = ([pltpu.VMEM((T, W), BF16)] * 2
               + [pltpu.VMEM((nch * HEAD_DIM, W), F32)] * 2
               + [pltpu.VMEM((T, W), F32)] * 2
               + [pltpu.VMEM((nch * SUBLANES, W), F32)] * 2)
    body = functools.partial(_hgrn_kernel, T=T, hb=hb, layer=layer, has_init=init is not None, emit_state=emit_state,
                             unroll_blocks=unroll_blocks, unroll_scan=unroll_scan)
    out = pl.pallas_call(
        _without_ref(body, len(args) - 1),
        grid=(nseq, ng),
        in_specs=in_specs, out_specs=out_specs, out_shape=out_shape,
        input_output_aliases=aliases,
        scratch_shapes=scratch,
        compiler_params=_cparams("arbitrary", "arbitrary"),
        name="hgrn",
    )(*args)
    return out if emit_state else out[0]


def _dense_attn_kernel(q_ref, k_ref, v_ref, o_ref, kout_ref, vout_ref, *, H, scale):
    heads = [slice(h * HEAD_DIM, (h + 1) * HEAD_DIM) for h in range(H)]
    problems = []
    for h, sl in enumerate(heads):
        kout_ref[:, h, :] = k_ref[:, sl]
        vout_ref[:, h, :] = v_ref[:, sl]
        q = (q_ref[:, sl] * scale).astype(BF16)
        s = lax.dot_general(q, k_ref[:, sl].astype(BF16), NT, preferred_element_type=F32)
        problems.append(([s], [v_ref[:, sl].astype(BF16)]))
    for sl, o in zip(heads, _softmax_pv(problems)):
        o_ref[:, sl] = o.astype(o_ref.dtype)


def dense_attn(p, dst, *, nseq, T, H, col0, dst_col0):
    W = H * HEAD_DIM
    cb = col0 // W
    in_specs, args, aliases = _into(dst, [pl.BlockSpec((T, W), lambda b, k=k: (b, cb + k)) for k in range(3)],
                                    [p, p, p])
    cache_spec = pl.BlockSpec((None, T, H, HEAD_DIM), lambda b: (b, 0, 0, 0))
    cache_shape = jax.ShapeDtypeStruct((nseq, T, H, HEAD_DIM), p.dtype)
    return pl.pallas_call(
        _without_ref(functools.partial(_dense_attn_kernel, H=H, scale=HEAD_DIM ** -0.5), 3),
        grid=(nseq,),
        in_specs=in_specs,
        out_specs=[pl.BlockSpec((T, W), lambda b: (b, dst_col0 // W)), cache_spec, cache_spec],
        out_shape=[jax.ShapeDtypeStruct(dst.shape, dst.dtype), cache_shape, cache_shape],
        input_output_aliases=aliases,
        compiler_params=_cparams("arbitrary"),
        name="dense_attn",
    )(*args)


NA_QROWS = 4
NA_KROWS = 12


def na_bias_tables(rpb, rows):
    W = GRID_W
    kr = min(NA_ROWS, rows)
    assert kr == NA_ROWS and rows >= NA_KROWS and rows % NA_QROWS == 0
    nblk = rows // NA_QROWS
    col = np.arange(W)
    cs = np.clip(col - NA_COLS // 2, 0, W - NA_COLS)
    col_ok = (col[None, :] >= cs[:, None]) & (col[None, :] < cs[:, None] + NA_COLS)
    ci = np.clip(col[None, :] - col[:, None] + NA_COLS - 1, 0, 2 * NA_COLS - 2)
    col_sel = (ci[..., None] == np.arange(2 * NA_COLS - 1)).astype(np.float32)
    G = jnp.einsum("hac,xyc->haxy", rpb.astype(F32), col_sel, precision=lax.Precision.HIGHEST)
    G = jnp.where(col_ok, G, NEG)
    outside = 2 * NA_ROWS - 1
    G = jnp.concatenate([G, jnp.full_like(G[:, :1], NEG)], axis=1)
    tables = []
    for blk in (0, 1, nblk - 1):
        r0 = blk * NA_QROWS
        u0 = min(max(r0 - NA_ROWS // 2, 0), rows - NA_KROWS)
        r = r0 + np.arange(NA_QROWS)
        ka = u0 + np.arange(NA_KROWS)
        start = np.clip(r - kr // 2, 0, rows - kr)
        row_ok = (ka[None, :] >= start[:, None]) & (ka[None, :] < start[:, None] + kr)
        ri = np.where(row_ok, ka[None, :] - r[:, None] + NA_ROWS - 1, outside)
        tables.append(jnp.concatenate(
            [jnp.concatenate([G[:, int(a)] for a in ri[q]], axis=-1) for q in range(NA_QROWS)], axis=-2))
    return jnp.stack(tables)


def _na_kernel(q_ref, k_ref, v_ref, kc_ref, vc_ref, bias_ref, o_ref, *, rows, scale, hp):
    blk = pl.program_id(2)
    u0 = jnp.clip(blk * NA_QROWS - NA_ROWS // 2, 0, rows - NA_KROWS)
    band = pl.ds(pl.multiple_of(u0 * GRID_W, GRID_W), NA_KROWS * GRID_W)
    heads = [slice(h * HEAD_DIM, (h + 1) * HEAD_DIM) for h in range(hp)]
    problems = []
    for h, sl in enumerate(heads):
        q = (q_ref[:, sl] * scale).astype(BF16)
        s_lat = lax.dot_general(q, k_ref[band, sl].astype(BF16), NT, preferred_element_type=F32) + bias_ref[h]
        s_ctx = lax.dot_general(q, kc_ref[:, sl].astype(BF16), NT, preferred_element_type=F32)
        problems.append(([s_lat, s_ctx], [v_ref[band, sl].astype(BF16), vc_ref[:, sl].astype(BF16)]))
    for sl, o in zip(heads, _softmax_pv(problems)):
        o_ref[:, sl] = o.astype(o_ref.dtype)


def na_attn(p, kc, vc, bias, dst, *, nseq, T, H, row0, col0, dst_col0, hp):
    rows = T // GRID_W
    nblk = rows // NA_QROWS
    tq = NA_QROWS * GRID_W
    L = kc.shape[0] // nseq
    W = hp * HEAD_DIM
    qb0, kb0, cb = row0 // tq, row0 // T, col0 // W
    ng = H // hp

    def kind(blk):
        return jnp.where(blk == 0, 0, jnp.where(blk == nblk - 1, 2, 1))

    in_specs, args, aliases = _into(dst, [
        pl.BlockSpec((tq, W), lambda b, h, i: (qb0 + b * nblk + i, cb + h)),
        pl.BlockSpec((T, W), lambda b, h, i: (kb0 + b, cb + ng + h)),
        pl.BlockSpec((T, W), lambda b, h, i: (kb0 + b, cb + 2 * ng + h)),
        pl.BlockSpec((L, W), lambda b, h, i: (b, h)),
        pl.BlockSpec((L, W), lambda b, h, i: (b, h)),
        pl.BlockSpec((None, hp, tq, NA_KROWS * GRID_W), lambda b, h, i: (kind(i), h, 0, 0))],
        [p, p, p, kc, vc, bias])
    return pl.pallas_call(
        _without_ref(functools.partial(_na_kernel, rows=rows, scale=HEAD_DIM ** -0.5, hp=hp), 6),
        grid=(nseq, ng, nblk),
        in_specs=in_specs,
        out_specs=pl.BlockSpec((tq, W), lambda b, h, i: (qb0 + b * nblk + i, dst_col0 // W + h)),
        out_shape=jax.ShapeDtypeStruct(dst.shape, dst.dtype),
        input_output_aliases=aliases,
        compiler_params=_cparams("arbitrary", "arbitrary", "arbitrary"),
        name="na_attn",
    )(*args)


ROPE_SWAP = np.concatenate([np.arange(16, 32), np.arange(0, 16), np.arange(48, 64), np.arange(32, 48)])


def rope_tables(T, tm):
    t = jnp.arange(T)
    half = QK_ROPE // 2
    inv = jnp.power(ROPE_BASE, -jnp.arange(0, half, 2, dtype=F32) / half)
    ang_r = (t // GRID_W).astype(F32)[:, None] * inv
    ang_c = (t % GRID_W).astype(F32)[:, None] * inv
    cos = jnp.concatenate([jnp.cos(ang_r), jnp.cos(ang_r), jnp.cos(ang_c), jnp.cos(ang_c)], axis=-1)
    sin = jnp.concatenate([-jnp.sin(ang_r), jnp.sin(ang_r), -jnp.sin(ang_c), jnp.sin(ang_c)], axis=-1)
    cos = jnp.concatenate([cos, jnp.ones((T, LANES - QK_ROPE), F32)], axis=-1)
    sin = jnp.concatenate([sin, jnp.zeros((T, LANES - QK_ROPE), F32)], axis=-1)
    cos = jnp.concatenate([cos, jnp.ones((tm, LANES), F32)], axis=0)
    sin = jnp.concatenate([sin, jnp.zeros((tm, LANES), F32)], axis=0)
    return cos, sin


def _mla_mid_kernel(pr_ref, qg_ref, kvg_ref, cos_ref, sin_ref, cq_ref, ckv32_ref, ckv16_ref, k2_ref, *, qr, kvr):
    cq_ref[...] = _rms(pr_ref[:, :qr], qg_ref[...]).astype(cq_ref.dtype)
    ckv = _rms(pr_ref[:, qr:qr + kvr], kvg_ref[...])
    ckv32_ref[...] = ckv
    ckv16_ref[...] = ckv.astype(ckv16_ref.dtype)
    x = pr_ref[:, qr + kvr:]
    rot = x * cos_ref[...] + pltpu.roll(x, LANES // 2, axis=1) * sin_ref[...]
    lane = lax.broadcasted_iota(jnp.int32, rot.shape, 1)
    k2_ref[...] = jnp.where(lane < QK_ROPE, rot, 0.0).astype(k2_ref.dtype)


def mla_mid(pr, q_norm_g, kv_norm_g, cos, sin, rows, *, qr, kvr, tm):
    n = pr.shape[0]
    pos = pl.BlockSpec((tm, LANES), lambda i: (rows.pos_block(i, tm), 0))

    def out(w):
        return pl.BlockSpec((tm, w), lambda i: (i, 0))

    return pl.pallas_call(
        functools.partial(_mla_mid_kernel, qr=qr, kvr=kvr),
        grid=(n // tm,),
        in_specs=[pl.BlockSpec((tm, pr.shape[1]), lambda i: (i, 0)), _vec_spec(qr), _vec_spec(kvr), pos, pos],
        out_specs=[out(qr), out(kvr), out(kvr), out(LANES)],
        out_shape=[jax.ShapeDtypeStruct((n, qr), BF16), jax.ShapeDtypeStruct((n, kvr), F32),
                   jax.ShapeDtypeStruct((n, kvr), BF16), jax.ShapeDtypeStruct((n, LANES), BF16)],
        compiler_params=_cparams("arbitrary"),
        name="mla_mid",
    )(pr, q_norm_g.reshape(1, qr), kv_norm_g.reshape(1, kvr), cos, sin)


MLA_Q_IN = 3 * LANES
MLA_Q_OUT = 2 * LANES


def widen_w_uq(w_uq, heads):
    r = w_uq.shape[0]
    w = w_uq.reshape(r, heads, QK_NOPE + QK_ROPE)
    nope, pe = w[..., :QK_NOPE], w[..., QK_NOPE:]
    return jnp.concatenate([nope, pe, pe, pe[..., ROPE_SWAP], jnp.zeros_like(pe)], axis=-1).reshape(r, heads * MLA_Q_IN)


def _mla_q_kernel(x_ref, w_ref, cos_ref, sin_ref, o_ref, *, hpt, scale):
    acc = jnp.dot(x_ref[...], w_ref[...], preferred_element_type=F32)
    cos, sin = cos_ref[...], sin_ref[...]
    for j in range(hpt):
        a = acc[:, j * MLA_Q_IN:(j + 1) * MLA_Q_IN]
        o_ref[:, j * MLA_Q_OUT:j * MLA_Q_OUT + LANES] = (a[:, :LANES] * scale).astype(o_ref.dtype)
        q2 = a[:, LANES:2 * LANES] * cos + a[:, 2 * LANES:] * sin
        o_ref[:, j * MLA_Q_OUT + LANES:(j + 1) * MLA_Q_OUT] = (q2 * scale).astype(o_ref.dtype)


def mla_q(cq, w_uq_wide, cos, sin, rows, *, heads, scale, tm, hpt=4):
    n, r = cq.shape
    pos = pl.BlockSpec((tm, LANES), lambda i, j: (rows.pos_block(i, tm), 0))
    return pl.pallas_call(
        functools.partial(_mla_q_kernel, hpt=hpt, scale=scale),
        grid=(n // tm, heads // hpt),
        in_specs=[pl.BlockSpec((tm, r), lambda i, j: (i, 0)),
                  pl.BlockSpec((r, hpt * MLA_Q_IN), lambda i, j: (0, j)), pos, pos],
        out_specs=pl.BlockSpec((tm, hpt * MLA_Q_OUT), lambda i, j: (i, j)),
        out_shape=jax.ShapeDtypeStruct((n, heads * MLA_Q_OUT), BF16),
        compiler_params=_cparams("arbitrary", "arbitrary"),
        name="mla_q",
    )(cq, w_uq_wide, cos, sin)


def _mla_ctx_attn_kernel(q_ref, kv_ref, k2_ref, o_ref, *, heads):
    k2 = k2_ref[...]
    problems = []
    for h in range(heads):
        q = q_ref[:, h * MLA_Q_OUT:(h + 1) * MLA_Q_OUT]
        k = jnp.concatenate([kv_ref[:, 2 * h * LANES:(2 * h + 1) * LANES], k2], axis=1)
        s = lax.dot_general(q, k, NT, preferred_element_type=F32)
        problems.append(([s], [kv_ref[:, (2 * h + 1) * LANES:(2 * h + 2) * LANES]]))
    for h, o in enumerate(_softmax_pv(problems)):
        o_ref[:, h * V_DIM:(h + 1) * V_DIM] = o.astype(o_ref.dtype)


def mla_ctx_attn(q, kv, k2, dst, *, nseq, T, heads):
    in_specs, args, aliases = _into(dst, [pl.BlockSpec((T, heads * MLA_Q_OUT), lambda b: (b, 0)),
                                          pl.BlockSpec((T, kv.shape[1]), lambda b: (b, 0)),
                                          pl.BlockSpec((T, LANES), lambda b: (b, 0))], [q, kv, k2])
    return pl.pallas_call(
        _without_ref(functools.partial(_mla_ctx_attn_kernel, heads=heads), 3),
        grid=(nseq,),
        in_specs=in_specs,
        out_specs=pl.BlockSpec((T, heads * V_DIM), lambda b: (b, 0)),
        out_shape=jax.ShapeDtypeStruct(dst.shape, dst.dtype),
        input_output_aliases=aliases,
        compiler_params=_cparams("arbitrary"),
        name="mla_ctx_attn",
    )(*args)


def _mla_lat_attn_kernel(q_ref, k1_ref, v_ref, k2_ref, k1c_ref, vc_ref, k2c_ref, o_ref, *, sub):
    k = jnp.concatenate([k1_ref[...], k2_ref[...]], axis=1)
    kc = jnp.concatenate([k1c_ref[...], k2c_ref[...]], axis=1)
    blocks = [slice(s, s + sub) for s in range(0, q_ref.shape[0], sub)]
    problems = []
    for r in blocks:
        q = q_ref[r, :]
        s_lat = lax.dot_general(q, k, NT, preferred_element_type=F32)
        s_ctx = lax.dot_general(q, kc, NT, preferred_element_type=F32)
        problems.append(([s_lat, s_ctx], [v_ref[...], vc_ref[...]]))
    for r, o in zip(blocks, _softmax_pv(problems)):
        o_ref[r, :] = o.astype(o_ref.dtype)


def mla_lat_attn(q, kv, k2, kvc, k2c, dst, *, nseq, T, heads, row0, tq, sub):
    nq = T // tq
    P = kvc.shape[0] // nseq
    qb0, kb0 = row0 // tq, row0 // T
    in_specs, args, aliases = _into(dst, [
        pl.BlockSpec((tq, MLA_Q_OUT), lambda b, h, i: (qb0 + b * nq + i, h)),
        pl.BlockSpec((T, LANES), lambda b, h, i: (kb0 + b, 2 * h)),
        pl.BlockSpec((T, LANES), lambda b, h, i: (kb0 + b, 2 * h + 1)),
        pl.BlockSpec((T, LANES), lambda b, h, i: (kb0 + b, 0)),
        pl.BlockSpec((P, LANES), lambda b, h, i: (b, 2 * h)),
        pl.BlockSpec((P, LANES), lambda b, h, i: (b, 2 * h + 1)),
        pl.BlockSpec((P, LANES), lambda b, h, i: (b, 0))],
        [q, kv, kv, k2, kvc, kvc, k2c])
    return pl.pallas_call(
        _without_ref(functools.partial(_mla_lat_attn_kernel, sub=_fit(tq, sub)), 7),
        grid=(nseq, heads, nq),
        in_specs=in_specs,
        out_specs=pl.BlockSpec((tq, V_DIM), lambda b, h, i: (qb0 + b * nq + i, h)),
        out_shape=jax.ShapeDtypeStruct(dst.shape, dst.dtype),
        input_output_aliases=aliases,
        compiler_params=_cparams("arbitrary", "arbitrary", "arbitrary"),
        name="mla_lat_attn",
    )(*args)


def even_layer(h, rows, B, SEQ, Bd, state_f, state_b, cache_k, cache_v, lb_f, lb_b, w_in, j, hgrn_g, rpb, layer):
    D = h.shape[1]
    AW = D // 2
    AH = BH = AW // HEAD_DIM
    T = rows.t
    p = matmul_wcast(h, w_in, j, tm=1024, tn=1024, out_dtype=F32, name="even_in_proj")
    o, s_fw, s_bw = hgrn(p, lb_f, lb_b, hgrn_g, h, nseq=B, T=SEQ, row0=0, A_heads=AH, layer=layer, hb=_fit(AH, 4),
                         emit_state=True)
    o = hgrn(p, lb_f, lb_b, hgrn_g, o, nseq=Bd, T=T, row0=rows.nc, A_heads=AH, layer=layer, hb=_fit(AH, 2),
             init=(state_f, state_b))
    o, new_k, new_v = dense_attn(p, o, nseq=B, T=SEQ, H=BH, col0=5 * AW, dst_col0=AW)
    past = cache_k.shape[1]
    bias = na_bias_tables(rpb, T // GRID_W)
    o = na_attn(p, cache_k.reshape(Bd * past, BH * HEAD_DIM), cache_v.reshape(Bd * past, BH * HEAD_DIM), bias, o,
                nseq=Bd, T=T, H=BH, row0=rows.nc, col0=5 * AW, dst_col0=AW, hp=_fit(BH, 4))
    return o, s_fw, s_bw, new_k, new_v


def odd_layer(h, rows, B, SEQ, Bd, cache_ckv, cache_kpe, w_in, q_norm_g, w_uq, kv_norm_g, w_ukv):
    D = h.shape[1]
    heads = D // 128
    T = rows.t
    qr, kvr = w_uq.shape[0], w_ukv.shape[0]
    scale = (QK_NOPE + QK_ROPE) ** -0.5
    tm = rows.tile(512)
    w_in_wide = jnp.concatenate([w_in, w_in[:, qr + kvr + ROPE_SWAP]], axis=1).astype(BF16)
    pr = matmul(h, w_in_wide, tm=1024, tn=w_in_wide.shape[1], out_dtype=F32, name="odd_in_proj")
    tm_q = rows.tile(1024)
    cos, sin = rope_tables(T, max(tm, tm_q))
    cq, ckv32, ckv16, k2 = mla_mid(pr, q_norm_g, kv_norm_g, cos, sin, rows, qr=qr, kvr=kvr, tm=tm)
    q = mla_q(cq, widen_w_uq(w_uq, heads).astype(BF16), cos, sin, rows, heads=heads, scale=scale, tm=tm_q,
              hpt=_fit(heads, 8))
    w_ukv16 = w_ukv.astype(BF16)
    kv = matmul(ckv16, w_ukv16, tm=1024, tn=1024, out_dtype=BF16, name="mla_kv")
    past = cache_ckv.shape[1]
    kvc = matmul(cache_ckv.reshape(Bd * past, kvr).astype(BF16), w_ukv16, tm=1024, tn=1024, out_dtype=BF16,
                 name="mla_kv_cache")
    k2c = jnp.concatenate([jnp.zeros((Bd * past, LANES - QK_ROPE), F32), cache_kpe.reshape(Bd * past, QK_ROPE)],
                          axis=1).astype(BF16)
    o = mla_ctx_attn(q, kv, k2, h, nseq=B, T=SEQ, heads=heads)
    o = mla_lat_attn(q, kv, k2, kvc, k2c, o, nseq=Bd, T=T, heads=heads, row0=rows.nc, tq=rows.tile(2048), sub=512)
    new_ckv = ckv32[:rows.nc].reshape(B, SEQ, kvr)
    new_kpe = pr[:rows.nc, qr + kvr:qr + kvr + QK_ROPE].reshape(B, SEQ, QK_ROPE)
    return o, new_ckv, new_kpe


def kernel(x_prompt, x_sample, state_hgrn_fwd, state_hgrn_bwd, cache_na_k, cache_na_v, cache_mla_ckv, cache_mla_kpe, c, c_ctx, ada_w, ada_b, norm_g, hgrn_lb_fwd, hgrn_lb_bwd, w_in_even, hgrn_norm_g, na_rpb, w_out_even, w_in_odd, mla_q_norm_g, w_uq, mla_kv_norm_g, w_ukv, w_out_odd, mlp_w1, mlp_w2):
    B, SEQ, D = x_prompt.shape
    Bd, T, _ = x_sample.shape
    depth = ada_w.shape[0]
    rows = Rows(B * SEQ, Bd * T, T)
    x = (x_prompt.reshape(rows.nc, D), x_sample.reshape(rows.nl, D))
    cvec = jnp.zeros((MOD_ROWS, D), F32).at[0].set(c_ctx).at[1:1 + Bd].set(c)
    mod = modulation(cvec, ada_w, ada_b).reshape(depth, MOD_ROWS * 6, 1, D)

    w2 = mlp_w2.astype(BF16)
    new_sf, new_sb, new_nk, new_nv, new_ckv, new_kpe = [], [], [], [], [], []
    h = prenorm(*x, norm_g[0, 0], mod[0], 0, rows)
    for l in range(depth):
        j = l // 2
        if l % 2 == 0:
            o, sf, sb, nk, nv = even_layer(h, rows, B, SEQ, Bd, state_hgrn_fwd[:, j], state_hgrn_bwd[:, j],
                                           cache_na_k[:, j], cache_na_v[:, j], hgrn_lb_fwd, hgrn_lb_bwd,
                                           w_in_even, j, hgrn_norm_g[j], na_rpb[j], l)
            w_out = w_out_even
            new_sf.append(sf)
            new_sb.append(sb)
            new_nk.append(nk)
            new_nv.append(nv)
        else:
            o, ckv, kpe = odd_layer(h, rows, B, SEQ, Bd, cache_mla_ckv[:, j], cache_mla_kpe[:, j], w_in_odd[j],
                                    mla_q_norm_g[j], w_uq[j], mla_kv_norm_g[j], w_ukv[j])
            w_out = w_out_odd
            new_ckv.append(ckv)
            new_kpe.append(kpe)
        x, h = matmul_post(o, w_out.astype(BF16), j, x, norm_g[l, 1], mod[l], 0, rows,
                           pre=(norm_g[l, 2], mod[l], 3), name="out_proj_post")
        a = matmul_wcast(h, mlp_w1, l, tm=2048, tn=1024, out_dtype=BF16, act="relu2", vmem_limit=VMEM_LIMIT_BIG,
                         name="mlp_up")
        down = dict(tm=1024, vmem_limit=VMEM_LIMIT_BIG, name="mlp_down_post")
        if l + 1 < depth:
            x, h = matmul_post(a, w2, l, x, norm_g[l, 3], mod[l], 3, rows,
                               pre=(norm_g[l + 1, 0], mod[l + 1], 0), tk=1024, **down)
        else:
            x = matmul_post(a, w2, l, x, norm_g[l, 3], mod[l], 3, rows, split_out=True, tk=512, **down)
    return (x[0].reshape(B, SEQ, D), x[1].reshape(Bd, T, D),
            jnp.stack(new_sf, axis=1), jnp.stack(new_sb, axis=1), jnp.stack(new_nk, axis=1),
            jnp.stack(new_nv, axis=1), jnp.stack(new_ckv, axis=1), jnp.stack(new_kpe, axis=1))
```

```python
import functools

import numpy as np
import jax
import jax.numpy as jnp
from jax import lax
from jax.experimental import pallas as pl
from jax.experimental.pallas import tpu as pltpu

F32 = jnp.float32
BF16 = jnp.bfloat16

GRID_W = 64
HEAD_DIM = 128
NA_ROWS = 8
NA_COLS = 16
QK_NOPE = 128
QK_ROPE = 64
V_DIM = 128
ROPE_BASE = 10000.0
EPS = 1e-6
NEG = -1e30

LANES = 128
SUBLANES = 8
VMEM_LIMIT = 48 * 1024 * 1024
VMEM_LIMIT_BIG = 60 * 1024 * 1024
MOD_ROWS = 16

NT = (((1,), (1,)), ((), ()))
TN = (((0,), (0,)), ((), ()))


def _cparams(*sem, vmem_limit=VMEM_LIMIT):
    return pltpu.CompilerParams(dimension_semantics=sem, vmem_limit_bytes=vmem_limit)


def _fit(n, pref):
    t = min(pref, n)
    while n % t:
        t //= 2
    return t


def _silu(x):
    return x * jax.nn.sigmoid(x)


def _rms(x, g):
    return x * lax.rsqrt(jnp.mean(x * x, axis=-1, keepdims=True) + EPS) * g


def _softmax_pv(problems):
    ms = [functools.reduce(jnp.maximum, [jnp.max(s, axis=-1, keepdims=True) for s in scores])
          for scores, _ in problems]
    ps = [[jnp.exp(s - m) for s in scores] for (scores, _), m in zip(problems, ms)]
    ls = [functools.reduce(jnp.add, [jnp.sum(p, axis=-1, keepdims=True) for p in pp]) for pp in ps]
    os = [functools.reduce(jnp.add, [jnp.dot(p.astype(BF16), v, preferred_element_type=F32)
                                     for p, v in zip(pp, values)]) for pp, (_, values) in zip(ps, problems)]
    return [o / l for o, l in zip(os, ls)]


def _mod_kernel(c_ref, w_ref, b_ref, o_ref):
    s = _silu(c_ref[...]).astype(BF16)
    o_ref[...] = jnp.dot(s, w_ref[...].astype(BF16), preferred_element_type=F32) + b_ref[...]


def modulation(cvec, ada_w, ada_b, tn=512):
    L, D, N6 = ada_w.shape
    R = cvec.shape[0]
    tn = _fit(N6, tn)
    return pl.pallas_call(
        _mod_kernel,
        grid=(L, N6 // tn),
        in_specs=[pl.BlockSpec((R, D), lambda l, j: (0, 0)),
                  pl.BlockSpec((None, D, tn), lambda l, j: (l, 0, j)),
                  pl.BlockSpec((None, 1, tn), lambda l, j: (l, 0, j))],
        out_specs=pl.BlockSpec((None, R, tn), lambda l, j: (l, 0, j)),
        out_shape=jax.ShapeDtypeStruct((L, R, N6), F32),
        compiler_params=_cparams("arbitrary", "arbitrary"),
        name="modulation",
    )(cvec, ada_w, ada_b.reshape(L, 1, N6))


class Rows:
    def __init__(self, nc, nl, t):
        assert nc % t == 0, "latent sequences must start on a multiple of their length"
        self.nc, self.nl, self.t, self.n = nc, nl, t, nc + nl

    def tile(self, pref):
        return _fit(np.gcd(self.nc, self.t), pref)

    def mod_row(self, i, tm):
        ncb = self.nc // tm
        return jnp.where(i < ncb, 0, 1 + (i - ncb) // (self.t // tm))

    def pos_block(self, i, tm):
        ncb = self.nc // tm
        return jnp.where(i < ncb, self.t // tm, (i - ncb) % (self.t // tm))


def _mod_spec(rows, tm, k, D):
    return pl.BlockSpec((None, 1, D), lambda i, *_: (rows.mod_row(i, tm) * 6 + k, 0, 0))


def _vec_spec(D):
    return pl.BlockSpec((1, D), lambda i, *_: (0, 0))


def _split_specs(rows, tm, D):
    ncb = rows.nc // tm
    return [pl.BlockSpec((tm, D), lambda i, *_: (jnp.minimum(i, ncb - 1), 0)),
            pl.BlockSpec((tm, D), lambda i, *_: (jnp.maximum(i - ncb, 0), 0))]


def _into(dst, in_specs, args):
    return in_specs + [pl.BlockSpec(memory_space=pl.ANY)], args + [dst], {len(args): 0}


def _without_ref(kernel, idx):
    return lambda *refs: kernel(*refs[:idx], *refs[idx + 1:])


def _pre_kernel(xc_ref, xs_ref, g_ref, sh_ref, sc_ref, h_ref, *, ncb):
    x = jnp.where(pl.program_id(0) < ncb, xc_ref[...], xs_ref[...])
    h = _rms(x, g_ref[...]) * (1.0 + sc_ref[...]) + sh_ref[...]
    h_ref[...] = h.astype(h_ref.dtype)


def prenorm(xc, xs, g, mod3, k, rows, tm=512):
    D = xc.shape[1]
    tm = rows.tile(tm)
    return pl.pallas_call(
        functools.partial(_pre_kernel, ncb=rows.nc // tm),
        grid=(rows.n // tm,),
        in_specs=_split_specs(rows, tm, D) + [_vec_spec(D), _mod_spec(rows, tm, k, D),
                                              _mod_spec(rows, tm, k + 1, D)],
        out_specs=pl.BlockSpec((tm, D), lambda i: (i, 0)),
        out_shape=jax.ShapeDtypeStruct((rows.n, D), BF16),
        compiler_params=_cparams("arbitrary"),
        name="prenorm",
    )(xc, xs, g.reshape(1, D), mod3, mod3)


def _mm_kernel(x_ref, w_ref, o_ref, *scratch, nk, act):
    def finish(acc):
        if act == "relu2":
            a = jnp.maximum(acc, 0.0)
            acc = a * a
        o_ref[...] = acc.astype(o_ref.dtype)

    if nk == 1:
        finish(jnp.dot(x_ref[...], w_ref[...], preferred_element_type=F32))
        return
    acc_ref, = scratch
    k = pl.program_id(2)

    @pl.when(k == 0)
    def _():
        acc_ref[...] = jnp.zeros_like(acc_ref)

    acc_ref[...] += jnp.dot(x_ref[...], w_ref[...], preferred_element_type=F32)

    @pl.when(k == nk - 1)
    def _():
        finish(acc_ref[...])


def matmul(x, w, *, tm, tn, tk=None, out_dtype=F32, act=None, name="matmul"):
    M, K = x.shape
    _, N = w.shape
    tm, tn, tk = _fit(M, tm), _fit(N, tn), _fit(K, tk or K)
    nk = K // tk
    assert M % tm == 0 and N % tn == 0 and K % tk == 0
    return pl.pallas_call(
        functools.partial(_mm_kernel, nk=nk, act=act),
        grid=(M // tm, N // tn, nk),
        in_specs=[pl.BlockSpec((tm, tk), lambda i, j, k: (i, k)),
                  pl.BlockSpec((tk, tn), lambda i, j, k: (k, j))],
        out_specs=pl.BlockSpec((tm, tn), lambda i, j, k: (i, j)),
        out_shape=jax.ShapeDtypeStruct((M, N), out_dtype),
        scratch_shapes=[pltpu.VMEM((tm, tn), F32)] if nk > 1 else [],
        compiler_params=_cparams("arbitrary", "arbitrary", "arbitrary"),
        name=name,
    )(x, w)


def _mm_wcast_kernel(x_ref, w_ref, o_ref, wb_ref, *, act):
    @pl.when(pl.program_id(1) == 0)
    def _():
        wb_ref[...] = w_ref[...].astype(BF16)

    acc = jnp.dot(x_ref[...], wb_ref[...], preferred_element_type=F32)
    if act == "relu2":
        a = jnp.maximum(acc, 0.0)
        acc = a * a
    o_ref[...] = acc.astype(o_ref.dtype)


def matmul_wcast(x, w, layer, *, tm, tn, out_dtype=F32, act=None, vmem_limit=VMEM_LIMIT, name="matmul_wcast"):
    M, K = x.shape
    _, _, N = w.shape
    tm, tn = _fit(M, tm), _fit(N, tn)
    return pl.pallas_call(
        functools.partial(_mm_wcast_kernel, act=act),
        grid=(N // tn, M // tm),
        in_specs=[pl.BlockSpec((tm, K), lambda j, i: (i, 0)),
                  pl.BlockSpec((None, K, tn), lambda j, i: (layer, 0, j))],
        out_specs=pl.BlockSpec((tm, tn), lambda j, i: (i, j)),
        out_shape=jax.ShapeDtypeStruct((M, N), out_dtype),
        scratch_shapes=[pltpu.VMEM((K, tn), BF16)],
        compiler_params=_cparams("arbitrary", "arbitrary", vmem_limit=vmem_limit),
        name=name,
    )(x, w)


def _mm_post_kernel(*refs, nk, with_pre, sub, ncb, split_in, split_out):
    it = iter(refs)
    a_ref, w_ref = next(it), next(it)
    x_refs = [next(it) for _ in range(2 if split_in else 1)]
    gp_ref, gate_ref = next(it), next(it)
    gn_ref, sh_ref, sc_ref = (next(it), next(it), next(it)) if with_pre else (None, None, None)
    x1_refs = [next(it) for _ in range(2 if split_out else 1)]
    h_ref = next(it) if with_pre else None
    tm = a_ref.shape[0]
    is_ctx = pl.program_id(0) < ncb

    def each_out(fn):
        if split_out:
            pl.when(is_ctx)(lambda: fn(x1_refs[0]))
            pl.when(jnp.logical_not(is_ctx))(lambda: fn(x1_refs[1]))
        else:
            fn(x1_refs[0])

    def finish(x1_ref, r, y):
        x = jnp.where(is_ctx, x_refs[0][r, :], x_refs[1][r, :]) if split_in else x_refs[0][r, :]
        x1 = x + gate_ref[...] * _rms(y, gp_ref[...])
        x1_ref[r, :] = x1
        if with_pre:
            h_ref[r, :] = (_rms(x1, gn_ref[...]) * (1.0 + sc_ref[...]) + sh_ref[...]).astype(h_ref.dtype)

    def product(r):
        return jnp.dot(a_ref[r, :], w_ref[...], preferred_element_type=F32)

    blocks = [slice(s, s + sub) for s in range(0, tm, sub)]
    everything = slice(None)

    def single(x1_ref):
        for r in blocks:
            finish(x1_ref, r, product(r))

    def first(x1_ref):
        x1_ref[...] = product(everything)

    def accumulate(x1_ref):
        x1_ref[...] += product(everything)

    def last(x1_ref):
        for r in blocks:
            finish(x1_ref, r, x1_ref[r, :] + product(r))

    if nk == 1:
        each_out(single)
        return
    k = pl.program_id(1)
    pl.when(k == 0)(lambda: each_out(first))
    pl.when(jnp.logical_and(k > 0, k < nk - 1))(lambda: each_out(accumulate))
    pl.when(k == nk - 1)(lambda: each_out(last))


def matmul_post(a, w, layer, x, g_post, mod3, k, rows, pre=None, *, split_out=False, tm=512, tk=None, sub=256,
                vmem_limit=VMEM_LIMIT, name="matmul_post"):
    M, K = a.shape
    D = w.shape[2]
    tm = rows.tile(tm)
    tk = _fit(K, tk or K)
    nk = K // tk
    row_spec = pl.BlockSpec((tm, D), lambda i, kk: (i, 0))
    split_in = isinstance(x, tuple)
    x_specs, xs = (_split_specs(rows, tm, D), list(x)) if split_in else ([row_spec], [x])
    in_specs = [pl.BlockSpec((tm, tk), lambda i, kk: (i, kk)),
                pl.BlockSpec((None, tk, D), lambda i, kk: (layer, kk, 0)),
                *x_specs, _vec_spec(D), _mod_spec(rows, tm, k + 2, D)]
    args = [a, w, *xs, g_post.reshape(1, D), mod3]
    if split_out:
        out_specs = _split_specs(rows, tm, D)
        out_shape = [jax.ShapeDtypeStruct((rows.nc, D), F32), jax.ShapeDtypeStruct((rows.nl, D), F32)]
    else:
        out_specs = [row_spec]
        out_shape = [jax.ShapeDtypeStruct((M, D), F32)]
    if pre is not None:
        g_pre, mod3_pre, kp = pre
        in_specs += [_vec_spec(D), _mod_spec(rows, tm, kp, D), _mod_spec(rows, tm, kp + 1, D)]
        args += [g_pre.reshape(1, D), mod3_pre, mod3_pre]
        out_specs.append(row_spec)
        out_shape.append(jax.ShapeDtypeStruct((M, D), BF16))
    out = pl.pallas_call(
        functools.partial(_mm_post_kernel, nk=nk, with_pre=pre is not None, sub=_fit(tm, sub), ncb=rows.nc // tm,
                          split_in=split_in, split_out=split_out),
        grid=(M // tm, nk),
        in_specs=in_specs, out_specs=out_specs, out_shape=out_shape,
        compiler_params=_cparams("arbitrary", "arbitrary", vmem_limit=vmem_limit),
        name=name,
    )(*args)
    return out if len(out) > 1 else out[0]


HGRN_CHUNK = 64
HGRN_BLOCK = 256
HGRN_SAFE_EXPONENT = 60.0


def _hgrn_kernel(*refs, T, hb, layer, has_init, emit_state, unroll_blocks, unroll_scan):
    it = iter(refs)
    q_ref, ff_ref, fb_ref, i_ref, g_ref, lbf_ref, lbb_ref, ng_ref = (next(it) for _ in range(8))
    s0_refs = (next(it), next(it)) if has_init else None
    o_ref = next(it)
    s_out_refs = (next(it), next(it)) if emit_state else None
    qe_refs, ds_refs, oi_refs, ebt_refs = ((next(it), next(it)) for _ in range(4))
    C, R = HGRN_CHUNK, min(HGRN_BLOCK, T)
    nch, nblk, cpb, mid = T // C, T // R, R // C, C // 2
    f_refs, lb_refs = (ff_ref, fb_ref), (lbf_ref, lbb_ref)
    total_row = (C - 1, 0)

    def lower_bound(lb_ref, sl):
        z = lb_ref[:, sl]
        e = jnp.exp(z - jnp.max(z, axis=0, keepdims=True))
        sm = e / jnp.sum(e, axis=0, keepdims=True)
        return jnp.sum(sm[:layer + 1], axis=0, keepdims=True)

    row = lax.broadcasted_iota(jnp.int32, (R, R), 0)
    col = lax.broadcasted_iota(jnp.int32, (R, R), 1)
    same_chunk = (row // C) == (col // C)
    tri = (same_chunk & (row >= col), same_chunk & (row <= col))

    def cumsum(mask, x):
        hi = x.astype(BF16)
        r1 = x - hi.astype(F32)
        md = r1.astype(BF16)
        lo = (r1 - md.astype(F32)).astype(BF16)
        s = jnp.dot(mask.astype(BF16), jnp.concatenate([hi, md, lo], axis=1), preferred_element_type=F32)
        return s[:, :HEAD_DIM] + s[:, HEAD_DIM:2 * HEAD_DIM] + s[:, 2 * HEAD_DIM:]

    heads = [slice(h * HEAD_DIM, (h + 1) * HEAD_DIM) for h in range(hb)]
    lbs = [[lower_bound(lb_refs[d], sl) for sl in heads] for d in range(2)]
    chains = [(h, sl, d) for h, sl in enumerate(heads) for d in range(2)]

    def block_pass(blk, risk):
        r = pl.ds(pl.multiple_of(blk * R, R), R)
        qs, vb, kk, b, qm, km, kd, a = {}, {}, {}, {}, {}, {}, {}, {}
        for h, sl in enumerate(heads):
            qs[h] = _silu(q_ref[r, sl]) * HEAD_DIM ** -0.5
            vb[h] = i_ref[r, sl].astype(BF16)
        for h, sl, d in chains:
            lb = lbs[d][h]
            f = lb + (1.0 - lb) * jax.nn.sigmoid(f_refs[d][r, sl])
            kk[h, d] = 1.0 - f
            b[h, d] = cumsum(tri[d], jnp.log(f))
        for h, sl, d in chains:
            qmc, kmc, kd[h, d] = [], [], []
            for cc in range(cpb):
                cs = slice(cc * C, (cc + 1) * C)
                bc = b[h, d][cs]
                m = bc[mid:mid + 1]
                bt = bc[total_row[d]:total_row[d] + 1]
                qc = qs[h][cs] * jnp.exp(bc - m)
                kc = kk[h, d][cs] * jnp.exp(m - bc)
                risk = jnp.maximum(risk, jnp.maximum(jnp.abs(bc[:1] - m), jnp.abs(bc[C - 1:] - m)))
                rc = pl.ds(pl.multiple_of(blk * R + cc * C, C), C)
                qe_refs[d][rc, sl] = (qs[h][cs] * jnp.exp(bc)).astype(BF16)
                kd[h, d].append((kk[h, d][cs] * jnp.exp(bt - bc)).astype(BF16))
                slot = pl.ds(pl.multiple_of((blk * cpb + cc) * SUBLANES, SUBLANES), SUBLANES)
                ebt_refs[d][slot, sl] = jnp.broadcast_to(jnp.exp(bt), (SUBLANES, HEAD_DIM))
                qmc.append(qc.astype(BF16))
                kmc.append(kc.astype(BF16))
            qm[h, d] = jnp.concatenate(qmc, axis=0)
            km[h, d] = jnp.concatenate(kmc, axis=0)
        for h, sl, d in chains:
            s = lax.dot_general(qm[h, d], km[h, d], NT, preferred_element_type=F32)
            a[h, d] = jnp.where(tri[d], s, 0.0).astype(BF16)
        for h, sl, d in chains:
            oi_refs[d][r, sl] = jnp.dot(a[h, d], vb[h], preferred_element_type=F32)
        for h, sl, d in chains:
            for cc in range(cpb):
                rows_c = pl.ds(pl.multiple_of((blk * cpb + cc) * HEAD_DIM, HEAD_DIM), HEAD_DIM)
                ds_refs[d][rows_c, sl] = lax.dot_general(vb[h][cc * C:(cc + 1) * C], kd[h, d][cc], TN,
                                                         preferred_element_type=F32)
        return risk

    risk = lax.fori_loop(0, nblk, block_pass, jnp.zeros((1, HEAD_DIM), F32), unroll=unroll_blocks)

    def safe_scores(d, qs, kk, lf):
        scores = jnp.where(row == col, lax.dot_general(qs.astype(BF16), kk.astype(BF16), NT,
                                                       preferred_element_type=F32), 0.0)
        half = 1
        while half < C:
            same = (row // (2 * half)) == (col // (2 * half))
            r_up, c_up = (row % (2 * half)) >= half, (col % (2 * half)) >= half
            r_lo, c_lo = jnp.logical_not(r_up), jnp.logical_not(c_up)
            if d == 0:
                to_t, from_s, pair = same & r_up & c_up & (col <= row), same & r_lo & c_lo & (col > row), same & r_up & c_lo
            else:
                to_t, from_s, pair = same & r_lo & c_lo & (col >= row), same & r_up & c_up & (col < row), same & r_lo & c_up
            qd = (qs * jnp.exp(cumsum(to_t, lf))).astype(BF16)
            kd = (kk * jnp.exp(cumsum(from_s, lf))).astype(BF16)
            scores = scores + jnp.where(pair, lax.dot_general(qd, kd, NT, preferred_element_type=F32), 0.0)
            half *= 2
        return scores

    def safe_block(blk, carry):
        r = pl.ds(pl.multiple_of(blk * R, R), R)
        for h, sl, d in chains:
            lb = lbs[d][h]
            f = lb + (1.0 - lb) * jax.nn.sigmoid(f_refs[d][r, sl])
            scores = safe_scores(d, _silu(q_ref[r, sl]) * HEAD_DIM ** -0.5, 1.0 - f, jnp.log(f))
            oi_refs[d][r, sl] = jnp.dot(scores.astype(BF16), i_ref[r, sl].astype(BF16), preferred_element_type=F32)
        return carry

    @pl.when(jnp.max(risk) > HGRN_SAFE_EXPONENT)
    def _():
        lax.fori_loop(0, nblk, safe_block, 0)

    def scan_step(j, states):
        cs = [j, nch - 1 - j]
        rs = [pl.ds(pl.multiple_of(c * C, C), C) for c in cs]
        slots = [pl.ds(pl.multiple_of(c * SUBLANES, SUBLANES), SUBLANES) for c in cs]
        grow = [pl.ds(pl.multiple_of(c * HEAD_DIM, HEAD_DIM), HEAD_DIM) for c in cs]
        new = []
        for h, sl, d in chains:
            St = states[2 * h + d]
            oi_refs[d][rs[d], sl] += lax.dot_general(qe_refs[d][rs[d], sl], St.astype(BF16), NT,
                                                     preferred_element_type=F32)
            new.append(St * ebt_refs[d][slots[d], sl][:1] + ds_refs[d][grow[d], sl])
        return tuple(new)

    if has_init:
        init = tuple(s0_refs[d][h].T for h in range(hb) for d in range(2))
    else:
        init = tuple(jnp.zeros((HEAD_DIM, HEAD_DIM), F32) for _ in range(2 * hb))
    states = lax.fori_loop(0, nch, scan_step, init, unroll=unroll_scan)
    if emit_state:
        for h in range(hb):
            for d in range(2):
                s_out_refs[d][h] = states[2 * h + d].T

    def final_pass(blk, carry):
        r = pl.ds(pl.multiple_of(blk * R, R), R)
        for sl in heads:
            o = _rms(oi_refs[0][r, sl] + oi_refs[1][r, sl], ng_ref[...]) * _silu(g_ref[r, sl])
            o_ref[r, sl] = o.astype(o_ref.dtype)
        return carry

    lax.fori_loop(0, nblk, final_pass, 0)


def hgrn(p, lb_f, lb_b, norm_g, dst, *, nseq, T, row0, A_heads, layer, hb, init=None, emit_state=False,
         unroll_blocks=1, unroll_scan=8):
    assert T % min(HGRN_BLOCK, T) == 0 and A_heads % hb == 0
    rb0 = row0 // T
    H = A_heads
    W = hb * HEAD_DIM
    ng = H // hb

    def slab(k):
        return pl.BlockSpec((T, W), lambda b, h: (rb0 + b, k * ng + h))

    nl = lb_f.shape[0]
    in_specs = [slab(k) for k in range(5)] + [
        pl.BlockSpec((nl, W), lambda b, h: (0, h)),
        pl.BlockSpec((nl, W), lambda b, h: (0, h)),
        pl.BlockSpec((1, HEAD_DIM), lambda b, h: (0, 0))]
    args = [p] * 5 + [lb_f, lb_b, norm_g.reshape(1, HEAD_DIM)]
    state_spec = pl.BlockSpec((None, hb, HEAD_DIM, HEAD_DIM), lambda b, h: (b, h, 0, 0))
    if init is not None:
        in_specs += [state_spec, state_spec]
        args += list(init)
    in_specs, args, aliases = _into(dst, in_specs, args)
    out_specs = [pl.BlockSpec((T, W), lambda b, h: (rb0 + b, h))]
    out_shape = [jax.ShapeDtypeStruct(dst.shape, dst.dtype)]
    if emit_state:
        out_specs += [state_spec, state_spec]
        out_shape += [jax.ShapeDtypeStruct((nseq, H, HEAD_DIM, HEAD_DIM), F32)] * 2
    nch = T // HGRN_CHUNK
    scratch = ([pltpu.VMEM((T, W), BF16)] * 2
               + [pltpu.VMEM((nch * HEAD_DIM, W), F32)] * 2
               + [pltpu.VMEM((T, W), F32)] * 2
               + [pltpu.VMEM((nch * SUBLANES, W), F32)] * 2)
    body = functools.partial(_hgrn_kernel, T=T, hb=hb, layer=layer, has_init=init is not None, emit_state=emit_state,
                             unroll_blocks=unroll_blocks, unroll_scan=unroll_scan)
    out = pl.pallas_call(
        _without_ref(body, len(args) - 1),
        grid=(nseq, ng),
        in_specs=in_specs, out_specs=out_specs, out_shape=out_shape,
        input_output_aliases=aliases,
        scratch_shapes=scratch,
        compiler_params=_cparams("arbitrary", "arbitrary"),
        name="hgrn",
    )(*args)
    return out if emit_state else out[0]


def _dense_attn_kernel(q_ref, k_ref, v_ref, o_ref, kout_ref, vout_ref, *, H, scale):
    heads = [slice(h * HEAD_DIM, (h + 1) * HEAD_DIM) for h in range(H)]
    problems = []
    for h, sl in enumerate(heads):
        kout_ref[:, h, :] = k_ref[:, sl]
        vout_ref[:, h, :] = v_ref[:, sl]
        q = (q_ref[:, sl] * scale).astype(BF16)
        s = lax.dot_general(q, k_ref[:, sl].astype(BF16), NT, preferred_element_type=F32)
        problems.append(([s], [v_ref[:, sl].astype(BF16)]))
    for sl, o in zip(heads, _softmax_pv(problems)):
        o_ref[:, sl] = o.astype(o_ref.dtype)


def dense_attn(p, dst, *, nseq, T, H, col0, dst_col0):
    W = H * HEAD_DIM
    cb = col0 // W
    in_specs, args, aliases = _into(dst, [pl.BlockSpec((T, W), lambda b, k=k: (b, cb + k)) for k in range(3)],
                                    [p, p, p])
    cache_spec = pl.BlockSpec((None, T, H, HEAD_DIM), lambda b: (b, 0, 0, 0))
    cache_shape = jax.ShapeDtypeStruct((nseq, T, H, HEAD_DIM), p.dtype)
    return pl.pallas_call(
        _without_ref(functools.partial(_dense_attn_kernel, H=H, scale=HEAD_DIM ** -0.5), 3),
        grid=(nseq,),
        in_specs=in_specs,
        out_specs=[pl.BlockSpec((T, W), lambda b: (b, dst_col0 // W)), cache_spec, cache_spec],
        out_shape=[jax.ShapeDtypeStruct(dst.shape, dst.dtype), cache_shape, cache_shape],
        input_output_aliases=aliases,
        compiler_params=_cparams("arbitrary"),
        name="dense_attn",
    )(*args)


NA_QROWS = 4
NA_KROWS = 12


def na_bias_tables(rpb, rows):
    W = GRID_W
    kr = min(NA_ROWS, rows)
    assert kr == NA_ROWS and rows >= NA_KROWS and rows % NA_QROWS == 0
    nblk = rows // NA_QROWS
    col = np.arange(W)
    cs = np.clip(col - NA_COLS // 2, 0, W - NA_COLS)
    col_ok = (col[None, :] >= cs[:, None]) & (col[None, :] < cs[:, None] + NA_COLS)
    ci = np.clip(col[None, :] - col[:, None] + NA_COLS - 1, 0, 2 * NA_COLS - 2)
    col_sel = (ci[..., None] == np.arange(2 * NA_COLS - 1)).astype(np.float32)
    G = jnp.einsum("hac,xyc->haxy", rpb.astype(F32), col_sel, precision=lax.Precision.HIGHEST)
    G = jnp.where(col_ok, G, NEG)
    outside = 2 * NA_ROWS - 1
    G = jnp.concatenate([G, jnp.full_like(G[:, :1], NEG)], axis=1)
    tables = []
    for blk in (0, 1, nblk - 1):
        r0 = blk * NA_QROWS
        u0 = min(max(r0 - NA_ROWS // 2, 0), rows - NA_KROWS)
        r = r0 + np.arange(NA_QROWS)
        ka = u0 + np.arange(NA_KROWS)
        start = np.clip(r - kr // 2, 0, rows - kr)
        row_ok = (ka[None, :] >= start[:, None]) & (ka[None, :] < start[:, None] + kr)
        ri = np.where(row_ok, ka[None, :] - r[:, None] + NA_ROWS - 1, outside)
        tables.append(jnp.concatenate(
            [jnp.concatenate([G[:, int(a)] for a in ri[q]], axis=-1) for q in range(NA_QROWS)], axis=-2))
    return jnp.stack(tables)


def _na_kernel(q_ref, k_ref, v_ref, kc_ref, vc_ref, bias_ref, o_ref, *, rows, scale, hp):
    blk = pl.program_id(2)
    u0 = jnp.clip(blk * NA_QROWS - NA_ROWS // 2, 0, rows - NA_KROWS)
    band = pl.ds(pl.multiple_of(u0 * GRID_W, GRID_W), NA_KROWS * GRID_W)
    heads = [slice(h * HEAD_DIM, (h + 1) * HEAD_DIM) for h in range(hp)]
    problems = []
    for h, sl in enumerate(heads):
        q = (q_ref[:, sl] * scale).astype(BF16)
        s_lat = lax.dot_general(q, k_ref[band, sl].astype(BF16), NT, preferred_element_type=F32) + bias_ref[h]
        s_ctx = lax.dot_general(q, kc_ref[:, sl].astype(BF16), NT, preferred_element_type=F32)
        problems.append(([s_lat, s_ctx], [v_ref[band, sl].astype(BF16), vc_ref[:, sl].astype(BF16)]))
    for sl, o in zip(heads, _softmax_pv(problems)):
        o_ref[:, sl] = o.astype(o_ref.dtype)


def na_attn(p, kc, vc, bias, dst, *, nseq, T, H, row0, col0, dst_col0, hp):
    rows = T // GRID_W
    nblk = rows // NA_QROWS
    tq = NA_QROWS * GRID_W
    L = kc.shape[0] // nseq
    W = hp * HEAD_DIM
    qb0, kb0, cb = row0 // tq, row0 // T, col0 // W
    ng = H // hp

    def kind(blk):
        return jnp.where(blk == 0, 0, jnp.where(blk == nblk - 1, 2, 1))

    in_specs, args, aliases = _into(dst, [
        pl.BlockSpec((tq, W), lambda b, h, i: (qb0 + b * nblk + i, cb + h)),
        pl.BlockSpec((T, W), lambda b, h, i: (kb0 + b, cb + ng + h)),
        pl.BlockSpec((T, W), lambda b, h, i: (kb0 + b, cb + 2 * ng + h)),
        pl.BlockSpec((L, W), lambda b, h, i: (b, h)),
        pl.BlockSpec((L, W), lambda b, h, i: (b, h)),
        pl.BlockSpec((None, hp, tq, NA_KROWS * GRID_W), lambda b, h, i: (kind(i), h, 0, 0))],
        [p, p, p, kc, vc, bias])
    return pl.pallas_call(
        _without_ref(functools.partial(_na_kernel, rows=rows, scale=HEAD_DIM ** -0.5, hp=hp), 6),
        grid=(nseq, ng, nblk),
        in_specs=in_specs,
        out_specs=pl.BlockSpec((tq, W), lambda b, h, i: (qb0 + b * nblk + i, dst_col0 // W + h)),
        out_shape=jax.ShapeDtypeStruct(dst.shape, dst.dtype),
        input_output_aliases=aliases,
        compiler_params=_cparams("arbitrary", "arbitrary", "arbitrary"),
        name="na_attn",
    )(*args)


ROPE_SWAP = np.concatenate([np.arange(16, 32), np.arange(0, 16), np.arange(48, 64), np.arange(32, 48)])


def rope_tables(T, tm):
    t = jnp.arange(T)
    half = QK_ROPE // 2
    inv = jnp.power(ROPE_BASE, -jnp.arange(0, half, 2, dtype=F32) / half)
    ang_r = (t // GRID_W).astype(F32)[:, None] * inv
    ang_c = (t % GRID_W).astype(F32)[:, None] * inv
    cos = jnp.concatenate([jnp.cos(ang_r), jnp.cos(ang_r), jnp.cos(ang_c), jnp.cos(ang_c)], axis=-1)
    sin = jnp.concatenate([-jnp.sin(ang_r), jnp.sin(ang_r), -jnp.sin(ang_c), jnp.sin(ang_c)], axis=-1)
    cos = jnp.concatenate([cos, jnp.ones((T, LANES - QK_ROPE), F32)], axis=-1)
    sin = jnp.concatenate([sin, jnp.zeros((T, LANES - QK_ROPE), F32)], axis=-1)
    cos = jnp.concatenate([cos, jnp.ones((tm, LANES), F32)], axis=0)
    sin = jnp.concatenate([sin, jnp.zeros((tm, LANES), F32)], axis=0)
    return cos, sin


def _mla_in_kernel(x_ref, w_ref, qg_ref, kvg_ref, cos_ref, sin_ref, cq_ref, ckv32_ref, ckv16_ref, k2_ref, kpe_ref,
                   *, qr, kvr):
    pr = jnp.dot(x_ref[...], w_ref[...], preferred_element_type=F32)
    cq_ref[...] = _rms(pr[:, :qr], qg_ref[...]).astype(cq_ref.dtype)
    ckv = _rms(pr[:, qr:qr + kvr], kvg_ref[...])
    ckv32_ref[...] = ckv
    ckv16_ref[...] = ckv.astype(ckv16_ref.dtype)
    x = pr[:, qr + kvr:]
    kpe_ref[...] = x
    rot = x * cos_ref[...] + pltpu.roll(x, LANES // 2, axis=1) * sin_ref[...]
    lane = lax.broadcasted_iota(jnp.int32, rot.shape, 1)
    k2_ref[...] = jnp.where(lane < QK_ROPE, rot, 0.0).astype(k2_ref.dtype)


def mla_in(h, w_in_wide, q_norm_g, kv_norm_g, cos, sin, rows, *, qr, kvr, tm):
    n, D = h.shape
    pos = pl.BlockSpec((tm, LANES), lambda i: (rows.pos_block(i, tm), 0))

    def out(w):
        return pl.BlockSpec((tm, w), lambda i: (i, 0))

    return pl.pallas_call(
        functools.partial(_mla_in_kernel, qr=qr, kvr=kvr),
        grid=(n // tm,),
        in_specs=[pl.BlockSpec((tm, D), lambda i: (i, 0)), pl.BlockSpec(w_in_wide.shape, lambda i: (0, 0)),
                  _vec_spec(qr), _vec_spec(kvr), pos, pos],
        out_specs=[out(qr), out(kvr), out(kvr), out(LANES), out(LANES)],
        out_shape=[jax.ShapeDtypeStruct((n, qr), BF16), jax.ShapeDtypeStruct((n, kvr), F32),
                   jax.ShapeDtypeStruct((n, kvr), BF16), jax.ShapeDtypeStruct((n, LANES), BF16),
                   jax.ShapeDtypeStruct((n, LANES), F32)],
        compiler_params=_cparams("arbitrary"),
        name="mla_in",
    )(h, w_in_wide, q_norm_g.reshape(1, qr), kv_norm_g.reshape(1, kvr), cos, sin)


MLA_Q_IN = 3 * LANES
MLA_Q_OUT = 2 * LANES


def widen_w_uq(w_uq, heads):
    r = w_uq.shape[0]
    w = w_uq.reshape(r, heads, QK_NOPE + QK_ROPE)
    nope, pe = w[..., :QK_NOPE], w[..., QK_NOPE:]
    return jnp.concatenate([nope, pe, pe, pe[..., ROPE_SWAP], jnp.zeros_like(pe)], axis=-1).reshape(r, heads * MLA_Q_IN)


def _mla_q_kernel(x_ref, w_ref, cos_ref, sin_ref, o_ref, *, hpt, scale):
    acc = jnp.dot(x_ref[...], w_ref[...], preferred_element_type=F32)
    cos, sin = cos_ref[...], sin_ref[...]
    for j in range(hpt):
        a = acc[:, j * MLA_Q_IN:(j + 1) * MLA_Q_IN]
        o_ref[:, j * MLA_Q_OUT:j * MLA_Q_OUT + LANES] = (a[:, :LANES] * scale).astype(o_ref.dtype)
        q2 = a[:, LANES:2 * LANES] * cos + a[:, 2 * LANES:] * sin
        o_ref[:, j * MLA_Q_OUT + LANES:(j + 1) * MLA_Q_OUT] = (q2 * scale).astype(o_ref.dtype)


def mla_q(cq, w_uq_wide, cos, sin, rows, *, heads, scale, tm, hpt=4):
    n, r = cq.shape
    pos = pl.BlockSpec((tm, LANES), lambda i, j: (rows.pos_block(i, tm), 0))
    return pl.pallas_call(
        functools.partial(_mla_q_kernel, hpt=hpt, scale=scale),
        grid=(n // tm, heads // hpt),
        in_specs=[pl.BlockSpec((tm, r), lambda i, j: (i, 0)),
                  pl.BlockSpec((r, hpt * MLA_Q_IN), lambda i, j: (0, j)), pos, pos],
        out_specs=pl.BlockSpec((tm, hpt * MLA_Q_OUT), lambda i, j: (i, j)),
        out_shape=jax.ShapeDtypeStruct((n, heads * MLA_Q_OUT), BF16),
        compiler_params=_cparams("arbitrary", "arbitrary"),
        name="mla_q",
    )(cq, w_uq_wide, cos, sin)


def _mla_ctx_attn_kernel(q_ref, kv_ref, k2_ref, o_ref, *, heads):
    k2 = k2_ref[...]
    problems = []
    for h in range(heads):
        q = q_ref[:, h * MLA_Q_OUT:(h + 1) * MLA_Q_OUT]
        k = jnp.concatenate([kv_ref[:, 2 * h * LANES:(2 * h + 1) * LANES], k2], axis=1)
        s = lax.dot_general(q, k, NT, preferred_element_type=F32)
        problems.append(([s], [kv_ref[:, (2 * h + 1) * LANES:(2 * h + 2) * LANES]]))
    for h, o in enumerate(_softmax_pv(problems)):
        o_ref[:, h * V_DIM:(h + 1) * V_DIM] = o.astype(o_ref.dtype)


def mla_ctx_attn(q, kv, k2, dst, *, nseq, T, heads):
    in_specs, args, aliases = _into(dst, [pl.BlockSpec((T, heads * MLA_Q_OUT), lambda b: (b, 0)),
                                          pl.BlockSpec((T, kv.shape[1]), lambda b: (b, 0)),
                                          pl.BlockSpec((T, LANES), lambda b: (b, 0))], [q, kv, k2])
    return pl.pallas_call(
        _without_ref(functools.partial(_mla_ctx_attn_kernel, heads=heads), 3),
        grid=(nseq,),
        in_specs=in_specs,
        out_specs=pl.BlockSpec((T, heads * V_DIM), lambda b: (b, 0)),
        out_shape=jax.ShapeDtypeStruct(dst.shape, dst.dtype),
        input_output_aliases=aliases,
        compiler_params=_cparams("arbitrary"),
        name="mla_ctx_attn",
    )(*args)


def _mla_lat_attn_kernel(q_ref, k1_ref, v_ref, k2_ref, k1c_ref, vc_ref, k2c_ref, o_ref, *, sub):
    k = jnp.concatenate([k1_ref[...], k2_ref[...]], axis=1)
    kc = jnp.concatenate([k1c_ref[...], k2c_ref[...]], axis=1)
    blocks = [slice(s, s + sub) for s in range(0, q_ref.shape[0], sub)]
    problems = []
    for r in blocks:
        q = q_ref[r, :]
        s_lat = lax.dot_general(q, k, NT, preferred_element_type=F32)
        s_ctx = lax.dot_general(q, kc, NT, preferred_element_type=F32)
        problems.append(([s_lat, s_ctx], [v_ref[...], vc_ref[...]]))
    for r, o in zip(blocks, _softmax_pv(problems)):
        o_ref[r, :] = o.astype(o_ref.dtype)


def mla_lat_attn(q, kv, k2, kvc, k2c, dst, *, nseq, T, heads, row0, tq, sub):
    nq = T // tq
    P = kvc.shape[0] // nseq
    qb0, kb0 = row0 // tq, row0 // T
    in_specs, args, aliases = _into(dst, [
        pl.BlockSpec((tq, MLA_Q_OUT), lambda b, h, i: (qb0 + b * nq + i, h)),
        pl.BlockSpec((T, LANES), lambda b, h, i: (kb0 + b, 2 * h)),
        pl.BlockSpec((T, LANES), lambda b, h, i: (kb0 + b, 2 * h + 1)),
        pl.BlockSpec((T, LANES), lambda b, h, i: (kb0 + b, 0)),
        pl.BlockSpec((P, LANES), lambda b, h, i: (b, 2 * h)),
        pl.BlockSpec((P, LANES), lambda b, h, i: (b, 2 * h + 1)),
        pl.BlockSpec((P, LANES), lambda b, h, i: (b, 0))],
        [q, kv, kv, k2, kvc, kvc, k2c])
    return pl.pallas_call(
        _without_ref(functools.partial(_mla_lat_attn_kernel, sub=_fit(tq, sub)), 7),
        grid=(nseq, heads, nq),
        in_specs=in_specs,
        out_specs=pl.BlockSpec((tq, V_DIM), lambda b, h, i: (qb0 + b * nq + i, h)),
        out_shape=jax.ShapeDtypeStruct(dst.shape, dst.dtype),
        input_output_aliases=aliases,
        compiler_params=_cparams("arbitrary", "arbitrary", "arbitrary"),
        name="mla_lat_attn",
    )(*args)


def even_layer(h, rows, B, SEQ, Bd, state_f, state_b, cache_k, cache_v, lb_f, lb_b, w_in, j, hgrn_g, rpb, layer):
    D = h.shape[1]
    AW = D // 2
    AH = BH = AW // HEAD_DIM
    T = rows.t
    p = matmul_wcast(h, w_in, j, tm=1024, tn=1024, out_dtype=F32, name="even_in_proj")
    o, s_fw, s_bw = hgrn(p, lb_f, lb_b, hgrn_g, h, nseq=B, T=SEQ, row0=0, A_heads=AH, layer=layer, hb=_fit(AH, 4),
                         emit_state=True)
    o = hgrn(p, lb_f, lb_b, hgrn_g, o, nseq=Bd, T=T, row0=rows.nc, A_heads=AH, layer=layer, hb=_fit(AH, 2),
             init=(state_f, state_b))
    o, new_k, new_v = dense_attn(p, o, nseq=B, T=SEQ, H=BH, col0=5 * AW, dst_col0=AW)
    past = cache_k.shape[1]
    bias = na_bias_tables(rpb, T // GRID_W)
    o = na_attn(p, cache_k.reshape(Bd * past, BH * HEAD_DIM), cache_v.reshape(Bd * past, BH * HEAD_DIM), bias, o,
                nseq=Bd, T=T, H=BH, row0=rows.nc, col0=5 * AW, dst_col0=AW, hp=_fit(BH, 4))
    return o, s_fw, s_bw, new_k, new_v


def odd_layer(h, rows, B, SEQ, Bd, cache_ckv, cache_kpe, w_in, q_norm_g, w_uq, kv_norm_g, w_ukv):
    D = h.shape[1]
    heads = D // 128
    T = rows.t
    qr, kvr = w_uq.shape[0], w_ukv.shape[0]
    scale = (QK_NOPE + QK_ROPE) ** -0.5
    tm = rows.tile(1024)
    w_in_wide = jnp.concatenate([w_in, w_in[:, qr + kvr + ROPE_SWAP]], axis=1).astype(BF16)
    cos, sin = rope_tables(T, tm)
    cq, ckv32, ckv16, k2, kpe = mla_in(h, w_in_wide, q_norm_g, kv_norm_g, cos, sin, rows, qr=qr, kvr=kvr, tm=tm)
    q = mla_q(cq, widen_w_uq(w_uq, heads).astype(BF16), cos, sin, rows, heads=heads, scale=scale, tm=tm,
              hpt=_fit(heads, 8))
    w_ukv16 = w_ukv.astype(BF16)
    kv = matmul(ckv16, w_ukv16, tm=2048, tn=2048, out_dtype=BF16, name="mla_kv")
    past = cache_ckv.shape[1]
    kvc = matmul(cache_ckv.reshape(Bd * past, kvr).astype(BF16), w_ukv16, tm=1024, tn=1024, out_dtype=BF16,
                 name="mla_kv_cache")
    k2c = jnp.concatenate([jnp.zeros((Bd * past, LANES - QK_ROPE), F32), cache_kpe.reshape(Bd * past, QK_ROPE)],
                          axis=1).astype(BF16)
    o = mla_ctx_attn(q, kv, k2, h, nseq=B, T=SEQ, heads=heads)
    o = mla_lat_attn(q, kv, k2, kvc, k2c, o, nseq=Bd, T=T, heads=heads, row0=rows.nc, tq=rows.tile(2048), sub=512)
    new_ckv = ckv32[:rows.nc].reshape(B, SEQ, kvr)
    new_kpe = kpe[:rows.nc, :QK_ROPE].reshape(B, SEQ, QK_ROPE)
    return o, new_ckv, new_kpe


def kernel(x_prompt, x_sample, state_hgrn_fwd, state_hgrn_bwd, cache_na_k, cache_na_v, cache_mla_ckv, cache_mla_kpe, c, c_ctx, ada_w, ada_b, norm_g, hgrn_lb_fwd, hgrn_lb_bwd, w_in_even, hgrn_norm_g, na_rpb, w_out_even, w_in_odd, mla_q_norm_g, w_uq, mla_kv_norm_g, w_ukv, w_out_odd, mlp_w1, mlp_w2):
    B, SEQ, D = x_prompt.shape
    Bd, T, _ = x_sample.shape
    depth = ada_w.shape[0]
    rows = Rows(B * SEQ, Bd * T, T)
    x = (x_prompt.reshape(rows.nc, D), x_sample.reshape(rows.nl, D))
    cvec = jnp.zeros((MOD_ROWS, D), F32).at[0].set(c_ctx).at[1:1 + Bd].set(c)
    mod = modulation(cvec, ada_w, ada_b).reshape(depth, MOD_ROWS * 6, 1, D)

    w2 = mlp_w2.astype(BF16)
    new_sf, new_sb, new_nk, new_nv, new_ckv, new_kpe = [], [], [], [], [], []
    h = prenorm(*x, norm_g[0, 0], mod[0], 0, rows)
    for l in range(depth):
        j = l // 2
        if l % 2 == 0:
            o, sf, sb, nk, nv = even_layer(h, rows, B, SEQ, Bd, state_hgrn_fwd[:, j], state_hgrn_bwd[:, j],
                                           cache_na_k[:, j], cache_na_v[:, j], hgrn_lb_fwd, hgrn_lb_bwd,
                                           w_in_even, j, hgrn_norm_g[j], na_rpb[j], l)
            w_out = w_out_even
            new_sf.append(sf)
            new_sb.append(sb)
            new_nk.append(nk)
            new_nv.append(nv)
        else:
            o, ckv, kpe = odd_layer(h, rows, B, SEQ, Bd, cache_mla_ckv[:, j], cache_mla_kpe[:, j], w_in_odd[j],
                                    mla_q_norm_g[j], w_uq[j], mla_kv_norm_g[j], w_ukv[j])
            w_out = w_out_odd
            new_ckv.append(ckv)
            new_kpe.append(kpe)
        x, h = matmul_post(o, w_out.astype(BF16), j, x, norm_g[l, 1], mod[l], 0, rows,
                           pre=(norm_g[l, 2], mod[l], 3), name="out_proj_post")
        a = matmul_wcast(h, mlp_w1, l, tm=2048, tn=1024, out_dtype=BF16, act="relu2", vmem_limit=VMEM_LIMIT_BIG,
                         name="mlp_up")
        down = dict(tm=1024, vmem_limit=VMEM_LIMIT_BIG, name="mlp_down_post")
        if l + 1 < depth:
            x, h = matmul_post(a, w2, l, x, norm_g[l, 3], mod[l], 3, rows,
                               pre=(norm_g[l + 1, 0], mod[l + 1], 0), tk=1024, **down)
        else:
            x = matmul_post(a, w2, l, x, norm_g[l, 3], mod[l], 3, rows, split_out=True, tk=512, **down)
    return (x[0].reshape(B, SEQ, D), x[1].reshape(Bd, T, D),
            jnp.stack(new_sf, axis=1), jnp.stack(new_sb, axis=1), jnp.stack(new_nk, axis=1),
            jnp.stack(new_nv, axis=1), jnp.stack(new_ckv, axis=1), jnp.stack(new_kpe, axis=1))
```

```python
import functools

import numpy as np
import jax
import jax.numpy as jnp
from jax import lax
from jax.experimental import pallas as pl
from jax.experimental.pallas import tpu as pltpu

F32 = jnp.float32
BF16 = jnp.bfloat16

GRID_W = 64
HEAD_DIM = 128
NA_ROWS = 8
NA_COLS = 16
QK_NOPE = 128
QK_ROPE = 64
V_DIM = 128
ROPE_BASE = 10000.0
EPS = 1e-6
NEG = -1e30
LOG2E = float(np.log2(np.e))

LANES = 128
SUBLANES = 8
VMEM_LIMIT = 48 * 1024 * 1024
VMEM_LIMIT_BIG = 60 * 1024 * 1024
MOD_ROWS = 16

NT = (((1,), (1,)), ((), ()))
TN = (((0,), (0,)), ((), ()))


def _cparams(*sem, vmem_limit=VMEM_LIMIT):
    return pltpu.CompilerParams(dimension_semantics=sem, vmem_limit_bytes=vmem_limit)


def _fit(n, pref):
    t = min(pref, n)
    while n % t:
        t //= 2
    return t


def _silu(x):
    return x * jax.nn.sigmoid(x)


def _rms(x, g):
    return x * lax.rsqrt(jnp.mean(x * x, axis=-1, keepdims=True) + EPS) * g


def _softmax_pv(problems):
    ms = [functools.reduce(jnp.maximum, [jnp.max(s, axis=-1, keepdims=True) for s in scores])
          for scores, _ in problems]
    ps = [[jnp.exp2(s - m) for s in scores] for (scores, _), m in zip(problems, ms)]
    ls = [functools.reduce(jnp.add, [jnp.sum(p, axis=-1, keepdims=True) for p in pp]) for pp in ps]
    os = [functools.reduce(jnp.add, [jnp.dot(p.astype(BF16), v, preferred_element_type=F32)
                                     for p, v in zip(pp, values)]) for pp, (_, values) in zip(ps, problems)]
    return [o / l for o, l in zip(os, ls)]


def _mod_kernel(c_ref, w_ref, b_ref, o_ref):
    s = _silu(c_ref[...]).astype(BF16)
    o_ref[...] = jnp.dot(s, w_ref[...].astype(BF16), preferred_element_type=F32) + b_ref[...]


def modulation(cvec, ada_w, ada_b, tn=512):
    L, D, N6 = ada_w.shape
    R = cvec.shape[0]
    tn = _fit(N6, tn)
    return pl.pallas_call(
        _mod_kernel,
        grid=(L, N6 // tn),
        in_specs=[pl.BlockSpec((R, D), lambda l, j: (0, 0)),
                  pl.BlockSpec((None, D, tn), lambda l, j: (l, 0, j)),
                  pl.BlockSpec((None, 1, tn), lambda l, j: (l, 0, j))],
        out_specs=pl.BlockSpec((None, R, tn), lambda l, j: (l, 0, j)),
        out_shape=jax.ShapeDtypeStruct((L, R, N6), F32),
        compiler_params=_cparams("arbitrary", "arbitrary"),
        name="modulation",
    )(cvec, ada_w, ada_b.reshape(L, 1, N6))


class Rows:
    def __init__(self, nc, nl, t):
        assert nc % t == 0, "latent sequences must start on a multiple of their length"
        self.nc, self.nl, self.t, self.n = nc, nl, t, nc + nl

    def tile(self, pref):
        return _fit(np.gcd(self.nc, self.t), pref)

    def mod_row(self, i, tm):
        ncb = self.nc // tm
        return jnp.where(i < ncb, 0, 1 + (i - ncb) // (self.t // tm))

    def pos_block(self, i, tm):
        ncb = self.nc // tm
        return jnp.where(i < ncb, self.t // tm, (i - ncb) % (self.t // tm))


def _mod_spec(rows, tm, k, D):
    return pl.BlockSpec((None, 1, D), lambda i, *_: (rows.mod_row(i, tm) * 6 + k, 0, 0))


def _vec_spec(D):
    return pl.BlockSpec((1, D), lambda i, *_: (0, 0))


def _split_specs(rows, tm, D):
    ncb = rows.nc // tm
    return [pl.BlockSpec((tm, D), lambda i, *_: (jnp.minimum(i, ncb - 1), 0)),
            pl.BlockSpec((tm, D), lambda i, *_: (jnp.maximum(i - ncb, 0), 0))]


def _into(dst, in_specs, args):
    return in_specs + [pl.BlockSpec(memory_space=pl.ANY)], args + [dst], {len(args): 0}


def _without_ref(kernel, idx):
    return lambda *refs: kernel(*refs[:idx], *refs[idx + 1:])


def _pre_kernel(xc_ref, xs_ref, g_ref, sh_ref, sc_ref, h_ref, *, ncb):
    x = jnp.where(pl.program_id(0) < ncb, xc_ref[...], xs_ref[...])
    h = _rms(x, g_ref[...]) * (1.0 + sc_ref[...]) + sh_ref[...]
    h_ref[...] = h.astype(h_ref.dtype)


def prenorm(xc, xs, g, mod3, k, rows, tm=512):
    D = xc.shape[1]
    tm = rows.tile(tm)
    return pl.pallas_call(
        functools.partial(_pre_kernel, ncb=rows.nc // tm),
        grid=(rows.n // tm,),
        in_specs=_split_specs(rows, tm, D) + [_vec_spec(D), _mod_spec(rows, tm, k, D),
                                              _mod_spec(rows, tm, k + 1, D)],
        out_specs=pl.BlockSpec((tm, D), lambda i: (i, 0)),
        out_shape=jax.ShapeDtypeStruct((rows.n, D), BF16),
        compiler_params=_cparams("arbitrary"),
        name="prenorm",
    )(xc, xs, g.reshape(1, D), mod3, mod3)


def _mm_kernel(x_ref, w_ref, o_ref, *scratch, nk, act):
    def finish(acc):
        if act == "relu2":
            a = jnp.maximum(acc, 0.0)
            acc = a * a
        o_ref[...] = acc.astype(o_ref.dtype)

    if nk == 1:
        finish(jnp.dot(x_ref[...], w_ref[...], preferred_element_type=F32))
        return
    acc_ref, = scratch
    k = pl.program_id(2)

    @pl.when(k == 0)
    def _():
        acc_ref[...] = jnp.zeros_like(acc_ref)

    acc_ref[...] += jnp.dot(x_ref[...], w_ref[...], preferred_element_type=F32)

    @pl.when(k == nk - 1)
    def _():
        finish(acc_ref[...])


def matmul(x, w, *, tm, tn, tk=None, out_dtype=F32, act=None, name="matmul"):
    M, K = x.shape
    _, N = w.shape
    tm, tn, tk = _fit(M, tm), _fit(N, tn), _fit(K, tk or K)
    nk = K // tk
    assert M % tm == 0 and N % tn == 0 and K % tk == 0
    return pl.pallas_call(
        functools.partial(_mm_kernel, nk=nk, act=act),
        grid=(M // tm, N // tn, nk),
        in_specs=[pl.BlockSpec((tm, tk), lambda i, j, k: (i, k)),
                  pl.BlockSpec((tk, tn), lambda i, j, k: (k, j))],
        out_specs=pl.BlockSpec((tm, tn), lambda i, j, k: (i, j)),
        out_shape=jax.ShapeDtypeStruct((M, N), out_dtype),
        scratch_shapes=[pltpu.VMEM((tm, tn), F32)] if nk > 1 else [],
        compiler_params=_cparams("arbitrary", "arbitrary", "arbitrary"),
        name=name,
    )(x, w)


def _mm_wcast_kernel(x_ref, w_ref, o_ref, wb_ref, *, act):
    @pl.when(pl.program_id(1) == 0)
    def _():
        wb_ref[...] = w_ref[...].astype(BF16)

    acc = jnp.dot(x_ref[...], wb_ref[...], preferred_element_type=F32)
    if act == "relu2":
        a = jnp.maximum(acc, 0.0)
        acc = a * a
    o_ref[...] = acc.astype(o_ref.dtype)


def matmul_wcast(x, w, layer, *, tm, tn, out_dtype=F32, act=None, vmem_limit=VMEM_LIMIT, name="matmul_wcast"):
    M, K = x.shape
    _, _, N = w.shape
    tm, tn = _fit(M, tm), _fit(N, tn)
    return pl.pallas_call(
        functools.partial(_mm_wcast_kernel, act=act),
        grid=(N // tn, M // tm),
        in_specs=[pl.BlockSpec((tm, K), lambda j, i: (i, 0)),
                  pl.BlockSpec((None, K, tn), lambda j, i: (layer, 0, j))],
        out_specs=pl.BlockSpec((tm, tn), lambda j, i: (i, j)),
        out_shape=jax.ShapeDtypeStruct((M, N), out_dtype),
        scratch_shapes=[pltpu.VMEM((K, tn), BF16)],
        compiler_params=_cparams("arbitrary", "arbitrary", vmem_limit=vmem_limit),
        name=name,
    )(x, w)


def _mm_post_kernel(*refs, nk, with_pre, sub, ncb, split_in, split_out):
    it = iter(refs)
    a_ref, w_ref = next(it), next(it)
    x_refs = [next(it) for _ in range(2 if split_in else 1)]
    gp_ref, gate_ref = next(it), next(it)
    gn_ref, sh_ref, sc_ref = (next(it), next(it), next(it)) if with_pre else (None, None, None)
    x1_refs = [next(it) for _ in range(2 if split_out else 1)]
    h_ref = next(it) if with_pre else None
    tm = a_ref.shape[0]
    is_ctx = pl.program_id(0) < ncb

    def each_out(fn):
        if split_out:
            pl.when(is_ctx)(lambda: fn(x1_refs[0]))
            pl.when(jnp.logical_not(is_ctx))(lambda: fn(x1_refs[1]))
        else:
            fn(x1_refs[0])

    def finish(x1_ref, r, y):
        x = jnp.where(is_ctx, x_refs[0][r, :], x_refs[1][r, :]) if split_in else x_refs[0][r, :]
        x1 = x + gate_ref[...] * _rms(y, gp_ref[...])
        x1_ref[r, :] = x1
        if with_pre:
            h_ref[r, :] = (_rms(x1, gn_ref[...]) * (1.0 + sc_ref[...]) + sh_ref[...]).astype(h_ref.dtype)

    def product(r):
        return jnp.dot(a_ref[r, :], w_ref[...], preferred_element_type=F32)

    blocks = [slice(s, s + sub) for s in range(0, tm, sub)]
    everything = slice(None)

    def single(x1_ref):
        for r in blocks:
            finish(x1_ref, r, product(r))

    def first(x1_ref):
        x1_ref[...] = product(everything)

    def accumulate(x1_ref):
        x1_ref[...] += product(everything)

    def last(x1_ref):
        for r in blocks:
            finish(x1_ref, r, x1_ref[r, :] + product(r))

    if nk == 1:
        each_out(single)
        return
    k = pl.program_id(1)
    pl.when(k == 0)(lambda: each_out(first))
    pl.when(jnp.logical_and(k > 0, k < nk - 1))(lambda: each_out(accumulate))
    pl.when(k == nk - 1)(lambda: each_out(last))


def matmul_post(a, w, layer, x, g_post, mod3, k, rows, pre=None, *, split_out=False, tm=512, tk=None, sub=256,
                vmem_limit=VMEM_LIMIT, name="matmul_post"):
    M, K = a.shape
    D = w.shape[2]
    tm = rows.tile(tm)
    tk = _fit(K, tk or K)
    nk = K // tk
    row_spec = pl.BlockSpec((tm, D), lambda i, kk: (i, 0))
    split_in = isinstance(x, tuple)
    x_specs, xs = (_split_specs(rows, tm, D), list(x)) if split_in else ([row_spec], [x])
    in_specs = [pl.BlockSpec((tm, tk), lambda i, kk: (i, kk)),
                pl.BlockSpec((None, tk, D), lambda i, kk: (layer, kk, 0)),
                *x_specs, _vec_spec(D), _mod_spec(rows, tm, k + 2, D)]
    args = [a, w, *xs, g_post.reshape(1, D), mod3]
    if split_out:
        out_specs = _split_specs(rows, tm, D)
        out_shape = [jax.ShapeDtypeStruct((rows.nc, D), F32), jax.ShapeDtypeStruct((rows.nl, D), F32)]
    else:
        out_specs = [row_spec]
        out_shape = [jax.ShapeDtypeStruct((M, D), F32)]
    if pre is not None:
        g_pre, mod3_pre, kp = pre
        in_specs += [_vec_spec(D), _mod_spec(rows, tm, kp, D), _mod_spec(rows, tm, kp + 1, D)]
        args += [g_pre.reshape(1, D), mod3_pre, mod3_pre]
        out_specs.append(row_spec)
        out_shape.append(jax.ShapeDtypeStruct((M, D), BF16))
    out = pl.pallas_call(
        functools.partial(_mm_post_kernel, nk=nk, with_pre=pre is not None, sub=_fit(tm, sub), ncb=rows.nc // tm,
                          split_in=split_in, split_out=split_out),
        grid=(M // tm, nk),
        in_specs=in_specs, out_specs=out_specs, out_shape=out_shape,
        compiler_params=_cparams("arbitrary", "arbitrary", vmem_limit=vmem_limit),
        name=name,
    )(*args)
    return out if len(out) > 1 else out[0]


HGRN_CHUNK = 64
HGRN_BLOCK = 256
HGRN_SAFE_EXPONENT = 60.0


def _hgrn_kernel(*refs, T, hb, layer, has_init, emit_state, unroll_blocks, unroll_scan):
    it = iter(refs)
    q_ref, ff_ref, fb_ref, i_ref, g_ref, lbf_ref, lbb_ref, ng_ref = (next(it) for _ in range(8))
    s0_refs = (next(it), next(it)) if has_init else None
    o_ref = next(it)
    s_out_refs = (next(it), next(it)) if emit_state else None
    qe_refs, ds_refs, oi_refs, ebt_refs = ((next(it), next(it)) for _ in range(4))
    C, R = HGRN_CHUNK, min(HGRN_BLOCK, T)
    nch, nblk, cpb, mid = T // C, T // R, R // C, C // 2
    f_refs, lb_refs = (ff_ref, fb_ref), (lbf_ref, lbb_ref)
    total_row = (C - 1, 0)

    def lower_bound(lb_ref, sl):
        z = lb_ref[:, sl]
        e = jnp.exp(z - jnp.max(z, axis=0, keepdims=True))
        sm = e / jnp.sum(e, axis=0, keepdims=True)
        return jnp.sum(sm[:layer + 1], axis=0, keepdims=True)

    row = lax.broadcasted_iota(jnp.int32, (R, R), 0)
    col = lax.broadcasted_iota(jnp.int32, (R, R), 1)
    same_chunk = (row // C) == (col // C)
    tri = (same_chunk & (row >= col), same_chunk & (row <= col))

    def cumsum(mask, x):
        hi = x.astype(BF16)
        r1 = x - hi.astype(F32)
        md = r1.astype(BF16)
        lo = (r1 - md.astype(F32)).astype(BF16)
        s = jnp.dot(mask.astype(BF16), jnp.concatenate([hi, md, lo], axis=1), preferred_element_type=F32)
        return s[:, :HEAD_DIM] + s[:, HEAD_DIM:2 * HEAD_DIM] + s[:, 2 * HEAD_DIM:]

    heads = [slice(h * HEAD_DIM, (h + 1) * HEAD_DIM) for h in range(hb)]
    lbs = [[lower_bound(lb_refs[d], sl) for sl in heads] for d in range(2)]
    chains = [(h, sl, d) for h, sl in enumerate(heads) for d in range(2)]

    def block_pass(blk, risk):
        r = pl.ds(pl.multiple_of(blk * R, R), R)
        qs, vb, kk, b, qm, km, kd, a = {}, {}, {}, {}, {}, {}, {}, {}
        for h, sl in enumerate(heads):
            qs[h] = _silu(q_ref[r, sl]) * HEAD_DIM ** -0.5
            vb[h] = i_ref[r, sl].astype(BF16)
        for h, sl, d in chains:
            lb = lbs[d][h]
            f = lb + (1.0 - lb) * jax.nn.sigmoid(f_refs[d][r, sl])
            kk[h, d] = 1.0 - f
            b[h, d] = cumsum(tri[d], jnp.log(f))
        for h, sl, d in chains:
            qmc, kmc, kd[h, d] = [], [], []
            for cc in range(cpb):
                cs = slice(cc * C, (cc + 1) * C)
                bc = b[h, d][cs]
                m = bc[mid:mid + 1]
                bt = bc[total_row[d]:total_row[d] + 1]
                qc = qs[h][cs] * jnp.exp(bc - m)
                kc = kk[h, d][cs] * jnp.exp(m - bc)
                risk = jnp.maximum(risk, jnp.maximum(jnp.abs(bc[:1] - m), jnp.abs(bc[C - 1:] - m)))
                rc = pl.ds(pl.multiple_of(blk * R + cc * C, C), C)
                qe_refs[d][rc, sl] = (qs[h][cs] * jnp.exp(bc)).astype(BF16)
                kd[h, d].append((kk[h, d][cs] * jnp.exp(bt - bc)).astype(BF16))
                slot = pl.ds(pl.multiple_of((blk * cpb + cc) * SUBLANES, SUBLANES), SUBLANES)
                ebt_refs[d][slot, sl] = jnp.broadcast_to(jnp.exp(bt), (SUBLANES, HEAD_DIM))
                qmc.append(qc.astype(BF16))
                kmc.append(kc.astype(BF16))
            qm[h, d] = jnp.concatenate(qmc, axis=0)
            km[h, d] = jnp.concatenate(kmc, axis=0)
        for h, sl, d in chains:
            s = lax.dot_general(qm[h, d], km[h, d], NT, preferred_element_type=F32)
            a[h, d] = jnp.where(tri[d], s, 0.0).astype(BF16)
        for h, sl, d in chains:
            oi_refs[d][r, sl] = jnp.dot(a[h, d], vb[h], preferred_element_type=F32)
        for h, sl, d in chains:
            for cc in range(cpb):
                rows_c = pl.ds(pl.multiple_of((blk * cpb + cc) * HEAD_DIM, HEAD_DIM), HEAD_DIM)
                ds_refs[d][rows_c, sl] = lax.dot_general(vb[h][cc * C:(cc + 1) * C], kd[h, d][cc], TN,
                                                         preferred_element_type=F32)
        return risk

    risk = lax.fori_loop(0, nblk, block_pass, jnp.zeros((1, HEAD_DIM), F32), unroll=unroll_blocks)

    def safe_scores(d, qs, kk, lf):
        scores = jnp.where(row == col, lax.dot_general(qs.astype(BF16), kk.astype(BF16), NT,
                                                       preferred_element_type=F32), 0.0)
        half = 1
        while half < C:
            same = (row // (2 * half)) == (col // (2 * half))
            r_up, c_up = (row % (2 * half)) >= half, (col % (2 * half)) >= half
            r_lo, c_lo = jnp.logical_not(r_up), jnp.logical_not(c_up)
            if d == 0:
                to_t, from_s, pair = same & r_up & c_up & (col <= row), same & r_lo & c_lo & (col > row), same & r_up & c_lo
            else:
                to_t, from_s, pair = same & r_lo & c_lo & (col >= row), same & r_up & c_up & (col < row), same & r_lo & c_up
            qd = (qs * jnp.exp(cumsum(to_t, lf))).astype(BF16)
            kd = (kk * jnp.exp(cumsum(from_s, lf))).astype(BF16)
            scores = scores + jnp.where(pair, lax.dot_general(qd, kd, NT, preferred_element_type=F32), 0.0)
            half *= 2
        return scores

    def safe_block(blk, carry):
        r = pl.ds(pl.multiple_of(blk * R, R), R)
        for h, sl, d in chains:
            lb = lbs[d][h]
            f = lb + (1.0 - lb) * jax.nn.sigmoid(f_refs[d][r, sl])
            scores = safe_scores(d, _silu(q_ref[r, sl]) * HEAD_DIM ** -0.5, 1.0 - f, jnp.log(f))
            oi_refs[d][r, sl] = jnp.dot(scores.astype(BF16), i_ref[r, sl].astype(BF16), preferred_element_type=F32)
        return carry

    @pl.when(jnp.max(risk) > HGRN_SAFE_EXPONENT)
    def _():
        lax.fori_loop(0, nblk, safe_block, 0)

    def scan_step(j, states):
        cs = [j, nch - 1 - j]
        rs = [pl.ds(pl.multiple_of(c * C, C), C) for c in cs]
        slots = [pl.ds(pl.multiple_of(c * SUBLANES, SUBLANES), SUBLANES) for c in cs]
        grow = [pl.ds(pl.multiple_of(c * HEAD_DIM, HEAD_DIM), HEAD_DIM) for c in cs]
        new = []
        for h, sl, d in chains:
            St = states[2 * h + d]
            oi_refs[d][rs[d], sl] += lax.dot_general(qe_refs[d][rs[d], sl], St.astype(BF16), NT,
                                                     preferred_element_type=F32)
            new.append(St * ebt_refs[d][slots[d], sl][:1] + ds_refs[d][grow[d], sl])
        return tuple(new)

    if has_init:
        init = tuple(s0_refs[d][h].T for h in range(hb) for d in range(2))
    else:
        init = tuple(jnp.zeros((HEAD_DIM, HEAD_DIM), F32) for _ in range(2 * hb))
    states = lax.fori_loop(0, nch, scan_step, init, unroll=unroll_scan)
    if emit_state:
        for h in range(hb):
            for d in range(2):
                s_out_refs[d][h] = states[2 * h + d].T

    def final_pass(blk, carry):
        r = pl.ds(pl.multiple_of(blk * R, R), R)
        for sl in heads:
            o = _rms(oi_refs[0][r, sl] + oi_refs[1][r, sl], ng_ref[...]) * _silu(g_ref[r, sl])
            o_ref[r, sl] = o.astype(o_ref.dtype)
        return carry

    lax.fori_loop(0, nblk, final_pass, 0)


def hgrn(p, lb_f, lb_b, norm_g, dst, *, nseq, T, row0, A_heads, layer, hb, init=None, emit_state=False,
         unroll_blocks=1, unroll_scan=8):
    assert T % min(HGRN_BLOCK, T) == 0 and A_heads % hb == 0
    rb0 = row0 // T
    H = A_heads
    W = hb * HEAD_DIM
    ng = H // hb

    def slab(k):
        return pl.BlockSpec((T, W), lambda b, h: (rb0 + b, k * ng + h))

    nl = lb_f.shape[0]
    in_specs = [slab(k) for k in range(5)] + [
        pl.BlockSpec((nl, W), lambda b, h: (0, h)),
        pl.BlockSpec((nl, W), lambda b, h: (0, h)),
        pl.BlockSpec((1, HEAD_DIM), lambda b, h: (0, 0))]
    args = [p] * 5 + [lb_f, lb_b, norm_g.reshape(1, HEAD_DIM)]
    state_spec = pl.BlockSpec((None, hb, HEAD_DIM, HEAD_DIM), lambda b, h: (b, h, 0, 0))
    if init is not None:
        in_specs += [state_spec, state_spec]
        args += list(init)
    in_specs, args, aliases = _into(dst, in_specs, args)
    out_specs = [pl.BlockSpec((T, W), lambda b, h: (rb0 + b, h))]
    out_shape = [jax.ShapeDtypeStruct(dst.shape, dst.dtype)]
    if emit_state:
        out_specs += [state_spec, state_spec]
        out_shape += [jax.ShapeDtypeStruct((nseq, H, HEAD_DIM, HEAD_DIM), F32)] * 2
    nch = T // HGRN_CHUNK
    scratch = ([pltpu.VMEM((T, W), BF16)] * 2
               + [pltpu.VMEM((nch * HEAD_DIM, W), F32)] * 2
               + [pltpu.VMEM((T, W), F32)] * 2
               + [pltpu.VMEM((nch * SUBLANES, W), F32)] * 2)
    body = functools.partial(_hgrn_kernel, T=T, hb=hb, layer=layer, has_init=init is not None, emit_state=emit_state,
                             unroll_blocks=unroll_blocks, unroll_scan=unroll_scan)
    out = pl.pallas_call(
        _without_ref(body, len(args) - 1),
        grid=(nseq, ng),
        in_specs=in_specs, out_specs=out_specs, out_shape=out_shape,
        input_output_aliases=aliases,
        scratch_shapes=scratch,
        compiler_params=_cparams("arbitrary", "arbitrary"),
        name="hgrn",
    )(*args)
    return out if emit_state else out[0]


def _dense_attn_kernel(q_ref, k_ref, v_ref, o_ref, kout_ref, vout_ref, *, H, scale):
    heads = [slice(h * HEAD_DIM, (h + 1) * HEAD_DIM) for h in range(H)]
    problems = []
    for h, sl in enumerate(heads):
        kout_ref[:, h, :] = k_ref[:, sl]
        vout_ref[:, h, :] = v_ref[:, sl]
        q = (q_ref[:, sl] * scale).astype(BF16)
        s = lax.dot_general(q, k_ref[:, sl].astype(BF16), NT, preferred_element_type=F32)
        problems.append(([s], [v_ref[:, sl].astype(BF16)]))
    for sl, o in zip(heads, _softmax_pv(problems)):
        o_ref[:, sl] = o.astype(o_ref.dtype)


def dense_attn(p, dst, *, nseq, T, H, col0, dst_col0):
    W = H * HEAD_DIM
    cb = col0 // W
    in_specs, args, aliases = _into(dst, [pl.BlockSpec((T, W), lambda b, k=k: (b, cb + k)) for k in range(3)],
                                    [p, p, p])
    cache_spec = pl.BlockSpec((None, T, H, HEAD_DIM), lambda b: (b, 0, 0, 0))
    cache_shape = jax.ShapeDtypeStruct((nseq, T, H, HEAD_DIM), p.dtype)
    return pl.pallas_call(
        _without_ref(functools.partial(_dense_attn_kernel, H=H, scale=HEAD_DIM ** -0.5 * LOG2E), 3),
        grid=(nseq,),
        in_specs=in_specs,
        out_specs=[pl.BlockSpec((T, W), lambda b: (b, dst_col0 // W)), cache_spec, cache_spec],
        out_shape=[jax.ShapeDtypeStruct(dst.shape, dst.dtype), cache_shape, cache_shape],
        input_output_aliases=aliases,
        compiler_params=_cparams("arbitrary"),
        name="dense_attn",
    )(*args)


NA_QROWS = 4
NA_KROWS = 12


def na_bias_tables(rpb, rows):
    W = GRID_W
    kr = min(NA_ROWS, rows)
    assert kr == NA_ROWS and rows >= NA_KROWS and rows % NA_QROWS == 0
    nblk = rows // NA_QROWS
    col = np.arange(W)
    cs = np.clip(col - NA_COLS // 2, 0, W - NA_COLS)
    col_ok = (col[None, :] >= cs[:, None]) & (col[None, :] < cs[:, None] + NA_COLS)
    ci = np.clip(col[None, :] - col[:, None] + NA_COLS - 1, 0, 2 * NA_COLS - 2)
    col_sel = (ci[..., None] == np.arange(2 * NA_COLS - 1)).astype(np.float32)
    G = jnp.einsum("hac,xyc->haxy", rpb.astype(F32), col_sel, precision=lax.Precision.HIGHEST)
    G = jnp.where(col_ok, G, NEG)
    outside = 2 * NA_ROWS - 1
    G = jnp.concatenate([G, jnp.full_like(G[:, :1], NEG)], axis=1)
    tables = []
    for blk in (0, 1, nblk - 1):
        r0 = blk * NA_QROWS
        u0 = min(max(r0 - NA_ROWS // 2, 0), rows - NA_KROWS)
        r = r0 + np.arange(NA_QROWS)
        ka = u0 + np.arange(NA_KROWS)
        start = np.clip(r - kr // 2, 0, rows - kr)
        row_ok = (ka[None, :] >= start[:, None]) & (ka[None, :] < start[:, None] + kr)
        ri = np.where(row_ok, ka[None, :] - r[:, None] + NA_ROWS - 1, outside)
        tables.append(jnp.concatenate(
            [jnp.concatenate([G[:, int(a)] for a in ri[q]], axis=-1) for q in range(NA_QROWS)], axis=-2))
    return jnp.stack(tables) * LOG2E


def _na_kernel(q_ref, k_ref, v_ref, kc_ref, vc_ref, bias_ref, o_ref, *, rows, scale, hp):
    blk = pl.program_id(2)
    u0 = jnp.clip(blk * NA_QROWS - NA_ROWS // 2, 0, rows - NA_KROWS)
    band = pl.ds(pl.multiple_of(u0 * GRID_W, GRID_W), NA_KROWS * GRID_W)
    heads = [slice(h * HEAD_DIM, (h + 1) * HEAD_DIM) for h in range(hp)]
    problems = []
    for h, sl in enumerate(heads):
        q = (q_ref[:, sl] * scale).astype(BF16)
        s_lat = lax.dot_general(q, k_ref[band, sl].astype(BF16), NT, preferred_element_type=F32) + bias_ref[h]
        s_ctx = lax.dot_general(q, kc_ref[:, sl].astype(BF16), NT, preferred_element_type=F32)
        problems.append(([s_lat, s_ctx], [v_ref[band, sl].astype(BF16), vc_ref[:, sl].astype(BF16)]))
    for sl, o in zip(heads, _softmax_pv(problems)):
        o_ref[:, sl] = o.astype(o_ref.dtype)


def na_attn(p, kc, vc, bias, dst, *, nseq, T, H, row0, col0, dst_col0, hp):
    rows = T // GRID_W
    nblk = rows // NA_QROWS
    tq = NA_QROWS * GRID_W
    L = kc.shape[0] // nseq
    W = hp * HEAD_DIM
    qb0, kb0, cb = row0 // tq, row0 // T, col0 // W
    ng = H // hp

    def kind(blk):
        return jnp.where(blk == 0, 0, jnp.where(blk == nblk - 1, 2, 1))

    in_specs, args, aliases = _into(dst, [
        pl.BlockSpec((tq, W), lambda b, h, i: (qb0 + b * nblk + i, cb + h)),
        pl.BlockSpec((T, W), lambda b, h, i: (kb0 + b, cb + ng + h)),
        pl.BlockSpec((T, W), lambda b, h, i: (kb0 + b, cb + 2 * ng + h)),
        pl.BlockSpec((L, W), lambda b, h, i: (b, h)),
        pl.BlockSpec((L, W), lambda b, h, i: (b, h)),
        pl.BlockSpec((None, hp, tq, NA_KROWS * GRID_W), lambda b, h, i: (kind(i), h, 0, 0))],
        [p, p, p, kc, vc, bias])
    return pl.pallas_call(
        _without_ref(functools.partial(_na_kernel, rows=rows, scale=HEAD_DIM ** -0.5 * LOG2E, hp=hp), 6),
        grid=(nseq, ng, nblk),
        in_specs=in_specs,
        out_specs=pl.BlockSpec((tq, W), lambda b, h, i: (qb0 + b * nblk + i, dst_col0 // W + h)),
        out_shape=jax.ShapeDtypeStruct(dst.shape, dst.dtype),
        input_output_aliases=aliases,
        compiler_params=_cparams("arbitrary", "arbitrary", "arbitrary"),
        name="na_attn",
    )(*args)


ROPE_SWAP = np.concatenate([np.arange(16, 32), np.arange(0, 16), np.arange(48, 64), np.arange(32, 48)])


def rope_tables(T, tm):
    t = jnp.arange(T)
    half = QK_ROPE // 2
    inv = jnp.power(ROPE_BASE, -jnp.arange(0, half, 2, dtype=F32) / half)
    ang_r = (t // GRID_W).astype(F32)[:, None] * inv
    ang_c = (t % GRID_W).astype(F32)[:, None] * inv
    cos = jnp.concatenate([jnp.cos(ang_r), jnp.cos(ang_r), jnp.cos(ang_c), jnp.cos(ang_c)], axis=-1)
    sin = jnp.concatenate([-jnp.sin(ang_r), jnp.sin(ang_r), -jnp.sin(ang_c), jnp.sin(ang_c)], axis=-1)
    cos = jnp.concatenate([cos, jnp.ones((T, LANES - QK_ROPE), F32)], axis=-1)
    sin = jnp.concatenate([sin, jnp.zeros((T, LANES - QK_ROPE), F32)], axis=-1)
    cos = jnp.concatenate([cos, jnp.ones((tm, LANES), F32)], axis=0)
    sin = jnp.concatenate([sin, jnp.zeros((tm, LANES), F32)], axis=0)
    return cos, sin


def _mla_in_kernel(x_ref, w_ref, qg_ref, kvg_ref, cos_ref, sin_ref, cq_ref, ckv32_ref, ckv16_ref, k2_ref, kpe_ref,
                   *, qr, kvr):
    pr = jnp.dot(x_ref[...], w_ref[...], preferred_element_type=F32)
    cq_ref[...] = _rms(pr[:, :qr], qg_ref[...]).astype(cq_ref.dtype)
    ckv = _rms(pr[:, qr:qr + kvr], kvg_ref[...])
    ckv32_ref[...] = ckv
    ckv16_ref[...] = ckv.astype(ckv16_ref.dtype)
    x = pr[:, qr + kvr:]
    kpe_ref[...] = x
    rot = x * cos_ref[...] + pltpu.roll(x, LANES // 2, axis=1) * sin_ref[...]
    lane = lax.broadcasted_iota(jnp.int32, rot.shape, 1)
    k2_ref[...] = jnp.where(lane < QK_ROPE, rot, 0.0).astype(k2_ref.dtype)


def mla_in(h, w_in_wide, q_norm_g, kv_norm_g, cos, sin, rows, *, qr, kvr, tm):
    n, D = h.shape
    pos = pl.BlockSpec((tm, LANES), lambda i: (rows.pos_block(i, tm), 0))

    def out(w):
        return pl.BlockSpec((tm, w), lambda i: (i, 0))

    return pl.pallas_call(
        functools.partial(_mla_in_kernel, qr=qr, kvr=kvr),
        grid=(n // tm,),
        in_specs=[pl.BlockSpec((tm, D), lambda i: (i, 0)), pl.BlockSpec(w_in_wide.shape, lambda i: (0, 0)),
                  _vec_spec(qr), _vec_spec(kvr), pos, pos],
        out_specs=[out(qr), out(kvr), out(kvr), out(LANES), out(LANES)],
        out_shape=[jax.ShapeDtypeStruct((n, qr), BF16), jax.ShapeDtypeStruct((n, kvr), F32),
                   jax.ShapeDtypeStruct((n, kvr), BF16), jax.ShapeDtypeStruct((n, LANES), BF16),
                   jax.ShapeDtypeStruct((n, LANES), F32)],
        compiler_params=_cparams("arbitrary"),
        name="mla_in",
    )(h, w_in_wide, q_norm_g.reshape(1, qr), kv_norm_g.reshape(1, kvr), cos, sin)


MLA_Q_IN = 3 * LANES
MLA_Q_OUT = 2 * LANES


def widen_w_uq(w_uq, heads):
    r = w_uq.shape[0]
    w = w_uq.reshape(r, heads, QK_NOPE + QK_ROPE)
    nope, pe = w[..., :QK_NOPE], w[..., QK_NOPE:]
    return jnp.concatenate([nope, pe, pe, pe[..., ROPE_SWAP], jnp.zeros_like(pe)], axis=-1).reshape(r, heads * MLA_Q_IN)


def _mla_q_kernel(x_ref, w_ref, cos_ref, sin_ref, o_ref, *, hpt, scale):
    acc = jnp.dot(x_ref[...], w_ref[...], preferred_element_type=F32)
    cos, sin = cos_ref[...], sin_ref[...]
    for j in range(hpt):
        a = acc[:, j * MLA_Q_IN:(j + 1) * MLA_Q_IN]
        o_ref[:, j * MLA_Q_OUT:j * MLA_Q_OUT + LANES] = (a[:, :LANES] * scale).astype(o_ref.dtype)
        q2 = a[:, LANES:2 * LANES] * cos + a[:, 2 * LANES:] * sin
        o_ref[:, j * MLA_Q_OUT + LANES:(j + 1) * MLA_Q_OUT] = (q2 * scale).astype(o_ref.dtype)


def mla_q(cq, w_uq_wide, cos, sin, rows, *, heads, scale, tm, hpt=4):
    n, r = cq.shape
    pos = pl.BlockSpec((tm, LANES), lambda i, j: (rows.pos_block(i, tm), 0))
    return pl.pallas_call(
        functools.partial(_mla_q_kernel, hpt=hpt, scale=scale),
        grid=(n // tm, heads // hpt),
        in_specs=[pl.BlockSpec((tm, r), lambda i, j: (i, 0)),
                  pl.BlockSpec((r, hpt * MLA_Q_IN), lambda i, j: (0, j)), pos, pos],
        out_specs=pl.BlockSpec((tm, hpt * MLA_Q_OUT), lambda i, j: (i, j)),
        out_shape=jax.ShapeDtypeStruct((n, heads * MLA_Q_OUT), BF16),
        compiler_params=_cparams("arbitrary", "arbitrary"),
        name="mla_q",
    )(cq, w_uq_wide, cos, sin)


def _mla_ctx_attn_kernel(q_ref, kv_ref, k2_ref, o_ref, *, heads):
    k2 = k2_ref[...]
    problems = []
    for h in range(heads):
        q = q_ref[:, h * MLA_Q_OUT:(h + 1) * MLA_Q_OUT]
        k = jnp.concatenate([kv_ref[:, 2 * h * LANES:(2 * h + 1) * LANES], k2], axis=1)
        s = lax.dot_general(q, k, NT, preferred_element_type=F32)
        problems.append(([s], [kv_ref[:, (2 * h + 1) * LANES:(2 * h + 2) * LANES]]))
    for h, o in enumerate(_softmax_pv(problems)):
        o_ref[:, h * V_DIM:(h + 1) * V_DIM] = o.astype(o_ref.dtype)


def mla_ctx_attn(q, kv, k2, dst, *, nseq, T, heads):
    in_specs, args, aliases = _into(dst, [pl.BlockSpec((T, heads * MLA_Q_OUT), lambda b: (b, 0)),
                                          pl.BlockSpec((T, kv.shape[1]), lambda b: (b, 0)),
                                          pl.BlockSpec((T, LANES), lambda b: (b, 0))], [q, kv, k2])
    return pl.pallas_call(
        _without_ref(functools.partial(_mla_ctx_attn_kernel, heads=heads), 3),
        grid=(nseq,),
        in_specs=in_specs,
        out_specs=pl.BlockSpec((T, heads * V_DIM), lambda b: (b, 0)),
        out_shape=jax.ShapeDtypeStruct(dst.shape, dst.dtype),
        input_output_aliases=aliases,
        compiler_params=_cparams("arbitrary"),
        name="mla_ctx_attn",
    )(*args)


def _mla_lat_attn_kernel(q_ref, k1_ref, v_ref, k2_ref, k1c_ref, vc_ref, k2c_ref, o_ref, *, sub):
    k = jnp.concatenate([k1_ref[...], k2_ref[...]], axis=1)
    kc = jnp.concatenate([k1c_ref[...], k2c_ref[...]], axis=1)
    blocks = [slice(s, s + sub) for s in range(0, q_ref.shape[0], sub)]
    problems = []
    for r in blocks:
        q = q_ref[r, :]
        s_lat = lax.dot_general(q, k, NT, preferred_element_type=F32)
        s_ctx = lax.dot_general(q, kc, NT, preferred_element_type=F32)
        problems.append(([s_lat, s_ctx], [v_ref[...], vc_ref[...]]))
    for r, o in zip(blocks, _softmax_pv(problems)):
        o_ref[r, :] = o.astype(o_ref.dtype)


def mla_lat_attn(q, kv, k2, kvc, k2c, dst, *, nseq, T, heads, row0, tq, sub):
    nq = T // tq
    P = kvc.shape[0] // nseq
    qb0, kb0 = row0 // tq, row0 // T
    in_specs, args, aliases = _into(dst, [
        pl.BlockSpec((tq, MLA_Q_OUT), lambda b, h, i: (qb0 + b * nq + i, h)),
        pl.BlockSpec((T, LANES), lambda b, h, i: (kb0 + b, 2 * h)),
        pl.BlockSpec((T, LANES), lambda b, h, i: (kb0 + b, 2 * h + 1)),
        pl.BlockSpec((T, LANES), lambda b, h, i: (kb0 + b, 0)),
        pl.BlockSpec((P, LANES), lambda b, h, i: (b, 2 * h)),
        pl.BlockSpec((P, LANES), lambda b, h, i: (b, 2 * h + 1)),
        pl.BlockSpec((P, LANES), lambda b, h, i: (b, 0))],
        [q, kv, kv, k2, kvc, kvc, k2c])
    return pl.pallas_call(
        _without_ref(functools.partial(_mla_lat_attn_kernel, sub=_fit(tq, sub)), 7),
        grid=(nseq, heads, nq),
        in_specs=in_specs,
        out_specs=pl.BlockSpec((tq, V_DIM), lambda b, h, i: (qb0 + b * nq + i, h)),
        out_shape=jax.ShapeDtypeStruct(dst.shape, dst.dtype),
        input_output_aliases=aliases,
        compiler_params=_cparams("arbitrary", "arbitrary", "arbitrary"),
        name="mla_lat_attn",
    )(*args)


def even_layer(h, rows, B, SEQ, Bd, state_f, state_b, cache_k, cache_v, lb_f, lb_b, w_in, j, hgrn_g, rpb, layer):
    D = h.shape[1]
    AW = D // 2
    AH = BH = AW // HEAD_DIM
    T = rows.t
    p = matmul_wcast(h, w_in, j, tm=1024, tn=1024, out_dtype=F32, name="even_in_proj")
    o, s_fw, s_bw = hgrn(p, lb_f, lb_b, hgrn_g, h, nseq=B, T=SEQ, row0=0, A_heads=AH, layer=layer, hb=_fit(AH, 4),
                         emit_state=True)
    o = hgrn(p, lb_f, lb_b, hgrn_g, o, nseq=Bd, T=T, row0=rows.nc, A_heads=AH, layer=layer, hb=_fit(AH, 2),
             init=(state_f, state_b))
    o, new_k, new_v = dense_attn(p, o, nseq=B, T=SEQ, H=BH, col0=5 * AW, dst_col0=AW)
    past = cache_k.shape[1]
    bias = na_bias_tables(rpb, T // GRID_W)
    o = na_attn(p, cache_k.reshape(Bd * past, BH * HEAD_DIM), cache_v.reshape(Bd * past, BH * HEAD_DIM), bias, o,
                nseq=Bd, T=T, H=BH, row0=rows.nc, col0=5 * AW, dst_col0=AW, hp=_fit(BH, 4))
    return o, s_fw, s_bw, new_k, new_v


def odd_layer(h, rows, B, SEQ, Bd, cache_ckv, cache_kpe, w_in, q_norm_g, w_uq, kv_norm_g, w_ukv):
    D = h.shape[1]
    heads = D // 128
    T = rows.t
    qr, kvr = w_uq.shape[0], w_ukv.shape[0]
    scale = (QK_NOPE + QK_ROPE) ** -0.5 * LOG2E
    tm = rows.tile(1024)
    w_in_wide = jnp.concatenate([w_in, w_in[:, qr + kvr + ROPE_SWAP]], axis=1).astype(BF16)
    cos, sin = rope_tables(T, tm)
    cq, ckv32, ckv16, k2, kpe = mla_in(h, w_in_wide, q_norm_g, kv_norm_g, cos, sin, rows, qr=qr, kvr=kvr, tm=tm)
    q = mla_q(cq, widen_w_uq(w_uq, heads).astype(BF16), cos, sin, rows, heads=heads, scale=scale, tm=tm,
              hpt=_fit(heads, 8))
    w_ukv16 = w_ukv.astype(BF16)
    kv = matmul(ckv16, w_ukv16, tm=2048, tn=2048, out_dtype=BF16, name="mla_kv")
    past = cache_ckv.shape[1]
    kvc = matmul(cache_ckv.reshape(Bd * past, kvr).astype(BF16), w_ukv16, tm=1024, tn=1024, out_dtype=BF16,
                 name="mla_kv_cache")
    k2c = jnp.concatenate([jnp.zeros((Bd * past, LANES - QK_ROPE), F32), cache_kpe.reshape(Bd * past, QK_ROPE)],
                          axis=1).astype(BF16)
    o = mla_ctx_attn(q, kv, k2, h, nseq=B, T=SEQ, heads=heads)
    o = mla_lat_attn(q, kv, k2, kvc, k2c, o, nseq=Bd, T=T, heads=heads, row0=rows.nc, tq=rows.tile(2048), sub=512)
    new_ckv = ckv32[:rows.nc].reshape(B, SEQ, kvr)
    new_kpe = kpe[:rows.nc, :QK_ROPE].reshape(B, SEQ, QK_ROPE)
    return o, new_ckv, new_kpe


def kernel(x_prompt, x_sample, state_hgrn_fwd, state_hgrn_bwd, cache_na_k, cache_na_v, cache_mla_ckv, cache_mla_kpe, c, c_ctx, ada_w, ada_b, norm_g, hgrn_lb_fwd, hgrn_lb_bwd, w_in_even, hgrn_norm_g, na_rpb, w_out_even, w_in_odd, mla_q_norm_g, w_uq, mla_kv_norm_g, w_ukv, w_out_odd, mlp_w1, mlp_w2):
    B, SEQ, D = x_prompt.shape
    Bd, T, _ = x_sample.shape
    depth = ada_w.shape[0]
    rows = Rows(B * SEQ, Bd * T, T)
    x = (x_prompt.reshape(rows.nc, D), x_sample.reshape(rows.nl, D))
    cvec = jnp.zeros((MOD_ROWS, D), F32).at[0].set(c_ctx).at[1:1 + Bd].set(c)
    mod = modulation(cvec, ada_w, ada_b).reshape(depth, MOD_ROWS * 6, 1, D)

    w2 = mlp_w2.astype(BF16)
    new_sf, new_sb, new_nk, new_nv, new_ckv, new_kpe = [], [], [], [], [], []
    h = prenorm(*x, norm_g[0, 0], mod[0], 0, rows)
    for l in range(depth):
        j = l // 2
        if l % 2 == 0:
            o, sf, sb, nk, nv = even_layer(h, rows, B, SEQ, Bd, state_hgrn_fwd[:, j], state_hgrn_bwd[:, j],
                                           cache_na_k[:, j], cache_na_v[:, j], hgrn_lb_fwd, hgrn_lb_bwd,
                                           w_in_even, j, hgrn_norm_g[j], na_rpb[j], l)
            w_out = w_out_even
            new_sf.append(sf)
            new_sb.append(sb)
            new_nk.append(nk)
            new_nv.append(nv)
        else:
            o, ckv, kpe = odd_layer(h, rows, B, SEQ, Bd, cache_mla_ckv[:, j], cache_mla_kpe[:, j], w_in_odd[j],
                                    mla_q_norm_g[j], w_uq[j], mla_kv_norm_g[j], w_ukv[j])
            w_out = w_out_odd
            new_ckv.append(ckv)
            new_kpe.append(kpe)
        x, h = matmul_post(o, w_out.astype(BF16), j, x, norm_g[l, 1], mod[l], 0, rows,
                           pre=(norm_g[l, 2], mod[l], 3), name="out_proj_post")
        a = matmul_wcast(h, mlp_w1, l, tm=2048, tn=1024, out_dtype=BF16, act="relu2", vmem_limit=VMEM_LIMIT_BIG,
                         name="mlp_up")
        down = dict(tm=1024, vmem_limit=VMEM_LIMIT_BIG, name="mlp_down_post")
        if l + 1 < depth:
            x, h = matmul_post(a, w2, l, x, norm_g[l, 3], mod[l], 3, rows,
                               pre=(norm_g[l + 1, 0], mod[l + 1], 0), tk=1024, **down)
        else:
            x = matmul_post(a, w2, l, x, norm_g[l, 3], mod[l], 3, rows, split_out=True, tk=512, **down)
    return (x[0].reshape(B, SEQ, D), x[1].reshape(Bd, T, D),
            jnp.stack(new_sf, axis=1), jnp.stack(new_sb, axis=1), jnp.stack(new_nk, axis=1),
            jnp.stack(new_nv, axis=1), jnp.stack(new_ckv, axis=1), jnp.stack(new_kpe, axis=1))
```

```python
import functools

import numpy as np
import jax
import jax.numpy as jnp
from jax import lax
from jax.experimental import pallas as pl
from jax.experimental.pallas import tpu as pltpu

F32 = jnp.float32
BF16 = jnp.bfloat16

GRID_W = 64
HEAD_DIM = 128
NA_ROWS = 8
NA_COLS = 16
QK_NOPE = 128
QK_ROPE = 64
V_DIM = 128
ROPE_BASE = 10000.0
EPS = 1e-6
NEG = -1e30
LOG2E = float(np.log2(np.e))

LANES = 128
SUBLANES = 8
VMEM_LIMIT = 48 * 1024 * 1024
VMEM_LIMIT_BIG = 60 * 1024 * 1024
MOD_ROWS = 16

NT = (((1,), (1,)), ((), ()))
TN = (((0,), (0,)), ((), ()))


def _cparams(*sem, vmem_limit=VMEM_LIMIT):
    return pltpu.CompilerParams(dimension_semantics=sem, vmem_limit_bytes=vmem_limit)


def _fit(n, pref):
    t = min(pref, n)
    while n % t:
        t //= 2
    return t


def _silu(x):
    return x * jax.nn.sigmoid(x)


def _rms(x, g):
    return x * lax.rsqrt(jnp.mean(x * x, axis=-1, keepdims=True) + EPS) * g


def _softmax_pv(problems):
    ms = [functools.reduce(jnp.maximum, [jnp.max(s, axis=-1, keepdims=True) for s in scores])
          for scores, _ in problems]
    ps = [[jnp.exp2(s - m) for s in scores] for (scores, _), m in zip(problems, ms)]
    ls = [functools.reduce(jnp.add, [jnp.sum(p, axis=-1, keepdims=True) for p in pp]) for pp in ps]
    os = [functools.reduce(jnp.add, [jnp.dot(p.astype(BF16), v, preferred_element_type=F32)
                                     for p, v in zip(pp, values)]) for pp, (_, values) in zip(ps, problems)]
    return [o / l for o, l in zip(os, ls)]


def _mod_kernel(c_ref, w_ref, b_ref, o_ref):
    s = _silu(c_ref[...]).astype(BF16)
    o_ref[...] = jnp.dot(s, w_ref[...].astype(BF16), preferred_element_type=F32) + b_ref[...]


def modulation(cvec, ada_w, ada_b, tn=512):
    L, D, N6 = ada_w.shape
    R = cvec.shape[0]
    tn = _fit(N6, tn)
    return pl.pallas_call(
        _mod_kernel,
        grid=(L, N6 // tn),
        in_specs=[pl.BlockSpec((R, D), lambda l, j: (0, 0)),
                  pl.BlockSpec((None, D, tn), lambda l, j: (l, 0, j)),
                  pl.BlockSpec((None, 1, tn), lambda l, j: (l, 0, j))],
        out_specs=pl.BlockSpec((None, R, tn), lambda l, j: (l, 0, j)),
        out_shape=jax.ShapeDtypeStruct((L, R, N6), F32),
        compiler_params=_cparams("arbitrary", "arbitrary"),
        name="modulation",
    )(cvec, ada_w, ada_b.reshape(L, 1, N6))


class Rows:
    def __init__(self, nc, nl, t):
        assert nc % t == 0, "latent sequences must start on a multiple of their length"
        self.nc, self.nl, self.t, self.n = nc, nl, t, nc + nl

    def tile(self, pref):
        return _fit(np.gcd(self.nc, self.t), pref)

    def mod_row(self, i, tm):
        ncb = self.nc // tm
        return jnp.where(i < ncb, 0, 1 + (i - ncb) // (self.t // tm))

    def pos_block(self, i, tm):
        ncb = self.nc // tm
        return jnp.where(i < ncb, self.t // tm, (i - ncb) % (self.t // tm))


def _mod_spec(rows, tm, k, D, first=0):
    return pl.BlockSpec((None, 1, D), lambda i, *_: (rows.mod_row(first + i, tm) * 6 + k, 0, 0))


def _vec_spec(D):
    return pl.BlockSpec((1, D), lambda i, *_: (0, 0))


def _split_specs(rows, tm, D):
    ncb = rows.nc // tm
    return [pl.BlockSpec((tm, D), lambda i, *_: (jnp.minimum(i, ncb - 1), 0)),
            pl.BlockSpec((tm, D), lambda i, *_: (jnp.maximum(i - ncb, 0), 0))]


def _into(dst, in_specs, args):
    return in_specs + [pl.BlockSpec(memory_space=pl.ANY)], args + [dst], {len(args): 0}


def _without_ref(kernel, idx):
    return lambda *refs: kernel(*refs[:idx], *refs[idx + 1:])


def _pre_kernel(xc_ref, xs_ref, g_ref, sh_ref, sc_ref, h_ref, *, ncb):
    x = jnp.where(pl.program_id(0) < ncb, xc_ref[...], xs_ref[...])
    h = _rms(x, g_ref[...]) * (1.0 + sc_ref[...]) + sh_ref[...]
    h_ref[...] = h.astype(h_ref.dtype)


def prenorm(xc, xs, g, mod3, k, rows, tm=512):
    D = xc.shape[1]
    tm = rows.tile(tm)
    return pl.pallas_call(
        functools.partial(_pre_kernel, ncb=rows.nc // tm),
        grid=(rows.n // tm,),
        in_specs=_split_specs(rows, tm, D) + [_vec_spec(D), _mod_spec(rows, tm, k, D),
                                              _mod_spec(rows, tm, k + 1, D)],
        out_specs=pl.BlockSpec((tm, D), lambda i: (i, 0)),
        out_shape=jax.ShapeDtypeStruct((rows.n, D), BF16),
        compiler_params=_cparams("arbitrary"),
        name="prenorm",
    )(xc, xs, g.reshape(1, D), mod3, mod3)


def _mm_kernel(x_ref, w_ref, o_ref, *scratch, nk, act):
    def finish(acc):
        if act == "relu2":
            a = jnp.maximum(acc, 0.0)
            acc = a * a
        o_ref[...] = acc.astype(o_ref.dtype)

    if nk == 1:
        finish(jnp.dot(x_ref[...], w_ref[...], preferred_element_type=F32))
        return
    acc_ref, = scratch
    k = pl.program_id(2)

    @pl.when(k == 0)
    def _():
        acc_ref[...] = jnp.zeros_like(acc_ref)

    acc_ref[...] += jnp.dot(x_ref[...], w_ref[...], preferred_element_type=F32)

    @pl.when(k == nk - 1)
    def _():
        finish(acc_ref[...])


def matmul(x, w, *, tm, tn, tk=None, out_dtype=F32, act=None, name="matmul"):
    M, K = x.shape
    _, N = w.shape
    tm, tn, tk = _fit(M, tm), _fit(N, tn), _fit(K, tk or K)
    nk = K // tk
    assert M % tm == 0 and N % tn == 0 and K % tk == 0
    return pl.pallas_call(
        functools.partial(_mm_kernel, nk=nk, act=act),
        grid=(M // tm, N // tn, nk),
        in_specs=[pl.BlockSpec((tm, tk), lambda i, j, k: (i, k)),
                  pl.BlockSpec((tk, tn), lambda i, j, k: (k, j))],
        out_specs=pl.BlockSpec((tm, tn), lambda i, j, k: (i, j)),
        out_shape=jax.ShapeDtypeStruct((M, N), out_dtype),
        scratch_shapes=[pltpu.VMEM((tm, tn), F32)] if nk > 1 else [],
        compiler_params=_cparams("arbitrary", "arbitrary", "arbitrary"),
        name=name,
    )(x, w)


def _mm_wcast_kernel(x_ref, w_ref, o_ref, wb_ref, *, act):
    @pl.when(pl.program_id(1) == 0)
    def _():
        wb_ref[...] = w_ref[...].astype(BF16)

    acc = jnp.dot(x_ref[...], wb_ref[...], preferred_element_type=F32)
    if act == "relu2":
        a = jnp.maximum(acc, 0.0)
        acc = a * a
    o_ref[...] = acc.astype(o_ref.dtype)


def matmul_wcast(x, w, layer, *, tm, tn, out_dtype=F32, act=None, vmem_limit=VMEM_LIMIT, name="matmul_wcast"):
    M, K = x.shape
    _, _, N = w.shape
    tm, tn = _fit(M, tm), _fit(N, tn)
    return pl.pallas_call(
        functools.partial(_mm_wcast_kernel, act=act),
        grid=(N // tn, M // tm),
        in_specs=[pl.BlockSpec((tm, K), lambda j, i: (i, 0)),
                  pl.BlockSpec((None, K, tn), lambda j, i: (layer, 0, j))],
        out_specs=pl.BlockSpec((tm, tn), lambda j, i: (i, j)),
        out_shape=jax.ShapeDtypeStruct((M, N), out_dtype),
        scratch_shapes=[pltpu.VMEM((K, tn), BF16)],
        compiler_params=_cparams("arbitrary", "arbitrary", vmem_limit=vmem_limit),
        name=name,
    )(x, w)


def _mm_post_kernel(*refs, nk, with_pre, sub, ncb, split_in):
    it = iter(refs)
    a_ref, w_ref = next(it), next(it)
    x_refs = [next(it) for _ in range(2 if split_in else 1)]
    gp_ref, gate_ref = next(it), next(it)
    gn_ref, sh_ref, sc_ref = (next(it), next(it), next(it)) if with_pre else (None, None, None)
    x1_ref = next(it)
    h_ref = next(it) if with_pre else None
    tm = a_ref.shape[0]
    is_ctx = pl.program_id(0) < ncb

    def finish(r, y):
        x = jnp.where(is_ctx, x_refs[0][r, :], x_refs[1][r, :]) if split_in else x_refs[0][r, :]
        x1 = x + gate_ref[...] * _rms(y, gp_ref[...])
        x1_ref[r, :] = x1
        if with_pre:
            h_ref[r, :] = (_rms(x1, gn_ref[...]) * (1.0 + sc_ref[...]) + sh_ref[...]).astype(h_ref.dtype)

    def product(r):
        return jnp.dot(a_ref[r, :], w_ref[...], preferred_element_type=F32)

    blocks = [slice(s, s + sub) for s in range(0, tm, sub)]
    everything = slice(None)

    if nk == 1:
        for r in blocks:
            finish(r, product(r))
        return
    k = pl.program_id(1)

    @pl.when(k == 0)
    def _():
        x1_ref[...] = product(everything)

    @pl.when(jnp.logical_and(k > 0, k < nk - 1))
    def _():
        x1_ref[...] += product(everything)

    @pl.when(k == nk - 1)
    def _():
        for r in blocks:
            finish(r, x1_ref[r, :] + product(r))


def matmul_post(a, w, layer, x, g_post, mod3, k, rows, pre=None, *, tiles=None, tm=512, tk=None, sub=256,
                vmem_limit=VMEM_LIMIT, name="matmul_post"):
    M, K = a.shape
    D = w.shape[2]
    tm = rows.tile(tm)
    tk = _fit(K, tk or K)
    nk = K // tk
    first, count = tiles or (0, M // tm)
    out_spec = pl.BlockSpec((tm, D), lambda i, kk: (i, 0))
    split_in = isinstance(x, tuple)
    assert not (split_in and first), "a row range of a split residual is not supported"
    x_specs, xs = ((_split_specs(rows, tm, D), list(x)) if split_in
                   else ([pl.BlockSpec((tm, D), lambda i, kk: (first + i, 0))], [x]))
    in_specs = [pl.BlockSpec((tm, tk), lambda i, kk: (first + i, kk)),
                pl.BlockSpec((None, tk, D), lambda i, kk: (layer, kk, 0)),
                *x_specs, _vec_spec(D), _mod_spec(rows, tm, k + 2, D, first)]
    args = [a, w, *xs, g_post.reshape(1, D), mod3]
    out_specs = [out_spec]
    out_shape = [jax.ShapeDtypeStruct((count * tm, D), F32)]
    if pre is not None:
        g_pre, mod3_pre, kp = pre
        in_specs += [_vec_spec(D), _mod_spec(rows, tm, kp, D, first), _mod_spec(rows, tm, kp + 1, D, first)]
        args += [g_pre.reshape(1, D), mod3_pre, mod3_pre]
        out_specs.append(out_spec)
        out_shape.append(jax.ShapeDtypeStruct((count * tm, D), BF16))
    out = pl.pallas_call(
        functools.partial(_mm_post_kernel, nk=nk, with_pre=pre is not None, sub=_fit(tm, sub), ncb=rows.nc // tm,
                          split_in=split_in),
        grid=(count, nk),
        in_specs=in_specs, out_specs=out_specs, out_shape=out_shape,
        compiler_params=_cparams("arbitrary", "arbitrary", vmem_limit=vmem_limit),
        name=name,
    )(*args)
    return out if len(out) > 1 else out[0]


HGRN_CHUNK = 64
HGRN_BLOCK = 256
HGRN_SAFE_EXPONENT = 60.0


def _hgrn_kernel(*refs, T, hb, layer, has_init, emit_state, unroll_blocks, unroll_scan):
    it = iter(refs)
    q_ref, ff_ref, fb_ref, i_ref, g_ref, lbf_ref, lbb_ref, ng_ref = (next(it) for _ in range(8))
    s0_refs = (next(it), next(it)) if has_init else None
    o_ref = next(it)
    s_out_refs = (next(it), next(it)) if emit_state else None
    qe_refs, ds_refs, oi_refs, ebt_refs = ((next(it), next(it)) for _ in range(4))
    C, R = HGRN_CHUNK, min(HGRN_BLOCK, T)
    nch, nblk, cpb, mid = T // C, T // R, R // C, C // 2
    f_refs, lb_refs = (ff_ref, fb_ref), (lbf_ref, lbb_ref)
    total_row = (C - 1, 0)

    def lower_bound(lb_ref, sl):
        z = lb_ref[:, sl]
        e = jnp.exp(z - jnp.max(z, axis=0, keepdims=True))
        sm = e / jnp.sum(e, axis=0, keepdims=True)
        return jnp.sum(sm[:layer + 1], axis=0, keepdims=True)

    row = lax.broadcasted_iota(jnp.int32, (R, R), 0)
    col = lax.broadcasted_iota(jnp.int32, (R, R), 1)
    same_chunk = (row // C) == (col // C)
    tri = (same_chunk & (row >= col), same_chunk & (row <= col))

    def cumsum(mask, x):
        hi = x.astype(BF16)
        r1 = x - hi.astype(F32)
        md = r1.astype(BF16)
        lo = (r1 - md.astype(F32)).astype(BF16)
        s = jnp.dot(mask.astype(BF16), jnp.concatenate([hi, md, lo], axis=1), preferred_element_type=F32)
        return s[:, :HEAD_DIM] + s[:, HEAD_DIM:2 * HEAD_DIM] + s[:, 2 * HEAD_DIM:]

    heads = [slice(h * HEAD_DIM, (h + 1) * HEAD_DIM) for h in range(hb)]
    lbs = [[lower_bound(lb_refs[d], sl) for sl in heads] for d in range(2)]
    chains = [(h, sl, d) for h, sl in enumerate(heads) for d in range(2)]

    def block_pass(blk, risk):
        r = pl.ds(pl.multiple_of(blk * R, R), R)
        qs, vb, kk, b, qm, km, kd, a = {}, {}, {}, {}, {}, {}, {}, {}
        for h, sl in enumerate(heads):
            qs[h] = _silu(q_ref[r, sl]) * HEAD_DIM ** -0.5
            vb[h] = i_ref[r, sl].astype(BF16)
        for h, sl, d in chains:
            lb = lbs[d][h]
            f = lb + (1.0 - lb) * jax.nn.sigmoid(f_refs[d][r, sl])
            kk[h, d] = 1.0 - f
            b[h, d] = cumsum(tri[d], jnp.log(f))
        for h, sl, d in chains:
            qmc, kmc, kd[h, d] = [], [], []
            for cc in range(cpb):
                cs = slice(cc * C, (cc + 1) * C)
                bc = b[h, d][cs]
                m = bc[mid:mid + 1]
                bt = bc[total_row[d]:total_row[d] + 1]
                qc = qs[h][cs] * jnp.exp(bc - m)
                kc = kk[h, d][cs] * jnp.exp(m - bc)
                risk = jnp.maximum(risk, jnp.maximum(jnp.abs(bc[:1] - m), jnp.abs(bc[C - 1:] - m)))
                rc = pl.ds(pl.multiple_of(blk * R + cc * C, C), C)
                qe_refs[d][rc, sl] = (qs[h][cs] * jnp.exp(bc)).astype(BF16)
                kd[h, d].append((kk[h, d][cs] * jnp.exp(bt - bc)).astype(BF16))
                slot = pl.ds(pl.multiple_of((blk * cpb + cc) * SUBLANES, SUBLANES), SUBLANES)
                ebt_refs[d][slot, sl] = jnp.broadcast_to(jnp.exp(bt), (SUBLANES, HEAD_DIM))
                qmc.append(qc.astype(BF16))
                kmc.append(kc.astype(BF16))
            qm[h, d] = jnp.concatenate(qmc, axis=0)
            km[h, d] = jnp.concatenate(kmc, axis=0)
        for h, sl, d in chains:
            s = lax.dot_general(qm[h, d], km[h, d], NT, preferred_element_type=F32)
            a[h, d] = jnp.where(tri[d], s, 0.0).astype(BF16)
        for h, sl, d in chains:
            oi_refs[d][r, sl] = jnp.dot(a[h, d], vb[h], preferred_element_type=F32)
        for h, sl, d in chains:
            for cc in range(cpb):
                rows_c = pl.ds(pl.multiple_of((blk * cpb + cc) * HEAD_DIM, HEAD_DIM), HEAD_DIM)
                ds_refs[d][rows_c, sl] = lax.dot_general(vb[h][cc * C:(cc + 1) * C], kd[h, d][cc], TN,
                                                         preferred_element_type=F32)
        return risk

    risk = lax.fori_loop(0, nblk, block_pass, jnp.zeros((1, HEAD_DIM), F32), unroll=unroll_blocks)

    def safe_scores(d, qs, kk, lf):
        scores = jnp.where(row == col, lax.dot_general(qs.astype(BF16), kk.astype(BF16), NT,
                                                       preferred_element_type=F32), 0.0)
        half = 1
        while half < C:
            same = (row // (2 * half)) == (col // (2 * half))
            r_up, c_up = (row % (2 * half)) >= half, (col % (2 * half)) >= half
            r_lo, c_lo = jnp.logical_not(r_up), jnp.logical_not(c_up)
            if d == 0:
                to_t, from_s, pair = same & r_up & c_up & (col <= row), same & r_lo & c_lo & (col > row), same & r_up & c_lo
            else:
                to_t, from_s, pair = same & r_lo & c_lo & (col >= row), same & r_up & c_up & (col < row), same & r_lo & c_up
            qd = (qs * jnp.exp(cumsum(to_t, lf))).astype(BF16)
            kd = (kk * jnp.exp(cumsum(from_s, lf))).astype(BF16)
            scores = scores + jnp.where(pair, lax.dot_general(qd, kd, NT, preferred_element_type=F32), 0.0)
            half *= 2
        return scores

    def safe_block(blk, carry):
        r = pl.ds(pl.multiple_of(blk * R, R), R)
        for h, sl, d in chains:
            lb = lbs[d][h]
            f = lb + (1.0 - lb) * jax.nn.sigmoid(f_refs[d][r, sl])
            scores = safe_scores(d, _silu(q_ref[r, sl]) * HEAD_DIM ** -0.5, 1.0 - f, jnp.log(f))
            oi_refs[d][r, sl] = jnp.dot(scores.astype(BF16), i_ref[r, sl].astype(BF16), preferred_element_type=F32)
        return carry

    @pl.when(jnp.max(risk) > HGRN_SAFE_EXPONENT)
    def _():
        lax.fori_loop(0, nblk, safe_block, 0)

    def scan_step(j, states):
        cs = [j, nch - 1 - j]
        rs = [pl.ds(pl.multiple_of(c * C, C), C) for c in cs]
        slots = [pl.ds(pl.multiple_of(c * SUBLANES, SUBLANES), SUBLANES) for c in cs]
        grow = [pl.ds(pl.multiple_of(c * HEAD_DIM, HEAD_DIM), HEAD_DIM) for c in cs]
        new = []
        for h, sl, d in chains:
            St = states[2 * h + d]
            oi_refs[d][rs[d], sl] += lax.dot_general(qe_refs[d][rs[d], sl], St.astype(BF16), NT,
                                                     preferred_element_type=F32)
            new.append(St * ebt_refs[d][slots[d], sl][:1] + ds_refs[d][grow[d], sl])
        return tuple(new)

    if has_init:
        init = tuple(s0_refs[d][h].T for h in range(hb) for d in range(2))
    else:
        init = tuple(jnp.zeros((HEAD_DIM, HEAD_DIM), F32) for _ in range(2 * hb))
    states = lax.fori_loop(0, nch, scan_step, init, unroll=unroll_scan)
    if emit_state:
        for h in range(hb):
            for d in range(2):
                s_out_refs[d][h] = states[2 * h + d].T

    def final_pass(blk, carry):
        r = pl.ds(pl.multiple_of(blk * R, R), R)
        for sl in heads:
            o = _rms(oi_refs[0][r, sl] + oi_refs[1][r, sl], ng_ref[...]) * _silu(g_ref[r, sl])
            o_ref[r, sl] = o.astype(o_ref.dtype)
        return carry

    lax.fori_loop(0, nblk, final_pass, 0)


def hgrn(p, lb_f, lb_b, norm_g, dst, *, nseq, T, row0, A_heads, layer, hb, init=None, emit_state=False,
         unroll_blocks=1, unroll_scan=8):
    assert T % min(HGRN_BLOCK, T) == 0 and A_heads % hb == 0
    rb0 = row0 // T
    H = A_heads
    W = hb * HEAD_DIM
    ng = H // hb

    def slab(k):
        return pl.BlockSpec((T, W), lambda b, h: (rb0 + b, k * ng + h))

    nl = lb_f.shape[0]
    in_specs = [slab(k) for k in range(5)] + [
        pl.BlockSpec((nl, W), lambda b, h: (0, h)),
        pl.BlockSpec((nl, W), lambda b, h: (0, h)),
        pl.BlockSpec((1, HEAD_DIM), lambda b, h: (0, 0))]
    args = [p] * 5 + [lb_f, lb_b, norm_g.reshape(1, HEAD_DIM)]
    state_spec = pl.BlockSpec((None, hb, HEAD_DIM, HEAD_DIM), lambda b, h: (b, h, 0, 0))
    if init is not None:
        in_specs += [state_spec, state_spec]
        args += list(init)
    in_specs, args, aliases = _into(dst, in_specs, args)
    out_specs = [pl.BlockSpec((T, W), lambda b, h: (rb0 + b, h))]
    out_shape = [jax.ShapeDtypeStruct(dst.shape, dst.dtype)]
    if emit_state:
        out_specs += [state_spec, state_spec]
        out_shape += [jax.ShapeDtypeStruct((nseq, H, HEAD_DIM, HEAD_DIM), F32)] * 2
    nch = T // HGRN_CHUNK
    scratch = ([pltpu.VMEM((T, W), BF16)] * 2
               + [pltpu.VMEM((nch * HEAD_DIM, W), F32)] * 2
               + [pltpu.VMEM((T, W), F32)] * 2
               + [pltpu.VMEM((nch * SUBLANES, W), F32)] * 2)
    body = functools.partial(_hgrn_kernel, T=T, hb=hb, layer=layer, has_init=init is not None, emit_state=emit_state,
                             unroll_blocks=unroll_blocks, unroll_scan=unroll_scan)
    out = pl.pallas_call(
        _without_ref(body, len(args) - 1),
        grid=(nseq, ng),
        in_specs=in_specs, out_specs=out_specs, out_shape=out_shape,
        input_output_aliases=aliases,
        scratch_shapes=scratch,
        compiler_params=_cparams("arbitrary", "arbitrary"),
        name="hgrn",
    )(*args)
    return out if emit_state else out[0]


def _dense_attn_kernel(q_ref, k_ref, v_ref, o_ref, kout_ref, vout_ref, *, H, scale):
    heads = [slice(h * HEAD_DIM, (h + 1) * HEAD_DIM) for h in range(H)]
    problems = []
    for h, sl in enumerate(heads):
        kout_ref[:, h, :] = k_ref[:, sl]
        vout_ref[:, h, :] = v_ref[:, sl]
        q = (q_ref[:, sl] * scale).astype(BF16)
        s = lax.dot_general(q, k_ref[:, sl].astype(BF16), NT, preferred_element_type=F32)
        problems.append(([s], [v_ref[:, sl].astype(BF16)]))
    for sl, o in zip(heads, _softmax_pv(problems)):
        o_ref[:, sl] = o.astype(o_ref.dtype)


def dense_attn(p, dst, *, nseq, T, H, col0, dst_col0):
    W = H * HEAD_DIM
    cb = col0 // W
    in_specs, args, aliases = _into(dst, [pl.BlockSpec((T, W), lambda b, k=k: (b, cb + k)) for k in range(3)],
                                    [p, p, p])
    cache_spec = pl.BlockSpec((None, T, H, HEAD_DIM), lambda b: (b, 0, 0, 0))
    cache_shape = jax.ShapeDtypeStruct((nseq, T, H, HEAD_DIM), p.dtype)
    return pl.pallas_call(
        _without_ref(functools.partial(_dense_attn_kernel, H=H, scale=HEAD_DIM ** -0.5 * LOG2E), 3),
        grid=(nseq,),
        in_specs=in_specs,
        out_specs=[pl.BlockSpec((T, W), lambda b: (b, dst_col0 // W)), cache_spec, cache_spec],
        out_shape=[jax.ShapeDtypeStruct(dst.shape, dst.dtype), cache_shape, cache_shape],
        input_output_aliases=aliases,
        compiler_params=_cparams("arbitrary"),
        name="dense_attn",
    )(*args)


NA_QROWS = 4
NA_KROWS = 12


def na_bias_tables(rpb, rows):
    W = GRID_W
    kr = min(NA_ROWS, rows)
    assert kr == NA_ROWS and rows >= NA_KROWS and rows % NA_QROWS == 0
    nblk = rows // NA_QROWS
    col = np.arange(W)
    cs = np.clip(col - NA_COLS // 2, 0, W - NA_COLS)
    col_ok = (col[None, :] >= cs[:, None]) & (col[None, :] < cs[:, None] + NA_COLS)
    ci = np.clip(col[None, :] - col[:, None] + NA_COLS - 1, 0, 2 * NA_COLS - 2)
    col_sel = (ci[..., None] == np.arange(2 * NA_COLS - 1)).astype(np.float32)
    G = jnp.einsum("hac,xyc->haxy", rpb.astype(F32), col_sel, precision=lax.Precision.HIGHEST)
    G = jnp.where(col_ok, G, NEG)
    outside = 2 * NA_ROWS - 1
    G = jnp.concatenate([G, jnp.full_like(G[:, :1], NEG)], axis=1)
    tables = []
    for blk in (0, 1, nblk - 1):
        r0 = blk * NA_QROWS
        u0 = min(max(r0 - NA_ROWS // 2, 0), rows - NA_KROWS)
        r = r0 + np.arange(NA_QROWS)
        ka = u0 + np.arange(NA_KROWS)
        start = np.clip(r - kr // 2, 0, rows - kr)
        row_ok = (ka[None, :] >= start[:, None]) & (ka[None, :] < start[:, None] + kr)
        ri = np.where(row_ok, ka[None, :] - r[:, None] + NA_ROWS - 1, outside)
        tables.append(jnp.concatenate(
            [jnp.concatenate([G[:, int(a)] for a in ri[q]], axis=-1) for q in range(NA_QROWS)], axis=-2))
    return jnp.stack(tables) * LOG2E


def _na_kernel(q_ref, k_ref, v_ref, kc_ref, vc_ref, bias_ref, o_ref, *, rows, scale, hp):
    blk = pl.program_id(2)
    u0 = jnp.clip(blk * NA_QROWS - NA_ROWS // 2, 0, rows - NA_KROWS)
    band = pl.ds(pl.multiple_of(u0 * GRID_W, GRID_W), NA_KROWS * GRID_W)
    heads = [slice(h * HEAD_DIM, (h + 1) * HEAD_DIM) for h in range(hp)]
    problems = []
    for h, sl in enumerate(heads):
        q = (q_ref[:, sl] * scale).astype(BF16)
        s_lat = lax.dot_general(q, k_ref[band, sl].astype(BF16), NT, preferred_element_type=F32) + bias_ref[h]
        s_ctx = lax.dot_general(q, kc_ref[:, sl].astype(BF16), NT, preferred_element_type=F32)
        problems.append(([s_lat, s_ctx], [v_ref[band, sl].astype(BF16), vc_ref[:, sl].astype(BF16)]))
    for sl, o in zip(heads, _softmax_pv(problems)):
        o_ref[:, sl] = o.astype(o_ref.dtype)


def na_attn(p, kc, vc, bias, dst, *, nseq, T, H, row0, col0, dst_col0, hp):
    rows = T // GRID_W
    nblk = rows // NA_QROWS
    tq = NA_QROWS * GRID_W
    L = kc.shape[0] // nseq
    W = hp * HEAD_DIM
    qb0, kb0, cb = row0 // tq, row0 // T, col0 // W
    ng = H // hp

    def kind(blk):
        return jnp.where(blk == 0, 0, jnp.where(blk == nblk - 1, 2, 1))

    in_specs, args, aliases = _into(dst, [
        pl.BlockSpec((tq, W), lambda b, h, i: (qb0 + b * nblk + i, cb + h)),
        pl.BlockSpec((T, W), lambda b, h, i: (kb0 + b, cb + ng + h)),
        pl.BlockSpec((T, W), lambda b, h, i: (kb0 + b, cb + 2 * ng + h)),
        pl.BlockSpec((L, W), lambda b, h, i: (b, h)),
        pl.BlockSpec((L, W), lambda b, h, i: (b, h)),
        pl.BlockSpec((None, hp, tq, NA_KROWS * GRID_W), lambda b, h, i: (kind(i), h, 0, 0))],
        [p, p, p, kc, vc, bias])
    return pl.pallas_call(
        _without_ref(functools.partial(_na_kernel, rows=rows, scale=HEAD_DIM ** -0.5 * LOG2E, hp=hp), 6),
        grid=(nseq, ng, nblk),
        in_specs=in_specs,
        out_specs=pl.BlockSpec((tq, W), lambda b, h, i: (qb0 + b * nblk + i, dst_col0 // W + h)),
        out_shape=jax.ShapeDtypeStruct(dst.shape, dst.dtype),
        input_output_aliases=aliases,
        compiler_params=_cparams("arbitrary", "arbitrary", "arbitrary"),
        name="na_attn",
    )(*args)


ROPE_SWAP = np.concatenate([np.arange(16, 32), np.arange(0, 16), np.arange(48, 64), np.arange(32, 48)])


def rope_tables(T, tm):
    t = jnp.arange(T)
    half = QK_ROPE // 2
    inv = jnp.power(ROPE_BASE, -jnp.arange(0, half, 2, dtype=F32) / half)
    ang_r = (t // GRID_W).astype(F32)[:, None] * inv
    ang_c = (t % GRID_W).astype(F32)[:, None] * inv
    cos = jnp.concatenate([jnp.cos(ang_r), jnp.cos(ang_r), jnp.cos(ang_c), jnp.cos(ang_c)], axis=-1)
    sin = jnp.concatenate([-jnp.sin(ang_r), jnp.sin(ang_r), -jnp.sin(ang_c), jnp.sin(ang_c)], axis=-1)
    cos = jnp.concatenate([cos, jnp.ones((T, LANES - QK_ROPE), F32)], axis=-1)
    sin = jnp.concatenate([sin, jnp.zeros((T, LANES - QK_ROPE), F32)], axis=-1)
    cos = jnp.concatenate([cos, jnp.ones((tm, LANES), F32)], axis=0)
    sin = jnp.concatenate([sin, jnp.zeros((tm, LANES), F32)], axis=0)
    return cos, sin


def _mla_in_kernel(x_ref, w_ref, qg_ref, kvg_ref, cos_ref, sin_ref, cq_ref, ckv16_ref, k2_ref, ckv32_ref, kpe_ref,
                   *, qr, kvr, ncb):
    pr = jnp.dot(x_ref[...], w_ref[...], preferred_element_type=F32)
    cq_ref[...] = _rms(pr[:, :qr], qg_ref[...]).astype(cq_ref.dtype)
    ckv = _rms(pr[:, qr:qr + kvr], kvg_ref[...])
    ckv16_ref[...] = ckv.astype(ckv16_ref.dtype)
    x = pr[:, qr + kvr:]

    @pl.when(pl.program_id(0) < ncb)
    def _():
        ckv32_ref[...] = ckv
        kpe_ref[...] = x

    rot = x * cos_ref[...] + pltpu.roll(x, LANES // 2, axis=1) * sin_ref[...]
    lane = lax.broadcasted_iota(jnp.int32, rot.shape, 1)
    k2_ref[...] = jnp.where(lane < QK_ROPE, rot, 0.0).astype(k2_ref.dtype)


def mla_in(h, w_in_wide, q_norm_g, kv_norm_g, cos, sin, rows, *, qr, kvr, tm):
    n, D = h.shape
    ncb = rows.nc // tm
    pos = pl.BlockSpec((tm, LANES), lambda i: (rows.pos_block(i, tm), 0))

    def out(w):
        return pl.BlockSpec((tm, w), lambda i: (i, 0))

    def ctx_out(w):
        return pl.BlockSpec((tm, w), lambda i: (jnp.minimum(i, ncb - 1), 0))

    return pl.pallas_call(
        functools.partial(_mla_in_kernel, qr=qr, kvr=kvr, ncb=ncb),
        grid=(n // tm,),
        in_specs=[pl.BlockSpec((tm, D), lambda i: (i, 0)), pl.BlockSpec(w_in_wide.shape, lambda i: (0, 0)),
                  _vec_spec(qr), _vec_spec(kvr), pos, pos],
        out_specs=[out(qr), out(kvr), out(LANES), ctx_out(kvr), ctx_out(LANES)],
        out_shape=[jax.ShapeDtypeStruct((n, qr), BF16), jax.ShapeDtypeStruct((n, kvr), BF16),
                   jax.ShapeDtypeStruct((n, LANES), BF16), jax.ShapeDtypeStruct((rows.nc, kvr), F32),
                   jax.ShapeDtypeStruct((rows.nc, LANES), F32)],
        compiler_params=_cparams("arbitrary"),
        name="mla_in",
    )(h, w_in_wide, q_norm_g.reshape(1, qr), kv_norm_g.reshape(1, kvr), cos, sin)


MLA_Q_IN = 3 * LANES
MLA_Q_OUT = 2 * LANES


def widen_w_uq(w_uq, heads):
    r = w_uq.shape[0]
    w = w_uq.reshape(r, heads, QK_NOPE + QK_ROPE)
    nope, pe = w[..., :QK_NOPE], w[..., QK_NOPE:]
    return jnp.concatenate([nope, pe, pe, pe[..., ROPE_SWAP], jnp.zeros_like(pe)], axis=-1).reshape(r, heads * MLA_Q_IN)


def _mla_q_kernel(x_ref, w_ref, cos_ref, sin_ref, o_ref, *, hpt, scale):
    acc = jnp.dot(x_ref[...], w_ref[...], preferred_element_type=F32)
    cos, sin = cos_ref[...], sin_ref[...]
    for j in range(hpt):
        a = acc[:, j * MLA_Q_IN:(j + 1) * MLA_Q_IN]
        o_ref[:, j * MLA_Q_OUT:j * MLA_Q_OUT + LANES] = (a[:, :LANES] * scale).astype(o_ref.dtype)
        q2 = a[:, LANES:2 * LANES] * cos + a[:, 2 * LANES:] * sin
        o_ref[:, j * MLA_Q_OUT + LANES:(j + 1) * MLA_Q_OUT] = (q2 * scale).astype(o_ref.dtype)


def mla_q(cq, w_uq_wide, cos, sin, rows, *, heads, scale, tm, hpt=4):
    n, r = cq.shape
    pos = pl.BlockSpec((tm, LANES), lambda i, j: (rows.pos_block(i, tm), 0))
    return pl.pallas_call(
        functools.partial(_mla_q_kernel, hpt=hpt, scale=scale),
        grid=(n // tm, heads // hpt),
        in_specs=[pl.BlockSpec((tm, r), lambda i, j: (i, 0)),
                  pl.BlockSpec((r, hpt * MLA_Q_IN), lambda i, j: (0, j)), pos, pos],
        out_specs=pl.BlockSpec((tm, hpt * MLA_Q_OUT), lambda i, j: (i, j)),
        out_shape=jax.ShapeDtypeStruct((n, heads * MLA_Q_OUT), BF16),
        compiler_params=_cparams("arbitrary", "arbitrary"),
        name="mla_q",
    )(cq, w_uq_wide, cos, sin)


def _mla_ctx_attn_kernel(q_ref, kv_ref, k2_ref, o_ref, *, heads):
    k2 = k2_ref[...]
    problems = []
    for h in range(heads):
        q = q_ref[:, h * MLA_Q_OUT:(h + 1) * MLA_Q_OUT]
        k = jnp.concatenate([kv_ref[:, 2 * h * LANES:(2 * h + 1) * LANES], k2], axis=1)
        s = lax.dot_general(q, k, NT, preferred_element_type=F32)
        problems.append(([s], [kv_ref[:, (2 * h + 1) * LANES:(2 * h + 2) * LANES]]))
    for h, o in enumerate(_softmax_pv(problems)):
        o_ref[:, h * V_DIM:(h + 1) * V_DIM] = o.astype(o_ref.dtype)


def mla_ctx_attn(q, kv, k2, dst, *, nseq, T, heads):
    in_specs, args, aliases = _into(dst, [pl.BlockSpec((T, heads * MLA_Q_OUT), lambda b: (b, 0)),
                                          pl.BlockSpec((T, kv.shape[1]), lambda b: (b, 0)),
                                          pl.BlockSpec((T, LANES), lambda b: (b, 0))], [q, kv, k2])
    return pl.pallas_call(
        _without_ref(functools.partial(_mla_ctx_attn_kernel, heads=heads), 3),
        grid=(nseq,),
        in_specs=in_specs,
        out_specs=pl.BlockSpec((T, heads * V_DIM), lambda b: (b, 0)),
        out_shape=jax.ShapeDtypeStruct(dst.shape, dst.dtype),
        input_output_aliases=aliases,
        compiler_params=_cparams("arbitrary"),
        name="mla_ctx_attn",
    )(*args)


def _mla_lat_attn_kernel(q_ref, k1_ref, v_ref, k2_ref, k1c_ref, vc_ref, k2c_ref, o_ref, *, sub):
    k = jnp.concatenate([k1_ref[...], k2_ref[...]], axis=1)
    kc = jnp.concatenate([k1c_ref[...], k2c_ref[...]], axis=1)
    blocks = [slice(s, s + sub) for s in range(0, q_ref.shape[0], sub)]
    problems = []
    for r in blocks:
        q = q_ref[r, :]
        s_lat = lax.dot_general(q, k, NT, preferred_element_type=F32)
        s_ctx = lax.dot_general(q, kc, NT, preferred_element_type=F32)
        problems.append(([s_lat, s_ctx], [v_ref[...], vc_ref[...]]))
    for r, o in zip(blocks, _softmax_pv(problems)):
        o_ref[r, :] = o.astype(o_ref.dtype)


def mla_lat_attn(q, kv, k2, kvc, k2c, dst, *, nseq, T, heads, row0, tq, sub):
    nq = T // tq
    P = kvc.shape[0] // nseq
    qb0, kb0 = row0 // tq, row0 // T
    in_specs, args, aliases = _into(dst, [
        pl.BlockSpec((tq, MLA_Q_OUT), lambda b, h, i: (qb0 + b * nq + i, h)),
        pl.BlockSpec((T, LANES), lambda b, h, i: (kb0 + b, 2 * h)),
        pl.BlockSpec((T, LANES), lambda b, h, i: (kb0 + b, 2 * h + 1)),
        pl.BlockSpec((T, LANES), lambda b, h, i: (kb0 + b, 0)),
        pl.BlockSpec((P, LANES), lambda b, h, i: (b, 2 * h)),
        pl.BlockSpec((P, LANES), lambda b, h, i: (b, 2 * h + 1)),
        pl.BlockSpec((P, LANES), lambda b, h, i: (b, 0))],
        [q, kv, kv, k2, kvc, kvc, k2c])
    return pl.pallas_call(
        _without_ref(functools.partial(_mla_lat_attn_kernel, sub=_fit(tq, sub)), 7),
        grid=(nseq, heads, nq),
        in_specs=in_specs,
        out_specs=pl.BlockSpec((tq, V_DIM), lambda b, h, i: (qb0 + b * nq + i, h)),
        out_shape=jax.ShapeDtypeStruct(dst.shape, dst.dtype),
        input_output_aliases=aliases,
        compiler_params=_cparams("arbitrary", "arbitrary", "arbitrary"),
        name="mla_lat_attn",
    )(*args)


def even_layer(h, rows, B, SEQ, Bd, state_f, state_b, cache_k, cache_v, lb_f, lb_b, w_in, j, hgrn_g, rpb, layer):
    D = h.shape[1]
    AW = D // 2
    AH = BH = AW // HEAD_DIM
    T = rows.t
    p = matmul_wcast(h, w_in, j, tm=1024, tn=1024, out_dtype=F32, name="even_in_proj")
    o, s_fw, s_bw = hgrn(p, lb_f, lb_b, hgrn_g, h, nseq=B, T=SEQ, row0=0, A_heads=AH, layer=layer, hb=_fit(AH, 4),
                         emit_state=True)
    o = hgrn(p, lb_f, lb_b, hgrn_g, o, nseq=Bd, T=T, row0=rows.nc, A_heads=AH, layer=layer, hb=_fit(AH, 2),
             init=(state_f, state_b))
    o, new_k, new_v = dense_attn(p, o, nseq=B, T=SEQ, H=BH, col0=5 * AW, dst_col0=AW)
    past = cache_k.shape[1]
    bias = na_bias_tables(rpb, T // GRID_W)
    o = na_attn(p, cache_k.reshape(Bd * past, BH * HEAD_DIM), cache_v.reshape(Bd * past, BH * HEAD_DIM), bias, o,
                nseq=Bd, T=T, H=BH, row0=rows.nc, col0=5 * AW, dst_col0=AW, hp=_fit(BH, 4))
    return o, s_fw, s_bw, new_k, new_v


def odd_layer(h, rows, B, SEQ, Bd, cache_ckv, cache_kpe, w_in, q_norm_g, w_uq, kv_norm_g, w_ukv):
    D = h.shape[1]
    heads = D // 128
    T = rows.t
    qr, kvr = w_uq.shape[0], w_ukv.shape[0]
    scale = (QK_NOPE + QK_ROPE) ** -0.5 * LOG2E
    tm = rows.tile(1024)
    w_in_wide = jnp.concatenate([w_in, w_in[:, qr + kvr + ROPE_SWAP]], axis=1).astype(BF16)
    cos, sin = rope_tables(T, tm)
    cq, ckv16, k2, ckv32, kpe = mla_in(h, w_in_wide, q_norm_g, kv_norm_g, cos, sin, rows, qr=qr, kvr=kvr, tm=tm)
    q = mla_q(cq, widen_w_uq(w_uq, heads).astype(BF16), cos, sin, rows, heads=heads, scale=scale, tm=tm,
              hpt=_fit(heads, 8))
    w_ukv16 = w_ukv.astype(BF16)
    kv = matmul(ckv16, w_ukv16, tm=2048, tn=2048, out_dtype=BF16, name="mla_kv")
    past = cache_ckv.shape[1]
    kvc = matmul(cache_ckv.reshape(Bd * past, kvr).astype(BF16), w_ukv16, tm=1024, tn=1024, out_dtype=BF16,
                 name="mla_kv_cache")
    k2c = jnp.concatenate([jnp.zeros((Bd * past, LANES - QK_ROPE), F32), cache_kpe.reshape(Bd * past, QK_ROPE)],
                          axis=1).astype(BF16)
    o = mla_ctx_attn(q, kv, k2, h, nseq=B, T=SEQ, heads=heads)
    o = mla_lat_attn(q, kv, k2, kvc, k2c, o, nseq=Bd, T=T, heads=heads, row0=rows.nc, tq=rows.tile(2048), sub=512)
    new_ckv = ckv32.reshape(B, SEQ, kvr)
    new_kpe = kpe[:, :QK_ROPE].reshape(B, SEQ, QK_ROPE)
    return o, new_ckv, new_kpe


def kernel(x_prompt, x_sample, state_hgrn_fwd, state_hgrn_bwd, cache_na_k, cache_na_v, cache_mla_ckv, cache_mla_kpe, c, c_ctx, ada_w, ada_b, norm_g, hgrn_lb_fwd, hgrn_lb_bwd, w_in_even, hgrn_norm_g, na_rpb, w_out_even, w_in_odd, mla_q_norm_g, w_uq, mla_kv_norm_g, w_ukv, w_out_odd, mlp_w1, mlp_w2):
    B, SEQ, D = x_prompt.shape
    Bd, T, _ = x_sample.shape
    depth = ada_w.shape[0]
    rows = Rows(B * SEQ, Bd * T, T)
    x = (x_prompt.reshape(rows.nc, D), x_sample.reshape(rows.nl, D))
    cvec = jnp.zeros((MOD_ROWS, D), F32).at[0].set(c_ctx).at[1:1 + Bd].set(c)
    mod = modulation(cvec, ada_w, ada_b).reshape(depth, MOD_ROWS * 6, 1, D)

    w2 = mlp_w2.astype(BF16)
    new_sf, new_sb, new_nk, new_nv, new_ckv, new_kpe = [], [], [], [], [], []
    h = prenorm(*x, norm_g[0, 0], mod[0], 0, rows)
    for l in range(depth):
        j = l // 2
        if l % 2 == 0:
            o, sf, sb, nk, nv = even_layer(h, rows, B, SEQ, Bd, state_hgrn_fwd[:, j], state_hgrn_bwd[:, j],
                                           cache_na_k[:, j], cache_na_v[:, j], hgrn_lb_fwd, hgrn_lb_bwd,
                                           w_in_even, j, hgrn_norm_g[j], na_rpb[j], l)
            w_out = w_out_even
            new_sf.append(sf)
            new_sb.append(sb)
            new_nk.append(nk)
            new_nv.append(nv)
        else:
            o, ckv, kpe = odd_layer(h, rows, B, SEQ, Bd, cache_mla_ckv[:, j], cache_mla_kpe[:, j], w_in_odd[j],
                                    mla_q_norm_g[j], w_uq[j], mla_kv_norm_g[j], w_ukv[j])
            w_out = w_out_odd
            new_ckv.append(ckv)
            new_kpe.append(kpe)
        x, h = matmul_post(o, w_out.astype(BF16), j, x, norm_g[l, 1], mod[l], 0, rows,
                           pre=(norm_g[l, 2], mod[l], 3), name="out_proj_post")
        a = matmul_wcast(h, mlp_w1, l, tm=2048, tn=1024, out_dtype=BF16, act="relu2", vmem_limit=VMEM_LIMIT_BIG,
                         name="mlp_up")
        down = dict(tm=1024, tk=1024, vmem_limit=VMEM_LIMIT_BIG, name="mlp_down_post")
        if l + 1 < depth:
            x, h = matmul_post(a, w2, l, x, norm_g[l, 3], mod[l], 3, rows, pre=(norm_g[l + 1, 0], mod[l + 1], 0),
                               **down)
        else:
            ncb = rows.nc // rows.tile(down["tm"])
            x = tuple(matmul_post(a, w2, l, x, norm_g[l, 3], mod[l], 3, rows, tiles=t, **down)
                      for t in ((0, ncb), (ncb, rows.nl // rows.tile(down["tm"]))))
    return (x[0].reshape(B, SEQ, D), x[1].reshape(Bd, T, D),
            jnp.stack(new_sf, axis=1), jnp.stack(new_sb, axis=1), jnp.stack(new_nk, axis=1),
            jnp.stack(new_nv, axis=1), jnp.stack(new_ckv, axis=1), jnp.stack(new_kpe, axis=1))
```

```python
import functools

import numpy as np
import jax
import jax.numpy as jnp
from jax import lax
from jax.experimental import pallas as pl
from jax.experimental.pallas import tpu as pltpu

F32 = jnp.float32
BF16 = jnp.bfloat16

GRID_W = 64
HEAD_DIM = 128
NA_ROWS = 8
NA_COLS = 16
QK_NOPE = 128
QK_ROPE = 64
V_DIM = 128
ROPE_BASE = 10000.0
EPS = 1e-6
NEG = -1e30
LOG2E = float(np.log2(np.e))

LANES = 128
SUBLANES = 8
VMEM_LIMIT = 48 * 1024 * 1024
VMEM_LIMIT_BIG = 60 * 1024 * 1024
MOD_ROWS = 16

NT = (((1,), (1,)), ((), ()))
TN = (((0,), (0,)), ((), ()))


def _cparams(*sem, vmem_limit=VMEM_LIMIT):
    return pltpu.CompilerParams(dimension_semantics=sem, vmem_limit_bytes=vmem_limit)


def _fit(n, pref):
    t = min(pref, n)
    while n % t:
        t //= 2
    return t


def _silu(x):
    return x * jax.nn.sigmoid(x)


def _rms(x, g):
    return x * lax.rsqrt(jnp.mean(x * x, axis=-1, keepdims=True) + EPS) * g


def _softmax_pv(problems):
    ms = [functools.reduce(jnp.maximum, [jnp.max(s, axis=-1, keepdims=True) for s in scores])
          for scores, _ in problems]
    ps = [[jnp.exp2(s - m) for s in scores] for (scores, _), m in zip(problems, ms)]
    ls = [functools.reduce(jnp.add, [jnp.sum(p, axis=-1, keepdims=True) for p in pp]) for pp in ps]
    os = [functools.reduce(jnp.add, [jnp.dot(p.astype(BF16), v, preferred_element_type=F32)
                                     for p, v in zip(pp, values)]) for pp, (_, values) in zip(ps, problems)]
    return [o / l for o, l in zip(os, ls)]


def _mod_kernel(c_ref, w_ref, b_ref, o_ref):
    s = _silu(c_ref[...]).astype(BF16)
    o_ref[...] = jnp.dot(s, w_ref[...].astype(BF16), preferred_element_type=F32) + b_ref[...]


def modulation(cvec, ada_w, ada_b, tn=512):
    L, D, N6 = ada_w.shape
    R = cvec.shape[0]
    tn = _fit(N6, tn)
    return pl.pallas_call(
        _mod_kernel,
        grid=(L, N6 // tn),
        in_specs=[pl.BlockSpec((R, D), lambda l, j: (0, 0)),
                  pl.BlockSpec((None, D, tn), lambda l, j: (l, 0, j)),
                  pl.BlockSpec((None, 1, tn), lambda l, j: (l, 0, j))],
        out_specs=pl.BlockSpec((None, R, tn), lambda l, j: (l, 0, j)),
        out_shape=jax.ShapeDtypeStruct((L, R, N6), F32),
        compiler_params=_cparams("arbitrary", "arbitrary"),
        name="modulation",
    )(cvec, ada_w, ada_b.reshape(L, 1, N6))


class Rows:
    def __init__(self, nc, nl, t):
        assert nc % t == 0, "latent sequences must start on a multiple of their length"
        self.nc, self.nl, self.t, self.n = nc, nl, t, nc + nl

    def tile(self, pref):
        return _fit(np.gcd(self.nc, self.t), pref)

    def mod_row(self, i, tm):
        ncb = self.nc // tm
        return jnp.where(i < ncb, 0, 1 + (i - ncb) // (self.t // tm))

    def pos_block(self, i, tm):
        ncb = self.nc // tm
        return jnp.where(i < ncb, self.t // tm, (i - ncb) % (self.t // tm))


def _mod_spec(rows, tm, k, D, first=0):
    return pl.BlockSpec((None, 1, D), lambda i, *_: (rows.mod_row(first + i, tm) * 6 + k, 0, 0))


def _vec_spec(D):
    return pl.BlockSpec((1, D), lambda i, *_: (0, 0))


def _split_specs(rows, tm, D):
    ncb = rows.nc // tm
    return [pl.BlockSpec((tm, D), lambda i, *_: (jnp.minimum(i, ncb - 1), 0)),
            pl.BlockSpec((tm, D), lambda i, *_: (jnp.maximum(i - ncb, 0), 0))]


def _into(dst, in_specs, args):
    return in_specs + [pl.BlockSpec(memory_space=pl.ANY)], args + [dst], {len(args): 0}


def _without_ref(kernel, idx):
    return lambda *refs: kernel(*refs[:idx], *refs[idx + 1:])


PRENORM_ROWS = 16


def _pre_kernel(xc_ref, xs_ref, g_ref, sh_ref, sc_ref, h_ref, *, ncb):
    def norm(x_ref):
        for s in range(0, h_ref.shape[0], PRENORM_ROWS):
            r = slice(s, s + PRENORM_ROWS)
            h = _rms(x_ref[r, :], g_ref[...]) * (1.0 + sc_ref[...]) + sh_ref[...]
            h_ref[r, :] = h.astype(h_ref.dtype)

    is_ctx = pl.program_id(0) < ncb
    pl.when(is_ctx)(lambda: norm(xc_ref))
    pl.when(jnp.logical_not(is_ctx))(lambda: norm(xs_ref))


def prenorm(xc, xs, g, mod3, k, rows, tm=512):
    D = xc.shape[1]
    tm = rows.tile(tm)
    return pl.pallas_call(
        functools.partial(_pre_kernel, ncb=rows.nc // tm),
        grid=(rows.n // tm,),
        in_specs=_split_specs(rows, tm, D) + [_vec_spec(D), _mod_spec(rows, tm, k, D),
                                              _mod_spec(rows, tm, k + 1, D)],
        out_specs=pl.BlockSpec((tm, D), lambda i: (i, 0)),
        out_shape=jax.ShapeDtypeStruct((rows.n, D), BF16),
        compiler_params=_cparams("arbitrary"),
        name="prenorm",
    )(xc, xs, g.reshape(1, D), mod3, mod3)


def _mm_kernel(x_ref, w_ref, o_ref, *scratch, nk, act):
    def finish(acc):
        if act == "relu2":
            a = jnp.maximum(acc, 0.0)
            acc = a * a
        o_ref[...] = acc.astype(o_ref.dtype)

    if nk == 1:
        finish(jnp.dot(x_ref[...], w_ref[...], preferred_element_type=F32))
        return
    acc_ref, = scratch
    k = pl.program_id(2)

    @pl.when(k == 0)
    def _():
        acc_ref[...] = jnp.zeros_like(acc_ref)

    acc_ref[...] += jnp.dot(x_ref[...], w_ref[...], preferred_element_type=F32)

    @pl.when(k == nk - 1)
    def _():
        finish(acc_ref[...])


def matmul(x, w, *, tm, tn, tk=None, out_dtype=F32, act=None, name="matmul"):
    M, K = x.shape
    _, N = w.shape
    tm, tn, tk = _fit(M, tm), _fit(N, tn), _fit(K, tk or K)
    nk = K // tk
    assert M % tm == 0 and N % tn == 0 and K % tk == 0
    return pl.pallas_call(
        functools.partial(_mm_kernel, nk=nk, act=act),
        grid=(M // tm, N // tn, nk),
        in_specs=[pl.BlockSpec((tm, tk), lambda i, j, k: (i, k)),
                  pl.BlockSpec((tk, tn), lambda i, j, k: (k, j))],
        out_specs=pl.BlockSpec((tm, tn), lambda i, j, k: (i, j)),
        out_shape=jax.ShapeDtypeStruct((M, N), out_dtype),
        scratch_shapes=[pltpu.VMEM((tm, tn), F32)] if nk > 1 else [],
        compiler_params=_cparams("arbitrary", "arbitrary", "arbitrary"),
        name=name,
    )(x, w)


def _mm_wcast_kernel(x_ref, w_ref, o_ref, wb_ref, *, act):
    @pl.when(pl.program_id(1) == 0)
    def _():
        wb_ref[...] = w_ref[...].astype(BF16)

    acc = jnp.dot(x_ref[...], wb_ref[...], preferred_element_type=F32)
    if act == "relu2":
        a = jnp.maximum(acc, 0.0)
        acc = a * a
    o_ref[...] = acc.astype(o_ref.dtype)


def matmul_wcast(x, w, layer, *, tm, tn, out_dtype=F32, act=None, vmem_limit=VMEM_LIMIT, name="matmul_wcast"):
    M, K = x.shape
    _, _, N = w.shape
    tm, tn = _fit(M, tm), _fit(N, tn)
    return pl.pallas_call(
        functools.partial(_mm_wcast_kernel, act=act),
        grid=(N // tn, M // tm),
        in_specs=[pl.BlockSpec((tm, K), lambda j, i: (i, 0)),
                  pl.BlockSpec((None, K, tn), lambda j, i: (layer, 0, j))],
        out_specs=pl.BlockSpec((tm, tn), lambda j, i: (i, j)),
        out_shape=jax.ShapeDtypeStruct((M, N), out_dtype),
        scratch_shapes=[pltpu.VMEM((K, tn), BF16)],
        compiler_params=_cparams("arbitrary", "arbitrary", vmem_limit=vmem_limit),
        name=name,
    )(x, w)


def _mm_post_kernel(*refs, nk, with_pre, sub, ncb, split_in):
    it = iter(refs)
    a_ref, w_ref = next(it), next(it)
    x_refs = [next(it) for _ in range(2 if split_in else 1)]
    gp_ref, gate_ref = next(it), next(it)
    gn_ref, sh_ref, sc_ref = (next(it), next(it), next(it)) if with_pre else (None, None, None)
    x1_ref = next(it)
    h_ref = next(it) if with_pre else None
    tm = a_ref.shape[0]
    is_ctx = pl.program_id(0) < ncb

    def finish(r, y):
        x = jnp.where(is_ctx, x_refs[0][r, :], x_refs[1][r, :]) if split_in else x_refs[0][r, :]
        x1 = x + gate_ref[...] * _rms(y, gp_ref[...])
        x1_ref[r, :] = x1
        if with_pre:
            h_ref[r, :] = (_rms(x1, gn_ref[...]) * (1.0 + sc_ref[...]) + sh_ref[...]).astype(h_ref.dtype)

    def product(r):
        return jnp.dot(a_ref[r, :], w_ref[...], preferred_element_type=F32)

    blocks = [slice(s, s + sub) for s in range(0, tm, sub)]
    everything = slice(None)

    if nk == 1:
        for r in blocks:
            finish(r, product(r))
        return
    k = pl.program_id(1)

    @pl.when(k == 0)
    def _():
        x1_ref[...] = product(everything)

    @pl.when(jnp.logical_and(k > 0, k < nk - 1))
    def _():
        x1_ref[...] += product(everything)

    @pl.when(k == nk - 1)
    def _():
        for r in blocks:
            finish(r, x1_ref[r, :] + product(r))


def matmul_post(a, w, layer, x, g_post, mod3, k, rows, pre=None, *, tiles=None, tm=512, tk=None, sub=256,
                vmem_limit=VMEM_LIMIT, name="matmul_post"):
    M, K = a.shape
    D = w.shape[2]
    tm = rows.tile(tm)
    tk = _fit(K, tk or K)
    nk = K // tk
    first, count = tiles or (0, M // tm)
    out_spec = pl.BlockSpec((tm, D), lambda i, kk: (i, 0))
    split_in = isinstance(x, tuple)
    assert not (split_in and first), "a row range of a split residual is not supported"
    x_specs, xs = ((_split_specs(rows, tm, D), list(x)) if split_in
                   else ([pl.BlockSpec((tm, D), lambda i, kk: (first + i, 0))], [x]))
    in_specs = [pl.BlockSpec((tm, tk), lambda i, kk: (first + i, kk)),
                pl.BlockSpec((None, tk, D), lambda i, kk: (layer, kk, 0)),
                *x_specs, _vec_spec(D), _mod_spec(rows, tm, k + 2, D, first)]
    args = [a, w, *xs, g_post.reshape(1, D), mod3]
    out_specs = [out_spec]
    out_shape = [jax.ShapeDtypeStruct((count * tm, D), F32)]
    if pre is not None:
        g_pre, mod3_pre, kp = pre
        in_specs += [_vec_spec(D), _mod_spec(rows, tm, kp, D, first), _mod_spec(rows, tm, kp + 1, D, first)]
        args += [g_pre.reshape(1, D), mod3_pre, mod3_pre]
        out_specs.append(out_spec)
        out_shape.append(jax.ShapeDtypeStruct((count * tm, D), BF16))
    out = pl.pallas_call(
        functools.partial(_mm_post_kernel, nk=nk, with_pre=pre is not None, sub=_fit(tm, sub), ncb=rows.nc // tm,
                          split_in=split_in),
        grid=(count, nk),
        in_specs=in_specs, out_specs=out_specs, out_shape=out_shape,
        compiler_params=_cparams("arbitrary", "arbitrary", vmem_limit=vmem_limit),
        name=name,
    )(*args)
    return out if len(out) > 1 else out[0]


HGRN_CHUNK = 64
HGRN_BLOCK = 256
HGRN_SAFE_EXPONENT = 60.0


def _hgrn_kernel(*refs, T, hb, layer, has_init, emit_state, unroll_blocks, unroll_scan):
    it = iter(refs)
    q_ref, ff_ref, fb_ref, i_ref, g_ref, lbf_ref, lbb_ref, ng_ref = (next(it) for _ in range(8))
    s0_refs = (next(it), next(it)) if has_init else None
    o_ref = next(it)
    s_out_refs = (next(it), next(it)) if emit_state else None
    qe_refs, ds_refs, oi_refs, ebt_refs = ((next(it), next(it)) for _ in range(4))
    C, R = HGRN_CHUNK, min(HGRN_BLOCK, T)
    nch, nblk, cpb, mid = T // C, T // R, R // C, C // 2
    f_refs, lb_refs = (ff_ref, fb_ref), (lbf_ref, lbb_ref)
    total_row = (C - 1, 0)

    def lower_bound(lb_ref, sl):
        z = lb_ref[:, sl]
        e = jnp.exp(z - jnp.max(z, axis=0, keepdims=True))
        sm = e / jnp.sum(e, axis=0, keepdims=True)
        return jnp.sum(sm[:layer + 1], axis=0, keepdims=True)

    row = lax.broadcasted_iota(jnp.int32, (R, R), 0)
    col = lax.broadcasted_iota(jnp.int32, (R, R), 1)
    same_chunk = (row // C) == (col // C)
    tri = (same_chunk & (row >= col), same_chunk & (row <= col))

    def cumsum(mask, x):
        hi = x.astype(BF16)
        r1 = x - hi.astype(F32)
        md = r1.astype(BF16)
        lo = (r1 - md.astype(F32)).astype(BF16)
        s = jnp.dot(mask.astype(BF16), jnp.concatenate([hi, md, lo], axis=1), preferred_element_type=F32)
        return s[:, :HEAD_DIM] + s[:, HEAD_DIM:2 * HEAD_DIM] + s[:, 2 * HEAD_DIM:]

    heads = [slice(h * HEAD_DIM, (h + 1) * HEAD_DIM) for h in range(hb)]
    lbs = [[lower_bound(lb_refs[d], sl) for sl in heads] for d in range(2)]
    chains = [(h, sl, d) for h, sl in enumerate(heads) for d in range(2)]

    def block_pass(blk, risk):
        r = pl.ds(pl.multiple_of(blk * R, R), R)
        qs, vb, kk, b, qm, km, kd, a = {}, {}, {}, {}, {}, {}, {}, {}
        for h, sl in enumerate(heads):
            qs[h] = _silu(q_ref[r, sl]) * HEAD_DIM ** -0.5
            vb[h] = i_ref[r, sl].astype(BF16)
        for h, sl, d in chains:
            lb = lbs[d][h]
            f = lb + (1.0 - lb) * jax.nn.sigmoid(f_refs[d][r, sl])
            kk[h, d] = 1.0 - f
            b[h, d] = cumsum(tri[d], jnp.log(f))
        for h, sl, d in chains:
            qmc, kmc, kd[h, d] = [], [], []
            for cc in range(cpb):
                cs = slice(cc * C, (cc + 1) * C)
                bc = b[h, d][cs]
                m = bc[mid:mid + 1]
                bt = bc[total_row[d]:total_row[d] + 1]
                qc = qs[h][cs] * jnp.exp(bc - m)
                kc = kk[h, d][cs] * jnp.exp(m - bc)
                risk = jnp.maximum(risk, jnp.maximum(jnp.abs(bc[:1] - m), jnp.abs(bc[C - 1:] - m)))
                rc = pl.ds(pl.multiple_of(blk * R + cc * C, C), C)
                qe_refs[d][rc, sl] = (qs[h][cs] * jnp.exp(bc)).astype(BF16)
                kd[h, d].append((kk[h, d][cs] * jnp.exp(bt - bc)).astype(BF16))
                slot = pl.ds(pl.multiple_of((blk * cpb + cc) * SUBLANES, SUBLANES), SUBLANES)
                ebt_refs[d][slot, sl] = jnp.broadcast_to(jnp.exp(bt), (SUBLANES, HEAD_DIM))
                qmc.append(qc.astype(BF16))
                kmc.append(kc.astype(BF16))
            qm[h, d] = jnp.concatenate(qmc, axis=0)
            km[h, d] = jnp.concatenate(kmc, axis=0)
        for h, sl, d in chains:
            s = lax.dot_general(qm[h, d], km[h, d], NT, preferred_element_type=F32)
            a[h, d] = jnp.where(tri[d], s, 0.0).astype(BF16)
        for h, sl, d in chains:
            oi_refs[d][r, sl] = jnp.dot(a[h, d], vb[h], preferred_element_type=F32)
        for h, sl, d in chains:
            for cc in range(cpb):
                rows_c = pl.ds(pl.multiple_of((blk * cpb + cc) * HEAD_DIM, HEAD_DIM), HEAD_DIM)
                ds_refs[d][rows_c, sl] = lax.dot_general(vb[h][cc * C:(cc + 1) * C], kd[h, d][cc], TN,
                                                         preferred_element_type=F32)
        return risk

    risk = lax.fori_loop(0, nblk, block_pass, jnp.zeros((1, HEAD_DIM), F32), unroll=unroll_blocks)

    def safe_scores(d, qs, kk, lf):
        scores = jnp.where(row == col, lax.dot_general(qs.astype(BF16), kk.astype(BF16), NT,
                                                       preferred_element_type=F32), 0.0)
        half = 1
        while half < C:
            same = (row // (2 * half)) == (col // (2 * half))
            r_up, c_up = (row % (2 * half)) >= half, (col % (2 * half)) >= half
            r_lo, c_lo = jnp.logical_not(r_up), jnp.logical_not(c_up)
            if d == 0:
                to_t, from_s, pair = same & r_up & c_up & (col <= row), same & r_lo & c_lo & (col > row), same & r_up & c_lo
            else:
                to_t, from_s, pair = same & r_lo & c_lo & (col >= row), same & r_up & c_up & (col < row), same & r_lo & c_up
            qd = (qs * jnp.exp(cumsum(to_t, lf))).astype(BF16)
            kd = (kk * jnp.exp(cumsum(from_s, lf))).astype(BF16)
            scores = scores + jnp.where(pair, lax.dot_general(qd, kd, NT, preferred_element_type=F32), 0.0)
            half *= 2
        return scores

    def safe_block(blk, carry):
        r = pl.ds(pl.multiple_of(blk * R, R), R)
        for h, sl, d in chains:
            lb = lbs[d][h]
            f = lb + (1.0 - lb) * jax.nn.sigmoid(f_refs[d][r, sl])
            scores = safe_scores(d, _silu(q_ref[r, sl]) * HEAD_DIM ** -0.5, 1.0 - f, jnp.log(f))
            oi_refs[d][r, sl] = jnp.dot(scores.astype(BF16), i_ref[r, sl].astype(BF16), preferred_element_type=F32)
        return carry

    @pl.when(jnp.max(risk) > HGRN_SAFE_EXPONENT)
    def _():
        lax.fori_loop(0, nblk, safe_block, 0)

    def scan_step(j, states):
        cs = [j, nch - 1 - j]
        rs = [pl.ds(pl.multiple_of(c * C, C), C) for c in cs]
        slots = [pl.ds(pl.multiple_of(c * SUBLANES, SUBLANES), SUBLANES) for c in cs]
        grow = [pl.ds(pl.multiple_of(c * HEAD_DIM, HEAD_DIM), HEAD_DIM) for c in cs]
        new = []
        for h, sl, d in chains:
            St = states[2 * h + d]
            oi_refs[d][rs[d], sl] += lax.dot_general(qe_refs[d][rs[d], sl], St.astype(BF16), NT,
                                                     preferred_element_type=F32)
            new.append(St * ebt_refs[d][slots[d], sl][:1] + ds_refs[d][grow[d], sl])
        return tuple(new)

    if has_init:
        init = tuple(s0_refs[d][h].T for h in range(hb) for d in range(2))
    else:
        init = tuple(jnp.zeros((HEAD_DIM, HEAD_DIM), F32) for _ in range(2 * hb))
    states = lax.fori_loop(0, nch, scan_step, init, unroll=unroll_scan)
    if emit_state:
        for h in range(hb):
            for d in range(2):
                s_out_refs[d][h] = states[2 * h + d].T

    def final_pass(blk, carry):
        r = pl.ds(pl.multiple_of(blk * R, R), R)
        for sl in heads:
            o = _rms(oi_refs[0][r, sl] + oi_refs[1][r, sl], ng_ref[...]) * _silu(g_ref[r, sl])
            o_ref[r, sl] = o.astype(o_ref.dtype)
        return carry

    lax.fori_loop(0, nblk, final_pass, 0)


def hgrn(p, lb_f, lb_b, norm_g, dst, *, nseq, T, row0, A_heads, layer, hb, init=None, emit_state=False,
         unroll_blocks=1, unroll_scan=8):
    assert T % min(HGRN_BLOCK, T) == 0 and A_heads % hb == 0
    rb0 = row0 // T
    H = A_heads
    W = hb * HEAD_DIM
    ng = H // hb

    def slab(k):
        return pl.BlockSpec((T, W), lambda b, h: (rb0 + b, k * ng + h))

    nl = lb_f.shape[0]
    in_specs = [slab(k) for k in range(5)] + [
        pl.BlockSpec((nl, W), lambda b, h: (0, h)),
        pl.BlockSpec((nl, W), lambda b, h: (0, h)),
        pl.BlockSpec((1, HEAD_DIM), lambda b, h: (0, 0))]
    args = [p] * 5 + [lb_f, lb_b, norm_g.reshape(1, HEAD_DIM)]
    state_spec = pl.BlockSpec((None, hb, HEAD_DIM, HEAD_DIM), lambda b, h: (b, h, 0, 0))
    if init is not None:
        in_specs += [state_spec, state_spec]
        args += list(init)
    in_specs, args, aliases = _into(dst, in_specs, args)
    out_specs = [pl.BlockSpec((T, W), lambda b, h: (rb0 + b, h))]
    out_shape = [jax.ShapeDtypeStruct(dst.shape, dst.dtype)]
    if emit_state:
        out_specs += [state_spec, state_spec]
        out_shape += [jax.ShapeDtypeStruct((nseq, H, HEAD_DIM, HEAD_DIM), F32)] * 2
    nch = T // HGRN_CHUNK
    scratch = ([pltpu.VMEM((T, W), BF16)] * 2
               + [pltpu.VMEM((nch * HEAD_DIM, W), F32)] * 2
               + [pltpu.VMEM((T, W), F32)] * 2
               + [pltpu.VMEM((nch * SUBLANES, W), F32)] * 2)
    body = functools.partial(_hgrn_kernel, T=T, hb=hb, layer=layer, has_init=init is not None, emit_state=emit_state,
                             unroll_blocks=unroll_blocks, unroll_scan=unroll_scan)
    out = pl.pallas_call(
        _without_ref(body, len(args) - 1),
        grid=(nseq, ng),
        in_specs=in_specs, out_specs=out_specs, out_shape=out_shape,
        input_output_aliases=aliases,
        scratch_shapes=scratch,
        compiler_params=_cparams("arbitrary", "arbitrary"),
        name="hgrn",
    )(*args)
    return out if emit_state else out[0]


def _dense_attn_kernel(q_ref, k_ref, v_ref, o_ref, kout_ref, vout_ref, *, H, scale):
    heads = [slice(h * HEAD_DIM, (h + 1) * HEAD_DIM) for h in range(H)]
    problems = []
    for h, sl in enumerate(heads):
        kout_ref[:, h, :] = k_ref[:, sl]
        vout_ref[:, h, :] = v_ref[:, sl]
        q = (q_ref[:, sl] * scale).astype(BF16)
        s = lax.dot_general(q, k_ref[:, sl].astype(BF16), NT, preferred_element_type=F32)
        problems.append(([s], [v_ref[:, sl].astype(BF16)]))
    for sl, o in zip(heads, _softmax_pv(problems)):
        o_ref[:, sl] = o.astype(o_ref.dtype)


def dense_attn(p, dst, *, nseq, T, H, col0, dst_col0):
    W = H * HEAD_DIM
    cb = col0 // W
    in_specs, args, aliases = _into(dst, [pl.BlockSpec((T, W), lambda b, k=k: (b, cb + k)) for k in range(3)],
                                    [p, p, p])
    cache_spec = pl.BlockSpec((None, T, H, HEAD_DIM), lambda b: (b, 0, 0, 0))
    cache_shape = jax.ShapeDtypeStruct((nseq, T, H, HEAD_DIM), p.dtype)
    return pl.pallas_call(
        _without_ref(functools.partial(_dense_attn_kernel, H=H, scale=HEAD_DIM ** -0.5 * LOG2E), 3),
        grid=(nseq,),
        in_specs=in_specs,
        out_specs=[pl.BlockSpec((T, W), lambda b: (b, dst_col0 // W)), cache_spec, cache_spec],
        out_shape=[jax.ShapeDtypeStruct(dst.shape, dst.dtype), cache_shape, cache_shape],
        input_output_aliases=aliases,
        compiler_params=_cparams("arbitrary"),
        name="dense_attn",
    )(*args)


NA_QROWS = 4
NA_KROWS = 12


def na_bias_tables(rpb, rows):
    W = GRID_W
    kr = min(NA_ROWS, rows)
    assert kr == NA_ROWS and rows >= NA_KROWS and rows % NA_QROWS == 0
    nblk = rows // NA_QROWS
    col = np.arange(W)
    cs = np.clip(col - NA_COLS // 2, 0, W - NA_COLS)
    col_ok = (col[None, :] >= cs[:, None]) & (col[None, :] < cs[:, None] + NA_COLS)
    ci = np.clip(col[None, :] - col[:, None] + NA_COLS - 1, 0, 2 * NA_COLS - 2)
    col_sel = (ci[..., None] == np.arange(2 * NA_COLS - 1)).astype(np.float32)
    G = jnp.einsum("hac,xyc->haxy", rpb.astype(F32), col_sel, precision=lax.Precision.HIGHEST)
    G = jnp.where(col_ok, G, NEG)
    outside = 2 * NA_ROWS - 1
    G = jnp.concatenate([G, jnp.full_like(G[:, :1], NEG)], axis=1)
    tables = []
    for blk in (0, 1, nblk - 1):
        r0 = blk * NA_QROWS
        u0 = min(max(r0 - NA_ROWS // 2, 0), rows - NA_KROWS)
        r = r0 + np.arange(NA_QROWS)
        ka = u0 + np.arange(NA_KROWS)
        start = np.clip(r - kr // 2, 0, rows - kr)
        row_ok = (ka[None, :] >= start[:, None]) & (ka[None, :] < start[:, None] + kr)
        ri = np.where(row_ok, ka[None, :] - r[:, None] + NA_ROWS - 1, outside)
        tables.append(jnp.concatenate(
            [jnp.concatenate([G[:, int(a)] for a in ri[q]], axis=-1) for q in range(NA_QROWS)], axis=-2))
    return jnp.stack(tables) * LOG2E


def _na_kernel(q_ref, k_ref, v_ref, kc_ref, vc_ref, bias_ref, o_ref, *, rows, scale, hp):
    blk = pl.program_id(2)
    u0 = jnp.clip(blk * NA_QROWS - NA_ROWS // 2, 0, rows - NA_KROWS)
    band = pl.ds(pl.multiple_of(u0 * GRID_W, GRID_W), NA_KROWS * GRID_W)
    heads = [slice(h * HEAD_DIM, (h + 1) * HEAD_DIM) for h in range(hp)]
    problems = []
    for h, sl in enumerate(heads):
        q = (q_ref[:, sl] * scale).astype(BF16)
        s_lat = lax.dot_general(q, k_ref[band, sl].astype(BF16), NT, preferred_element_type=F32) + bias_ref[h]
        s_ctx = lax.dot_general(q, kc_ref[:, sl].astype(BF16), NT, preferred_element_type=F32)
        problems.append(([s_lat, s_ctx], [v_ref[band, sl].astype(BF16), vc_ref[:, sl].astype(BF16)]))
    for sl, o in zip(heads, _softmax_pv(problems)):
        o_ref[:, sl] = o.astype(o_ref.dtype)


def na_attn(p, kc, vc, bias, dst, *, nseq, T, H, row0, col0, dst_col0, hp):
    rows = T // GRID_W
    nblk = rows // NA_QROWS
    tq = NA_QROWS * GRID_W
    L = kc.shape[0] // nseq
    W = hp * HEAD_DIM
    qb0, kb0, cb = row0 // tq, row0 // T, col0 // W
    ng = H // hp

    def kind(blk):
        return jnp.where(blk == 0, 0, jnp.where(blk == nblk - 1, 2, 1))

    in_specs, args, aliases = _into(dst, [
        pl.BlockSpec((tq, W), lambda b, h, i: (qb0 + b * nblk + i, cb + h)),
        pl.BlockSpec((T, W), lambda b, h, i: (kb0 + b, cb + ng + h)),
        pl.BlockSpec((T, W), lambda b, h, i: (kb0 + b, cb + 2 * ng + h)),
        pl.BlockSpec((L, W), lambda b, h, i: (b, h)),
        pl.BlockSpec((L, W), lambda b, h, i: (b, h)),
        pl.BlockSpec((None, hp, tq, NA_KROWS * GRID_W), lambda b, h, i: (kind(i), h, 0, 0))],
        [p, p, p, kc, vc, bias])
    return pl.pallas_call(
        _without_ref(functools.partial(_na_kernel, rows=rows, scale=HEAD_DIM ** -0.5 * LOG2E, hp=hp), 6),
        grid=(nseq, ng, nblk),
        in_specs=in_specs,
        out_specs=pl.BlockSpec((tq, W), lambda b, h, i: (qb0 + b * nblk + i, dst_col0 // W + h)),
        out_shape=jax.ShapeDtypeStruct(dst.shape, dst.dtype),
        input_output_aliases=aliases,
        compiler_params=_cparams("arbitrary", "arbitrary", "arbitrary"),
        name="na_attn",
    )(*args)


ROPE_SWAP = np.concatenate([np.arange(16, 32), np.arange(0, 16), np.arange(48, 64), np.arange(32, 48)])


def rope_tables(T, tm):
    t = jnp.arange(T)
    half = QK_ROPE // 2
    inv = jnp.power(ROPE_BASE, -jnp.arange(0, half, 2, dtype=F32) / half)
    ang_r = (t // GRID_W).astype(F32)[:, None] * inv
    ang_c = (t % GRID_W).astype(F32)[:, None] * inv
    cos = jnp.concatenate([jnp.cos(ang_r), jnp.cos(ang_r), jnp.cos(ang_c), jnp.cos(ang_c)], axis=-1)
    sin = jnp.concatenate([-jnp.sin(ang_r), jnp.sin(ang_r), -jnp.sin(ang_c), jnp.sin(ang_c)], axis=-1)
    cos = jnp.concatenate([cos, jnp.ones((T, LANES - QK_ROPE), F32)], axis=-1)
    sin = jnp.concatenate([sin, jnp.zeros((T, LANES - QK_ROPE), F32)], axis=-1)
    cos = jnp.concatenate([cos, jnp.ones((tm, LANES), F32)], axis=0)
    sin = jnp.concatenate([sin, jnp.zeros((tm, LANES), F32)], axis=0)
    return cos, sin


def _mla_in_kernel(x_ref, w_ref, qg_ref, kvg_ref, cos_ref, sin_ref, cq_ref, ckv16_ref, k2_ref, ckv32_ref, kpe_ref,
                   *, qr, kvr, ncb):
    pr = jnp.dot(x_ref[...], w_ref[...], preferred_element_type=F32)
    cq_ref[...] = _rms(pr[:, :qr], qg_ref[...]).astype(cq_ref.dtype)
    ckv = _rms(pr[:, qr:qr + kvr], kvg_ref[...])
    ckv16_ref[...] = ckv.astype(ckv16_ref.dtype)
    x = pr[:, qr + kvr:]

    @pl.when(pl.program_id(0) < ncb)
    def _():
        ckv32_ref[...] = ckv
        kpe_ref[...] = x

    rot = x * cos_ref[...] + pltpu.roll(x, LANES // 2, axis=1) * sin_ref[...]
    lane = lax.broadcasted_iota(jnp.int32, rot.shape, 1)
    k2_ref[...] = jnp.where(lane < QK_ROPE, rot, 0.0).astype(k2_ref.dtype)


def mla_in(h, w_in_wide, q_norm_g, kv_norm_g, cos, sin, rows, *, qr, kvr, tm):
    n, D = h.shape
    ncb = rows.nc // tm
    pos = pl.BlockSpec((tm, LANES), lambda i: (rows.pos_block(i, tm), 0))

    def out(w):
        return pl.BlockSpec((tm, w), lambda i: (i, 0))

    def ctx_out(w):
        return pl.BlockSpec((tm, w), lambda i: (jnp.minimum(i, ncb - 1), 0))

    return pl.pallas_call(
        functools.partial(_mla_in_kernel, qr=qr, kvr=kvr, ncb=ncb),
        grid=(n // tm,),
        in_specs=[pl.BlockSpec((tm, D), lambda i: (i, 0)), pl.BlockSpec(w_in_wide.shape, lambda i: (0, 0)),
                  _vec_spec(qr), _vec_spec(kvr), pos, pos],
        out_specs=[out(qr), out(kvr), out(LANES), ctx_out(kvr), ctx_out(LANES)],
        out_shape=[jax.ShapeDtypeStruct((n, qr), BF16), jax.ShapeDtypeStruct((n, kvr), BF16),
                   jax.ShapeDtypeStruct((n, LANES), BF16), jax.ShapeDtypeStruct((rows.nc, kvr), F32),
                   jax.ShapeDtypeStruct((rows.nc, LANES), F32)],
        compiler_params=_cparams("arbitrary"),
        name="mla_in",
    )(h, w_in_wide, q_norm_g.reshape(1, qr), kv_norm_g.reshape(1, kvr), cos, sin)


MLA_Q_IN = 3 * LANES
MLA_Q_OUT = 2 * LANES


def widen_w_uq(w_uq, heads):
    r = w_uq.shape[0]
    w = w_uq.reshape(r, heads, QK_NOPE + QK_ROPE)
    nope, pe = w[..., :QK_NOPE], w[..., QK_NOPE:]
    return jnp.concatenate([nope, pe, pe, pe[..., ROPE_SWAP], jnp.zeros_like(pe)], axis=-1).reshape(r, heads * MLA_Q_IN)


def _mla_q_kernel(x_ref, w_ref, cos_ref, sin_ref, o_ref, *, hpt, scale):
    acc = jnp.dot(x_ref[...], w_ref[...], preferred_element_type=F32)
    cos, sin = cos_ref[...], sin_ref[...]
    for j in range(hpt):
        a = acc[:, j * MLA_Q_IN:(j + 1) * MLA_Q_IN]
        o_ref[:, j * MLA_Q_OUT:j * MLA_Q_OUT + LANES] = (a[:, :LANES] * scale).astype(o_ref.dtype)
        q2 = a[:, LANES:2 * LANES] * cos + a[:, 2 * LANES:] * sin
        o_ref[:, j * MLA_Q_OUT + LANES:(j + 1) * MLA_Q_OUT] = (q2 * scale).astype(o_ref.dtype)


def mla_q(cq, w_uq_wide, cos, sin, rows, *, heads, scale, tm, hpt=4):
    n, r = cq.shape
    pos = pl.BlockSpec((tm, LANES), lambda i, j: (rows.pos_block(i, tm), 0))
    return pl.pallas_call(
        functools.partial(_mla_q_kernel, hpt=hpt, scale=scale),
        grid=(n // tm, heads // hpt),
        in_specs=[pl.BlockSpec((tm, r), lambda i, j: (i, 0)),
                  pl.BlockSpec((r, hpt * MLA_Q_IN), lambda i, j: (0, j)), pos, pos],
        out_specs=pl.BlockSpec((tm, hpt * MLA_Q_OUT), lambda i, j: (i, j)),
        out_shape=jax.ShapeDtypeStruct((n, heads * MLA_Q_OUT), BF16),
        compiler_params=_cparams("arbitrary", "arbitrary"),
        name="mla_q",
    )(cq, w_uq_wide, cos, sin)


def _mla_ctx_attn_kernel(q_ref, kv_ref, k2_ref, o_ref, *, heads):
    k2 = k2_ref[...]
    problems = []
    for h in range(heads):
        q = q_ref[:, h * MLA_Q_OUT:(h + 1) * MLA_Q_OUT]
        k = jnp.concatenate([kv_ref[:, 2 * h * LANES:(2 * h + 1) * LANES], k2], axis=1)
        s = lax.dot_general(q, k, NT, preferred_element_type=F32)
        problems.append(([s], [kv_ref[:, (2 * h + 1) * LANES:(2 * h + 2) * LANES]]))
    for h, o in enumerate(_softmax_pv(problems)):
        o_ref[:, h * V_DIM:(h + 1) * V_DIM] = o.astype(o_ref.dtype)


def mla_ctx_attn(q, kv, k2, dst, *, nseq, T, heads):
    in_specs, args, aliases = _into(dst, [pl.BlockSpec((T, heads * MLA_Q_OUT), lambda b: (b, 0)),
                                          pl.BlockSpec((T, kv.shape[1]), lambda b: (b, 0)),
                                          pl.BlockSpec((T, LANES), lambda b: (b, 0))], [q, kv, k2])
    return pl.pallas_call(
        _without_ref(functools.partial(_mla_ctx_attn_kernel, heads=heads), 3),
        grid=(nseq,),
        in_specs=in_specs,
        out_specs=pl.BlockSpec((T, heads * V_DIM), lambda b: (b, 0)),
        out_shape=jax.ShapeDtypeStruct(dst.shape, dst.dtype),
        input_output_aliases=aliases,
        compiler_params=_cparams("arbitrary"),
        name="mla_ctx_attn",
    )(*args)


def _mla_lat_attn_kernel(q_ref, k1_ref, v_ref, k2_ref, k1c_ref, vc_ref, k2c_ref, o_ref, *, sub):
    k = jnp.concatenate([k1_ref[...], k2_ref[...]], axis=1)
    kc = jnp.concatenate([k1c_ref[...], k2c_ref[...]], axis=1)
    blocks = [slice(s, s + sub) for s in range(0, q_ref.shape[0], sub)]
    problems = []
    for r in blocks:
        q = q_ref[r, :]
        s_lat = lax.dot_general(q, k, NT, preferred_element_type=F32)
        s_ctx = lax.dot_general(q, kc, NT, preferred_element_type=F32)
        problems.append(([s_lat, s_ctx], [v_ref[...], vc_ref[...]]))
    for r, o in zip(blocks, _softmax_pv(problems)):
        o_ref[r, :] = o.astype(o_ref.dtype)


def mla_lat_attn(q, kv, k2, kvc, k2c, dst, *, nseq, T, heads, row0, tq, sub):
    nq = T // tq
    P = kvc.shape[0] // nseq
    qb0, kb0 = row0 // tq, row0 // T
    in_specs, args, aliases = _into(dst, [
        pl.BlockSpec((tq, MLA_Q_OUT), lambda b, h, i: (qb0 + b * nq + i, h)),
        pl.BlockSpec((T, LANES), lambda b, h, i: (kb0 + b, 2 * h)),
        pl.BlockSpec((T, LANES), lambda b, h, i: (kb0 + b, 2 * h + 1)),
        pl.BlockSpec((T, LANES), lambda b, h, i: (kb0 + b, 0)),
        pl.BlockSpec((P, LANES), lambda b, h, i: (b, 2 * h)),
        pl.BlockSpec((P, LANES), lambda b, h, i: (b, 2 * h + 1)),
        pl.BlockSpec((P, LANES), lambda b, h, i: (b, 0))],
        [q, kv, kv, k2, kvc, kvc, k2c])
    return pl.pallas_call(
        _without_ref(functools.partial(_mla_lat_attn_kernel, sub=_fit(tq, sub)), 7),
        grid=(nseq, heads, nq),
        in_specs=in_specs,
        out_specs=pl.BlockSpec((tq, V_DIM), lambda b, h, i: (qb0 + b * nq + i, h)),
        out_shape=jax.ShapeDtypeStruct(dst.shape, dst.dtype),
        input_output_aliases=aliases,
        compiler_params=_cparams("arbitrary", "arbitrary", "arbitrary"),
        name="mla_lat_attn",
    )(*args)


def even_layer(h, rows, B, SEQ, Bd, state_f, state_b, cache_k, cache_v, lb_f, lb_b, w_in, j, hgrn_g, rpb, layer):
    D = h.shape[1]
    AW = D // 2
    AH = BH = AW // HEAD_DIM
    T = rows.t
    p = matmul_wcast(h, w_in, j, tm=1024, tn=1024, out_dtype=F32, name="even_in_proj")
    o, s_fw, s_bw = hgrn(p, lb_f, lb_b, hgrn_g, h, nseq=B, T=SEQ, row0=0, A_heads=AH, layer=layer, hb=_fit(AH, 8),
                         emit_state=True)
    o = hgrn(p, lb_f, lb_b, hgrn_g, o, nseq=Bd, T=T, row0=rows.nc, A_heads=AH, layer=layer, hb=_fit(AH, 2),
             init=(state_f, state_b))
    o, new_k, new_v = dense_attn(p, o, nseq=B, T=SEQ, H=BH, col0=5 * AW, dst_col0=AW)
    past = cache_k.shape[1]
    bias = na_bias_tables(rpb, T // GRID_W)
    o = na_attn(p, cache_k.reshape(Bd * past, BH * HEAD_DIM), cache_v.reshape(Bd * past, BH * HEAD_DIM), bias, o,
                nseq=Bd, T=T, H=BH, row0=rows.nc, col0=5 * AW, dst_col0=AW, hp=_fit(BH, 4))
    return o, s_fw, s_bw, new_k, new_v


def odd_layer(h, rows, B, SEQ, Bd, cache_ckv, cache_kpe, w_in, q_norm_g, w_uq, kv_norm_g, w_ukv):
    D = h.shape[1]
    heads = D // 128
    T = rows.t
    qr, kvr = w_uq.shape[0], w_ukv.shape[0]
    scale = (QK_NOPE + QK_ROPE) ** -0.5 * LOG2E
    tm = rows.tile(1024)
    w_in_wide = jnp.concatenate([w_in, w_in[:, qr + kvr + ROPE_SWAP]], axis=1).astype(BF16)
    cos, sin = rope_tables(T, tm)
    cq, ckv16, k2, ckv32, kpe = mla_in(h, w_in_wide, q_norm_g, kv_norm_g, cos, sin, rows, qr=qr, kvr=kvr, tm=tm)
    q = mla_q(cq, widen_w_uq(w_uq, heads).astype(BF16), cos, sin, rows, heads=heads, scale=scale, tm=tm,
              hpt=_fit(heads, 8))
    w_ukv16 = w_ukv.astype(BF16)
    kv = matmul(ckv16, w_ukv16, tm=2048, tn=2048, out_dtype=BF16, name="mla_kv")
    past = cache_ckv.shape[1]
    kvc = matmul(cache_ckv.reshape(Bd * past, kvr).astype(BF16), w_ukv16, tm=1024, tn=1024, out_dtype=BF16,
                 name="mla_kv_cache")
    k2c = jnp.concatenate([jnp.zeros((Bd * past, LANES - QK_ROPE), F32), cache_kpe.reshape(Bd * past, QK_ROPE)],
                          axis=1).astype(BF16)
    o = mla_ctx_attn(q, kv, k2, h, nseq=B, T=SEQ, heads=heads)
    o = mla_lat_attn(q, kv, k2, kvc, k2c, o, nseq=Bd, T=T, heads=heads, row0=rows.nc, tq=rows.tile(2048), sub=512)
    new_ckv = ckv32.reshape(B, SEQ, kvr)
    new_kpe = kpe[:, :QK_ROPE].reshape(B, SEQ, QK_ROPE)
    return o, new_ckv, new_kpe


def kernel(x_prompt, x_sample, state_hgrn_fwd, state_hgrn_bwd, cache_na_k, cache_na_v, cache_mla_ckv, cache_mla_kpe, c, c_ctx, ada_w, ada_b, norm_g, hgrn_lb_fwd, hgrn_lb_bwd, w_in_even, hgrn_norm_g, na_rpb, w_out_even, w_in_odd, mla_q_norm_g, w_uq, mla_kv_norm_g, w_ukv, w_out_odd, mlp_w1, mlp_w2):
    B, SEQ, D = x_prompt.shape
    Bd, T, _ = x_sample.shape
    depth = ada_w.shape[0]
    rows = Rows(B * SEQ, Bd * T, T)
    x = (x_prompt.reshape(rows.nc, D), x_sample.reshape(rows.nl, D))
    cvec = jnp.zeros((MOD_ROWS, D), F32).at[0].set(c_ctx).at[1:1 + Bd].set(c)
    mod = modulation(cvec, ada_w, ada_b).reshape(depth, MOD_ROWS * 6, 1, D)

    w2 = mlp_w2.astype(BF16)
    new_sf, new_sb, new_nk, new_nv, new_ckv, new_kpe = [], [], [], [], [], []
    h = prenorm(*x, norm_g[0, 0], mod[0], 0, rows)
    for l in range(depth):
        j = l // 2
        if l % 2 == 0:
            o, sf, sb, nk, nv = even_layer(h, rows, B, SEQ, Bd, state_hgrn_fwd[:, j], state_hgrn_bwd[:, j],
                                           cache_na_k[:, j], cache_na_v[:, j], hgrn_lb_fwd, hgrn_lb_bwd,
                                           w_in_even, j, hgrn_norm_g[j], na_rpb[j], l)
            w_out = w_out_even
            new_sf.append(sf)
            new_sb.append(sb)
            new_nk.append(nk)
            new_nv.append(nv)
        else:
            o, ckv, kpe = odd_layer(h, rows, B, SEQ, Bd, cache_mla_ckv[:, j], cache_mla_kpe[:, j], w_in_odd[j],
                                    mla_q_norm_g[j], w_uq[j], mla_kv_norm_g[j], w_ukv[j])
            w_out = w_out_odd
            new_ckv.append(ckv)
            new_kpe.append(kpe)
        x, h = matmul_post(o, w_out.astype(BF16), j, x, norm_g[l, 1], mod[l], 0, rows,
                           pre=(norm_g[l, 2], mod[l], 3), name="out_proj_post")
        a = matmul_wcast(h, mlp_w1, l, tm=2048, tn=1024, out_dtype=BF16, act="relu2", vmem_limit=VMEM_LIMIT_BIG,
                         name="mlp_up")
        down = dict(tm=1024, tk=1024, vmem_limit=VMEM_LIMIT_BIG, name="mlp_down_post")
        if l + 1 < depth:
            x, h = matmul_post(a, w2, l, x, norm_g[l, 3], mod[l], 3, rows, pre=(norm_g[l + 1, 0], mod[l + 1], 0),
                               **down)
        else:
            ncb = rows.nc // rows.tile(down["tm"])
            x = tuple(matmul_post(a, w2, l, x, norm_g[l, 3], mod[l], 3, rows, tiles=t, **down)
                      for t in ((0, ncb), (ncb, rows.nl // rows.tile(down["tm"]))))
    return (x[0].reshape(B, SEQ, D), x[1].reshape(Bd, T, D),
            jnp.stack(new_sf, axis=1), jnp.stack(new_sb, axis=1), jnp.stack(new_nk, axis=1),
            jnp.stack(new_nv, axis=1), jnp.stack(new_ckv, axis=1), jnp.stack(new_kpe, axis=1))
```

```python
import functools

import numpy as np
import jax
import jax.numpy as jnp
from jax import lax
from jax.experimental import pallas as pl
from jax.experimental.pallas import tpu as pltpu

F32 = jnp.float32
BF16 = jnp.bfloat16

GRID_W = 64
HEAD_DIM = 128
NA_ROWS = 8
NA_COLS = 16
QK_NOPE = 128
QK_ROPE = 64
V_DIM = 128
ROPE_BASE = 10000.0
EPS = 1e-6
NEG = -1e30
LOG2E = float(np.log2(np.e))

LANES = 128
SUBLANES = 8
VMEM_LIMIT = 48 * 1024 * 1024
VMEM_LIMIT_BIG = 60 * 1024 * 1024
MOD_ROWS = 16

NT = (((1,), (1,)), ((), ()))
TN = (((0,), (0,)), ((), ()))


def _cparams(*sem, vmem_limit=VMEM_LIMIT):
    return pltpu.CompilerParams(dimension_semantics=sem, vmem_limit_bytes=vmem_limit)


def _fit(n, pref):
    t = min(pref, n)
    while n % t:
        t //= 2
    return t


def _silu(x):
    return x * jax.nn.sigmoid(x)


def _rms(x, g):
    return x * lax.rsqrt(jnp.mean(x * x, axis=-1, keepdims=True) + EPS) * g


def _softmax_pv(problems):
    ms = [functools.reduce(jnp.maximum, [jnp.max(s, axis=-1, keepdims=True) for s in scores])
          for scores, _ in problems]
    ps = [[jnp.exp2(s - m) for s in scores] for (scores, _), m in zip(problems, ms)]
    ls = [functools.reduce(jnp.add, [jnp.sum(p, axis=-1, keepdims=True) for p in pp]) for pp in ps]
    os = [functools.reduce(jnp.add, [jnp.dot(p.astype(BF16), v, preferred_element_type=F32)
                                     for p, v in zip(pp, values)]) for pp, (_, values) in zip(ps, problems)]
    return [o / l for o, l in zip(os, ls)]


def _mod_kernel(c_ref, w_ref, b_ref, o_ref):
    s = _silu(c_ref[...]).astype(BF16)
    o_ref[...] = jnp.dot(s, w_ref[...].astype(BF16), preferred_element_type=F32) + b_ref[...]


def modulation(cvec, ada_w, ada_b, tn=512):
    L, D, N6 = ada_w.shape
    R = cvec.shape[0]
    tn = _fit(N6, tn)
    return pl.pallas_call(
        _mod_kernel,
        grid=(L, N6 // tn),
        in_specs=[pl.BlockSpec((R, D), lambda l, j: (0, 0)),
                  pl.BlockSpec((None, D, tn), lambda l, j: (l, 0, j)),
                  pl.BlockSpec((None, 1, tn), lambda l, j: (l, 0, j))],
        out_specs=pl.BlockSpec((None, R, tn), lambda l, j: (l, 0, j)),
        out_shape=jax.ShapeDtypeStruct((L, R, N6), F32),
        compiler_params=_cparams("arbitrary", "arbitrary"),
        name="modulation",
    )(cvec, ada_w, ada_b.reshape(L, 1, N6))


class Rows:
    def __init__(self, nc, nl, t):
        assert nc % t == 0, "latent sequences must start on a multiple of their length"
        self.nc, self.nl, self.t, self.n = nc, nl, t, nc + nl

    def tile(self, pref):
        return _fit(np.gcd(self.nc, self.t), pref)

    def mod_row(self, i, tm):
        ncb = self.nc // tm
        return jnp.where(i < ncb, 0, 1 + (i - ncb) // (self.t // tm))

    def pos_block(self, i, tm):
        ncb = self.nc // tm
        return jnp.where(i < ncb, self.t // tm, (i - ncb) % (self.t // tm))


def _mod_spec(rows, tm, k, D, first=0):
    return pl.BlockSpec((None, 1, D), lambda i, *_: (rows.mod_row(first + i, tm) * 6 + k, 0, 0))


def _vec_spec(D):
    return pl.BlockSpec((1, D), lambda i, *_: (0, 0))


def _split_specs(rows, tm, D):
    ncb = rows.nc // tm
    return [pl.BlockSpec((tm, D), lambda i, *_: (jnp.minimum(i, ncb - 1), 0)),
            pl.BlockSpec((tm, D), lambda i, *_: (jnp.maximum(i - ncb, 0), 0))]


def _into(dst, in_specs, args):
    return in_specs + [pl.BlockSpec(memory_space=pl.ANY)], args + [dst], {len(args): 0}


def _without_ref(kernel, idx):
    return lambda *refs: kernel(*refs[:idx], *refs[idx + 1:])


PRENORM_ROWS = 16


def _pre_kernel(xc_ref, xs_ref, g_ref, sh_ref, sc_ref, h_ref, *, ncb):
    def norm(x_ref):
        for s in range(0, h_ref.shape[0], PRENORM_ROWS):
            r = slice(s, s + PRENORM_ROWS)
            h = _rms(x_ref[r, :], g_ref[...]) * (1.0 + sc_ref[...]) + sh_ref[...]
            h_ref[r, :] = h.astype(h_ref.dtype)

    is_ctx = pl.program_id(0) < ncb
    pl.when(is_ctx)(lambda: norm(xc_ref))
    pl.when(jnp.logical_not(is_ctx))(lambda: norm(xs_ref))


def prenorm(xc, xs, g, mod3, k, rows, tm=512):
    D = xc.shape[1]
    tm = rows.tile(tm)
    return pl.pallas_call(
        functools.partial(_pre_kernel, ncb=rows.nc // tm),
        grid=(rows.n // tm,),
        in_specs=_split_specs(rows, tm, D) + [_vec_spec(D), _mod_spec(rows, tm, k, D),
                                              _mod_spec(rows, tm, k + 1, D)],
        out_specs=pl.BlockSpec((tm, D), lambda i: (i, 0)),
        out_shape=jax.ShapeDtypeStruct((rows.n, D), BF16),
        compiler_params=_cparams("arbitrary"),
        name="prenorm",
    )(xc, xs, g.reshape(1, D), mod3, mod3)


def _mm_kernel(x_ref, w_ref, o_ref, *scratch, nk, act):
    def finish(acc):
        if act == "relu2":
            a = jnp.maximum(acc, 0.0)
            acc = a * a
        o_ref[...] = acc.astype(o_ref.dtype)

    if nk == 1:
        finish(jnp.dot(x_ref[...], w_ref[...], preferred_element_type=F32))
        return
    acc_ref, = scratch
    k = pl.program_id(2)

    @pl.when(k == 0)
    def _():
        acc_ref[...] = jnp.zeros_like(acc_ref)

    acc_ref[...] += jnp.dot(x_ref[...], w_ref[...], preferred_element_type=F32)

    @pl.when(k == nk - 1)
    def _():
        finish(acc_ref[...])


def matmul(x, w, *, tm, tn, tk=None, out_dtype=F32, act=None, name="matmul"):
    M, K = x.shape
    _, N = w.shape
    tm, tn, tk = _fit(M, tm), _fit(N, tn), _fit(K, tk or K)
    nk = K // tk
    assert M % tm == 0 and N % tn == 0 and K % tk == 0
    return pl.pallas_call(
        functools.partial(_mm_kernel, nk=nk, act=act),
        grid=(M // tm, N // tn, nk),
        in_specs=[pl.BlockSpec((tm, tk), lambda i, j, k: (i, k)),
                  pl.BlockSpec((tk, tn), lambda i, j, k: (k, j))],
        out_specs=pl.BlockSpec((tm, tn), lambda i, j, k: (i, j)),
        out_shape=jax.ShapeDtypeStruct((M, N), out_dtype),
        scratch_shapes=[pltpu.VMEM((tm, tn), F32)] if nk > 1 else [],
        compiler_params=_cparams("arbitrary", "arbitrary", "arbitrary"),
        name=name,
    )(x, w)


def _mm_wcast_kernel(x_ref, w_ref, o_ref, wb_ref, *, act):
    @pl.when(pl.program_id(1) == 0)
    def _():
        wb_ref[...] = w_ref[...].astype(BF16)

    acc = jnp.dot(x_ref[...], wb_ref[...], preferred_element_type=F32)
    if act == "relu2":
        a = jnp.maximum(acc, 0.0)
        acc = a * a
    o_ref[...] = acc.astype(o_ref.dtype)


def matmul_wcast(x, w, layer, *, tm, tn, out_dtype=F32, act=None, vmem_limit=VMEM_LIMIT, name="matmul_wcast"):
    M, K = x.shape
    _, _, N = w.shape
    tm, tn = _fit(M, tm), _fit(N, tn)
    return pl.pallas_call(
        functools.partial(_mm_wcast_kernel, act=act),
        grid=(N // tn, M // tm),
        in_specs=[pl.BlockSpec((tm, K), lambda j, i: (i, 0)),
                  pl.BlockSpec((None, K, tn), lambda j, i: (layer, 0, j))],
        out_specs=pl.BlockSpec((tm, tn), lambda j, i: (i, j)),
        out_shape=jax.ShapeDtypeStruct((M, N), out_dtype),
        scratch_shapes=[pltpu.VMEM((K, tn), BF16)],
        compiler_params=_cparams("arbitrary", "arbitrary", vmem_limit=vmem_limit),
        name=name,
    )(x, w)


def _mm_post_kernel(*refs, nk, with_pre, sub, ncb, split_in):
    it = iter(refs)
    a_ref, w_ref = next(it), next(it)
    x_refs = [next(it) for _ in range(2 if split_in else 1)]
    gp_ref, gate_ref = next(it), next(it)
    gn_ref, sh_ref, sc_ref = (next(it), next(it), next(it)) if with_pre else (None, None, None)
    x1_ref = next(it)
    h_ref = next(it) if with_pre else None
    tm = a_ref.shape[0]
    is_ctx = pl.program_id(0) < ncb

    def finish(r, y):
        x = jnp.where(is_ctx, x_refs[0][r, :], x_refs[1][r, :]) if split_in else x_refs[0][r, :]
        x1 = x + gate_ref[...] * _rms(y, gp_ref[...])
        x1_ref[r, :] = x1
        if with_pre:
            h_ref[r, :] = (_rms(x1, gn_ref[...]) * (1.0 + sc_ref[...]) + sh_ref[...]).astype(h_ref.dtype)

    def product(r):
        return jnp.dot(a_ref[r, :], w_ref[...], preferred_element_type=F32)

    blocks = [slice(s, s + sub) for s in range(0, tm, sub)]
    everything = slice(None)

    if nk == 1:
        for r in blocks:
            finish(r, product(r))
        return
    k = pl.program_id(1)

    @pl.when(k == 0)
    def _():
        x1_ref[...] = product(everything)

    @pl.when(jnp.logical_and(k > 0, k < nk - 1))
    def _():
        x1_ref[...] += product(everything)

    @pl.when(k == nk - 1)
    def _():
        for r in blocks:
            finish(r, x1_ref[r, :] + product(r))


def matmul_post(a, w, layer, x, g_post, mod3, k, rows, pre=None, *, tiles=None, tm=512, tk=None, sub=256,
                vmem_limit=VMEM_LIMIT, name="matmul_post"):
    M, K = a.shape
    D = w.shape[2]
    tm = rows.tile(tm)
    tk = _fit(K, tk or K)
    nk = K // tk
    first, count = tiles or (0, M // tm)
    out_spec = pl.BlockSpec((tm, D), lambda i, kk: (i, 0))
    split_in = isinstance(x, tuple)
    assert not (split_in and first), "a row range of a split residual is not supported"
    x_specs, xs = ((_split_specs(rows, tm, D), list(x)) if split_in
                   else ([pl.BlockSpec((tm, D), lambda i, kk: (first + i, 0))], [x]))
    in_specs = [pl.BlockSpec((tm, tk), lambda i, kk: (first + i, kk)),
                pl.BlockSpec((None, tk, D), lambda i, kk: (layer, kk, 0)),
                *x_specs, _vec_spec(D), _mod_spec(rows, tm, k + 2, D, first)]
    args = [a, w, *xs, g_post.reshape(1, D), mod3]
    out_specs = [out_spec]
    out_shape = [jax.ShapeDtypeStruct((count * tm, D), F32)]
    if pre is not None:
        g_pre, mod3_pre, kp = pre
        in_specs += [_vec_spec(D), _mod_spec(rows, tm, kp, D, first), _mod_spec(rows, tm, kp + 1, D, first)]
        args += [g_pre.reshape(1, D), mod3_pre, mod3_pre]
        out_specs.append(out_spec)
        out_shape.append(jax.ShapeDtypeStruct((count * tm, D), BF16))
    out = pl.pallas_call(
        functools.partial(_mm_post_kernel, nk=nk, with_pre=pre is not None, sub=_fit(tm, sub), ncb=rows.nc // tm,
                          split_in=split_in),
        grid=(count, nk),
        in_specs=in_specs, out_specs=out_specs, out_shape=out_shape,
        compiler_params=_cparams("arbitrary", "arbitrary", vmem_limit=vmem_limit),
        name=name,
    )(*args)
    return out if len(out) > 1 else out[0]


HGRN_CHUNK = 64
HGRN_BLOCK = 256
HGRN_SAFE_EXPONENT = 60.0


def _hgrn_kernel(*refs, T, hb, layer, has_init, emit_state, unroll_blocks, unroll_scan):
    it = iter(refs)
    q_ref, ff_ref, fb_ref, i_ref, g_ref, lbf_ref, lbb_ref, ng_ref = (next(it) for _ in range(8))
    s0_refs = (next(it), next(it)) if has_init else None
    o_ref = next(it)
    s_out_refs = (next(it), next(it)) if emit_state else None
    qe_refs, ds_refs, oi_refs, ebt_refs = ((next(it), next(it)) for _ in range(4))
    C, R = HGRN_CHUNK, min(HGRN_BLOCK, T)
    nch, nblk, cpb, mid = T // C, T // R, R // C, C // 2
    f_refs, lb_refs = (ff_ref, fb_ref), (lbf_ref, lbb_ref)
    total_row = (C - 1, 0)

    def lower_bound(lb_ref, sl):
        z = lb_ref[:, sl]
        e = jnp.exp(z - jnp.max(z, axis=0, keepdims=True))
        sm = e / jnp.sum(e, axis=0, keepdims=True)
        return jnp.sum(sm[:layer + 1], axis=0, keepdims=True)

    row = lax.broadcasted_iota(jnp.int32, (R, R), 0)
    col = lax.broadcasted_iota(jnp.int32, (R, R), 1)
    same_chunk = (row // C) == (col // C)
    tri = (same_chunk & (row >= col), same_chunk & (row <= col))

    def cumsum(mask, x):
        hi = x.astype(BF16)
        r1 = x - hi.astype(F32)
        md = r1.astype(BF16)
        lo = (r1 - md.astype(F32)).astype(BF16)
        s = jnp.dot(mask.astype(BF16), jnp.concatenate([hi, md, lo], axis=1), preferred_element_type=F32)
        return s[:, :HEAD_DIM] + s[:, HEAD_DIM:2 * HEAD_DIM] + s[:, 2 * HEAD_DIM:]

    heads = [slice(h * HEAD_DIM, (h + 1) * HEAD_DIM) for h in range(hb)]
    lbs = [[lower_bound(lb_refs[d], sl) for sl in heads] for d in range(2)]
    chains = [(h, sl, d) for h, sl in enumerate(heads) for d in range(2)]

    def block_pass(blk, risk):
        r = pl.ds(pl.multiple_of(blk * R, R), R)
        qs, vb, kk, b, qm, km, kd, a = {}, {}, {}, {}, {}, {}, {}, {}
        for h, sl in enumerate(heads):
            qs[h] = _silu(q_ref[r, sl]) * HEAD_DIM ** -0.5
            vb[h] = i_ref[r, sl].astype(BF16)
        for h, sl, d in chains:
            lb = lbs[d][h]
            f = lb + (1.0 - lb) * jax.nn.sigmoid(f_refs[d][r, sl])
            kk[h, d] = 1.0 - f
            b[h, d] = cumsum(tri[d], jnp.log(f))
        for h, sl, d in chains:
            qmc, kmc, kd[h, d] = [], [], []
            for cc in range(cpb):
                cs = slice(cc * C, (cc + 1) * C)
                bc = b[h, d][cs]
                m = bc[mid:mid + 1]
                bt = bc[total_row[d]:total_row[d] + 1]
                qc = qs[h][cs] * jnp.exp(bc - m)
                kc = kk[h, d][cs] * jnp.exp(m - bc)
                risk = jnp.maximum(risk, jnp.maximum(jnp.abs(bc[:1] - m), jnp.abs(bc[C - 1:] - m)))
                rc = pl.ds(pl.multiple_of(blk * R + cc * C, C), C)
                qe_refs[d][rc, sl] = (qs[h][cs] * jnp.exp(bc)).astype(BF16)
                kd[h, d].append((kk[h, d][cs] * jnp.exp(bt - bc)).astype(BF16))
                slot = pl.ds(pl.multiple_of((blk * cpb + cc) * SUBLANES, SUBLANES), SUBLANES)
                ebt_refs[d][slot, sl] = jnp.broadcast_to(jnp.exp(bt), (SUBLANES, HEAD_DIM))
                qmc.append(qc.astype(BF16))
                kmc.append(kc.astype(BF16))
            qm[h, d] = jnp.concatenate(qmc, axis=0)
            km[h, d] = jnp.concatenate(kmc, axis=0)
        for h, sl, d in chains:
            s = lax.dot_general(qm[h, d], km[h, d], NT, preferred_element_type=F32)
            a[h, d] = jnp.where(tri[d], s, 0.0).astype(BF16)
        for h, sl, d in chains:
            oi_refs[d][r, sl] = jnp.dot(a[h, d], vb[h], preferred_element_type=F32)
        for h, sl, d in chains:
            for cc in range(cpb):
                rows_c = pl.ds(pl.multiple_of((blk * cpb + cc) * HEAD_DIM, HEAD_DIM), HEAD_DIM)
                ds_refs[d][rows_c, sl] = lax.dot_general(vb[h][cc * C:(cc + 1) * C], kd[h, d][cc], TN,
                                                         preferred_element_type=F32)
        return risk

    risk = lax.fori_loop(0, nblk, block_pass, jnp.zeros((1, HEAD_DIM), F32), unroll=unroll_blocks)

    def safe_scores(d, qs, kk, lf):
        scores = jnp.where(row == col, lax.dot_general(qs.astype(BF16), kk.astype(BF16), NT,
                                                       preferred_element_type=F32), 0.0)
        half = 1
        while half < C:
            same = (row // (2 * half)) == (col // (2 * half))
            r_up, c_up = (row % (2 * half)) >= half, (col % (2 * half)) >= half
            r_lo, c_lo = jnp.logical_not(r_up), jnp.logical_not(c_up)
            if d == 0:
                to_t, from_s, pair = same & r_up & c_up & (col <= row), same & r_lo & c_lo & (col > row), same & r_up & c_lo
            else:
                to_t, from_s, pair = same & r_lo & c_lo & (col >= row), same & r_up & c_up & (col < row), same & r_lo & c_up
            qd = (qs * jnp.exp(cumsum(to_t, lf))).astype(BF16)
            kd = (kk * jnp.exp(cumsum(from_s, lf))).astype(BF16)
            scores = scores + jnp.where(pair, lax.dot_general(qd, kd, NT, preferred_element_type=F32), 0.0)
            half *= 2
        return scores

    def safe_block(blk, carry):
        r = pl.ds(pl.multiple_of(blk * R, R), R)
        for h, sl, d in chains:
            lb = lbs[d][h]
            f = lb + (1.0 - lb) * jax.nn.sigmoid(f_refs[d][r, sl])
            scores = safe_scores(d, _silu(q_ref[r, sl]) * HEAD_DIM ** -0.5, 1.0 - f, jnp.log(f))
            oi_refs[d][r, sl] = jnp.dot(scores.astype(BF16), i_ref[r, sl].astype(BF16), preferred_element_type=F32)
        return carry

    @pl.when(jnp.max(risk) > HGRN_SAFE_EXPONENT)
    def _():
        lax.fori_loop(0, nblk, safe_block, 0)

    def scan_step(j, states):
        cs = [j, nch - 1 - j]
        rs = [pl.ds(pl.multiple_of(c * C, C), C) for c in cs]
        slots = [pl.ds(pl.multiple_of(c * SUBLANES, SUBLANES), SUBLANES) for c in cs]
        grow = [pl.ds(pl.multiple_of(c * HEAD_DIM, HEAD_DIM), HEAD_DIM) for c in cs]
        new = []
        for h, sl, d in chains:
            St = states[2 * h + d]
            oi_refs[d][rs[d], sl] += lax.dot_general(qe_refs[d][rs[d], sl], St.astype(BF16), NT,
                                                     preferred_element_type=F32)
            new.append(St * ebt_refs[d][slots[d], sl][:1] + ds_refs[d][grow[d], sl])
        return tuple(new)

    if has_init:
        init = tuple(s0_refs[d][h].T for h in range(hb) for d in range(2))
    else:
        init = tuple(jnp.zeros((HEAD_DIM, HEAD_DIM), F32) for _ in range(2 * hb))
    states = lax.fori_loop(0, nch, scan_step, init, unroll=unroll_scan)
    if emit_state:
        for h in range(hb):
            for d in range(2):
                s_out_refs[d][h] = states[2 * h + d].T

    def final_pass(blk, carry):
        r = pl.ds(pl.multiple_of(blk * R, R), R)
        for sl in heads:
            o = _rms(oi_refs[0][r, sl] + oi_refs[1][r, sl], ng_ref[...]) * _silu(g_ref[r, sl])
            o_ref[r, sl] = o.astype(o_ref.dtype)
        return carry

    lax.fori_loop(0, nblk, final_pass, 0)


def hgrn(p, lb_f, lb_b, norm_g, dst, *, nseq, T, row0, A_heads, layer, hb, init=None, emit_state=False,
         unroll_blocks=1, unroll_scan=8):
    assert T % min(HGRN_BLOCK, T) == 0 and A_heads % hb == 0
    rb0 = row0 // T
    H = A_heads
    W = hb * HEAD_DIM
    ng = H // hb

    def slab(k):
        return pl.BlockSpec((T, W), lambda b, h: (rb0 + b, k * ng + h))

    nl = lb_f.shape[0]
    in_specs = [slab(k) for k in range(5)] + [
        pl.BlockSpec((nl, W), lambda b, h: (0, h)),
        pl.BlockSpec((nl, W), lambda b, h: (0, h)),
        pl.BlockSpec((1, HEAD_DIM), lambda b, h: (0, 0))]
    args = [p] * 5 + [lb_f, lb_b, norm_g.reshape(1, HEAD_DIM)]
    state_spec = pl.BlockSpec((None, hb, HEAD_DIM, HEAD_DIM), lambda b, h: (b, h, 0, 0))
    if init is not None:
        in_specs += [state_spec, state_spec]
        args += list(init)
    in_specs, args, aliases = _into(dst, in_specs, args)
    out_specs = [pl.BlockSpec((T, W), lambda b, h: (rb0 + b, h))]
    out_shape = [jax.ShapeDtypeStruct(dst.shape, dst.dtype)]
    if emit_state:
        out_specs += [state_spec, state_spec]
        out_shape += [jax.ShapeDtypeStruct((nseq, H, HEAD_DIM, HEAD_DIM), F32)] * 2
    nch = T // HGRN_CHUNK
    scratch = ([pltpu.VMEM((T, W), BF16)] * 2
               + [pltpu.VMEM((nch * HEAD_DIM, W), F32)] * 2
               + [pltpu.VMEM((T, W), F32)] * 2
               + [pltpu.VMEM((nch * SUBLANES, W), F32)] * 2)
    body = functools.partial(_hgrn_kernel, T=T, hb=hb, layer=layer, has_init=init is not None, emit_state=emit_state,
                             unroll_blocks=unroll_blocks, unroll_scan=unroll_scan)
    out = pl.pallas_call(
        _without_ref(body, len(args) - 1),
        grid=(nseq, ng),
        in_specs=in_specs, out_specs=out_specs, out_shape=out_shape,
        input_output_aliases=aliases,
        scratch_shapes=scratch,
        compiler_params=_cparams("arbitrary", "arbitrary"),
        name="hgrn",
    )(*args)
    return out if emit_state else out[0]


def _dense_attn_kernel(q_ref, k_ref, v_ref, o_ref, kout_ref, vout_ref, *, H, scale):
    heads = [slice(h * HEAD_DIM, (h + 1) * HEAD_DIM) for h in range(H)]
    problems = []
    for h, sl in enumerate(heads):
        kout_ref[:, h, :] = k_ref[:, sl]
        vout_ref[:, h, :] = v_ref[:, sl]
        q = (q_ref[:, sl] * scale).astype(BF16)
        s = lax.dot_general(q, k_ref[:, sl].astype(BF16), NT, preferred_element_type=F32)
        problems.append(([s], [v_ref[:, sl].astype(BF16)]))
    for sl, o in zip(heads, _softmax_pv(problems)):
        o_ref[:, sl] = o.astype(o_ref.dtype)


def dense_attn(p, dst, *, nseq, T, H, col0, dst_col0):
    W = H * HEAD_DIM
    cb = col0 // W
    in_specs, args, aliases = _into(dst, [pl.BlockSpec((T, W), lambda b, k=k: (b, cb + k)) for k in range(3)],
                                    [p, p, p])
    cache_spec = pl.BlockSpec((None, T, H, HEAD_DIM), lambda b: (b, 0, 0, 0))
    cache_shape = jax.ShapeDtypeStruct((nseq, T, H, HEAD_DIM), p.dtype)
    return pl.pallas_call(
        _without_ref(functools.partial(_dense_attn_kernel, H=H, scale=HEAD_DIM ** -0.5 * LOG2E), 3),
        grid=(nseq,),
        in_specs=in_specs,
        out_specs=[pl.BlockSpec((T, W), lambda b: (b, dst_col0 // W)), cache_spec, cache_spec],
        out_shape=[jax.ShapeDtypeStruct(dst.shape, dst.dtype), cache_shape, cache_shape],
        input_output_aliases=aliases,
        compiler_params=_cparams("arbitrary"),
        name="dense_attn",
    )(*args)


NA_QROWS = 4
NA_KROWS = 12


def na_bias_tables(rpb, rows):
    W = GRID_W
    kr = min(NA_ROWS, rows)
    assert kr == NA_ROWS and rows >= NA_KROWS and rows % NA_QROWS == 0
    nblk = rows // NA_QROWS
    col = np.arange(W)
    cs = np.clip(col - NA_COLS // 2, 0, W - NA_COLS)
    col_ok = (col[None, :] >= cs[:, None]) & (col[None, :] < cs[:, None] + NA_COLS)
    ci = np.clip(col[None, :] - col[:, None] + NA_COLS - 1, 0, 2 * NA_COLS - 2)
    col_sel = (ci[..., None] == np.arange(2 * NA_COLS - 1)).astype(np.float32)
    G = jnp.einsum("hac,xyc->haxy", rpb.astype(F32), col_sel, precision=lax.Precision.HIGHEST)
    G = jnp.where(col_ok, G, NEG)
    outside = 2 * NA_ROWS - 1
    G = jnp.concatenate([G, jnp.full_like(G[:, :1], NEG)], axis=1)
    tables = []
    for blk in (0, 1, nblk - 1):
        r0 = blk * NA_QROWS
        u0 = min(max(r0 - NA_ROWS // 2, 0), rows - NA_KROWS)
        r = r0 + np.arange(NA_QROWS)
        ka = u0 + np.arange(NA_KROWS)
        start = np.clip(r - kr // 2, 0, rows - kr)
        row_ok = (ka[None, :] >= start[:, None]) & (ka[None, :] < start[:, None] + kr)
        ri = np.where(row_ok, ka[None, :] - r[:, None] + NA_ROWS - 1, outside)
        tables.append(jnp.concatenate(
            [jnp.concatenate([G[:, int(a)] for a in ri[q]], axis=-1) for q in range(NA_QROWS)], axis=-2))
    return jnp.stack(tables) * LOG2E


def _na_kernel(q_ref, k_ref, v_ref, kc_ref, vc_ref, bias_ref, o_ref, *, rows, scale, hp):
    blk = pl.program_id(2)
    u0 = jnp.clip(blk * NA_QROWS - NA_ROWS // 2, 0, rows - NA_KROWS)
    band = pl.ds(pl.multiple_of(u0 * GRID_W, GRID_W), NA_KROWS * GRID_W)
    heads = [slice(h * HEAD_DIM, (h + 1) * HEAD_DIM) for h in range(hp)]
    problems = []
    for h, sl in enumerate(heads):
        q = (q_ref[:, sl] * scale).astype(BF16)
        s_lat = lax.dot_general(q, k_ref[band, sl].astype(BF16), NT, preferred_element_type=F32) + bias_ref[h]
        s_ctx = lax.dot_general(q, kc_ref[:, sl].astype(BF16), NT, preferred_element_type=F32)
        problems.append(([s_lat, s_ctx], [v_ref[band, sl].astype(BF16), vc_ref[:, sl].astype(BF16)]))
    for sl, o in zip(heads, _softmax_pv(problems)):
        o_ref[:, sl] = o.astype(o_ref.dtype)


def na_attn(p, kc, vc, bias, dst, *, nseq, T, H, row0, col0, dst_col0, hp):
    rows = T // GRID_W
    nblk = rows // NA_QROWS
    tq = NA_QROWS * GRID_W
    L = kc.shape[0] // nseq
    W = hp * HEAD_DIM
    qb0, kb0, cb = row0 // tq, row0 // T, col0 // W
    ng = H // hp

    def kind(blk):
        return jnp.where(blk == 0, 0, jnp.where(blk == nblk - 1, 2, 1))

    in_specs, args, aliases = _into(dst, [
        pl.BlockSpec((tq, W), lambda b, h, i: (qb0 + b * nblk + i, cb + h)),
        pl.BlockSpec((T, W), lambda b, h, i: (kb0 + b, cb + ng + h)),
        pl.BlockSpec((T, W), lambda b, h, i: (kb0 + b, cb + 2 * ng + h)),
        pl.BlockSpec((L, W), lambda b, h, i: (b, h)),
        pl.BlockSpec((L, W), lambda b, h, i: (b, h)),
        pl.BlockSpec((None, hp, tq, NA_KROWS * GRID_W), lambda b, h, i: (kind(i), h, 0, 0))],
        [p, p, p, kc, vc, bias])
    return pl.pallas_call(
        _without_ref(functools.partial(_na_kernel, rows=rows, scale=HEAD_DIM ** -0.5 * LOG2E, hp=hp), 6),
        grid=(nseq, ng, nblk),
        in_specs=in_specs,
        out_specs=pl.BlockSpec((tq, W), lambda b, h, i: (qb0 + b * nblk + i, dst_col0 // W + h)),
        out_shape=jax.ShapeDtypeStruct(dst.shape, dst.dtype),
        input_output_aliases=aliases,
        compiler_params=_cparams("arbitrary", "arbitrary", "arbitrary"),
        name="na_attn",
    )(*args)


ROPE_SWAP = np.concatenate([np.arange(16, 32), np.arange(0, 16), np.arange(48, 64), np.arange(32, 48)])


def rope_tables(T, tm):
    t = jnp.arange(T)
    half = QK_ROPE // 2
    inv = jnp.power(ROPE_BASE, -jnp.arange(0, half, 2, dtype=F32) / half)
    ang_r = (t // GRID_W).astype(F32)[:, None] * inv
    ang_c = (t % GRID_W).astype(F32)[:, None] * inv
    cos = jnp.concatenate([jnp.cos(ang_r), jnp.cos(ang_r), jnp.cos(ang_c), jnp.cos(ang_c)], axis=-1)
    sin = jnp.concatenate([-jnp.sin(ang_r), jnp.sin(ang_r), -jnp.sin(ang_c), jnp.sin(ang_c)], axis=-1)
    cos = jnp.concatenate([cos, jnp.ones((T, LANES - QK_ROPE), F32)], axis=-1)
    sin = jnp.concatenate([sin, jnp.zeros((T, LANES - QK_ROPE), F32)], axis=-1)
    cos = jnp.concatenate([cos, jnp.ones((tm, LANES), F32)], axis=0)
    sin = jnp.concatenate([sin, jnp.zeros((tm, LANES), F32)], axis=0)
    return cos, sin


def _mla_in_kernel(x_ref, w_ref, qg_ref, kvg_ref, cos_ref, sin_ref, cq_ref, ckv16_ref, k2_ref, ckv32_ref, kpe_ref,
                   *, qr, kvr, ncb):
    pr = jnp.dot(x_ref[...], w_ref[...], preferred_element_type=F32)
    cq_ref[...] = _rms(pr[:, :qr], qg_ref[...]).astype(cq_ref.dtype)
    ckv = _rms(pr[:, qr:qr + kvr], kvg_ref[...])
    ckv16_ref[...] = ckv.astype(ckv16_ref.dtype)
    x = pr[:, qr + kvr:]

    @pl.when(pl.program_id(0) < ncb)
    def _():
        ckv32_ref[...] = ckv
        kpe_ref[...] = x

    rot = x * cos_ref[...] + pltpu.roll(x, LANES // 2, axis=1) * sin_ref[...]
    lane = lax.broadcasted_iota(jnp.int32, rot.shape, 1)
    k2_ref[...] = jnp.where(lane < QK_ROPE, rot, 0.0).astype(k2_ref.dtype)


def mla_in(h, w_in_wide, q_norm_g, kv_norm_g, cos, sin, rows, *, qr, kvr, tm):
    n, D = h.shape
    ncb = rows.nc // tm
    pos = pl.BlockSpec((tm, LANES), lambda i: (rows.pos_block(i, tm), 0))

    def out(w):
        return pl.BlockSpec((tm, w), lambda i: (i, 0))

    def ctx_out(w):
        return pl.BlockSpec((tm, w), lambda i: (jnp.minimum(i, ncb - 1), 0))

    return pl.pallas_call(
        functools.partial(_mla_in_kernel, qr=qr, kvr=kvr, ncb=ncb),
        grid=(n // tm,),
        in_specs=[pl.BlockSpec((tm, D), lambda i: (i, 0)), pl.BlockSpec(w_in_wide.shape, lambda i: (0, 0)),
                  _vec_spec(qr), _vec_spec(kvr), pos, pos],
        out_specs=[out(qr), out(kvr), out(LANES), ctx_out(kvr), ctx_out(LANES)],
        out_shape=[jax.ShapeDtypeStruct((n, qr), BF16), jax.ShapeDtypeStruct((n, kvr), BF16),
                   jax.ShapeDtypeStruct((n, LANES), BF16), jax.ShapeDtypeStruct((rows.nc, kvr), F32),
                   jax.ShapeDtypeStruct((rows.nc, LANES), F32)],
        compiler_params=_cparams("arbitrary"),
        name="mla_in",
    )(h, w_in_wide, q_norm_g.reshape(1, qr), kv_norm_g.reshape(1, kvr), cos, sin)


MLA_Q_IN = 3 * LANES
MLA_Q_OUT = 2 * LANES


def widen_w_uq(w_uq, heads):
    r = w_uq.shape[0]
    w = w_uq.reshape(r, heads, QK_NOPE + QK_ROPE)
    nope, pe = w[..., :QK_NOPE], w[..., QK_NOPE:]
    return jnp.concatenate([nope, pe, pe, pe[..., ROPE_SWAP], jnp.zeros_like(pe)], axis=-1).reshape(r, heads * MLA_Q_IN)


def _mla_q_kernel(x_ref, w_ref, cos_ref, sin_ref, o_ref, *, hpt, scale):
    acc = jnp.dot(x_ref[...], w_ref[...], preferred_element_type=F32)
    cos, sin = cos_ref[...], sin_ref[...]
    for j in range(hpt):
        a = acc[:, j * MLA_Q_IN:(j + 1) * MLA_Q_IN]
        o_ref[:, j * MLA_Q_OUT:j * MLA_Q_OUT + LANES] = (a[:, :LANES] * scale).astype(o_ref.dtype)
        q2 = a[:, LANES:2 * LANES] * cos + a[:, 2 * LANES:] * sin
        o_ref[:, j * MLA_Q_OUT + LANES:(j + 1) * MLA_Q_OUT] = (q2 * scale).astype(o_ref.dtype)


def mla_q(cq, w_uq_wide, cos, sin, rows, *, heads, scale, tm, hpt=4):
    n, r = cq.shape
    pos = pl.BlockSpec((tm, LANES), lambda i, j: (rows.pos_block(i, tm), 0))
    return pl.pallas_call(
        functools.partial(_mla_q_kernel, hpt=hpt, scale=scale),
        grid=(n // tm, heads // hpt),
        in_specs=[pl.BlockSpec((tm, r), lambda i, j: (i, 0)),
                  pl.BlockSpec((r, hpt * MLA_Q_IN), lambda i, j: (0, j)), pos, pos],
        out_specs=pl.BlockSpec((tm, hpt * MLA_Q_OUT), lambda i, j: (i, j)),
        out_shape=jax.ShapeDtypeStruct((n, heads * MLA_Q_OUT), BF16),
        compiler_params=_cparams("arbitrary", "arbitrary"),
        name="mla_q",
    )(cq, w_uq_wide, cos, sin)


def _mla_ctx_attn_kernel(q_ref, kv_ref, k2_ref, o_ref, *, heads):
    k2 = k2_ref[...]
    problems = []
    for h in range(heads):
        q = q_ref[:, h * MLA_Q_OUT:(h + 1) * MLA_Q_OUT]
        k = jnp.concatenate([kv_ref[:, 2 * h * LANES:(2 * h + 1) * LANES], k2], axis=1)
        s = lax.dot_general(q, k, NT, preferred_element_type=F32)
        problems.append(([s], [kv_ref[:, (2 * h + 1) * LANES:(2 * h + 2) * LANES]]))
    for h, o in enumerate(_softmax_pv(problems)):
        o_ref[:, h * V_DIM:(h + 1) * V_DIM] = o.astype(o_ref.dtype)


def mla_ctx_attn(q, kv, k2, dst, *, nseq, T, heads):
    in_specs, args, aliases = _into(dst, [pl.BlockSpec((T, heads * MLA_Q_OUT), lambda b: (b, 0)),
                                          pl.BlockSpec((T, kv.shape[1]), lambda b: (b, 0)),
                                          pl.BlockSpec((T, LANES), lambda b: (b, 0))], [q, kv, k2])
    return pl.pallas_call(
        _without_ref(functools.partial(_mla_ctx_attn_kernel, heads=heads), 3),
        grid=(nseq,),
        in_specs=in_specs,
        out_specs=pl.BlockSpec((T, heads * V_DIM), lambda b: (b, 0)),
        out_shape=jax.ShapeDtypeStruct(dst.shape, dst.dtype),
        input_output_aliases=aliases,
        compiler_params=_cparams("arbitrary"),
        name="mla_ctx_attn",
    )(*args)


def _mla_lat_attn_kernel(q_ref, k1_ref, v_ref, k2_ref, k1c_ref, vc_ref, k2c_ref, o_ref, *, sub):
    k = jnp.concatenate([k1_ref[...], k2_ref[...]], axis=1)
    kc = jnp.concatenate([k1c_ref[...], k2c_ref[...]], axis=1)
    blocks = [slice(s, s + sub) for s in range(0, q_ref.shape[0], sub)]
    problems = []
    for r in blocks:
        q = q_ref[r, :]
        s_lat = lax.dot_general(q, k, NT, preferred_element_type=F32)
        s_ctx = lax.dot_general(q, kc, NT, preferred_element_type=F32)
        problems.append(([s_lat, s_ctx], [v_ref[...], vc_ref[...]]))
    for r, o in zip(blocks, _softmax_pv(problems)):
        o_ref[r, :] = o.astype(o_ref.dtype)


def mla_lat_attn(q, kv, k2, kvc, k2c, dst, *, nseq, T, heads, row0, tq, sub):
    nq = T // tq
    P = kvc.shape[0] // nseq
    qb0, kb0 = row0 // tq, row0 // T
    in_specs, args, aliases = _into(dst, [
        pl.BlockSpec((tq, MLA_Q_OUT), lambda b, h, i: (qb0 + b * nq + i, h)),
        pl.BlockSpec((T, LANES), lambda b, h, i: (kb0 + b, 2 * h)),
        pl.BlockSpec((T, LANES), lambda b, h, i: (kb0 + b, 2 * h + 1)),
        pl.BlockSpec((T, LANES), lambda b, h, i: (kb0 + b, 0)),
        pl.BlockSpec((P, LANES), lambda b, h, i: (b, 2 * h)),
        pl.BlockSpec((P, LANES), lambda b, h, i: (b, 2 * h + 1)),
        pl.BlockSpec((P, LANES), lambda b, h, i: (b, 0))],
        [q, kv, kv, k2, kvc, kvc, k2c])
    return pl.pallas_call(
        _without_ref(functools.partial(_mla_lat_attn_kernel, sub=_fit(tq, sub)), 7),
        grid=(nseq, heads, nq),
        in_specs=in_specs,
        out_specs=pl.BlockSpec((tq, V_DIM), lambda b, h, i: (qb0 + b * nq + i, h)),
        out_shape=jax.ShapeDtypeStruct(dst.shape, dst.dtype),
        input_output_aliases=aliases,
        compiler_params=_cparams("arbitrary", "arbitrary", "arbitrary"),
        name="mla_lat_attn",
    )(*args)


def even_layer(h, rows, B, SEQ, Bd, state_f, state_b, cache_k, cache_v, lb_f, lb_b, w_in, j, hgrn_g, rpb, layer):
    D = h.shape[1]
    AW = D // 2
    AH = BH = AW // HEAD_DIM
    T = rows.t
    p = matmul_wcast(h, w_in, j, tm=1024, tn=1024, out_dtype=F32, name="even_in_proj")
    o, s_fw, s_bw = hgrn(p, lb_f, lb_b, hgrn_g, h, nseq=B, T=SEQ, row0=0, A_heads=AH, layer=layer, hb=_fit(AH, 4),
                         emit_state=True)
    o = hgrn(p, lb_f, lb_b, hgrn_g, o, nseq=Bd, T=T, row0=rows.nc, A_heads=AH, layer=layer, hb=_fit(AH, 2),
             init=(state_f, state_b), unroll_blocks=2)
    o, new_k, new_v = dense_attn(p, o, nseq=B, T=SEQ, H=BH, col0=5 * AW, dst_col0=AW)
    past = cache_k.shape[1]
    bias = na_bias_tables(rpb, T // GRID_W)
    o = na_attn(p, cache_k.reshape(Bd * past, BH * HEAD_DIM), cache_v.reshape(Bd * past, BH * HEAD_DIM), bias, o,
                nseq=Bd, T=T, H=BH, row0=rows.nc, col0=5 * AW, dst_col0=AW, hp=_fit(BH, 4))
    return o, s_fw, s_bw, new_k, new_v


def odd_layer(h, rows, B, SEQ, Bd, cache_ckv, cache_kpe, w_in, q_norm_g, w_uq, kv_norm_g, w_ukv):
    D = h.shape[1]
    heads = D // 128
    T = rows.t
    qr, kvr = w_uq.shape[0], w_ukv.shape[0]
    scale = (QK_NOPE + QK_ROPE) ** -0.5 * LOG2E
    tm = rows.tile(1024)
    w_in_wide = jnp.concatenate([w_in, w_in[:, qr + kvr + ROPE_SWAP]], axis=1).astype(BF16)
    cos, sin = rope_tables(T, tm)
    cq, ckv16, k2, ckv32, kpe = mla_in(h, w_in_wide, q_norm_g, kv_norm_g, cos, sin, rows, qr=qr, kvr=kvr, tm=tm)
    q = mla_q(cq, widen_w_uq(w_uq, heads).astype(BF16), cos, sin, rows, heads=heads, scale=scale, tm=tm,
              hpt=_fit(heads, 8))
    w_ukv16 = w_ukv.astype(BF16)
    kv = matmul(ckv16, w_ukv16, tm=2048, tn=2048, out_dtype=BF16, name="mla_kv")
    past = cache_ckv.shape[1]
    kvc = matmul(cache_ckv.reshape(Bd * past, kvr).astype(BF16), w_ukv16, tm=1024, tn=1024, out_dtype=BF16,
                 name="mla_kv_cache")
    k2c = jnp.concatenate([jnp.zeros((Bd * past, LANES - QK_ROPE), F32), cache_kpe.reshape(Bd * past, QK_ROPE)],
                          axis=1).astype(BF16)
    o = mla_ctx_attn(q, kv, k2, h, nseq=B, T=SEQ, heads=heads)
    o = mla_lat_attn(q, kv, k2, kvc, k2c, o, nseq=Bd, T=T, heads=heads, row0=rows.nc, tq=rows.tile(2048), sub=512)
    new_ckv = ckv32.reshape(B, SEQ, kvr)
    new_kpe = kpe[:, :QK_ROPE].reshape(B, SEQ, QK_ROPE)
    return o, new_ckv, new_kpe


def kernel(x_prompt, x_sample, state_hgrn_fwd, state_hgrn_bwd, cache_na_k, cache_na_v, cache_mla_ckv, cache_mla_kpe, c, c_ctx, ada_w, ada_b, norm_g, hgrn_lb_fwd, hgrn_lb_bwd, w_in_even, hgrn_norm_g, na_rpb, w_out_even, w_in_odd, mla_q_norm_g, w_uq, mla_kv_norm_g, w_ukv, w_out_odd, mlp_w1, mlp_w2):
    B, SEQ, D = x_prompt.shape
    Bd, T, _ = x_sample.shape
    depth = ada_w.shape[0]
    rows = Rows(B * SEQ, Bd * T, T)
    x = (x_prompt.reshape(rows.nc, D), x_sample.reshape(rows.nl, D))
    cvec = jnp.zeros((MOD_ROWS, D), F32).at[0].set(c_ctx).at[1:1 + Bd].set(c)
    mod = modulation(cvec, ada_w, ada_b).reshape(depth, MOD_ROWS * 6, 1, D)

    w2 = mlp_w2.astype(BF16)
    new_sf, new_sb, new_nk, new_nv, new_ckv, new_kpe = [], [], [], [], [], []
    h = prenorm(*x, norm_g[0, 0], mod[0], 0, rows)
    for l in range(depth):
        j = l // 2
        if l % 2 == 0:
            o, sf, sb, nk, nv = even_layer(h, rows, B, SEQ, Bd, state_hgrn_fwd[:, j], state_hgrn_bwd[:, j],
                                           cache_na_k[:, j], cache_na_v[:, j], hgrn_lb_fwd, hgrn_lb_bwd,
                                           w_in_even, j, hgrn_norm_g[j], na_rpb[j], l)
            w_out = w_out_even
            new_sf.append(sf)
            new_sb.append(sb)
            new_nk.append(nk)
            new_nv.append(nv)
        else:
            o, ckv, kpe = odd_layer(h, rows, B, SEQ, Bd, cache_mla_ckv[:, j], cache_mla_kpe[:, j], w_in_odd[j],
                                    mla_q_norm_g[j], w_uq[j], mla_kv_norm_g[j], w_ukv[j])
            w_out = w_out_odd
            new_ckv.append(ckv)
            new_kpe.append(kpe)
        x, h = matmul_post(o, w_out.astype(BF16), j, x, norm_g[l, 1], mod[l], 0, rows,
                           pre=(norm_g[l, 2], mod[l], 3), name="out_proj_post")
        a = matmul_wcast(h, mlp_w1, l, tm=2048, tn=1024, out_dtype=BF16, act="relu2", vmem_limit=VMEM_LIMIT_BIG,
                         name="mlp_up")
        down = dict(tm=1024, tk=1024, vmem_limit=VMEM_LIMIT_BIG, name="mlp_down_post")
        if l + 1 < depth:
            x, h = matmul_post(a, w2, l, x, norm_g[l, 3], mod[l], 3, rows, pre=(norm_g[l + 1, 0], mod[l + 1], 0),
                               **down)
        else:
            ncb = rows.nc // rows.tile(down["tm"])
            x = tuple(matmul_post(a, w2, l, x, norm_g[l, 3], mod[l], 3, rows, tiles=t, **down)
                      for t in ((0, ncb), (ncb, rows.nl // rows.tile(down["tm"]))))
    return (x[0].reshape(B, SEQ, D), x[1].reshape(Bd, T, D),
            jnp.stack(new_sf, axis=1), jnp.stack(new_sb, axis=1), jnp.stack(new_nk, axis=1),
            jnp.stack(new_nv, axis=1), jnp.stack(new_ckv, axis=1), jnp.stack(new_kpe, axis=1))
```

```python
import functools

import numpy as np
import jax
import jax.numpy as jnp
from jax import lax
from jax.experimental import pallas as pl
from jax.experimental.pallas import tpu as pltpu

F32 = jnp.float32
BF16 = jnp.bfloat16

GRID_W = 64
HEAD_DIM = 128
NA_ROWS = 8
NA_COLS = 16
QK_NOPE = 128
QK_ROPE = 64
V_DIM = 128
ROPE_BASE = 10000.0
EPS = 1e-6
NEG = -1e30
LOG2E = float(np.log2(np.e))

LANES = 128
SUBLANES = 8
VMEM_LIMIT = 48 * 1024 * 1024
VMEM_LIMIT_BIG = 60 * 1024 * 1024
MOD_ROWS = 16

NT = (((1,), (1,)), ((), ()))
TN = (((0,), (0,)), ((), ()))


def _cparams(*sem, vmem_limit=VMEM_LIMIT):
    return pltpu.CompilerParams(dimension_semantics=sem, vmem_limit_bytes=vmem_limit)


def _fit(n, pref):
    t = min(pref, n)
    while n % t:
        t //= 2
    return t


def _silu(x):
    return x * jax.nn.sigmoid(x)


def _rms(x, g):
    return x * lax.rsqrt(jnp.mean(x * x, axis=-1, keepdims=True) + EPS) * g


def _softmax_pv(problems):
    ms = [functools.reduce(jnp.maximum, [jnp.max(s, axis=-1, keepdims=True) for s in scores])
          for scores, _ in problems]
    ps = [[jnp.exp2(s - m) for s in scores] for (scores, _), m in zip(problems, ms)]
    ls = [functools.reduce(jnp.add, [jnp.sum(p, axis=-1, keepdims=True) for p in pp]) for pp in ps]
    os = [functools.reduce(jnp.add, [jnp.dot(p.astype(BF16), v, preferred_element_type=F32)
                                     for p, v in zip(pp, values)]) for pp, (_, values) in zip(ps, problems)]
    return [o / l for o, l in zip(os, ls)]


def _mod_kernel(c_ref, w_ref, b_ref, o_ref):
    s = _silu(c_ref[...]).astype(BF16)
    o_ref[...] = jnp.dot(s, w_ref[...].astype(BF16), preferred_element_type=F32) + b_ref[...]


def modulation(cvec, ada_w, ada_b, tn=512):
    L, D, N6 = ada_w.shape
    R = cvec.shape[0]
    tn = _fit(N6, tn)
    return pl.pallas_call(
        _mod_kernel,
        grid=(L, N6 // tn),
        in_specs=[pl.BlockSpec((R, D), lambda l, j: (0, 0)),
                  pl.BlockSpec((None, D, tn), lambda l, j: (l, 0, j)),
                  pl.BlockSpec((None, 1, tn), lambda l, j: (l, 0, j))],
        out_specs=pl.BlockSpec((None, R, tn), lambda l, j: (l, 0, j)),
        out_shape=jax.ShapeDtypeStruct((L, R, N6), F32),
        compiler_params=_cparams("arbitrary", "arbitrary"),
        name="modulation",
    )(cvec, ada_w, ada_b.reshape(L, 1, N6))


class Rows:
    def __init__(self, nc, nl, t):
        assert nc % t == 0, "latent sequences must start on a multiple of their length"
        self.nc, self.nl, self.t, self.n = nc, nl, t, nc + nl

    def tile(self, pref):
        return _fit(np.gcd(self.nc, self.t), pref)

    def mod_row(self, i, tm):
        ncb = self.nc // tm
        return jnp.where(i < ncb, 0, 1 + (i - ncb) // (self.t // tm))

    def pos_block(self, i, tm):
        ncb = self.nc // tm
        return jnp.where(i < ncb, self.t // tm, (i - ncb) % (self.t // tm))


def _mod_spec(rows, tm, k, D, first=0):
    return pl.BlockSpec((None, 1, D), lambda i, *_: (rows.mod_row(first + i, tm) * 6 + k, 0, 0))


def _vec_spec(D):
    return pl.BlockSpec((1, D), lambda i, *_: (0, 0))


def _split_specs(rows, tm, D):
    ncb = rows.nc // tm
    return [pl.BlockSpec((tm, D), lambda i, *_: (jnp.minimum(i, ncb - 1), 0)),
            pl.BlockSpec((tm, D), lambda i, *_: (jnp.maximum(i - ncb, 0), 0))]


def _into(dst, in_specs, args):
    return in_specs + [pl.BlockSpec(memory_space=pl.ANY)], args + [dst], {len(args): 0}


def _without_ref(kernel, idx):
    return lambda *refs: kernel(*refs[:idx], *refs[idx + 1:])


PRENORM_ROWS = 16


def _pre_kernel(xc_ref, xs_ref, g_ref, sh_ref, sc_ref, h_ref, *, ncb):
    def norm(x_ref):
        for s in range(0, h_ref.shape[0], PRENORM_ROWS):
            r = slice(s, s + PRENORM_ROWS)
            h = _rms(x_ref[r, :], g_ref[...]) * (1.0 + sc_ref[...]) + sh_ref[...]
            h_ref[r, :] = h.astype(h_ref.dtype)

    is_ctx = pl.program_id(0) < ncb
    pl.when(is_ctx)(lambda: norm(xc_ref))
    pl.when(jnp.logical_not(is_ctx))(lambda: norm(xs_ref))


def prenorm(xc, xs, g, mod3, k, rows, tm=512):
    D = xc.shape[1]
    tm = rows.tile(tm)
    return pl.pallas_call(
        functools.partial(_pre_kernel, ncb=rows.nc // tm),
        grid=(rows.n // tm,),
        in_specs=_split_specs(rows, tm, D) + [_vec_spec(D), _mod_spec(rows, tm, k, D),
                                              _mod_spec(rows, tm, k + 1, D)],
        out_specs=pl.BlockSpec((tm, D), lambda i: (i, 0)),
        out_shape=jax.ShapeDtypeStruct((rows.n, D), BF16),
        compiler_params=_cparams("arbitrary"),
        name="prenorm",
    )(xc, xs, g.reshape(1, D), mod3, mod3)


def _mm_kernel(x_ref, w_ref, o_ref, *scratch, nk, act):
    def finish(acc):
        if act == "relu2":
            a = jnp.maximum(acc, 0.0)
            acc = a * a
        o_ref[...] = acc.astype(o_ref.dtype)

    if nk == 1:
        finish(jnp.dot(x_ref[...], w_ref[...], preferred_element_type=F32))
        return
    acc_ref, = scratch
    k = pl.program_id(2)

    @pl.when(k == 0)
    def _():
        acc_ref[...] = jnp.zeros_like(acc_ref)

    acc_ref[...] += jnp.dot(x_ref[...], w_ref[...], preferred_element_type=F32)

    @pl.when(k == nk - 1)
    def _():
        finish(acc_ref[...])


def matmul(x, w, *, tm, tn, tk=None, out_dtype=F32, act=None, name="matmul"):
    M, K = x.shape
    _, N = w.shape
    tm, tn, tk = _fit(M, tm), _fit(N, tn), _fit(K, tk or K)
    nk = K // tk
    assert M % tm == 0 and N % tn == 0 and K % tk == 0
    return pl.pallas_call(
        functools.partial(_mm_kernel, nk=nk, act=act),
        grid=(M // tm, N // tn, nk),
        in_specs=[pl.BlockSpec((tm, tk), lambda i, j, k: (i, k)),
                  pl.BlockSpec((tk, tn), lambda i, j, k: (k, j))],
        out_specs=pl.BlockSpec((tm, tn), lambda i, j, k: (i, j)),
        out_shape=jax.ShapeDtypeStruct((M, N), out_dtype),
        scratch_shapes=[pltpu.VMEM((tm, tn), F32)] if nk > 1 else [],
        compiler_params=_cparams("arbitrary", "arbitrary", "arbitrary"),
        name=name,
    )(x, w)


def _mm_wcast_kernel(x_ref, w_ref, o_ref, wb_ref, *, act):
    @pl.when(pl.program_id(1) == 0)
    def _():
        wb_ref[...] = w_ref[...].astype(BF16)

    acc = jnp.dot(x_ref[...], wb_ref[...], preferred_element_type=F32)
    if act == "relu2":
        a = jnp.maximum(acc, 0.0)
        acc = a * a
    o_ref[...] = acc.astype(o_ref.dtype)


def matmul_wcast(x, w, layer, *, tm, tn, out_dtype=F32, act=None, vmem_limit=VMEM_LIMIT, name="matmul_wcast"):
    M, K = x.shape
    _, _, N = w.shape
    tm, tn = _fit(M, tm), _fit(N, tn)
    return pl.pallas_call(
        functools.partial(_mm_wcast_kernel, act=act),
        grid=(N // tn, M // tm),
        in_specs=[pl.BlockSpec((tm, K), lambda j, i: (i, 0)),
                  pl.BlockSpec((None, K, tn), lambda j, i: (layer, 0, j))],
        out_specs=pl.BlockSpec((tm, tn), lambda j, i: (i, j)),
        out_shape=jax.ShapeDtypeStruct((M, N), out_dtype),
        scratch_shapes=[pltpu.VMEM((K, tn), BF16)],
        compiler_params=_cparams("arbitrary", "arbitrary", vmem_limit=vmem_limit),
        name=name,
    )(x, w)


def _mm_post_kernel(*refs, nk, with_pre, sub, ncb, split_in):
    it = iter(refs)
    a_ref, w_ref = next(it), next(it)
    x_refs = [next(it) for _ in range(2 if split_in else 1)]
    gp_ref, gate_ref = next(it), next(it)
    gn_ref, sh_ref, sc_ref = (next(it), next(it), next(it)) if with_pre else (None, None, None)
    x1_ref = next(it)
    h_ref = next(it) if with_pre else None
    tm = a_ref.shape[0]
    is_ctx = pl.program_id(0) < ncb

    def finish(r, y):
        x = jnp.where(is_ctx, x_refs[0][r, :], x_refs[1][r, :]) if split_in else x_refs[0][r, :]
        x1 = x + gate_ref[...] * _rms(y, gp_ref[...])
        x1_ref[r, :] = x1
        if with_pre:
            h_ref[r, :] = (_rms(x1, gn_ref[...]) * (1.0 + sc_ref[...]) + sh_ref[...]).astype(h_ref.dtype)

    def product(r):
        return jnp.dot(a_ref[r, :], w_ref[...], preferred_element_type=F32)

    blocks = [slice(s, s + sub) for s in range(0, tm, sub)]
    everything = slice(None)

    if nk == 1:
        for r in blocks:
            finish(r, product(r))
        return
    k = pl.program_id(1)

    @pl.when(k == 0)
    def _():
        x1_ref[...] = product(everything)

    @pl.when(jnp.logical_and(k > 0, k < nk - 1))
    def _():
        x1_ref[...] += product(everything)

    @pl.when(k == nk - 1)
    def _():
        for r in blocks:
            finish(r, x1_ref[r, :] + product(r))


def matmul_post(a, w, layer, x, g_post, mod3, k, rows, pre=None, *, tiles=None, tm=512, tk=None, sub=256,
                vmem_limit=VMEM_LIMIT, name="matmul_post"):
    M, K = a.shape
    D = w.shape[2]
    tm = rows.tile(tm)
    tk = _fit(K, tk or K)
    nk = K // tk
    first, count = tiles or (0, M // tm)
    out_spec = pl.BlockSpec((tm, D), lambda i, kk: (i, 0))
    split_in = isinstance(x, tuple)
    assert not (split_in and first), "a row range of a split residual is not supported"
    x_specs, xs = ((_split_specs(rows, tm, D), list(x)) if split_in
                   else ([pl.BlockSpec((tm, D), lambda i, kk: (first + i, 0))], [x]))
    in_specs = [pl.BlockSpec((tm, tk), lambda i, kk: (first + i, kk)),
                pl.BlockSpec((None, tk, D), lambda i, kk: (layer, kk, 0)),
                *x_specs, _vec_spec(D), _mod_spec(rows, tm, k + 2, D, first)]
    args = [a, w, *xs, g_post.reshape(1, D), mod3]
    out_specs = [out_spec]
    out_shape = [jax.ShapeDtypeStruct((count * tm, D), F32)]
    if pre is not None:
        g_pre, mod3_pre, kp = pre
        in_specs += [_vec_spec(D), _mod_spec(rows, tm, kp, D, first), _mod_spec(rows, tm, kp + 1, D, first)]
        args += [g_pre.reshape(1, D), mod3_pre, mod3_pre]
        out_specs.append(out_spec)
        out_shape.append(jax.ShapeDtypeStruct((count * tm, D), BF16))
    out = pl.pallas_call(
        functools.partial(_mm_post_kernel, nk=nk, with_pre=pre is not None, sub=_fit(tm, sub), ncb=rows.nc // tm,
                          split_in=split_in),
        grid=(count, nk),
        in_specs=in_specs, out_specs=out_specs, out_shape=out_shape,
        compiler_params=_cparams("arbitrary", "arbitrary", vmem_limit=vmem_limit),
        name=name,
    )(*args)
    return out if len(out) > 1 else out[0]


HGRN_CHUNK = 64
HGRN_BLOCK = 256
HGRN_SAFE_EXPONENT = 60.0


def _hgrn_kernel(*refs, T, hb, layer, has_init, emit_state, unroll_blocks, unroll_scan):
    it = iter(refs)
    q_ref, ff_ref, fb_ref, i_ref, g_ref, lbf_ref, lbb_ref, ng_ref = (next(it) for _ in range(8))
    s0_refs = (next(it), next(it)) if has_init else None
    o_ref = next(it)
    s_out_refs = (next(it), next(it)) if emit_state else None
    qe_refs, ds_refs, oi_refs, ebt_refs = ((next(it), next(it)) for _ in range(4))
    C, R = HGRN_CHUNK, min(HGRN_BLOCK, T)
    nch, nblk, cpb, mid = T // C, T // R, R // C, C // 2
    f_refs, lb_refs = (ff_ref, fb_ref), (lbf_ref, lbb_ref)
    total_row = (C - 1, 0)

    def lower_bound(lb_ref, sl):
        z = lb_ref[:, sl]
        e = jnp.exp(z - jnp.max(z, axis=0, keepdims=True))
        sm = e / jnp.sum(e, axis=0, keepdims=True)
        return jnp.sum(sm[:layer + 1], axis=0, keepdims=True)

    row = lax.broadcasted_iota(jnp.int32, (R, R), 0)
    col = lax.broadcasted_iota(jnp.int32, (R, R), 1)
    same_chunk = (row // C) == (col // C)
    tri = (same_chunk & (row >= col), same_chunk & (row <= col))

    def cumsum(mask, x):
        hi = x.astype(BF16)
        r1 = x - hi.astype(F32)
        md = r1.astype(BF16)
        lo = (r1 - md.astype(F32)).astype(BF16)
        s = jnp.dot(mask.astype(BF16), jnp.concatenate([hi, md, lo], axis=1), preferred_element_type=F32)
        return s[:, :HEAD_DIM] + s[:, HEAD_DIM:2 * HEAD_DIM] + s[:, 2 * HEAD_DIM:]

    heads = [slice(h * HEAD_DIM, (h + 1) * HEAD_DIM) for h in range(hb)]
    lbs = [[lower_bound(lb_refs[d], sl) for sl in heads] for d in range(2)]
    chains = [(h, sl, d) for h, sl in enumerate(heads) for d in range(2)]

    def block_pass(blk, risk):
        r = pl.ds(pl.multiple_of(blk * R, R), R)
        qs, vb, kk, b, qm, km, kd, a = {}, {}, {}, {}, {}, {}, {}, {}
        for h, sl in enumerate(heads):
            qs[h] = _silu(q_ref[r, sl]) * HEAD_DIM ** -0.5
            vb[h] = i_ref[r, sl].astype(BF16)
        for h, sl, d in chains:
            lb = lbs[d][h]
            f = lb + (1.0 - lb) * jax.nn.sigmoid(f_refs[d][r, sl])
            kk[h, d] = 1.0 - f
            b[h, d] = cumsum(tri[d], jnp.log(f))
        for h, sl, d in chains:
            qmc, kmc, kd[h, d] = [], [], []
            for cc in range(cpb):
                cs = slice(cc * C, (cc + 1) * C)
                bc = b[h, d][cs]
                m = bc[mid:mid + 1]
                bt = bc[total_row[d]:total_row[d] + 1]
                qc = qs[h][cs] * jnp.exp(bc - m)
                kc = kk[h, d][cs] * jnp.exp(m - bc)
                risk = jnp.maximum(risk, jnp.maximum(jnp.abs(bc[:1] - m), jnp.abs(bc[C - 1:] - m)))
                rc = pl.ds(pl.multiple_of(blk * R + cc * C, C), C)
                qe_refs[d][rc, sl] = (qs[h][cs] * jnp.exp(bc)).astype(BF16)
                kd[h, d].append((kk[h, d][cs] * jnp.exp(bt - bc)).astype(BF16))
                slot = pl.ds(pl.multiple_of((blk * cpb + cc) * SUBLANES, SUBLANES), SUBLANES)
                ebt_refs[d][slot, sl] = jnp.broadcast_to(jnp.exp(bt), (SUBLANES, HEAD_DIM))
                qmc.append(qc.astype(BF16))
                kmc.append(kc.astype(BF16))
            qm[h, d] = jnp.concatenate(qmc, axis=0)
            km[h, d] = jnp.concatenate(kmc, axis=0)
        for h, sl, d in chains:
            s = lax.dot_general(qm[h, d], km[h, d], NT, preferred_element_type=F32)
            a[h, d] = jnp.where(tri[d], s, 0.0).astype(BF16)
        for h, sl, d in chains:
            oi_refs[d][r, sl] = jnp.dot(a[h, d], vb[h], preferred_element_type=F32)
        for h, sl, d in chains:
            for cc in range(cpb):
                rows_c = pl.ds(pl.multiple_of((blk * cpb + cc) * HEAD_DIM, HEAD_DIM), HEAD_DIM)
                ds_refs[d][rows_c, sl] = lax.dot_general(vb[h][cc * C:(cc + 1) * C], kd[h, d][cc], TN,
                                                         preferred_element_type=F32)
        return risk

    risk = lax.fori_loop(0, nblk, block_pass, jnp.zeros((1, HEAD_DIM), F32), unroll=unroll_blocks)

    def safe_scores(d, qs, kk, lf):
        scores = jnp.where(row == col, lax.dot_general(qs.astype(BF16), kk.astype(BF16), NT,
                                                       preferred_element_type=F32), 0.0)
        half = 1
        while half < C:
            same = (row // (2 * half)) == (col // (2 * half))
            r_up, c_up = (row % (2 * half)) >= half, (col % (2 * half)) >= half
            r_lo, c_lo = jnp.logical_not(r_up), jnp.logical_not(c_up)
            if d == 0:
                to_t, from_s, pair = same & r_up & c_up & (col <= row), same & r_lo & c_lo & (col > row), same & r_up & c_lo
            else:
                to_t, from_s, pair = same & r_lo & c_lo & (col >= row), same & r_up & c_up & (col < row), same & r_lo & c_up
            qd = (qs * jnp.exp(cumsum(to_t, lf))).astype(BF16)
            kd = (kk * jnp.exp(cumsum(from_s, lf))).astype(BF16)
            scores = scores + jnp.where(pair, lax.dot_general(qd, kd, NT, preferred_element_type=F32), 0.0)
            half *= 2
        return scores

    def safe_block(blk, carry):
        r = pl.ds(pl.multiple_of(blk * R, R), R)
        for h, sl, d in chains:
            lb = lbs[d][h]
            f = lb + (1.0 - lb) * jax.nn.sigmoid(f_refs[d][r, sl])
            scores = safe_scores(d, _silu(q_ref[r, sl]) * HEAD_DIM ** -0.5, 1.0 - f, jnp.log(f))
            oi_refs[d][r, sl] = jnp.dot(scores.astype(BF16), i_ref[r, sl].astype(BF16), preferred_element_type=F32)
        return carry

    @pl.when(jnp.max(risk) > HGRN_SAFE_EXPONENT)
    def _():
        lax.fori_loop(0, nblk, safe_block, 0)

    def scan_step(j, states):
        cs = [j, nch - 1 - j]
        rs = [pl.ds(pl.multiple_of(c * C, C), C) for c in cs]
        slots = [pl.ds(pl.multiple_of(c * SUBLANES, SUBLANES), SUBLANES) for c in cs]
        grow = [pl.ds(pl.multiple_of(c * HEAD_DIM, HEAD_DIM), HEAD_DIM) for c in cs]
        new = []
        for h, sl, d in chains:
            St = states[2 * h + d]
            oi_refs[d][rs[d], sl] += lax.dot_general(qe_refs[d][rs[d], sl], St.astype(BF16), NT,
                                                     preferred_element_type=F32)
            new.append(St * ebt_refs[d][slots[d], sl][:1] + ds_refs[d][grow[d], sl])
        return tuple(new)

    if has_init:
        init = tuple(s0_refs[d][h].T for h in range(hb) for d in range(2))
    else:
        init = tuple(jnp.zeros((HEAD_DIM, HEAD_DIM), F32) for _ in range(2 * hb))
    states = lax.fori_loop(0, nch, scan_step, init, unroll=unroll_scan)
    if emit_state:
        for h in range(hb):
            for d in range(2):
                s_out_refs[d][h] = states[2 * h + d].T

    def final_pass(blk, carry):
        r = pl.ds(pl.multiple_of(blk * R, R), R)
        for sl in heads:
            o = _rms(oi_refs[0][r, sl] + oi_refs[1][r, sl], ng_ref[...]) * _silu(g_ref[r, sl])
            o_ref[r, sl] = o.astype(o_ref.dtype)
        return carry

    lax.fori_loop(0, nblk, final_pass, 0)


def hgrn(p, lb_f, lb_b, norm_g, dst, *, nseq, T, row0, A_heads, layer, hb, init=None, emit_state=False,
         unroll_blocks=1, unroll_scan=8):
    assert T % min(HGRN_BLOCK, T) == 0 and A_heads % hb == 0
    rb0 = row0 // T
    H = A_heads
    W = hb * HEAD_DIM
    ng = H // hb

    def slab(k):
        return pl.BlockSpec((T, W), lambda b, h: (rb0 + b, k * ng + h))

    nl = lb_f.shape[0]
    in_specs = [slab(k) for k in range(5)] + [
        pl.BlockSpec((nl, W), lambda b, h: (0, h)),
        pl.BlockSpec((nl, W), lambda b, h: (0, h)),
        pl.BlockSpec((1, HEAD_DIM), lambda b, h: (0, 0))]
    args = [p] * 5 + [lb_f, lb_b, norm_g.reshape(1, HEAD_DIM)]
    state_spec = pl.BlockSpec((None, hb, HEAD_DIM, HEAD_DIM), lambda b, h: (b, h, 0, 0))
    if init is not None:
        in_specs += [state_spec, state_spec]
        args += list(init)
    in_specs, args, aliases = _into(dst, in_specs, args)
    out_specs = [pl.BlockSpec((T, W), lambda b, h: (rb0 + b, h))]
    out_shape = [jax.ShapeDtypeStruct(dst.shape, dst.dtype)]
    if emit_state:
        out_specs += [state_spec, state_spec]
        out_shape += [jax.ShapeDtypeStruct((nseq, H, HEAD_DIM, HEAD_DIM), F32)] * 2
    nch = T // HGRN_CHUNK
    scratch = ([pltpu.VMEM((T, W), BF16)] * 2
               + [pltpu.VMEM((nch * HEAD_DIM, W), F32)] * 2
               + [pltpu.VMEM((T, W), F32)] * 2
               + [pltpu.VMEM((nch * SUBLANES, W), F32)] * 2)
    body = functools.partial(_hgrn_kernel, T=T, hb=hb, layer=layer, has_init=init is not None, emit_state=emit_state,
                             unroll_blocks=unroll_blocks, unroll_scan=unroll_scan)
    out = pl.pallas_call(
        _without_ref(body, len(args) - 1),
        grid=(nseq, ng),
        in_specs=in_specs, out_specs=out_specs, out_shape=out_shape,
        input_output_aliases=aliases,
        scratch_shapes=scratch,
        compiler_params=_cparams("arbitrary", "arbitrary"),
        name="hgrn",
    )(*args)
    return out if emit_state else out[0]


def _dense_attn_kernel(q_ref, k_ref, v_ref, o_ref, kout_ref, vout_ref, *, H, scale):
    heads = [slice(h * HEAD_DIM, (h + 1) * HEAD_DIM) for h in range(H)]
    problems = []
    for h, sl in enumerate(heads):
        kout_ref[:, h, :] = k_ref[:, sl]
        vout_ref[:, h, :] = v_ref[:, sl]
        q = (q_ref[:, sl] * scale).astype(BF16)
        s = lax.dot_general(q, k_ref[:, sl].astype(BF16), NT, preferred_element_type=F32)
        problems.append(([s], [v_ref[:, sl].astype(BF16)]))
    for sl, o in zip(heads, _softmax_pv(problems)):
        o_ref[:, sl] = o.astype(o_ref.dtype)


def dense_attn(p, dst, *, nseq, T, H, col0, dst_col0):
    W = H * HEAD_DIM
    cb = col0 // W
    in_specs, args, aliases = _into(dst, [pl.BlockSpec((T, W), lambda b, k=k: (b, cb + k)) for k in range(3)],
                                    [p, p, p])
    cache_spec = pl.BlockSpec((None, T, H, HEAD_DIM), lambda b: (b, 0, 0, 0))
    cache_shape = jax.ShapeDtypeStruct((nseq, T, H, HEAD_DIM), p.dtype)
    return pl.pallas_call(
        _without_ref(functools.partial(_dense_attn_kernel, H=H, scale=HEAD_DIM ** -0.5 * LOG2E), 3),
        grid=(nseq,),
        in_specs=in_specs,
        out_specs=[pl.BlockSpec((T, W), lambda b: (b, dst_col0 // W)), cache_spec, cache_spec],
        out_shape=[jax.ShapeDtypeStruct(dst.shape, dst.dtype), cache_shape, cache_shape],
        input_output_aliases=aliases,
        compiler_params=_cparams("arbitrary"),
        name="dense_attn",
    )(*args)


NA_QROWS = 4
NA_KROWS = 12


def na_bias_tables(rpb, rows):
    W = GRID_W
    kr = min(NA_ROWS, rows)
    assert kr == NA_ROWS and rows >= NA_KROWS and rows % NA_QROWS == 0
    nblk = rows // NA_QROWS
    col = np.arange(W)
    cs = np.clip(col - NA_COLS // 2, 0, W - NA_COLS)
    col_ok = (col[None, :] >= cs[:, None]) & (col[None, :] < cs[:, None] + NA_COLS)
    ci = np.clip(col[None, :] - col[:, None] + NA_COLS - 1, 0, 2 * NA_COLS - 2)
    col_sel = (ci[..., None] == np.arange(2 * NA_COLS - 1)).astype(np.float32)
    G = jnp.einsum("hac,xyc->haxy", rpb.astype(F32), col_sel, precision=lax.Precision.HIGHEST)
    G = jnp.where(col_ok, G, NEG)
    outside = 2 * NA_ROWS - 1
    G = jnp.concatenate([G, jnp.full_like(G[:, :1], NEG)], axis=1)
    tables = []
    for blk in (0, 1, nblk - 1):
        r0 = blk * NA_QROWS
        u0 = min(max(r0 - NA_ROWS // 2, 0), rows - NA_KROWS)
        r = r0 + np.arange(NA_QROWS)
        ka = u0 + np.arange(NA_KROWS)
        start = np.clip(r - kr // 2, 0, rows - kr)
        row_ok = (ka[None, :] >= start[:, None]) & (ka[None, :] < start[:, None] + kr)
        ri = np.where(row_ok, ka[None, :] - r[:, None] + NA_ROWS - 1, outside)
        tables.append(jnp.concatenate(
            [jnp.concatenate([G[:, int(a)] for a in ri[q]], axis=-1) for q in range(NA_QROWS)], axis=-2))
    return jnp.stack(tables) * LOG2E


def _na_kernel(q_ref, k_ref, v_ref, kc_ref, vc_ref, bias_ref, o_ref, *, rows, scale, hp):
    blk = pl.program_id(2)
    u0 = jnp.clip(blk * NA_QROWS - NA_ROWS // 2, 0, rows - NA_KROWS)
    band = pl.ds(pl.multiple_of(u0 * GRID_W, GRID_W), NA_KROWS * GRID_W)
    heads = [slice(h * HEAD_DIM, (h + 1) * HEAD_DIM) for h in range(hp)]
    problems = []
    for h, sl in enumerate(heads):
        q = (q_ref[:, sl] * scale).astype(BF16)
        s_lat = lax.dot_general(q, k_ref[band, sl].astype(BF16), NT, preferred_element_type=F32) + bias_ref[h]
        s_ctx = lax.dot_general(q, kc_ref[:, sl].astype(BF16), NT, preferred_element_type=F32)
        problems.append(([s_lat, s_ctx], [v_ref[band, sl].astype(BF16), vc_ref[:, sl].astype(BF16)]))
    for sl, o in zip(heads, _softmax_pv(problems)):
        o_ref[:, sl] = o.astype(o_ref.dtype)


def na_attn(p, kc, vc, bias, dst, *, nseq, T, H, row0, col0, dst_col0, hp):
    rows = T // GRID_W
    nblk = rows // NA_QROWS
    tq = NA_QROWS * GRID_W
    L = kc.shape[0] // nseq
    W = hp * HEAD_DIM
    qb0, kb0, cb = row0 // tq, row0 // T, col0 // W
    ng = H // hp

    def kind(blk):
        return jnp.where(blk == 0, 0, jnp.where(blk == nblk - 1, 2, 1))

    in_specs, args, aliases = _into(dst, [
        pl.BlockSpec((tq, W), lambda b, h, i: (qb0 + b * nblk + i, cb + h)),
        pl.BlockSpec((T, W), lambda b, h, i: (kb0 + b, cb + ng + h)),
        pl.BlockSpec((T, W), lambda b, h, i: (kb0 + b, cb + 2 * ng + h)),
        pl.BlockSpec((L, W), lambda b, h, i: (b, h)),
        pl.BlockSpec((L, W), lambda b, h, i: (b, h)),
        pl.BlockSpec((None, hp, tq, NA_KROWS * GRID_W), lambda b, h, i: (kind(i), h, 0, 0))],
        [p, p, p, kc, vc, bias])
    return pl.pallas_call(
        _without_ref(functools.partial(_na_kernel, rows=rows, scale=HEAD_DIM ** -0.5 * LOG2E, hp=hp), 6),
        grid=(nseq, ng, nblk),
        in_specs=in_specs,
        out_specs=pl.BlockSpec((tq, W), lambda b, h, i: (qb0 + b * nblk + i, dst_col0 // W + h)),
        out_shape=jax.ShapeDtypeStruct(dst.shape, dst.dtype),
        input_output_aliases=aliases,
        compiler_params=_cparams("arbitrary", "arbitrary", "arbitrary"),
        name="na_attn",
    )(*args)


ROPE_SWAP = np.concatenate([np.arange(16, 32), np.arange(0, 16), np.arange(48, 64), np.arange(32, 48)])


def rope_tables(T, tm):
    t = jnp.arange(T)
    half = QK_ROPE // 2
    inv = jnp.power(ROPE_BASE, -jnp.arange(0, half, 2, dtype=F32) / half)
    ang_r = (t // GRID_W).astype(F32)[:, None] * inv
    ang_c = (t % GRID_W).astype(F32)[:, None] * inv
    cos = jnp.concatenate([jnp.cos(ang_r), jnp.cos(ang_r), jnp.cos(ang_c), jnp.cos(ang_c)], axis=-1)
    sin = jnp.concatenate([-jnp.sin(ang_r), jnp.sin(ang_r), -jnp.sin(ang_c), jnp.sin(ang_c)], axis=-1)
    cos = jnp.concatenate([cos, jnp.ones((T, LANES - QK_ROPE), F32)], axis=-1)
    sin = jnp.concatenate([sin, jnp.zeros((T, LANES - QK_ROPE), F32)], axis=-1)
    cos = jnp.concatenate([cos, jnp.ones((tm, LANES), F32)], axis=0)
    sin = jnp.concatenate([sin, jnp.zeros((tm, LANES), F32)], axis=0)
    return cos, sin


def _mla_in_kernel(x_ref, w_ref, qg_ref, kvg_ref, cos_ref, sin_ref, cq_ref, ckv16_ref, k2_ref, ckv32_ref, kpe_ref,
                   *, qr, kvr, ncb):
    pr = jnp.dot(x_ref[...], w_ref[...], preferred_element_type=F32)
    cq_ref[...] = _rms(pr[:, :qr], qg_ref[...]).astype(cq_ref.dtype)
    ckv = _rms(pr[:, qr:qr + kvr], kvg_ref[...])
    ckv16_ref[...] = ckv.astype(ckv16_ref.dtype)
    x = pr[:, qr + kvr:]

    @pl.when(pl.program_id(0) < ncb)
    def _():
        ckv32_ref[...] = ckv
        kpe_ref[...] = x

    rot = x * cos_ref[...] + pltpu.roll(x, LANES // 2, axis=1) * sin_ref[...]
    lane = lax.broadcasted_iota(jnp.int32, rot.shape, 1)
    k2_ref[...] = jnp.where(lane < QK_ROPE, rot, 0.0).astype(k2_ref.dtype)


def mla_in(h, w_in_wide, q_norm_g, kv_norm_g, cos, sin, rows, *, qr, kvr, tm):
    n, D = h.shape
    ncb = rows.nc // tm
    pos = pl.BlockSpec((tm, LANES), lambda i: (rows.pos_block(i, tm), 0))

    def out(w):
        return pl.BlockSpec((tm, w), lambda i: (i, 0))

    def ctx_out(w):
        return pl.BlockSpec((tm, w), lambda i: (jnp.minimum(i, ncb - 1), 0))

    return pl.pallas_call(
        functools.partial(_mla_in_kernel, qr=qr, kvr=kvr, ncb=ncb),
        grid=(n // tm,),
        in_specs=[pl.BlockSpec((tm, D), lambda i: (i, 0)), pl.BlockSpec(w_in_wide.shape, lambda i: (0, 0)),
                  _vec_spec(qr), _vec_spec(kvr), pos, pos],
        out_specs=[out(qr), out(kvr), out(LANES), ctx_out(kvr), ctx_out(LANES)],
        out_shape=[jax.ShapeDtypeStruct((n, qr), BF16), jax.ShapeDtypeStruct((n, kvr), BF16),
                   jax.ShapeDtypeStruct((n, LANES), BF16), jax.ShapeDtypeStruct((rows.nc, kvr), F32),
                   jax.ShapeDtypeStruct((rows.nc, LANES), F32)],
        compiler_params=_cparams("arbitrary"),
        name="mla_in",
    )(h, w_in_wide, q_norm_g.reshape(1, qr), kv_norm_g.reshape(1, kvr), cos, sin)


MLA_Q_IN = 3 * LANES
MLA_Q_OUT = 2 * LANES


def widen_w_uq(w_uq, heads):
    r = w_uq.shape[0]
    w = w_uq.reshape(r, heads, QK_NOPE + QK_ROPE)
    nope, pe = w[..., :QK_NOPE], w[..., QK_NOPE:]
    return jnp.concatenate([nope, pe, pe, pe[..., ROPE_SWAP], jnp.zeros_like(pe)], axis=-1).reshape(r, heads * MLA_Q_IN)


def _mla_q_kernel(x_ref, w_ref, cos_ref, sin_ref, o_ref, *, hpt, scale):
    acc = jnp.dot(x_ref[...], w_ref[...], preferred_element_type=F32)
    cos, sin = cos_ref[...], sin_ref[...]
    for j in range(hpt):
        a = acc[:, j * MLA_Q_IN:(j + 1) * MLA_Q_IN]
        o_ref[:, j * MLA_Q_OUT:j * MLA_Q_OUT + LANES] = (a[:, :LANES] * scale).astype(o_ref.dtype)
        q2 = a[:, LANES:2 * LANES] * cos + a[:, 2 * LANES:] * sin
        o_ref[:, j * MLA_Q_OUT + LANES:(j + 1) * MLA_Q_OUT] = (q2 * scale).astype(o_ref.dtype)


def mla_q(cq, w_uq_wide, cos, sin, rows, *, heads, scale, tm, hpt=4):
    n, r = cq.shape
    pos = pl.BlockSpec((tm, LANES), lambda i, j: (rows.pos_block(i, tm), 0))
    return pl.pallas_call(
        functools.partial(_mla_q_kernel, hpt=hpt, scale=scale),
        grid=(n // tm, heads // hpt),
        in_specs=[pl.BlockSpec((tm, r), lambda i, j: (i, 0)),
                  pl.BlockSpec((r, hpt * MLA_Q_IN), lambda i, j: (0, j)), pos, pos],
        out_specs=pl.BlockSpec((tm, hpt * MLA_Q_OUT), lambda i, j: (i, j)),
        out_shape=jax.ShapeDtypeStruct((n, heads * MLA_Q_OUT), BF16),
        compiler_params=_cparams("arbitrary", "arbitrary"),
        name="mla_q",
    )(cq, w_uq_wide, cos, sin)


def _mla_ctx_attn_kernel(q_ref, kv_ref, k2_ref, o_ref, *, heads):
    k2 = k2_ref[...]
    problems = []
    for h in range(heads):
        q = q_ref[:, h * MLA_Q_OUT:(h + 1) * MLA_Q_OUT]
        k = jnp.concatenate([kv_ref[:, 2 * h * LANES:(2 * h + 1) * LANES], k2], axis=1)
        s = lax.dot_general(q, k, NT, preferred_element_type=F32)
        problems.append(([s], [kv_ref[:, (2 * h + 1) * LANES:(2 * h + 2) * LANES]]))
    for h, o in enumerate(_softmax_pv(problems)):
        o_ref[:, h * V_DIM:(h + 1) * V_DIM] = o.astype(o_ref.dtype)


def mla_ctx_attn(q, kv, k2, dst, *, nseq, T, heads):
    in_specs, args, aliases = _into(dst, [pl.BlockSpec((T, heads * MLA_Q_OUT), lambda b: (b, 0)),
                                          pl.BlockSpec((T, kv.shape[1]), lambda b: (b, 0)),
                                          pl.BlockSpec((T, LANES), lambda b: (b, 0))], [q, kv, k2])
    return pl.pallas_call(
        _without_ref(functools.partial(_mla_ctx_attn_kernel, heads=heads), 3),
        grid=(nseq,),
        in_specs=in_specs,
        out_specs=pl.BlockSpec((T, heads * V_DIM), lambda b: (b, 0)),
        out_shape=jax.ShapeDtypeStruct(dst.shape, dst.dtype),
        input_output_aliases=aliases,
        compiler_params=_cparams("arbitrary"),
        name="mla_ctx_attn",
    )(*args)


def _mla_lat_attn_kernel(q_ref, k1_ref, v_ref, k2_ref, k1c_ref, vc_ref, k2c_ref, o_ref, *, sub):
    k = jnp.concatenate([k1_ref[...], k2_ref[...]], axis=1)
    kc = jnp.concatenate([k1c_ref[...], k2c_ref[...]], axis=1)
    blocks = [slice(s, s + sub) for s in range(0, q_ref.shape[0], sub)]
    problems = []
    for r in blocks:
        q = q_ref[r, :]
        s_lat = lax.dot_general(q, k, NT, preferred_element_type=F32)
        s_ctx = lax.dot_general(q, kc, NT, preferred_element_type=F32)
        problems.append(([s_lat, s_ctx], [v_ref[...], vc_ref[...]]))
    for r, o in zip(blocks, _softmax_pv(problems)):
        o_ref[r, :] = o.astype(o_ref.dtype)


def mla_lat_attn(q, kv, k2, kvc, k2c, dst, *, nseq, T, heads, row0, tq, sub):
    nq = T // tq
    P = kvc.shape[0] // nseq
    qb0, kb0 = row0 // tq, row0 // T
    in_specs, args, aliases = _into(dst, [
        pl.BlockSpec((tq, MLA_Q_OUT), lambda b, h, i: (qb0 + b * nq + i, h)),
        pl.BlockSpec((T, LANES), lambda b, h, i: (kb0 + b, 2 * h)),
        pl.BlockSpec((T, LANES), lambda b, h, i: (kb0 + b, 2 * h + 1)),
        pl.BlockSpec((T, LANES), lambda b, h, i: (kb0 + b, 0)),
        pl.BlockSpec((P, LANES), lambda b, h, i: (b, 2 * h)),
        pl.BlockSpec((P, LANES), lambda b, h, i: (b, 2 * h + 1)),
        pl.BlockSpec((P, LANES), lambda b, h, i: (b, 0))],
        [q, kv, kv, k2, kvc, kvc, k2c])
    return pl.pallas_call(
        _without_ref(functools.partial(_mla_lat_attn_kernel, sub=_fit(tq, sub)), 7),
        grid=(nseq, heads, nq),
        in_specs=in_specs,
        out_specs=pl.BlockSpec((tq, V_DIM), lambda b, h, i: (qb0 + b * nq + i, h)),
        out_shape=jax.ShapeDtypeStruct(dst.shape, dst.dtype),
        input_output_aliases=aliases,
        compiler_params=_cparams("arbitrary", "arbitrary", "arbitrary"),
        name="mla_lat_attn",
    )(*args)


def even_layer(h, rows, B, SEQ, Bd, state_f, state_b, cache_k, cache_v, lb_f, lb_b, w_in, j, hgrn_g, rpb, layer):
    D = h.shape[1]
    AW = D // 2
    AH = BH = AW // HEAD_DIM
    T = rows.t
    p = matmul_wcast(h, w_in, j, tm=1024, tn=1024, out_dtype=F32, name="even_in_proj")
    o, s_fw, s_bw = hgrn(p, lb_f, lb_b, hgrn_g, h, nseq=B, T=SEQ, row0=0, A_heads=AH, layer=layer, hb=_fit(AH, 4),
                         emit_state=True)
    o = hgrn(p, lb_f, lb_b, hgrn_g, o, nseq=Bd, T=T, row0=rows.nc, A_heads=AH, layer=layer, hb=_fit(AH, 2),
             init=(state_f, state_b), unroll_blocks=2)
    o, new_k, new_v = dense_attn(p, o, nseq=B, T=SEQ, H=BH, col0=5 * AW, dst_col0=AW)
    past = cache_k.shape[1]
    bias = na_bias_tables(rpb, T // GRID_W)
    o = na_attn(p, cache_k.reshape(Bd * past, BH * HEAD_DIM), cache_v.reshape(Bd * past, BH * HEAD_DIM), bias, o,
                nseq=Bd, T=T, H=BH, row0=rows.nc, col0=5 * AW, dst_col0=AW, hp=_fit(BH, 4))
    return o, s_fw, s_bw, new_k, new_v


def odd_layer(h, rows, B, SEQ, Bd, cache_ckv, cache_kpe, w_in, q_norm_g, w_uq, kv_norm_g, w_ukv):
    D = h.shape[1]
    heads = D // 128
    T = rows.t
    qr, kvr = w_uq.shape[0], w_ukv.shape[0]
    scale = (QK_NOPE + QK_ROPE) ** -0.5 * LOG2E
    tm = rows.tile(1024)
    w_in_wide = jnp.concatenate([w_in, w_in[:, qr + kvr + ROPE_SWAP]], axis=1).astype(BF16)
    cos, sin = rope_tables(T, tm)
    cq, ckv16, k2, ckv32, kpe = mla_in(h, w_in_wide, q_norm_g, kv_norm_g, cos, sin, rows, qr=qr, kvr=kvr, tm=tm)
    q = mla_q(cq, widen_w_uq(w_uq, heads).astype(BF16), cos, sin, rows, heads=heads, scale=scale, tm=tm,
              hpt=_fit(heads, 8))
    w_ukv16 = w_ukv.astype(BF16)
    kv = matmul(ckv16, w_ukv16, tm=2048, tn=2048, out_dtype=BF16, name="mla_kv")
    past = cache_ckv.shape[1]
    kvc = matmul(cache_ckv.reshape(Bd * past, kvr).astype(BF16), w_ukv16, tm=1024, tn=1024, out_dtype=BF16,
                 name="mla_kv_cache")
    k2c = jnp.concatenate([jnp.zeros((Bd * past, LANES - QK_ROPE), F32), cache_kpe.reshape(Bd * past, QK_ROPE)],
                          axis=1).astype(BF16)
    o = mla_ctx_attn(q, kv, k2, h, nseq=B, T=SEQ, heads=heads)
    o = mla_lat_attn(q, kv, k2, kvc, k2c, o, nseq=Bd, T=T, heads=heads, row0=rows.nc, tq=rows.tile(2048), sub=512)
    new_ckv = ckv32.reshape(B, SEQ, kvr)
    new_kpe = kpe[:, :QK_ROPE].reshape(B, SEQ, QK_ROPE)
    return o, new_ckv, new_kpe


def kernel(x_prompt, x_sample, state_hgrn_fwd, state_hgrn_bwd, cache_na_k, cache_na_v, cache_mla_ckv, cache_mla_kpe, c, c_ctx, ada_w, ada_b, norm_g, hgrn_lb_fwd, hgrn_lb_bwd, w_in_even, hgrn_norm_g, na_rpb, w_out_even, w_in_odd, mla_q_norm_g, w_uq, mla_kv_norm_g, w_ukv, w_out_odd, mlp_w1, mlp_w2):
    B, SEQ, D = x_prompt.shape
    Bd, T, _ = x_sample.shape
    depth = ada_w.shape[0]
    rows = Rows(B * SEQ, Bd * T, T)
    x = (x_prompt.reshape(rows.nc, D), x_sample.reshape(rows.nl, D))
    cvec = jnp.zeros((MOD_ROWS, D), F32).at[0].set(c_ctx).at[1:1 + Bd].set(c)
    mod = modulation(cvec, ada_w, ada_b).reshape(depth, MOD_ROWS * 6, 1, D)

    w2 = mlp_w2.astype(BF16)
    new_sf, new_sb, new_nk, new_nv, new_ckv, new_kpe = [], [], [], [], [], []
    h = prenorm(*x, norm_g[0, 0], mod[0], 0, rows)
    for l in range(depth):
        j = l // 2
        if l % 2 == 0:
            o, sf, sb, nk, nv = even_layer(h, rows, B, SEQ, Bd, state_hgrn_fwd[:, j], state_hgrn_bwd[:, j],
                                           cache_na_k[:, j], cache_na_v[:, j], hgrn_lb_fwd, hgrn_lb_bwd,
                                           w_in_even, j, hgrn_norm_g[j], na_rpb[j], l)
            w_out = w_out_even
            new_sf.append(sf)
            new_sb.append(sb)
            new_nk.append(nk)
            new_nv.append(nv)
        else:
            o, ckv, kpe = odd_layer(h, rows, B, SEQ, Bd, cache_mla_ckv[:, j], cache_mla_kpe[:, j], w_in_odd[j],
                                    mla_q_norm_g[j], w_uq[j], mla_kv_norm_g[j], w_ukv[j])
            w_out = w_out_odd
            new_ckv.append(ckv)
            new_kpe.append(kpe)
        big = {} if isinstance(x, tuple) else dict(tm=1024, tk=1024, vmem_limit=VMEM_LIMIT_BIG)
        x, h = matmul_post(o, w_out.astype(BF16), j, x, norm_g[l, 1], mod[l], 0, rows,
                           pre=(norm_g[l, 2], mod[l], 3), name="out_proj_post", **big)
        a = matmul_wcast(h, mlp_w1, l, tm=2048, tn=1024, out_dtype=BF16, act="relu2", vmem_limit=VMEM_LIMIT_BIG,
                         name="mlp_up")
        down = dict(tm=1024, tk=1024, vmem_limit=VMEM_LIMIT_BIG, name="mlp_down_post")
        if l + 1 < depth:
            x, h = matmul_post(a, w2, l, x, norm_g[l, 3], mod[l], 3, rows, pre=(norm_g[l + 1, 0], mod[l + 1], 0),
                               **down)
        else:
            ncb = rows.nc // rows.tile(down["tm"])
            x = tuple(matmul_post(a, w2, l, x, norm_g[l, 3], mod[l], 3, rows, tiles=t, **down)
                      for t in ((0, ncb), (ncb, rows.nl // rows.tile(down["tm"]))))
    return (x[0].reshape(B, SEQ, D), x[1].reshape(Bd, T, D),
            jnp.stack(new_sf, axis=1), jnp.stack(new_sb, axis=1), jnp.stack(new_nk, axis=1),
            jnp.stack(new_nv, axis=1), jnp.stack(new_ckv, axis=1), jnp.stack(new_kpe, axis=1))
```

```python
import functools

import numpy as np
import jax
import jax.numpy as jnp
from jax import lax
from jax.experimental import pallas as pl
from jax.experimental.pallas import tpu as pltpu

F32 = jnp.float32
BF16 = jnp.bfloat16

GRID_W = 64
HEAD_DIM = 128
NA_ROWS = 8
NA_COLS = 16
QK_NOPE = 128
QK_ROPE = 64
V_DIM = 128
ROPE_BASE = 10000.0
EPS = 1e-6
NEG = -1e30
LOG2E = float(np.log2(np.e))

LANES = 128
SUBLANES = 8
VMEM_LIMIT = 48 * 1024 * 1024
VMEM_LIMIT_BIG = 60 * 1024 * 1024
MOD_ROWS = 16

NT = (((1,), (1,)), ((), ()))
TN = (((0,), (0,)), ((), ()))


def _cparams(*sem, vmem_limit=VMEM_LIMIT):
    return pltpu.CompilerParams(dimension_semantics=sem, vmem_limit_bytes=vmem_limit)


def _fit(n, pref):
    t = min(pref, n)
    while n % t:
        t //= 2
    return t


def _silu(x):
    return x * jax.nn.sigmoid(x)


def _rms(x, g):
    return x * lax.rsqrt(jnp.mean(x * x, axis=-1, keepdims=True) + EPS) * g


def _softmax_pv(problems):
    ms = [functools.reduce(jnp.maximum, [jnp.max(s, axis=-1, keepdims=True) for s in scores])
          for scores, _ in problems]
    ps = [[jnp.exp2(s - m) for s in scores] for (scores, _), m in zip(problems, ms)]
    ls = [functools.reduce(jnp.add, [jnp.sum(p, axis=-1, keepdims=True) for p in pp]) for pp in ps]
    os = [functools.reduce(jnp.add, [jnp.dot(p.astype(BF16), v, preferred_element_type=F32)
                                     for p, v in zip(pp, values)]) for pp, (_, values) in zip(ps, problems)]
    return [o / l for o, l in zip(os, ls)]


def _mod_kernel(c_ref, w_ref, b_ref, o_ref):
    s = _silu(c_ref[...]).astype(BF16)
    o_ref[...] = jnp.dot(s, w_ref[...].astype(BF16), preferred_element_type=F32) + b_ref[...]


def modulation(cvec, ada_w, ada_b, tn=512):
    L, D, N6 = ada_w.shape
    R = cvec.shape[0]
    tn = _fit(N6, tn)
    return pl.pallas_call(
        _mod_kernel,
        grid=(L, N6 // tn),
        in_specs=[pl.BlockSpec((R, D), lambda l, j: (0, 0)),
                  pl.BlockSpec((None, D, tn), lambda l, j: (l, 0, j)),
                  pl.BlockSpec((None, 1, tn), lambda l, j: (l, 0, j))],
        out_specs=pl.BlockSpec((None, R, tn), lambda l, j: (l, 0, j)),
        out_shape=jax.ShapeDtypeStruct((L, R, N6), F32),
        compiler_params=_cparams("arbitrary", "arbitrary"),
        name="modulation",
    )(cvec, ada_w, ada_b.reshape(L, 1, N6))


class Rows:
    def __init__(self, nc, nl, t):
        assert nc % t == 0, "latent sequences must start on a multiple of their length"
        self.nc, self.nl, self.t, self.n = nc, nl, t, nc + nl

    def tile(self, pref):
        return _fit(np.gcd(self.nc, self.t), pref)

    def mod_row(self, i, tm):
        ncb = self.nc // tm
        return jnp.where(i < ncb, 0, 1 + (i - ncb) // (self.t // tm))

    def pos_block(self, i, tm):
        ncb = self.nc // tm
        return jnp.where(i < ncb, self.t // tm, (i - ncb) % (self.t // tm))


def _mod_spec(rows, tm, k, D, first=0):
    return pl.BlockSpec((None, 1, D), lambda i, *_: (rows.mod_row(first + i, tm) * 6 + k, 0, 0))


def _vec_spec(D):
    return pl.BlockSpec((1, D), lambda i, *_: (0, 0))


def _split_specs(rows, tm, D):
    ncb = rows.nc // tm
    return [pl.BlockSpec((tm, D), lambda i, *_: (jnp.minimum(i, ncb - 1), 0)),
            pl.BlockSpec((tm, D), lambda i, *_: (jnp.maximum(i - ncb, 0), 0))]


def _into(dst, in_specs, args):
    return in_specs + [pl.BlockSpec(memory_space=pl.ANY)], args + [dst], {len(args): 0}


def _without_ref(kernel, idx):
    return lambda *refs: kernel(*refs[:idx], *refs[idx + 1:])


PRENORM_ROWS = 16


def _pre_kernel(xc_ref, xs_ref, g_ref, sh_ref, sc_ref, h_ref, *, ncb):
    def norm(x_ref):
        for s in range(0, h_ref.shape[0], PRENORM_ROWS):
            r = slice(s, s + PRENORM_ROWS)
            h = _rms(x_ref[r, :], g_ref[...]) * (1.0 + sc_ref[...]) + sh_ref[...]
            h_ref[r, :] = h.astype(h_ref.dtype)

    is_ctx = pl.program_id(0) < ncb
    pl.when(is_ctx)(lambda: norm(xc_ref))
    pl.when(jnp.logical_not(is_ctx))(lambda: norm(xs_ref))


def prenorm(xc, xs, g, mod3, k, rows, tm=512):
    D = xc.shape[1]
    tm = rows.tile(tm)
    return pl.pallas_call(
        functools.partial(_pre_kernel, ncb=rows.nc // tm),
        grid=(rows.n // tm,),
        in_specs=_split_specs(rows, tm, D) + [_vec_spec(D), _mod_spec(rows, tm, k, D),
                                              _mod_spec(rows, tm, k + 1, D)],
        out_specs=pl.BlockSpec((tm, D), lambda i: (i, 0)),
        out_shape=jax.ShapeDtypeStruct((rows.n, D), BF16),
        compiler_params=_cparams("arbitrary"),
        name="prenorm",
    )(xc, xs, g.reshape(1, D), mod3, mod3)


def _mm_kernel(x_ref, w_ref, o_ref, *scratch, nk, act):
    def finish(acc):
        if act == "relu2":
            a = jnp.maximum(acc, 0.0)
            acc = a * a
        o_ref[...] = acc.astype(o_ref.dtype)

    if nk == 1:
        finish(jnp.dot(x_ref[...], w_ref[...], preferred_element_type=F32))
        return
    acc_ref, = scratch
    k = pl.program_id(2)

    @pl.when(k == 0)
    def _():
        acc_ref[...] = jnp.zeros_like(acc_ref)

    acc_ref[...] += jnp.dot(x_ref[...], w_ref[...], preferred_element_type=F32)

    @pl.when(k == nk - 1)
    def _():
        finish(acc_ref[...])


def matmul(x, w, *, tm, tn, tk=None, out_dtype=F32, act=None, name="matmul"):
    M, K = x.shape
    _, N = w.shape
    tm, tn, tk = _fit(M, tm), _fit(N, tn), _fit(K, tk or K)
    nk = K // tk
    assert M % tm == 0 and N % tn == 0 and K % tk == 0
    return pl.pallas_call(
        functools.partial(_mm_kernel, nk=nk, act=act),
        grid=(M // tm, N // tn, nk),
        in_specs=[pl.BlockSpec((tm, tk), lambda i, j, k: (i, k)),
                  pl.BlockSpec((tk, tn), lambda i, j, k: (k, j))],
        out_specs=pl.BlockSpec((tm, tn), lambda i, j, k: (i, j)),
        out_shape=jax.ShapeDtypeStruct((M, N), out_dtype),
        scratch_shapes=[pltpu.VMEM((tm, tn), F32)] if nk > 1 else [],
        compiler_params=_cparams("arbitrary", "arbitrary", "arbitrary"),
        name=name,
    )(x, w)


def _mm_wcast_kernel(x_ref, w_ref, o_ref, wb_ref, *, act):
    @pl.when(pl.program_id(1) == 0)
    def _():
        wb_ref[...] = w_ref[...].astype(BF16)

    acc = jnp.dot(x_ref[...], wb_ref[...], preferred_element_type=F32)
    if act == "relu2":
        a = jnp.maximum(acc, 0.0)
        acc = a * a
    o_ref[...] = acc.astype(o_ref.dtype)


def matmul_wcast(x, w, layer, *, tm, tn, out_dtype=F32, act=None, vmem_limit=VMEM_LIMIT, name="matmul_wcast"):
    M, K = x.shape
    _, _, N = w.shape
    tm, tn = _fit(M, tm), _fit(N, tn)
    return pl.pallas_call(
        functools.partial(_mm_wcast_kernel, act=act),
        grid=(N // tn, M // tm),
        in_specs=[pl.BlockSpec((tm, K), lambda j, i: (i, 0)),
                  pl.BlockSpec((None, K, tn), lambda j, i: (layer, 0, j))],
        out_specs=pl.BlockSpec((tm, tn), lambda j, i: (i, j)),
        out_shape=jax.ShapeDtypeStruct((M, N), out_dtype),
        scratch_shapes=[pltpu.VMEM((K, tn), BF16)],
        compiler_params=_cparams("arbitrary", "arbitrary", vmem_limit=vmem_limit),
        name=name,
    )(x, w)


def _mm_post_kernel(*refs, nk, with_pre, sub, ncb, split_in, first, fetch_x):
    it = iter(refs)
    a_ref, w_ref = next(it), next(it)
    x_refs = [next(it) for _ in range(2 if split_in else 1)]
    gp_ref, gate_ref = next(it), next(it)
    gn_ref, sh_ref, sc_ref = (next(it), next(it), next(it)) if with_pre else (None, None, None)
    x1_ref = next(it)
    h_ref = next(it) if with_pre else None
    xbuf_ref, xsem = (next(it), next(it)) if fetch_x else (None, None)
    tm = a_ref.shape[0]
    is_ctx = pl.program_id(0) < ncb

    def x_copy():
        row0 = pl.multiple_of((first + pl.program_id(0)) * tm, tm)
        return pltpu.make_async_copy(x_refs[0].at[pl.ds(row0, tm), :], xbuf_ref, xsem.at[0])

    def finish(r, y):
        if fetch_x:
            x = xbuf_ref[r, :]
        else:
            x = jnp.where(is_ctx, x_refs[0][r, :], x_refs[1][r, :]) if split_in else x_refs[0][r, :]
        x1 = x + gate_ref[...] * _rms(y, gp_ref[...])
        x1_ref[r, :] = x1
        if with_pre:
            h_ref[r, :] = (_rms(x1, gn_ref[...]) * (1.0 + sc_ref[...]) + sh_ref[...]).astype(h_ref.dtype)

    def product(r):
        return jnp.dot(a_ref[r, :], w_ref[...], preferred_element_type=F32)

    blocks = [slice(s, s + sub) for s in range(0, tm, sub)]
    everything = slice(None)

    if nk == 1:
        for r in blocks:
            finish(r, product(r))
        return
    k = pl.program_id(1)

    @pl.when(k == 0)
    def _():
        if fetch_x:
            x_copy().start()
        x1_ref[...] = product(everything)

    @pl.when(jnp.logical_and(k > 0, k < nk - 1))
    def _():
        x1_ref[...] += product(everything)

    @pl.when(k == nk - 1)
    def _():
        if fetch_x:
            x_copy().wait()
        for r in blocks:
            finish(r, x1_ref[r, :] + product(r))


def matmul_post(a, w, layer, x, g_post, mod3, k, rows, pre=None, *, tiles=None, tm=512, tk=None, sub=256,
                vmem_limit=VMEM_LIMIT, name="matmul_post"):
    M, K = a.shape
    D = w.shape[2]
    tm = rows.tile(tm)
    tk = _fit(K, tk or K)
    nk = K // tk
    first, count = tiles or (0, M // tm)
    out_spec = pl.BlockSpec((tm, D), lambda i, kk: (i, 0))
    split_in = isinstance(x, tuple)
    assert not (split_in and first), "a row range of a split residual is not supported"
    fetch_x = nk > 1 and not split_in
    if split_in:
        x_specs, xs = _split_specs(rows, tm, D), list(x)
    elif fetch_x:
        x_specs, xs = [pl.BlockSpec(memory_space=pl.ANY)], [x]
    else:
        x_specs, xs = [pl.BlockSpec((tm, D), lambda i, kk: (first + i, 0))], [x]
    in_specs = [pl.BlockSpec((tm, tk), lambda i, kk: (first + i, kk)),
                pl.BlockSpec((None, tk, D), lambda i, kk: (layer, kk, 0)),
                *x_specs, _vec_spec(D), _mod_spec(rows, tm, k + 2, D, first)]
    args = [a, w, *xs, g_post.reshape(1, D), mod3]
    out_specs = [out_spec]
    out_shape = [jax.ShapeDtypeStruct((count * tm, D), F32)]
    if pre is not None:
        g_pre, mod3_pre, kp = pre
        in_specs += [_vec_spec(D), _mod_spec(rows, tm, kp, D, first), _mod_spec(rows, tm, kp + 1, D, first)]
        args += [g_pre.reshape(1, D), mod3_pre, mod3_pre]
        out_specs.append(out_spec)
        out_shape.append(jax.ShapeDtypeStruct((count * tm, D), BF16))
    out = pl.pallas_call(
        functools.partial(_mm_post_kernel, nk=nk, with_pre=pre is not None, sub=_fit(tm, sub), ncb=rows.nc // tm,
                          split_in=split_in, first=first, fetch_x=fetch_x),
        grid=(count, nk),
        in_specs=in_specs, out_specs=out_specs, out_shape=out_shape,
        scratch_shapes=[pltpu.VMEM((tm, D), F32), pltpu.SemaphoreType.DMA((1,))] if fetch_x else [],
        compiler_params=_cparams("arbitrary", "arbitrary", vmem_limit=vmem_limit),
        name=name,
    )(*args)
    return out if len(out) > 1 else out[0]


HGRN_CHUNK = 64
HGRN_BLOCK = 256
HGRN_SAFE_EXPONENT = 60.0


def _hgrn_kernel(*refs, T, hb, layer, has_init, emit_state, unroll_blocks, unroll_scan):
    it = iter(refs)
    q_ref, ff_ref, fb_ref, i_ref, g_ref, lbf_ref, lbb_ref, ng_ref = (next(it) for _ in range(8))
    s0_refs = (next(it), next(it)) if has_init else None
    o_ref = next(it)
    s_out_refs = (next(it), next(it)) if emit_state else None
    qe_refs, ds_refs, oi_refs, ebt_refs = ((next(it), next(it)) for _ in range(4))
    C, R = HGRN_CHUNK, min(HGRN_BLOCK, T)
    nch, nblk, cpb, mid = T // C, T // R, R // C, C // 2
    f_refs, lb_refs = (ff_ref, fb_ref), (lbf_ref, lbb_ref)
    total_row = (C - 1, 0)

    def lower_bound(lb_ref, sl):
        z = lb_ref[:, sl]
        e = jnp.exp(z - jnp.max(z, axis=0, keepdims=True))
        sm = e / jnp.sum(e, axis=0, keepdims=True)
        return jnp.sum(sm[:layer + 1], axis=0, keepdims=True)

    row = lax.broadcasted_iota(jnp.int32, (R, R), 0)
    col = lax.broadcasted_iota(jnp.int32, (R, R), 1)
    same_chunk = (row // C) == (col // C)
    tri = (same_chunk & (row >= col), same_chunk & (row <= col))

    def cumsum(mask, x):
        hi = x.astype(BF16)
        r1 = x - hi.astype(F32)
        md = r1.astype(BF16)
        lo = (r1 - md.astype(F32)).astype(BF16)
        s = jnp.dot(mask.astype(BF16), jnp.concatenate([hi, md, lo], axis=1), preferred_element_type=F32)
        return s[:, :HEAD_DIM] + s[:, HEAD_DIM:2 * HEAD_DIM] + s[:, 2 * HEAD_DIM:]

    heads = [slice(h * HEAD_DIM, (h + 1) * HEAD_DIM) for h in range(hb)]
    lbs = [[lower_bound(lb_refs[d], sl) for sl in heads] for d in range(2)]
    chains = [(h, sl, d) for h, sl in enumerate(heads) for d in range(2)]

    def block_pass(blk, risk):
        r = pl.ds(pl.multiple_of(blk * R, R), R)
        qs, vb, kk, b, qm, km, kd, a = {}, {}, {}, {}, {}, {}, {}, {}
        for h, sl in enumerate(heads):
            qs[h] = _silu(q_ref[r, sl]) * HEAD_DIM ** -0.5
            vb[h] = i_ref[r, sl].astype(BF16)
        for h, sl, d in chains:
            lb = lbs[d][h]
            f = lb + (1.0 - lb) * jax.nn.sigmoid(f_refs[d][r, sl])
            kk[h, d] = 1.0 - f
            b[h, d] = cumsum(tri[d], jnp.log(f))
        for h, sl, d in chains:
            qmc, kmc, kd[h, d] = [], [], []
            for cc in range(cpb):
                cs = slice(cc * C, (cc + 1) * C)
                bc = b[h, d][cs]
                m = bc[mid:mid + 1]
                bt = bc[total_row[d]:total_row[d] + 1]
                qc = qs[h][cs] * jnp.exp(bc - m)
                kc = kk[h, d][cs] * jnp.exp(m - bc)
                risk = jnp.maximum(risk, jnp.maximum(jnp.abs(bc[:1] - m), jnp.abs(bc[C - 1:] - m)))
                rc = pl.ds(pl.multiple_of(blk * R + cc * C, C), C)
                qe_refs[d][rc, sl] = (qs[h][cs] * jnp.exp(bc)).astype(BF16)
                kd[h, d].append((kk[h, d][cs] * jnp.exp(bt - bc)).astype(BF16))
                slot = pl.ds(pl.multiple_of((blk * cpb + cc) * SUBLANES, SUBLANES), SUBLANES)
                ebt_refs[d][slot, sl] = jnp.broadcast_to(jnp.exp(bt), (SUBLANES, HEAD_DIM))
                qmc.append(qc.astype(BF16))
                kmc.append(kc.astype(BF16))
            qm[h, d] = jnp.concatenate(qmc, axis=0)
            km[h, d] = jnp.concatenate(kmc, axis=0)
        for h, sl, d in chains:
            s = lax.dot_general(qm[h, d], km[h, d], NT, preferred_element_type=F32)
            a[h, d] = jnp.where(tri[d], s, 0.0).astype(BF16)
        for h, sl, d in chains:
            oi_refs[d][r, sl] = jnp.dot(a[h, d], vb[h], preferred_element_type=F32)
        for h, sl, d in chains:
            for cc in range(cpb):
                rows_c = pl.ds(pl.multiple_of((blk * cpb + cc) * HEAD_DIM, HEAD_DIM), HEAD_DIM)
                ds_refs[d][rows_c, sl] = lax.dot_general(vb[h][cc * C:(cc + 1) * C], kd[h, d][cc], TN,
                                                         preferred_element_type=F32)
        return risk

    risk = lax.fori_loop(0, nblk, block_pass, jnp.zeros((1, HEAD_DIM), F32), unroll=unroll_blocks)

    def safe_scores(d, qs, kk, lf):
        scores = jnp.where(row == col, lax.dot_general(qs.astype(BF16), kk.astype(BF16), NT,
                                                       preferred_element_type=F32), 0.0)
        half = 1
        while half < C:
            same = (row // (2 * half)) == (col // (2 * half))
            r_up, c_up = (row % (2 * half)) >= half, (col % (2 * half)) >= half
            r_lo, c_lo = jnp.logical_not(r_up), jnp.logical_not(c_up)
            if d == 0:
                to_t, from_s, pair = same & r_up & c_up & (col <= row), same & r_lo & c_lo & (col > row), same & r_up & c_lo
            else:
                to_t, from_s, pair = same & r_lo & c_lo & (col >= row), same & r_up & c_up & (col < row), same & r_lo & c_up
            qd = (qs * jnp.exp(cumsum(to_t, lf))).astype(BF16)
            kd = (kk * jnp.exp(cumsum(from_s, lf))).astype(BF16)
            scores = scores + jnp.where(pair, lax.dot_general(qd, kd, NT, preferred_element_type=F32), 0.0)
            half *= 2
        return scores

    def safe_block(blk, carry):
        r = pl.ds(pl.multiple_of(blk * R, R), R)
        for h, sl, d in chains:
            lb = lbs[d][h]
            f = lb + (1.0 - lb) * jax.nn.sigmoid(f_refs[d][r, sl])
            scores = safe_scores(d, _silu(q_ref[r, sl]) * HEAD_DIM ** -0.5, 1.0 - f, jnp.log(f))
            oi_refs[d][r, sl] = jnp.dot(scores.astype(BF16), i_ref[r, sl].astype(BF16), preferred_element_type=F32)
        return carry

    @pl.when(jnp.max(risk) > HGRN_SAFE_EXPONENT)
    def _():
        lax.fori_loop(0, nblk, safe_block, 0)

    def scan_step(j, states):
        cs = [j, nch - 1 - j]
        rs = [pl.ds(pl.multiple_of(c * C, C), C) for c in cs]
        slots = [pl.ds(pl.multiple_of(c * SUBLANES, SUBLANES), SUBLANES) for c in cs]
        grow = [pl.ds(pl.multiple_of(c * HEAD_DIM, HEAD_DIM), HEAD_DIM) for c in cs]
        new = []
        for h, sl, d in chains:
            St = states[2 * h + d]
            oi_refs[d][rs[d], sl] += lax.dot_general(qe_refs[d][rs[d], sl], St.astype(BF16), NT,
                                                     preferred_element_type=F32)
            new.append(St * ebt_refs[d][slots[d], sl][:1] + ds_refs[d][grow[d], sl])
        return tuple(new)

    if has_init:
        init = tuple(s0_refs[d][h].T for h in range(hb) for d in range(2))
    else:
        init = tuple(jnp.zeros((HEAD_DIM, HEAD_DIM), F32) for _ in range(2 * hb))
    states = lax.fori_loop(0, nch, scan_step, init, unroll=unroll_scan)
    if emit_state:
        for h in range(hb):
            for d in range(2):
                s_out_refs[d][h] = states[2 * h + d].T

    def final_pass(blk, carry):
        r = pl.ds(pl.multiple_of(blk * R, R), R)
        for sl in heads:
            o = _rms(oi_refs[0][r, sl] + oi_refs[1][r, sl], ng_ref[...]) * _silu(g_ref[r, sl])
            o_ref[r, sl] = o.astype(o_ref.dtype)
        return carry

    lax.fori_loop(0, nblk, final_pass, 0)


def hgrn(p, lb_f, lb_b, norm_g, dst, *, nseq, T, row0, A_heads, layer, hb, init=None, emit_state=False,
         unroll_blocks=1, unroll_scan=8):
    assert T % min(HGRN_BLOCK, T) == 0 and A_heads % hb == 0
    rb0 = row0 // T
    H = A_heads
    W = hb * HEAD_DIM
    ng = H // hb

    def slab(k):
        return pl.BlockSpec((T, W), lambda b, h: (rb0 + b, k * ng + h))

    nl = lb_f.shape[0]
    in_specs = [slab(k) for k in range(5)] + [
        pl.BlockSpec((nl, W), lambda b, h: (0, h)),
        pl.BlockSpec((nl, W), lambda b, h: (0, h)),
        pl.BlockSpec((1, HEAD_DIM), lambda b, h: (0, 0))]
    args = [p] * 5 + [lb_f, lb_b, norm_g.reshape(1, HEAD_DIM)]
    state_spec = pl.BlockSpec((None, hb, HEAD_DIM, HEAD_DIM), lambda b, h: (b, h, 0, 0))
    if init is not None:
        in_specs += [state_spec, state_spec]
        args += list(init)
    in_specs, args, aliases = _into(dst, in_specs, args)
    out_specs = [pl.BlockSpec((T, W), lambda b, h: (rb0 + b, h))]
    out_shape = [jax.ShapeDtypeStruct(dst.shape, dst.dtype)]
    if emit_state:
        out_specs += [state_spec, state_spec]
        out_shape += [jax.ShapeDtypeStruct((nseq, H, HEAD_DIM, HEAD_DIM), F32)] * 2
    nch = T // HGRN_CHUNK
    scratch = ([pltpu.VMEM((T, W), BF16)] * 2
               + [pltpu.VMEM((nch * HEAD_DIM, W), F32)] * 2
               + [pltpu.VMEM((T, W), F32)] * 2
               + [pltpu.VMEM((nch * SUBLANES, W), F32)] * 2)
    body = functools.partial(_hgrn_kernel, T=T, hb=hb, layer=layer, has_init=init is not None, emit_state=emit_state,
                             unroll_blocks=unroll_blocks, unroll_scan=unroll_scan)
    out = pl.pallas_call(
        _without_ref(body, len(args) - 1),
        grid=(nseq, ng),
        in_specs=in_specs, out_specs=out_specs, out_shape=out_shape,
        input_output_aliases=aliases,
        scratch_shapes=scratch,
        compiler_params=_cparams("arbitrary", "arbitrary"),
        name="hgrn",
    )(*args)
    return out if emit_state else out[0]


def _dense_attn_kernel(q_ref, k_ref, v_ref, o_ref, kout_ref, vout_ref, *, H, scale):
    heads = [slice(h * HEAD_DIM, (h + 1) * HEAD_DIM) for h in range(H)]
    problems = []
    for h, sl in enumerate(heads):
        kout_ref[:, h, :] = k_ref[:, sl]
        vout_ref[:, h, :] = v_ref[:, sl]
        q = (q_ref[:, sl] * scale).astype(BF16)
        s = lax.dot_general(q, k_ref[:, sl].astype(BF16), NT, preferred_element_type=F32)
        problems.append(([s], [v_ref[:, sl].astype(BF16)]))
    for sl, o in zip(heads, _softmax_pv(problems)):
        o_ref[:, sl] = o.astype(o_ref.dtype)


def dense_attn(p, dst, *, nseq, T, H, col0, dst_col0):
    W = H * HEAD_DIM
    cb = col0 // W
    in_specs, args, aliases = _into(dst, [pl.BlockSpec((T, W), lambda b, k=k: (b, cb + k)) for k in range(3)],
                                    [p, p, p])
    cache_spec = pl.BlockSpec((None, T, H, HEAD_DIM), lambda b: (b, 0, 0, 0))
    cache_shape = jax.ShapeDtypeStruct((nseq, T, H, HEAD_DIM), p.dtype)
    return pl.pallas_call(
        _without_ref(functools.partial(_dense_attn_kernel, H=H, scale=HEAD_DIM ** -0.5 * LOG2E), 3),
        grid=(nseq,),
        in_specs=in_specs,
        out_specs=[pl.BlockSpec((T, W), lambda b: (b, dst_col0 // W)), cache_spec, cache_spec],
        out_shape=[jax.ShapeDtypeStruct(dst.shape, dst.dtype), cache_shape, cache_shape],
        input_output_aliases=aliases,
        compiler_params=_cparams("arbitrary"),
        name="dense_attn",
    )(*args)


NA_QROWS = 4
NA_KROWS = 12


def na_bias_tables(rpb, rows):
    W = GRID_W
    kr = min(NA_ROWS, rows)
    assert kr == NA_ROWS and rows >= NA_KROWS and rows % NA_QROWS == 0
    nblk = rows // NA_QROWS
    col = np.arange(W)
    cs = np.clip(col - NA_COLS // 2, 0, W - NA_COLS)
    col_ok = (col[None, :] >= cs[:, None]) & (col[None, :] < cs[:, None] + NA_COLS)
    ci = np.clip(col[None, :] - col[:, None] + NA_COLS - 1, 0, 2 * NA_COLS - 2)
    col_sel = (ci[..., None] == np.arange(2 * NA_COLS - 1)).astype(np.float32)
    G = jnp.einsum("hac,xyc->haxy", rpb.astype(F32), col_sel, precision=lax.Precision.HIGHEST)
    G = jnp.where(col_ok, G, NEG)
    outside = 2 * NA_ROWS - 1
    G = jnp.concatenate([G, jnp.full_like(G[:, :1], NEG)], axis=1)
    tables = []
    for blk in (0, 1, nblk - 1):
        r0 = blk * NA_QROWS
        u0 = min(max(r0 - NA_ROWS // 2, 0), rows - NA_KROWS)
        r = r0 + np.arange(NA_QROWS)
        ka = u0 + np.arange(NA_KROWS)
        start = np.clip(r - kr // 2, 0, rows - kr)
        row_ok = (ka[None, :] >= start[:, None]) & (ka[None, :] < start[:, None] + kr)
        ri = np.where(row_ok, ka[None, :] - r[:, None] + NA_ROWS - 1, outside)
        tables.append(jnp.concatenate(
            [jnp.concatenate([G[:, int(a)] for a in ri[q]], axis=-1) for q in range(NA_QROWS)], axis=-2))
    return jnp.stack(tables) * LOG2E


def _na_kernel(q_ref, k_ref, v_ref, kc_ref, vc_ref, bias_ref, o_ref, *, rows, scale, hp):
    blk = pl.program_id(2)
    u0 = jnp.clip(blk * NA_QROWS - NA_ROWS // 2, 0, rows - NA_KROWS)
    band = pl.ds(pl.multiple_of(u0 * GRID_W, GRID_W), NA_KROWS * GRID_W)
    heads = [slice(h * HEAD_DIM, (h + 1) * HEAD_DIM) for h in range(hp)]
    problems = []
    for h, sl in enumerate(heads):
        q = (q_ref[:, sl] * scale).astype(BF16)
        s_lat = lax.dot_general(q, k_ref[band, sl].astype(BF16), NT, preferred_element_type=F32) + bias_ref[h]
        s_ctx = lax.dot_general(q, kc_ref[:, sl].astype(BF16), NT, preferred_element_type=F32)
        problems.append(([s_lat, s_ctx], [v_ref[band, sl].astype(BF16), vc_ref[:, sl].astype(BF16)]))
    for sl, o in zip(heads, _softmax_pv(problems)):
        o_ref[:, sl] = o.astype(o_ref.dtype)


def na_attn(p, kc, vc, bias, dst, *, nseq, T, H, row0, col0, dst_col0, hp):
    rows = T // GRID_W
    nblk = rows // NA_QROWS
    tq = NA_QROWS * GRID_W
    L = kc.shape[0] // nseq
    W = hp * HEAD_DIM
    qb0, kb0, cb = row0 // tq, row0 // T, col0 // W
    ng = H // hp

    def kind(blk):
        return jnp.where(blk == 0, 0, jnp.where(blk == nblk - 1, 2, 1))

    in_specs, args, aliases = _into(dst, [
        pl.BlockSpec((tq, W), lambda b, h, i: (qb0 + b * nblk + i, cb + h)),
        pl.BlockSpec((T, W), lambda b, h, i: (kb0 + b, cb + ng + h)),
        pl.BlockSpec((T, W), lambda b, h, i: (kb0 + b, cb + 2 * ng + h)),
        pl.BlockSpec((L, W), lambda b, h, i: (b, h)),
        pl.BlockSpec((L, W), lambda b, h, i: (b, h)),
        pl.BlockSpec((None, hp, tq, NA_KROWS * GRID_W), lambda b, h, i: (kind(i), h, 0, 0))],
        [p, p, p, kc, vc, bias])
    return pl.pallas_call(
        _without_ref(functools.partial(_na_kernel, rows=rows, scale=HEAD_DIM ** -0.5 * LOG2E, hp=hp), 6),
        grid=(nseq, ng, nblk),
        in_specs=in_specs,
        out_specs=pl.BlockSpec((tq, W), lambda b, h, i: (qb0 + b * nblk + i, dst_col0 // W + h)),
        out_shape=jax.ShapeDtypeStruct(dst.shape, dst.dtype),
        input_output_aliases=aliases,
        compiler_params=_cparams("arbitrary", "arbitrary", "arbitrary"),
        name="na_attn",
    )(*args)


ROPE_SWAP = np.concatenate([np.arange(16, 32), np.arange(0, 16), np.arange(48, 64), np.arange(32, 48)])


def rope_tables(T, tm):
    t = jnp.arange(T)
    half = QK_ROPE // 2
    inv = jnp.power(ROPE_BASE, -jnp.arange(0, half, 2, dtype=F32) / half)
    ang_r = (t // GRID_W).astype(F32)[:, None] * inv
    ang_c = (t % GRID_W).astype(F32)[:, None] * inv
    cos = jnp.concatenate([jnp.cos(ang_r), jnp.cos(ang_r), jnp.cos(ang_c), jnp.cos(ang_c)], axis=-1)
    sin = jnp.concatenate([-jnp.sin(ang_r), jnp.sin(ang_r), -jnp.sin(ang_c), jnp.sin(ang_c)], axis=-1)
    cos = jnp.concatenate([cos, jnp.ones((T, LANES - QK_ROPE), F32)], axis=-1)
    sin = jnp.concatenate([sin, jnp.zeros((T, LANES - QK_ROPE), F32)], axis=-1)
    cos = jnp.concatenate([cos, jnp.ones((tm, LANES), F32)], axis=0)
    sin = jnp.concatenate([sin, jnp.zeros((tm, LANES), F32)], axis=0)
    return cos, sin


def _mla_in_kernel(x_ref, w_ref, qg_ref, kvg_ref, cos_ref, sin_ref, cq_ref, ckv16_ref, k2_ref, ckv32_ref, kpe_ref,
                   *, qr, kvr, ncb):
    pr = jnp.dot(x_ref[...], w_ref[...], preferred_element_type=F32)
    cq_ref[...] = _rms(pr[:, :qr], qg_ref[...]).astype(cq_ref.dtype)
    ckv = _rms(pr[:, qr:qr + kvr], kvg_ref[...])
    ckv16_ref[...] = ckv.astype(ckv16_ref.dtype)
    x = pr[:, qr + kvr:]

    @pl.when(pl.program_id(0) < ncb)
    def _():
        ckv32_ref[...] = ckv
        kpe_ref[...] = x

    rot = x * cos_ref[...] + pltpu.roll(x, LANES // 2, axis=1) * sin_ref[...]
    lane = lax.broadcasted_iota(jnp.int32, rot.shape, 1)
    k2_ref[...] = jnp.where(lane < QK_ROPE, rot, 0.0).astype(k2_ref.dtype)


def mla_in(h, w_in_wide, q_norm_g, kv_norm_g, cos, sin, rows, *, qr, kvr, tm):
    n, D = h.shape
    ncb = rows.nc // tm
    pos = pl.BlockSpec((tm, LANES), lambda i: (rows.pos_block(i, tm), 0))

    def out(w):
        return pl.BlockSpec((tm, w), lambda i: (i, 0))

    def ctx_out(w):
        return pl.BlockSpec((tm, w), lambda i: (jnp.minimum(i, ncb - 1), 0))

    return pl.pallas_call(
        functools.partial(_mla_in_kernel, qr=qr, kvr=kvr, ncb=ncb),
        grid=(n // tm,),
        in_specs=[pl.BlockSpec((tm, D), lambda i: (i, 0)), pl.BlockSpec(w_in_wide.shape, lambda i: (0, 0)),
                  _vec_spec(qr), _vec_spec(kvr), pos, pos],
        out_specs=[out(qr), out(kvr), out(LANES), ctx_out(kvr), ctx_out(LANES)],
        out_shape=[jax.ShapeDtypeStruct((n, qr), BF16), jax.ShapeDtypeStruct((n, kvr), BF16),
                   jax.ShapeDtypeStruct((n, LANES), BF16), jax.ShapeDtypeStruct((rows.nc, kvr), F32),
                   jax.ShapeDtypeStruct((rows.nc, LANES), F32)],
        compiler_params=_cparams("arbitrary"),
        name="mla_in",
    )(h, w_in_wide, q_norm_g.reshape(1, qr), kv_norm_g.reshape(1, kvr), cos, sin)


MLA_Q_IN = 3 * LANES
MLA_Q_OUT = 2 * LANES


def widen_w_uq(w_uq, heads):
    r = w_uq.shape[0]
    w = w_uq.reshape(r, heads, QK_NOPE + QK_ROPE)
    nope, pe = w[..., :QK_NOPE], w[..., QK_NOPE:]
    return jnp.concatenate([nope, pe, pe, pe[..., ROPE_SWAP], jnp.zeros_like(pe)], axis=-1).reshape(r, heads * MLA_Q_IN)


def _mla_q_kernel(x_ref, w_ref, cos_ref, sin_ref, o_ref, *, hpt, scale):
    acc = jnp.dot(x_ref[...], w_ref[...], preferred_element_type=F32)
    cos, sin = cos_ref[...], sin_ref[...]
    for j in range(hpt):
        a = acc[:, j * MLA_Q_IN:(j + 1) * MLA_Q_IN]
        o_ref[:, j * MLA_Q_OUT:j * MLA_Q_OUT + LANES] = (a[:, :LANES] * scale).astype(o_ref.dtype)
        q2 = a[:, LANES:2 * LANES] * cos + a[:, 2 * LANES:] * sin
        o_ref[:, j * MLA_Q_OUT + LANES:(j + 1) * MLA_Q_OUT] = (q2 * scale).astype(o_ref.dtype)


def mla_q(cq, w_uq_wide, cos, sin, rows, *, heads, scale, tm, hpt=4):
    n, r = cq.shape
    pos = pl.BlockSpec((tm, LANES), lambda i, j: (rows.pos_block(i, tm), 0))
    return pl.pallas_call(
        functools.partial(_mla_q_kernel, hpt=hpt, scale=scale),
        grid=(n // tm, heads // hpt),
        in_specs=[pl.BlockSpec((tm, r), lambda i, j: (i, 0)),
                  pl.BlockSpec((r, hpt * MLA_Q_IN), lambda i, j: (0, j)), pos, pos],
        out_specs=pl.BlockSpec((tm, hpt * MLA_Q_OUT), lambda i, j: (i, j)),
        out_shape=jax.ShapeDtypeStruct((n, heads * MLA_Q_OUT), BF16),
        compiler_params=_cparams("arbitrary", "arbitrary"),
        name="mla_q",
    )(cq, w_uq_wide, cos, sin)


def _mla_ctx_attn_kernel(q_ref, kv_ref, k2_ref, o_ref, *, heads):
    k2 = k2_ref[...]
    problems = []
    for h in range(heads):
        q = q_ref[:, h * MLA_Q_OUT:(h + 1) * MLA_Q_OUT]
        k = jnp.concatenate([kv_ref[:, 2 * h * LANES:(2 * h + 1) * LANES], k2], axis=1)
        s = lax.dot_general(q, k, NT, preferred_element_type=F32)
        problems.append(([s], [kv_ref[:, (2 * h + 1) * LANES:(2 * h + 2) * LANES]]))
    for h, o in enumerate(_softmax_pv(problems)):
        o_ref[:, h * V_DIM:(h + 1) * V_DIM] = o.astype(o_ref.dtype)


def mla_ctx_attn(q, kv, k2, dst, *, nseq, T, heads):
    in_specs, args, aliases = _into(dst, [pl.BlockSpec((T, heads * MLA_Q_OUT), lambda b: (b, 0)),
                                          pl.BlockSpec((T, kv.shape[1]), lambda b: (b, 0)),
                                          pl.BlockSpec((T, LANES), lambda b: (b, 0))], [q, kv, k2])
    return pl.pallas_call(
        _without_ref(functools.partial(_mla_ctx_attn_kernel, heads=heads), 3),
        grid=(nseq,),
        in_specs=in_specs,
        out_specs=pl.BlockSpec((T, heads * V_DIM), lambda b: (b, 0)),
        out_shape=jax.ShapeDtypeStruct(dst.shape, dst.dtype),
        input_output_aliases=aliases,
        compiler_params=_cparams("arbitrary"),
        name="mla_ctx_attn",
    )(*args)


def _mla_lat_attn_kernel(q_ref, k1_ref, v_ref, k2_ref, k1c_ref, vc_ref, k2c_ref, o_ref, *, sub):
    k = jnp.concatenate([k1_ref[...], k2_ref[...]], axis=1)
    kc = jnp.concatenate([k1c_ref[...], k2c_ref[...]], axis=1)
    blocks = [slice(s, s + sub) for s in range(0, q_ref.shape[0], sub)]
    problems = []
    for r in blocks:
        q = q_ref[r, :]
        s_lat = lax.dot_general(q, k, NT, preferred_element_type=F32)
        s_ctx = lax.dot_general(q, kc, NT, preferred_element_type=F32)
        problems.append(([s_lat, s_ctx], [v_ref[...], vc_ref[...]]))
    for r, o in zip(blocks, _softmax_pv(problems)):
        o_ref[r, :] = o.astype(o_ref.dtype)


def mla_lat_attn(q, kv, k2, kvc, k2c, dst, *, nseq, T, heads, row0, tq, sub):
    nq = T // tq
    P = kvc.shape[0] // nseq
    qb0, kb0 = row0 // tq, row0 // T
    in_specs, args, aliases = _into(dst, [
        pl.BlockSpec((tq, MLA_Q_OUT), lambda b, h, i: (qb0 + b * nq + i, h)),
        pl.BlockSpec((T, LANES), lambda b, h, i: (kb0 + b, 2 * h)),
        pl.BlockSpec((T, LANES), lambda b, h, i: (kb0 + b, 2 * h + 1)),
        pl.BlockSpec((T, LANES), lambda b, h, i: (kb0 + b, 0)),
        pl.BlockSpec((P, LANES), lambda b, h, i: (b, 2 * h)),
        pl.BlockSpec((P, LANES), lambda b, h, i: (b, 2 * h + 1)),
        pl.BlockSpec((P, LANES), lambda b, h, i: (b, 0))],
        [q, kv, kv, k2, kvc, kvc, k2c])
    return pl.pallas_call(
        _without_ref(functools.partial(_mla_lat_attn_kernel, sub=_fit(tq, sub)), 7),
        grid=(nseq, heads, nq),
        in_specs=in_specs,
        out_specs=pl.BlockSpec((tq, V_DIM), lambda b, h, i: (qb0 + b * nq + i, h)),
        out_shape=jax.ShapeDtypeStruct(dst.shape, dst.dtype),
        input_output_aliases=aliases,
        compiler_params=_cparams("arbitrary", "arbitrary", "arbitrary"),
        name="mla_lat_attn",
    )(*args)


def even_layer(h, rows, B, SEQ, Bd, state_f, state_b, cache_k, cache_v, lb_f, lb_b, w_in, j, hgrn_g, rpb, layer):
    D = h.shape[1]
    AW = D // 2
    AH = BH = AW // HEAD_DIM
    T = rows.t
    p = matmul_wcast(h, w_in, j, tm=1024, tn=1024, out_dtype=F32, name="even_in_proj")
    o, s_fw, s_bw = hgrn(p, lb_f, lb_b, hgrn_g, h, nseq=B, T=SEQ, row0=0, A_heads=AH, layer=layer, hb=_fit(AH, 4),
                         emit_state=True)
    o = hgrn(p, lb_f, lb_b, hgrn_g, o, nseq=Bd, T=T, row0=rows.nc, A_heads=AH, layer=layer, hb=_fit(AH, 2),
             init=(state_f, state_b), unroll_blocks=2)
    o, new_k, new_v = dense_attn(p, o, nseq=B, T=SEQ, H=BH, col0=5 * AW, dst_col0=AW)
    past = cache_k.shape[1]
    bias = na_bias_tables(rpb, T // GRID_W)
    o = na_attn(p, cache_k.reshape(Bd * past, BH * HEAD_DIM), cache_v.reshape(Bd * past, BH * HEAD_DIM), bias, o,
                nseq=Bd, T=T, H=BH, row0=rows.nc, col0=5 * AW, dst_col0=AW, hp=_fit(BH, 4))
    return o, s_fw, s_bw, new_k, new_v


def odd_layer(h, rows, B, SEQ, Bd, cache_ckv, cache_kpe, w_in, q_norm_g, w_uq, kv_norm_g, w_ukv):
    D = h.shape[1]
    heads = D // 128
    T = rows.t
    qr, kvr = w_uq.shape[0], w_ukv.shape[0]
    scale = (QK_NOPE + QK_ROPE) ** -0.5 * LOG2E
    tm = rows.tile(1024)
    w_in_wide = jnp.concatenate([w_in, w_in[:, qr + kvr + ROPE_SWAP]], axis=1).astype(BF16)
    cos, sin = rope_tables(T, tm)
    cq, ckv16, k2, ckv32, kpe = mla_in(h, w_in_wide, q_norm_g, kv_norm_g, cos, sin, rows, qr=qr, kvr=kvr, tm=tm)
    q = mla_q(cq, widen_w_uq(w_uq, heads).astype(BF16), cos, sin, rows, heads=heads, scale=scale, tm=tm,
              hpt=_fit(heads, 8))
    w_ukv16 = w_ukv.astype(BF16)
    kv = matmul(ckv16, w_ukv16, tm=2048, tn=2048, out_dtype=BF16, name="mla_kv")
    past = cache_ckv.shape[1]
    kvc = matmul(cache_ckv.reshape(Bd * past, kvr).astype(BF16), w_ukv16, tm=1024, tn=1024, out_dtype=BF16,
                 name="mla_kv_cache")
    k2c = jnp.concatenate([jnp.zeros((Bd * past, LANES - QK_ROPE), F32), cache_kpe.reshape(Bd * past, QK_ROPE)],
                          axis=1).astype(BF16)
    o = mla_ctx_attn(q, kv, k2, h, nseq=B, T=SEQ, heads=heads)
    o = mla_lat_attn(q, kv, k2, kvc, k2c, o, nseq=Bd, T=T, heads=heads, row0=rows.nc, tq=rows.tile(2048), sub=512)
    new_ckv = ckv32.reshape(B, SEQ, kvr)
    new_kpe = kpe[:, :QK_ROPE].reshape(B, SEQ, QK_ROPE)
    return o, new_ckv, new_kpe


def kernel(x_prompt, x_sample, state_hgrn_fwd, state_hgrn_bwd, cache_na_k, cache_na_v, cache_mla_ckv, cache_mla_kpe, c, c_ctx, ada_w, ada_b, norm_g, hgrn_lb_fwd, hgrn_lb_bwd, w_in_even, hgrn_norm_g, na_rpb, w_out_even, w_in_odd, mla_q_norm_g, w_uq, mla_kv_norm_g, w_ukv, w_out_odd, mlp_w1, mlp_w2):
    B, SEQ, D = x_prompt.shape
    Bd, T, _ = x_sample.shape
    depth = ada_w.shape[0]
    rows = Rows(B * SEQ, Bd * T, T)
    x = (x_prompt.reshape(rows.nc, D), x_sample.reshape(rows.nl, D))
    cvec = jnp.zeros((MOD_ROWS, D), F32).at[0].set(c_ctx).at[1:1 + Bd].set(c)
    mod = modulation(cvec, ada_w, ada_b).reshape(depth, MOD_ROWS * 6, 1, D)

    w2 = mlp_w2.astype(BF16)
    new_sf, new_sb, new_nk, new_nv, new_ckv, new_kpe = [], [], [], [], [], []
    h = prenorm(*x, norm_g[0, 0], mod[0], 0, rows)
    for l in range(depth):
        j = l // 2
        if l % 2 == 0:
            o, sf, sb, nk, nv = even_layer(h, rows, B, SEQ, Bd, state_hgrn_fwd[:, j], state_hgrn_bwd[:, j],
                                           cache_na_k[:, j], cache_na_v[:, j], hgrn_lb_fwd, hgrn_lb_bwd,
                                           w_in_even, j, hgrn_norm_g[j], na_rpb[j], l)
            w_out = w_out_even
            new_sf.append(sf)
            new_sb.append(sb)
            new_nk.append(nk)
            new_nv.append(nv)
        else:
            o, ckv, kpe = odd_layer(h, rows, B, SEQ, Bd, cache_mla_ckv[:, j], cache_mla_kpe[:, j], w_in_odd[j],
                                    mla_q_norm_g[j], w_uq[j], mla_kv_norm_g[j], w_ukv[j])
            w_out = w_out_odd
            new_ckv.append(ckv)
            new_kpe.append(kpe)
        x, h = matmul_post(o, w_out.astype(BF16), j, x, norm_g[l, 1], mod[l], 0, rows,
                           pre=(norm_g[l, 2], mod[l], 3), name="out_proj_post")
        a = matmul_wcast(h, mlp_w1, l, tm=2048, tn=1024, out_dtype=BF16, act="relu2", vmem_limit=VMEM_LIMIT_BIG,
                         name="mlp_up")
        down = dict(tm=1024, tk=1024, vmem_limit=VMEM_LIMIT_BIG, name="mlp_down_post")
        if l + 1 < depth:
            x, h = matmul_post(a, w2, l, x, norm_g[l, 3], mod[l], 3, rows, pre=(norm_g[l + 1, 0], mod[l + 1], 0),
                               **down)
        else:
            ncb = rows.nc // rows.tile(down["tm"])
            x = tuple(matmul_post(a, w2, l, x, norm_g[l, 3], mod[l], 3, rows, tiles=t, **down)
                      for t in ((0, ncb), (ncb, rows.nl // rows.tile(down["tm"]))))
    return (x[0].reshape(B, SEQ, D), x[1].reshape(Bd, T, D),
            jnp.stack(new_sf, axis=1), jnp.stack(new_sb, axis=1), jnp.stack(new_nk, axis=1),
            jnp.stack(new_nv, axis=1), jnp.stack(new_ckv, axis=1), jnp.stack(new_kpe, axis=1))
```
